```python
import math
import numpy as np
import jax, jax.numpy as jnp
from jax import lax

D_MODEL = 2048
BATCH = 1
SEQ = 8192
DEPTH = 1

N_SUB = 3
DIFF_HEADS = 8
DIFF_QK_DIM = 64
DIFF_V_DIM = 2 * DIFF_QK_DIM
DN_HEADS = 8
DN_K_DIM = 128
DN_V_DIM = 128
DN_CONV = 4
DN_CHUNK = 64
DT_MIN = 1e-3
DT_MAX = 1e-1
D_FF = ((8 * D_MODEL // 3 + 255) // 256) * 256
REL_BUCKETS = 32
REL_MAX_DIST = 128
Q_BLOCK = 128
DEEPNORM_ALPHA = (2 * DEPTH) ** 0.25
DEEPNORM_BETA = (8 * DEPTH) ** -0.25
LN_EPS = 1e-5
RMS_EPS = 1e-6
IN_SIZES = (DIFF_HEADS * 2 * DIFF_QK_DIM, DIFF_HEADS * 2 * DIFF_QK_DIM, DIFF_HEADS * DIFF_V_DIM,
            DN_HEADS * DN_K_DIM, DN_HEADS * DN_K_DIM, DN_HEADS * DN_V_DIM, DN_HEADS * DN_V_DIM,
            DN_HEADS, DN_HEADS, D_MODEL, D_MODEL)
D_IN = sum(IN_SIZES)

kernel_name = 'hybrid_diffattn_gdn_macaron_deepnorm_adaln'


def layer_norm(x, g, b):
    xf = x.astype(jnp.float32)
    mu = xf.mean(-1, keepdims=True)
    var = jnp.square(xf - mu).mean(-1, keepdims=True)
    return ((xf - mu) * lax.rsqrt(var + LN_EPS) * g.astype(jnp.float32) + b.astype(jnp.float32)).astype(x.dtype)


def rms_norm(x, w, eps):
    xf = x.astype(jnp.float32)
    return (xf * lax.rsqrt(jnp.mean(xf * xf, -1, keepdims=True) + eps) * w.astype(jnp.float32)).astype(x.dtype)


def l2_normalize(x):
    return x * lax.rsqrt(jnp.sum(x * x, -1, keepdims=True) + RMS_EPS)


def split_cols(t, sizes):
    return jnp.split(t, np.cumsum(sizes)[:-1].tolist(), axis=-1)


def swiglu_ffn(h, w_in, w_out):
    g, u = jnp.split(h @ w_in, 2, axis=-1)
    return (jax.nn.silu(g) * u) @ w_out


def t5_bucket(rel):
    n = jnp.maximum(rel, 0)
    max_exact = REL_BUCKETS // 2
    nf = jnp.maximum(n, max_exact).astype(jnp.float32)
    large = max_exact + (jnp.log(nf / max_exact) / math.log(REL_MAX_DIST / max_exact)
                         * (REL_BUCKETS - max_exact)).astype(jnp.int32)
    large = jnp.minimum(large, REL_BUCKETS - 1)
    return jnp.where(n < max_exact, n, large)


def diff_attention(q, k, v, lam, rel_bias):
    B, S, H, _, dk = q.shape
    nblk = S // Q_BLOCK
    qb = q.reshape(B, nblk, Q_BLOCK, H, 2, dk).swapaxes(0, 1)
    kpos = jnp.arange(S, dtype=jnp.int32)
    scale = DIFF_QK_DIM ** -0.5

    def block(args):
        qi, bi = args
        qpos = bi * Q_BLOCK + jnp.arange(Q_BLOCK, dtype=jnp.int32)
        rel = qpos[:, None] - kpos[None, :]
        bias = rel_bias[t5_bucket(rel)].astype(jnp.float32).transpose(2, 0, 1)
        s = jnp.einsum('bqhmd,bkhmd->bhmqk', qi, k).astype(jnp.float32) * scale
        s = s + bias[None, :, None]
        s = jnp.where(rel >= 0, s, -jnp.inf)
        p = jax.nn.softmax(s, axis=-1)
        p = p[:, :, 0] - lam * p[:, :, 1]
        return jnp.einsum('bhqk,bkhd->bqhd', p.astype(v.dtype), v)

    o = lax.map(block, (qb, jnp.arange(nblk, dtype=jnp.int32)))
    return o.swapaxes(0, 1).reshape(B, S, H, v.shape[-1])


def gated_delta_rule(q, k, v, g, beta):
    B, S, H, dk = q.shape
    dv = v.shape[-1]
    C = DN_CHUNK
    N = S // C
    c5 = lambda t: t.reshape(B, N, C, H, t.shape[-1]).transpose(1, 0, 3, 2, 4)
    c4 = lambda t: t.reshape(B, N, C, H).transpose(1, 0, 3, 2)
    qc, kc, vc = c5(q), c5(k), c5(v)
    gc, bc = c4(g), c4(beta)
    G = jnp.cumsum(gc, axis=-1)
    tril = jnp.tril(jnp.ones((C, C), bool))
    strict = jnp.tril(jnp.ones((C, C), bool), -1)
    gdiff = G[..., :, None] - G[..., None, :]
    decay = jnp.where(tril, jnp.exp(jnp.where(tril, gdiff, 0.0)), 0.0)
    kb = kc * bc[..., None]
    A = jnp.where(strict, jnp.einsum('nbhid,nbhjd->nbhij', kb, kc) * decay, 0.0)
    eye = jnp.eye(C, dtype=jnp.float32)
    T = lax.linalg.triangular_solve(A + eye, jnp.broadcast_to(eye, A.shape), left_side=True,
                                    lower=True, unit_diagonal=True)
    u = T @ (vc * bc[..., None])
    w = T @ (kb * jnp.exp(G)[..., None])
    qk = jnp.where(tril, jnp.einsum('nbhid,nbhjd->nbhij', qc, kc) * decay, 0.0)
    q_dec = qc * jnp.exp(G)[..., None]
    k_dec = kc * jnp.exp(G[..., -1:] - G)[..., None]
    g_last = jnp.exp(G[..., -1])

    def step(state, inp):
        u_c, w_c, qd_c, kd_c, qk_c, gl_c = inp
        v_new = u_c - w_c @ state
        o = qd_c @ state + qk_c @ v_new
        state = state * gl_c[..., None, None] + jnp.einsum('bhcd,bhce->bhde', kd_c, v_new)
        return state, o

    s0 = jnp.zeros((B, H, dk, dv), jnp.float32)
    _, o = lax.scan(step, s0, (u, w, q_dec, k_dec, qk, g_last))
    return o.transpose(1, 0, 3, 2, 4).reshape(B, S, H, dv)


def hybrid_mixer(h, w_in, conv_w, a_log, dt_bias, dn_norm_w, lam_params, subln_w, rel_bias,
                 w_a, w_b, w_o, lam_init):
    B, S, _ = h.shape
    dq, dk_, dv_, nq, nk, nv, nz, nb, na, ga, gb = split_cols(h @ w_in, IN_SIZES)
    lp = lam_params.astype(jnp.float32)
    lam = jnp.exp(jnp.sum(lp[0] * lp[1])) - jnp.exp(jnp.sum(lp[2] * lp[3])) + lam_init
    ya = diff_attention(dq.reshape(B, S, DIFF_HEADS, 2, DIFF_QK_DIM),
                        dk_.reshape(B, S, DIFF_HEADS, 2, DIFF_QK_DIM),
                        dv_.reshape(B, S, DIFF_HEADS, DIFF_V_DIM), lam, rel_bias)
    ya = rms_norm(ya, subln_w, LN_EPS) * (1.0 - lam_init)
    ya = ya.reshape(B, S, DIFF_HEADS * DIFF_V_DIM) @ w_a
    qkv = jnp.concatenate([nq, nk, nv], axis=-1)
    ch = qkv.shape[-1]
    qkv = lax.conv_general_dilated(qkv, conv_w[:, None, :], window_strides=(1,),
                                   padding=((DN_CONV - 1, 0),),
                                   dimension_numbers=('NWC', 'WIO', 'NWC'), feature_group_count=ch)
    qkv = jax.nn.silu(qkv).astype(jnp.float32)
    cq, ck, cv = split_cols(qkv, (DN_HEADS * DN_K_DIM, DN_HEADS * DN_K_DIM, DN_HEADS * DN_V_DIM))
    q = l2_normalize(cq.reshape(B, S, DN_HEADS, DN_K_DIM)) * (DN_K_DIM ** -0.5)
    k = l2_normalize(ck.reshape(B, S, DN_HEADS, DN_K_DIM))
    v = cv.reshape(B, S, DN_HEADS, DN_V_DIM)
    beta = jax.nn.sigmoid(nb.astype(jnp.float32))
    g = -jnp.exp(a_log.astype(jnp.float32)) * jax.nn.softplus(na.astype(jnp.float32) + dt_bias.astype(jnp.float32))
    o = gated_delta_rule(q, k, v, g, beta)
    o = rms_norm(o, dn_norm_w, RMS_EPS) * jax.nn.silu(nz.reshape(B, S, DN_HEADS, DN_V_DIM).astype(jnp.float32))
    yb = o.reshape(B, S, DN_HEADS * DN_V_DIM).astype(h.dtype) @ w_b
    merged = jax.nn.sigmoid(ga) * ya + jax.nn.sigmoid(gb) * yb
    return merged @ w_o


def setup_inputs(seed: int = 0) -> dict:
    key = jax.random.key(seed)
    ks = jax.random.split(key, 20)
    f32 = jnp.float32
    nrm = lambda kk, shape, s: jax.random.normal(kk, shape, f32) * s
    diff_v_w = DIFF_HEADS * DIFF_V_DIM
    dn_v_w = DN_HEADS * DN_V_DIM
    conv_ch = 2 * DN_HEADS * DN_K_DIM + dn_v_w
    dt = jnp.exp(jax.random.uniform(ks[11], (DEPTH, DN_HEADS), f32, math.log(DT_MIN), math.log(DT_MAX)))
    return {
        'x': nrm(ks[0], (BATCH, SEQ, D_MODEL), 1.0),
        'c': nrm(ks[1], (BATCH, D_MODEL), 1.0),
        'w_ada': nrm(ks[2], (DEPTH, D_MODEL, N_SUB * 3 * D_MODEL), 0.5 * D_MODEL ** -0.5),
        'b_ada': nrm(ks[3], (DEPTH, N_SUB * 3 * D_MODEL), 0.02),
        'ln_g': 1.0 + nrm(ks[4], (DEPTH, N_SUB, D_MODEL), 0.02),
        'ln_b': nrm(ks[5], (DEPTH, N_SUB, D_MODEL), 0.02),
        'w_ffn_in': nrm(ks[6], (DEPTH, 2, D_MODEL, 2 * D_FF), D_MODEL ** -0.5),
        'w_ffn_out': nrm(ks[7], (DEPTH, 2, D_FF, D_MODEL), DEEPNORM_BETA * D_FF ** -0.5),
        'w_in': nrm(ks[8], (DEPTH, D_MODEL, D_IN), D_MODEL ** -0.5),
        'conv_w': nrm(ks[9], (DEPTH, DN_CONV, conv_ch), DN_CONV ** -0.5),
        'dn_a_log': jnp.log(jax.random.uniform(ks[10], (DEPTH, DN_HEADS), f32, 1.0, 16.0)),
        'dn_dt_bias': dt + jnp.log(-jnp.expm1(-dt)),
        'dn_norm_w': 1.0 + nrm(ks[12], (DEPTH, DN_V_DIM), 0.02),
        'diff_lambda': nrm(ks[13], (DEPTH, 4, DIFF_QK_DIM), 0.1),
        'diff_subln_w': 1.0 + nrm(ks[14], (DEPTH, DIFF_V_DIM), 0.02),
        'rel_bias': nrm(ks[15], (REL_BUCKETS, DIFF_HEADS), 0.5),
        'w_branch_a': nrm(ks[16], (DEPTH, diff_v_w, D_MODEL), diff_v_w ** -0.5),
        'w_branch_b': nrm(ks[17], (DEPTH, dn_v_w, D_MODEL), dn_v_w ** -0.5),
        'w_out': nrm(ks[18], (DEPTH, D_MODEL, D_MODEL), DEEPNORM_BETA * D_MODEL ** -0.5),
    }


def reference(x, c, w_ada, b_ada, ln_g, ln_b, w_ffn_in, w_ffn_out, w_in, conv_w, dn_a_log,
              dn_dt_bias, dn_norm_w, diff_lambda, diff_subln_w, rel_bias, w_branch_a, w_branch_b, w_out):
    B = x.shape[0]
    for l in range(DEPTH):
        lam_init = 0.8 - 0.6 * math.exp(-0.3 * l)
        ada = (jax.nn.silu(c) @ w_ada[l] + b_ada[l]).reshape(B, N_SUB, 3, D_MODEL)
        shift, scale, gate = ada[:, :, 0, None], ada[:, :, 1, None], ada[:, :, 2, None]
        y = swiglu_ffn(x * (1.0 + scale[:, 0]) + shift[:, 0], w_ffn_in[l, 0], w_ffn_out[l, 0])
        x = layer_norm(DEEPNORM_ALPHA * x + 0.5 * gate[:, 0] * y, ln_g[l, 0], ln_b[l, 0])
        y = hybrid_mixer(x * (1.0 + scale[:, 1]) + shift[:, 1], w_in[l], conv_w[l], dn_a_log[l],
                         dn_dt_bias[l], dn_norm_w[l], diff_lambda[l], diff_subln_w[l], rel_bias,
                         w_branch_a[l], w_branch_b[l], w_out[l], lam_init)
        x = layer_norm(DEEPNORM_ALPHA * x + gate[:, 1] * y, ln_g[l, 1], ln_b[l, 1])
        y = swiglu_ffn(x * (1.0 + scale[:, 2]) + shift[:, 2], w_ffn_in[l, 1], w_ffn_out[l, 1])
        x = layer_norm(DEEPNORM_ALPHA * x + 0.5 * gate[:, 2] * y, ln_g[l, 2], ln_b[l, 2])
    return x
```

```python
import functools
import math

import numpy as np
import jax
import jax.numpy as jnp
from jax import lax
from jax.experimental import pallas as pl
from jax.experimental.pallas import tpu as pltpu

N_SUB = 3
HEADS = 8
HEAD_W = 128
DIFF_QK_DIM = 64
DN_CONV = 4
DN_CHUNK = 64
REL_BUCKETS = 32
REL_MAX_DIST = 128
LN_EPS = 1e-5
RMS_EPS = 1e-6
MASK_VALUE = -1e30

F32 = jnp.float32
BF16 = jnp.bfloat16

V7X_VMEM_BYTES = 64 * 1024 * 1024
VMEM_LIMIT = 56 * 1024 * 1024


def _params(*sem):
    return pltpu.CompilerParams(dimension_semantics=sem, vmem_limit_bytes=VMEM_LIMIT)


def _tile(n, preferred):
    t = min(preferred, n)
    while n % t:
        t -= 128
    return t


def _silu(x):
    return x * jax.nn.sigmoid(x)


def _dot(a, b):
    return jnp.dot(a, b, preferred_element_type=F32)


def _dot_nt(a, b):
    return lax.dot_general(a, b, (((1,), (1,)), ((), ())), preferred_element_type=F32)


def _dot_tn(a, b):
    return lax.dot_general(a, b, (((0,), (0,)), ((), ())), preferred_element_type=F32)


def _layer_norm(r, g, b):
    mu = jnp.mean(r, axis=-1, keepdims=True)
    d = r - mu
    var = jnp.mean(d * d, axis=-1, keepdims=True)
    return d * lax.rsqrt(var + LN_EPS) * g + b


def _ada_kernel(c_ref, w_ref, b_ref, o_ref):
    sc = _silu(c_ref[...])
    o_ref[...] = jnp.sum(w_ref[...] * sc, axis=0, keepdims=True) + b_ref[...]


def _ada(c_col, w, b):
    d, n = w.shape
    tn = _tile(n, 1024)
    return pl.pallas_call(
        _ada_kernel,
        grid=(n // tn,),
        in_specs=[pl.BlockSpec((d, 1), lambda j: (0, 0)),
                  pl.BlockSpec((d, tn), lambda j: (0, j)),
                  pl.BlockSpec((1, tn), lambda j: (0, j))],
        out_specs=pl.BlockSpec((1, tn), lambda j: (0, j)),
        out_shape=jax.ShapeDtypeStruct((1, n), F32),
        compiler_params=_params("arbitrary"),
        name="ada",
    )(c_col, w, b)


def _ffn_kernel(x_ref, shift_ref, scale_ref, gate_ref, lng_ref, lnb_ref, wg_ref, wu_ref, wo_ref,
                o_ref, h_sc, acc_sc, *, alpha, gate_mul):
    j = pl.program_id(1)

    @pl.when(j == 0)
    def _():
        h_sc[...] = (x_ref[...] * (1.0 + scale_ref[...]) + shift_ref[...]).astype(BF16)
        acc_sc[...] = jnp.zeros_like(acc_sc)

    h = h_sc[...]
    g = _dot(h, wg_ref[...])
    u = _dot(h, wu_ref[...])
    a = (_silu(g) * u).astype(BF16)
    acc_sc[...] += _dot(a, wo_ref[...])

    @pl.when(j == pl.num_programs(1) - 1)
    def _():
        r = alpha * x_ref[...] + (gate_mul * gate_ref[...]) * acc_sc[...]
        o_ref[...] = _layer_norm(r, lng_ref[...], lnb_ref[...])


def _ffn(x, shift, scale, gate, lng, lnb, w_in, w_out, *, alpha, tm=512, tf=512):
    s, d = x.shape
    f = w_out.shape[0]
    tm, tf = _tile(s, tm), _tile(f, tf)
    nf = f // tf
    vec = pl.BlockSpec((1, d), lambda i, j: (0, 0))
    return pl.pallas_call(
        functools.partial(_ffn_kernel, alpha=alpha, gate_mul=0.5),
        grid=(s // tm, nf),
        in_specs=[pl.BlockSpec((tm, d), lambda i, j: (i, 0)), vec, vec, vec, vec, vec,
                  pl.BlockSpec((d, tf), lambda i, j: (0, j)),
                  pl.BlockSpec((d, tf), lambda i, j: (0, j + nf)),
                  pl.BlockSpec((tf, d), lambda i, j: (j, 0))],
        out_specs=pl.BlockSpec((tm, d), lambda i, j: (i, 0)),
        out_shape=jax.ShapeDtypeStruct((s, d), F32),
        scratch_shapes=[pltpu.VMEM((tm, d), BF16), pltpu.VMEM((tm, d), F32)],
        compiler_params=_params("parallel", "arbitrary"),
        name="ffn",
    )(x, shift, scale, gate, lng, lnb, w_in, w_in, w_out)


def _proj_kernel(x_ref, shift_ref, scale_ref, w_ref, ws_ref, o_ref, os_ref, h_sc):
    j = pl.program_id(1)

    @pl.when(j == 0)
    def _():
        h = (x_ref[...] * (1.0 + scale_ref[...]) + shift_ref[...]).astype(BF16)
        h_sc[...] = h
        os_ref[...] = _dot(h, ws_ref[...])

    o_ref[...] = _dot(h_sc[...], w_ref[...]).astype(BF16)


def _proj(x, shift, scale, w_main, w_small, *, tm=1024, tn=1024):
    s, d = x.shape
    n = w_main.shape[1]
    ns = w_small.shape[1]
    tm, tn = _tile(s, tm), _tile(n, tn)
    vec = pl.BlockSpec((1, d), lambda i, j: (0, 0))
    return pl.pallas_call(
        _proj_kernel,
        grid=(s // tm, n // tn),
        in_specs=[pl.BlockSpec((tm, d), lambda i, j: (i, 0)), vec, vec,
                  pl.BlockSpec((d, tn), lambda i, j: (0, j)),
                  pl.BlockSpec((d, ns), lambda i, j: (0, 0))],
        out_specs=[pl.BlockSpec((tm, tn), lambda i, j: (i, j)),
                   pl.BlockSpec((tm, ns), lambda i, j: (i, 0))],
        out_shape=[jax.ShapeDtypeStruct((s, n), BF16), jax.ShapeDtypeStruct((s, ns), F32)],
        scratch_shapes=[pltpu.VMEM((tm, d), BF16)],
        compiler_params=_params("parallel", "arbitrary"),
        name="proj",
    )(x, shift, scale, w_main, w_small)


def _bucket_starts():
    n = np.arange(0, 2 * REL_MAX_DIST)
    max_exact = REL_BUCKETS // 2
    nf = np.maximum(n, max_exact).astype(np.float32)
    large = max_exact + (np.log(nf / np.float32(max_exact)) / np.float32(math.log(REL_MAX_DIST / max_exact))
                         * np.float32(REL_BUCKETS - max_exact)).astype(np.int32)
    bucket = np.where(n < max_exact, n, np.minimum(large, REL_BUCKETS - 1))
    assert np.all(np.diff(bucket) >= 0) and bucket[-1] == REL_BUCKETS - 1
    return [int(np.min(n[bucket >= b])) for b in range(REL_BUCKETS)]


BUCKET_STARTS = _bucket_starts()


def _bias_kernel(rb_ref, o_ref, *, t):
    h = pl.program_id(0)
    jj = lax.broadcasted_iota(jnp.int32, (t, 2 * t), 0)
    ii = lax.broadcasted_iota(jnp.int32, (t, 2 * t), 1)
    ii = jnp.where(ii >= t, ii - t, ii)
    far = rb_ref[REL_BUCKETS - 1, h]
    for n in range(2):
        rel = ii - jj + n * t
        bias = jnp.full((t, 2 * t), rb_ref[0, h] - far, F32)
        for b in range(1, REL_BUCKETS):
            bias = jnp.where(rel >= BUCKET_STARTS[b], rb_ref[b, h] - far, bias)
        o_ref[0, n] = jnp.where(rel < 0, MASK_VALUE, bias)


def _bias_tiles(rel_bias, t):
    return pl.pallas_call(
        functools.partial(_bias_kernel, t=t),
        grid=(HEADS,),
        in_specs=[pl.BlockSpec(memory_space=pltpu.SMEM)],
        out_specs=pl.BlockSpec((1, 2, t, 2 * t), lambda h: (h, 0, 0, 0)),
        out_shape=jax.ShapeDtypeStruct((HEADS, 2, t, 2 * t), F32),
        compiler_params=_params("arbitrary"),
        name="bias_tiles",
    )(rel_bias)


def _attn_kernel(q_ref, k_ref, v_ref, bias_ref, lam_ref, sub_ref, o_ref,
                 vt_sc, m_sc, l_sc, acc_sc, *, t, lam_init):
    qi = pl.program_id(1)
    n_kv = vt_sc.shape[0]

    @pl.when(qi == 0)
    def _():
        def body(c, carry):
            r0 = pl.multiple_of(c * t, t)
            vt_sc[c] = v_ref[pl.ds(r0, t), :].astype(F32).T.astype(BF16)
            return carry
        lax.fori_loop(0, n_kv, body, 0)

    qt = (q_ref[...].astype(F32) * (DIFF_QK_DIM ** -0.5)).T
    row = lax.broadcasted_iota(jnp.int32, (HEAD_W, t), 0)
    zero = jnp.zeros_like(qt)
    qt2 = jnp.concatenate([jnp.where(row < DIFF_QK_DIM, qt, zero),
                           jnp.where(row >= DIFF_QK_DIM, qt, zero)], axis=1).astype(BF16)

    m_sc[...] = jnp.full_like(m_sc, -jnp.inf)
    l_sc[...] = jnp.zeros_like(l_sc)
    acc_sc[...] = jnp.zeros_like(acc_sc)

    def step(kb, bias):
        r0 = pl.multiple_of(kb * t, t)
        s = _dot(k_ref[pl.ds(r0, t), :], qt2)
        if bias is not None:
            s = s + bias
        m_old = m_sc[...]
        m_new = jnp.maximum(m_old, jnp.max(s, axis=0, keepdims=True))
        p = jnp.exp(s - m_new)
        a = jnp.exp(m_old - m_new)
        l_sc[...] = a * l_sc[...] + jnp.sum(p, axis=0, keepdims=True)
        acc_sc[...] = a * acc_sc[...] + _dot(vt_sc[kb], p.astype(BF16))
        m_sc[...] = m_new

    def far_body(kb, carry):
        step(kb, None)
        return carry
    lax.fori_loop(0, jnp.maximum(qi - 1, 0), far_body, 0)

    @pl.when(qi >= 1)
    def _():
        step(qi - 1, bias_ref[0, 1])
    step(qi, bias_ref[0, 0])

    lp = lam_ref[...]
    lam = (jnp.exp(jnp.sum(lp[0:1] * lp[1:2], axis=1, keepdims=True))
           - jnp.exp(jnp.sum(lp[2:3] * lp[3:4], axis=1, keepdims=True)) + lam_init)
    o = acc_sc[...] / l_sc[...]
    od = o[:, :t] - lam * o[:, t:]
    ms = jnp.mean(od * od, axis=0, keepdims=True)
    y = od * lax.rsqrt(ms + LN_EPS) * sub_ref[...] * (1.0 - lam_init)
    o_ref[...] = y.T.astype(BF16)


def _attn(proj, bias_tiles, lam_params, subln_col, *, q_blk, k_blk, v_blk, lam_init, t):
    s = proj.shape[0]
    return pl.pallas_call(
        functools.partial(_attn_kernel, t=t, lam_init=lam_init),
        grid=(HEADS, s // t),
        in_specs=[pl.BlockSpec((t, HEAD_W), lambda h, i: (i, q_blk + h)),
                  pl.BlockSpec((s, HEAD_W), lambda h, i: (0, k_blk + h)),
                  pl.BlockSpec((s, HEAD_W), lambda h, i: (0, v_blk + h)),
                  pl.BlockSpec((1, 2, t, 2 * t), lambda h, i: (h, 0, 0, 0)),
                  pl.BlockSpec(lam_params.shape, lambda h, i: (0, 0)),
                  pl.BlockSpec((HEAD_W, 1), lambda h, i: (0, 0))],
        out_specs=pl.BlockSpec((t, HEAD_W), lambda h, i: (i, h)),
        out_shape=jax.ShapeDtypeStruct((s, HEADS * HEAD_W), BF16),
        scratch_shapes=[pltpu.VMEM((s // t, HEAD_W, t), BF16),
                        pltpu.VMEM((1, 2 * t), F32), pltpu.VMEM((1, 2 * t), F32),
                        pltpu.VMEM((HEAD_W, 2 * t), F32)],
        compiler_params=_params("arbitrary", "arbitrary"),
        name="diff_attn",
    )(proj, proj, proj, bias_tiles, lam_params, subln_col)


def _gates_kernel(sm_ref, alog_ref, dtb_ref, o_ref):
    tr = sm_ref[...].T
    beta = jax.nn.sigmoid(tr[0:HEADS])
    x = tr[HEADS:2 * HEADS] + dtb_ref[...]
    softplus = jnp.maximum(x, 0.0) + jnp.log1p(jnp.exp(-jnp.abs(x)))
    g = -jnp.exp(alog_ref[...]) * softplus
    pos = lax.broadcasted_iota(jnp.int32, g.shape, 1) % DN_CHUNK
    shift = 1
    while shift < DN_CHUNK:
        g = g + jnp.where(pos >= shift, pltpu.roll(g, shift, axis=1), 0.0)
        shift *= 2
    o_ref[0:HEADS] = beta
    o_ref[HEADS:2 * HEADS] = g


def _gates(small, alog_col, dtb_col, *, tb=1024):
    s, ns = small.shape
    col = pl.BlockSpec((HEADS, 1), lambda i: (0, 0))
    return pl.pallas_call(
        _gates_kernel,
        grid=(s // tb,),
        in_specs=[pl.BlockSpec((tb, ns), lambda i: (i, 0)), col, col],
        out_specs=pl.BlockSpec((2 * HEADS, tb), lambda i: (0, i)),
        out_shape=jax.ShapeDtypeStruct((2 * HEADS, s), F32),
        compiler_params=_params("parallel"),
        name="gdn_gates",
    )(small, alog_col, dtb_col)


GROUP = 2 * DN_CHUNK


def _gdn_kernel(q_ref, k_ref, v_ref, z_ref, gt_ref, cwq_ref, cwk_ref, cwv_ref, nw_ref, o_ref,
                pad_sc, state_sc, o_sc, *, tb):
    h = pl.program_id(0)
    ib = pl.program_id(1)
    halo = 8

    @pl.when(ib == 0)
    def _():
        state_sc[...] = jnp.zeros_like(state_sc)
        pad_sc[:, 0:halo, :] = jnp.zeros((3, halo, HEAD_W), F32)

    def conv_silu(a, x_ref, cw_ref):
        pad_sc[a, halo:halo + tb, :] = x_ref[...].astype(F32)
        cw = cw_ref[...]
        y = cw[0:1] * pad_sc[a, halo - 3:halo - 3 + tb, :]
        for j in range(1, DN_CONV):
            y = y + cw[j:j + 1] * pad_sc[a, halo - 3 + j:halo - 3 + j + tb, :]
        pad_sc[a, 0:halo, :] = pad_sc[a, tb:tb + halo, :]
        return _silu(y)

    def l2n(x):
        return x * lax.rsqrt(jnp.sum(x * x, axis=-1, keepdims=True) + RMS_EPS)

    q = l2n(conv_silu(0, q_ref, cwq_ref)) * (HEAD_W ** -0.5)
    k = l2n(conv_silu(1, k_ref, cwk_ref))
    v = conv_silu(2, v_ref, cwv_ref)

    beta_t = jnp.broadcast_to(gt_ref[pl.ds(h, 1), :], (HEAD_W, tb))
    cum_t = jnp.broadcast_to(gt_ref[pl.ds(HEADS + h, 1), :], (HEAD_W, tb))
    beta_c = beta_t.T
    cum_c = cum_t.T

    ii = lax.broadcasted_iota(jnp.int32, (GROUP, GROUP), 0)
    jj = lax.broadcasted_iota(jnp.int32, (GROUP, GROUP), 1)
    same = (ii >= DN_CHUNK) == (jj >= DN_CHUNK)
    tril = jnp.logical_and(same, ii >= jj)
    strict = jnp.logical_and(same, ii > jj)
    eye = (ii == jj).astype(F32)

    for r in range(tb // GROUP):
        rows = slice(r * GROUP, (r + 1) * GROUP)
        qg, kg, vg = q[rows], k[rows], v[rows]
        cg, bg = cum_c[rows], beta_c[rows]
        gdiff = cg - cum_t[:, rows]
        decay = jnp.where(tril, jnp.exp(jnp.where(tril, gdiff, 0.0)), 0.0)
        kb = kg * bg
        kg16 = kg.astype(BF16)
        a = jnp.where(strict, _dot_nt(kb.astype(BF16), kg16) * decay, 0.0)
        x = a.astype(BF16)
        tinv = eye - a
        power = 2
        while power < DN_CHUNK:
            x2 = _dot(x, x)
            x = x2.astype(BF16)
            tinv = tinv + _dot(tinv.astype(BF16), x)
            power *= 2
        eg = jnp.exp(cg)
        rhs = jnp.concatenate([vg * bg, kb * eg], axis=1).astype(BF16)
        uw = _dot(tinv.astype(BF16), rhs)
        u, w = uw[:, :HEAD_W], uw[:, HEAD_W:]
        qk = (_dot_nt(qg.astype(BF16), kg16) * decay).astype(BF16)
        qd = (qg * eg).astype(BF16)

        v_new, o_inter = [], []
        for c in range(2):
            cr = slice(c * DN_CHUNK, (c + 1) * DN_CHUNK)
            g_last = cg[c * DN_CHUNK + DN_CHUNK - 1:c * DN_CHUNK + DN_CHUNK, :]
            kd = (kg[cr] * jnp.exp(g_last - cg[cr])).astype(BF16)
            st = state_sc[...]
            st16 = st.astype(BF16)
            vn = u[cr] - _dot(w[cr].astype(BF16), st16)
            o_inter.append(_dot(qd[cr], st16))
            state_sc[...] = st * jnp.exp(g_last) + _dot_tn(kd, vn.astype(BF16))
            v_new.append(vn)
        vn_all = jnp.concatenate(v_new, axis=0).astype(BF16)
        o_sc[rows, :] = jnp.concatenate(o_inter, axis=0) + _dot(qk, vn_all)

    o = o_sc[...]
    on = o * lax.rsqrt(jnp.mean(o * o, axis=-1, keepdims=True) + RMS_EPS) * nw_ref[...]
    o_ref[...] = (on * _silu(z_ref[...].astype(F32))).astype(BF16)


def _gdn(proj, gates_t, conv_w, norm_w, *, q_blk, k_blk, v_blk, z_blk, tb=256):
    s = proj.shape[0]
    col = lambda blk: pl.BlockSpec((tb, HEAD_W), lambda h, i: (i, blk + h))
    cw = lambda blk: pl.BlockSpec((DN_CONV, HEAD_W), lambda h, i: (0, blk + h))
    return pl.pallas_call(
        functools.partial(_gdn_kernel, tb=tb),
        grid=(HEADS, s // tb),
        in_specs=[col(q_blk), col(k_blk), col(v_blk), col(z_blk),
                  pl.BlockSpec((2 * HEADS, tb), lambda h, i: (0, i)),
                  cw(0), cw(HEADS), cw(2 * HEADS),
                  pl.BlockSpec((1, HEAD_W), lambda h, i: (0, 0))],
        out_specs=pl.BlockSpec((tb, HEAD_W), lambda h, i: (i, h)),
        out_shape=jax.ShapeDtypeStruct((s, HEADS * HEAD_W), BF16),
        scratch_shapes=[pltpu.VMEM((3, tb + 8, HEAD_W), F32),
                        pltpu.VMEM((HEAD_W, HEAD_W), F32),
                        pltpu.VMEM((tb, HEAD_W), F32)],
        compiler_params=_params("arbitrary", "arbitrary"),
        name="gdn",
    )(proj, proj, proj, proj, gates_t, conv_w, conv_w, conv_w, norm_w)


def _merge_kernel(ya_ref, yb_ref, ga_ref, gb_ref, x_ref, gate_ref, lng_ref, lnb_ref,
                  wa_ref, wb_ref, wo_ref, o_ref, *, alpha):
    a = _dot(ya_ref[...], wa_ref[...])
    b = _dot(yb_ref[...], wb_ref[...])
    merged = (jax.nn.sigmoid(ga_ref[...].astype(F32)) * a
              + jax.nn.sigmoid(gb_ref[...].astype(F32)) * b).astype(BF16)
    y = _dot(merged, wo_ref[...])
    r = alpha * x_ref[...] + gate_ref[...] * y
    o_ref[...] = _layer_norm(r, lng_ref[...], lnb_ref[...])


def _merge(ya, yb, proj, x, gate, lng, lnb, w_a, w_b, w_o, *, ga_blk, gb_blk, alpha, tm=256):
    s, d = x.shape
    dv = ya.shape[1]
    vec = pl.BlockSpec((1, d), lambda i: (0, 0))
    const = lambda shape: pl.BlockSpec(shape, lambda i: (0, 0), pipeline_mode=pl.Buffered(1))
    return pl.pallas_call(
        functools.partial(_merge_kernel, alpha=alpha),
        grid=(s // tm,),
        in_specs=[pl.BlockSpec((tm, dv), lambda i: (i, 0)),
                  pl.BlockSpec((tm, dv), lambda i: (i, 0)),
                  pl.BlockSpec((tm, d), lambda i: (i, ga_blk)),
                  pl.BlockSpec((tm, d), lambda i: (i, gb_blk)),
                  pl.BlockSpec((tm, d), lambda i: (i, 0)), vec, vec, vec,
                  const(w_a.shape), const(w_b.shape), const(w_o.shape)],
        out_specs=pl.BlockSpec((tm, d), lambda i: (i, 0)),
        out_shape=jax.ShapeDtypeStruct((s, d), F32),
        compiler_params=_params("parallel"),
        name="merge",
    )(ya, yb, proj, proj, x, gate, lng, lnb, w_a, w_b, w_o)


def kernel(x, c, w_ada, b_ada, ln_g, ln_b, w_ffn_in, w_ffn_out, w_in, conv_w, dn_a_log, dn_dt_bias,
           dn_norm_w, diff_lambda, diff_subln_w, rel_bias, w_branch_a, w_branch_b, w_out):
    bsz, s, d = x.shape
    assert bsz == 1, "one sequence per call"
    depth = w_ada.shape[0]
    alpha = (2 * depth) ** 0.25
    hw = HEADS * HEAD_W
    attn_t = 256
    assert d % HEAD_W == 0 and s % 1024 == 0

    o_nb = 7 * hw
    o_ga = o_nb + 2 * HEADS
    nblk = d // HEAD_W
    blk = {"ga": 0, "gb": 1, "dq": 2 * nblk, "dk": 2 * nblk + HEADS, "dv": 2 * nblk + 2 * HEADS,
           "nq": 2 * nblk + 3 * HEADS, "nk": 2 * nblk + 4 * HEADS, "nv": 2 * nblk + 5 * HEADS,
           "nz": 2 * nblk + 6 * HEADS}

    bias_tiles = _bias_tiles(rel_bias, attn_t)
    x2 = x[0]
    for l in range(depth):
        lam_init = 0.8 - 0.6 * math.exp(-0.3 * l)
        ada = _ada(c.reshape(d, 1), w_ada[l], b_ada[l].reshape(1, -1)).reshape(N_SUB, 3, 1, d)
        shift, scale, gate = ada[:, 0], ada[:, 1], ada[:, 2]
        lng, lnb = ln_g[l].reshape(N_SUB, 1, d), ln_b[l].reshape(N_SUB, 1, d)
        wfi, wfo = w_ffn_in[l].astype(BF16), w_ffn_out[l].astype(BF16)
        wi = w_in[l]
        w_main = jnp.concatenate([wi[:, o_ga:], wi[:, :o_nb]], axis=1).astype(BF16)
        w_small = jnp.pad(wi[:, o_nb:o_ga], ((0, 0), (0, HEAD_W - 2 * HEADS))).astype(BF16)

        x2 = _ffn(x2, shift[0], scale[0], gate[0], lng[0], lnb[0], wfi[0], wfo[0], alpha=alpha)

        proj, small = _proj(x2, shift[1], scale[1], w_main, w_small)
        ya = _attn(proj, bias_tiles, diff_lambda[l], diff_subln_w[l].reshape(HEAD_W, 1),
                   q_blk=blk["dq"], k_blk=blk["dk"], v_blk=blk["dv"], lam_init=lam_init, t=attn_t)
        gates_t = _gates(small, dn_a_log[l].reshape(HEADS, 1), dn_dt_bias[l].reshape(HEADS, 1))
        yb = _gdn(proj, gates_t, conv_w[l], dn_norm_w[l].reshape(1, HEAD_W),
                  q_blk=blk["nq"], k_blk=blk["nk"], v_blk=blk["nv"], z_blk=blk["nz"])
        x2 = _merge(ya, yb, proj, x2, gate[1], lng[1], lnb[1], w_branch_a[l].astype(BF16),
                    w_branch_b[l].astype(BF16), w_out[l].astype(BF16),
                    ga_blk=blk["ga"], gb_blk=blk["gb"], alpha=alpha)

        x2 = _ffn(x2, shift[2], scale[2], gate[2], lng[2], lnb[2], wfi[1], wfo[1], alpha=alpha)
    return x2[None]
```

```python
import functools
import math

import numpy as np
import jax
import jax.numpy as jnp
from jax import lax
from jax.experimental import pallas as pl
from jax.experimental.pallas import tpu as pltpu

N_SUB = 3
HEADS = 8
HEAD_W = 128
DIFF_QK_DIM = 64
DN_CONV = 4
DN_CHUNK = 64
REL_BUCKETS = 32
REL_MAX_DIST = 128
LN_EPS = 1e-5
RMS_EPS = 1e-6
MASK_VALUE = -1e30
LOG2E = math.log2(math.e)
V_ROWS = HEAD_W + 16

F32 = jnp.float32
BF16 = jnp.bfloat16

V7X_VMEM_BYTES = 64 * 1024 * 1024
VMEM_LIMIT = 56 * 1024 * 1024


def _params(*sem):
    return pltpu.CompilerParams(dimension_semantics=sem, vmem_limit_bytes=VMEM_LIMIT)


def _tile(n, preferred):
    t = min(preferred, n)
    while n % t:
        t -= 128
    return t


def _silu(x):
    return x * jax.nn.sigmoid(x)


def _dot(a, b):
    return jnp.dot(a, b, preferred_element_type=F32)


def _dot_nt(a, b):
    return lax.dot_general(a, b, (((1,), (1,)), ((), ())), preferred_element_type=F32)


def _dot_tn(a, b):
    return lax.dot_general(a, b, (((0,), (0,)), ((), ())), preferred_element_type=F32)


def _layer_norm(r, g, b):
    mu = jnp.mean(r, axis=-1, keepdims=True)
    d = r - mu
    var = jnp.mean(d * d, axis=-1, keepdims=True)
    return d * lax.rsqrt(var + LN_EPS) * g + b


def _ada_kernel(c_ref, w_ref, b_ref, o_ref):
    sc = _silu(c_ref[...])
    o_ref[...] = jnp.sum(w_ref[...] * sc, axis=0, keepdims=True) + b_ref[...]


def _ada(c_col, w, b):
    d, n = w.shape
    tn = _tile(n, 1024)
    return pl.pallas_call(
        _ada_kernel,
        grid=(n // tn,),
        in_specs=[pl.BlockSpec((d, 1), lambda j: (0, 0)),
                  pl.BlockSpec((d, tn), lambda j: (0, j)),
                  pl.BlockSpec((1, tn), lambda j: (0, j))],
        out_specs=pl.BlockSpec((1, tn), lambda j: (0, j)),
        out_shape=jax.ShapeDtypeStruct((1, n), F32),
        compiler_params=_params("arbitrary"),
        name="ada",
    )(c_col, w, b)


def _ffn_kernel(x_ref, shift_ref, scale_ref, gate_ref, lng_ref, lnb_ref, wg_ref, wu_ref, wo_ref,
                o_ref, h_sc, acc_sc, *, alpha, gate_mul):
    j = pl.program_id(1)

    @pl.when(j == 0)
    def _():
        h_sc[...] = (x_ref[...] * (1.0 + scale_ref[...]) + shift_ref[...]).astype(BF16)
        acc_sc[...] = jnp.zeros_like(acc_sc)

    h = h_sc[...]
    g = _dot(h, wg_ref[...])
    u = _dot(h, wu_ref[...])
    a = (_silu(g) * u).astype(BF16)
    acc_sc[...] += _dot(a, wo_ref[...])

    @pl.when(j == pl.num_programs(1) - 1)
    def _():
        r = alpha * x_ref[...] + (gate_mul * gate_ref[...]) * acc_sc[...]
        o_ref[...] = _layer_norm(r, lng_ref[...], lnb_ref[...])


def _ffn(x, shift, scale, gate, lng, lnb, w_in, w_out, *, alpha, tm=512, tf=512):
    s, d = x.shape
    f = w_out.shape[0]
    tm, tf = _tile(s, tm), _tile(f, tf)
    nf = f // tf
    vec = pl.BlockSpec((1, d), lambda i, j: (0, 0))
    return pl.pallas_call(
        functools.partial(_ffn_kernel, alpha=alpha, gate_mul=0.5),
        grid=(s // tm, nf),
        in_specs=[pl.BlockSpec((tm, d), lambda i, j: (i, 0)), vec, vec, vec, vec, vec,
                  pl.BlockSpec((d, tf), lambda i, j: (0, j)),
                  pl.BlockSpec((d, tf), lambda i, j: (0, j + nf)),
                  pl.BlockSpec((tf, d), lambda i, j: (j, 0))],
        out_specs=pl.BlockSpec((tm, d), lambda i, j: (i, 0)),
        out_shape=jax.ShapeDtypeStruct((s, d), F32),
        scratch_shapes=[pltpu.VMEM((tm, d), BF16), pltpu.VMEM((tm, d), F32)],
        compiler_params=_params("parallel", "arbitrary"),
        name="ffn",
    )(x, shift, scale, gate, lng, lnb, w_in, w_in, w_out)


def _proj_kernel(x_ref, shift_ref, scale_ref, w_ref, ws_ref, o_ref, os_ref, h_sc):
    j = pl.program_id(1)

    @pl.when(j == 0)
    def _():
        h = (x_ref[...] * (1.0 + scale_ref[...]) + shift_ref[...]).astype(BF16)
        h_sc[...] = h
        os_ref[...] = _dot(h, ws_ref[...])

    o_ref[...] = _dot(h_sc[...], w_ref[...]).astype(BF16)


def _proj(x, shift, scale, w_main, w_small, *, tm=1024, tn=1024):
    s, d = x.shape
    n = w_main.shape[1]
    ns = w_small.shape[1]
    tm, tn = _tile(s, tm), _tile(n, tn)
    vec = pl.BlockSpec((1, d), lambda i, j: (0, 0))
    return pl.pallas_call(
        _proj_kernel,
        grid=(s // tm, n // tn),
        in_specs=[pl.BlockSpec((tm, d), lambda i, j: (i, 0)), vec, vec,
                  pl.BlockSpec((d, tn), lambda i, j: (0, j)),
                  pl.BlockSpec((d, ns), lambda i, j: (0, 0))],
        out_specs=[pl.BlockSpec((tm, tn), lambda i, j: (i, j)),
                   pl.BlockSpec((tm, ns), lambda i, j: (i, 0))],
        out_shape=[jax.ShapeDtypeStruct((s, n), BF16), jax.ShapeDtypeStruct((s, ns), F32)],
        scratch_shapes=[pltpu.VMEM((tm, d), BF16)],
        compiler_params=_params("parallel", "arbitrary"),
        name="proj",
    )(x, shift, scale, w_main, w_small)


def _bucket_starts():
    n = np.arange(0, 2 * REL_MAX_DIST)
    max_exact = REL_BUCKETS // 2
    nf = np.maximum(n, max_exact).astype(np.float32)
    large = max_exact + (np.log(nf / np.float32(max_exact)) / np.float32(math.log(REL_MAX_DIST / max_exact))
                         * np.float32(REL_BUCKETS - max_exact)).astype(np.int32)
    bucket = np.where(n < max_exact, n, np.minimum(large, REL_BUCKETS - 1))
    assert np.all(np.diff(bucket) >= 0) and bucket[-1] == REL_BUCKETS - 1
    return [int(np.min(n[bucket >= b])) for b in range(REL_BUCKETS)]


BUCKET_STARTS = _bucket_starts()


def _bias_kernel(rb_ref, o_ref, *, t):
    h = pl.program_id(0)
    jj = lax.broadcasted_iota(jnp.int32, (t, 2 * t), 0)
    ii = lax.broadcasted_iota(jnp.int32, (t, 2 * t), 1)
    ii = jnp.where(ii >= t, ii - t, ii)
    far = rb_ref[REL_BUCKETS - 1, h]
    for n in range(2):
        rel = ii - jj + n * t
        bias = jnp.full((t, 2 * t), (rb_ref[0, h] - far) * LOG2E, F32)
        for b in range(1, REL_BUCKETS):
            bias = jnp.where(rel >= BUCKET_STARTS[b], (rb_ref[b, h] - far) * LOG2E, bias)
        o_ref[0, n] = jnp.where(rel < 0, MASK_VALUE, bias)


def _bias_tiles(rel_bias, t):
    return pl.pallas_call(
        functools.partial(_bias_kernel, t=t),
        grid=(HEADS,),
        in_specs=[pl.BlockSpec(memory_space=pltpu.SMEM)],
        out_specs=pl.BlockSpec((1, 2, t, 2 * t), lambda h: (h, 0, 0, 0)),
        out_shape=jax.ShapeDtypeStruct((HEADS, 2, t, 2 * t), F32),
        compiler_params=_params("arbitrary"),
        name="bias_tiles",
    )(rel_bias)


def _attn_kernel(q_ref, k_ref, v_ref, bias_ref, lam_ref, sub_ref, o_ref,
                 vt_sc, qt_sc, s_sc, m_sc, acc_sc, *, t, hp, lam_init):
    qi = pl.program_id(1)
    n_kv = vt_sc.shape[1]
    heads = range(hp)
    cols = [slice(hh * HEAD_W, (hh + 1) * HEAD_W) for hh in heads]

    @pl.when(qi == 0)
    def _():
        ones_tile = (lax.broadcasted_iota(jnp.int32, (V_ROWS - HEAD_W, t), 0) == 0).astype(BF16)

        def body(c, carry):
            r0 = pl.multiple_of(c * t, t)
            for hh in heads:
                vt_sc[hh, c, 0:HEAD_W] = v_ref[pl.ds(r0, t), cols[hh]].astype(F32).T.astype(BF16)
                vt_sc[hh, c, HEAD_W:V_ROWS] = ones_tile
            return carry
        lax.fori_loop(0, n_kv, body, 0)

    row = lax.broadcasted_iota(jnp.int32, (HEAD_W, t), 0)
    for hh in heads:
        qt = (q_ref[:, cols[hh]].astype(F32) * (DIFF_QK_DIM ** -0.5 * LOG2E)).T
        zero = jnp.zeros_like(qt)
        qt_sc[hh] = jnp.concatenate([jnp.where(row < DIFF_QK_DIM, qt, zero),
                                     jnp.where(row >= DIFF_QK_DIM, qt, zero)], axis=1).astype(BF16)

    m_sc[...] = jnp.full_like(m_sc, -jnp.inf)
    acc_sc[...] = jnp.zeros_like(acc_sc)

    def scores(c, slot, biased):
        r0 = pl.multiple_of(c * t, t)
        for hh in heads:
            s = _dot(k_ref[pl.ds(r0, t), cols[hh]], qt_sc[hh])
            if biased:
                s = s + bias_ref[hh, qi - c]
            s_sc[slot, hh] = s

    def consume(c, slot):
        for hh in heads:
            s = s_sc[slot, hh]
            m_old = m_sc[hh]
            m_new = jnp.maximum(m_old, jnp.max(s, axis=0, keepdims=True))
            p = jnp.exp2(s - m_new).astype(BF16)
            acc_sc[hh] = jnp.exp2(m_old - m_new) * acc_sc[hh] + _dot(vt_sc[hh, c], p)
            m_sc[hh] = m_new

    pl.when(qi >= 2)(lambda: scores(0, 0, False))
    pl.when(qi < 2)(lambda: scores(0, 0, True))

    def body(c, carry):
        far_next = c + 1 <= qi - 2
        for slot in range(2):
            mine = (c & 1) == slot

            @pl.when(jnp.logical_and(mine, far_next))
            def _():
                scores(c + 1, 1 - slot, False)
                consume(c, slot)

            @pl.when(jnp.logical_and(mine, jnp.logical_not(far_next)))
            def _():
                scores(c + 1, 1 - slot, True)
                consume(c, slot)
        return carry
    lax.fori_loop(0, qi, body, 0)

    for slot in range(2):
        pl.when((qi & 1) == slot)(functools.partial(consume, qi, slot))

    lp = lam_ref[...]
    lam = (jnp.exp(jnp.sum(lp[0:1] * lp[1:2], axis=1, keepdims=True))
           - jnp.exp(jnp.sum(lp[2:3] * lp[3:4], axis=1, keepdims=True)) + lam_init)
    for hh in heads:
        acc = acc_sc[hh]
        o = acc[0:HEAD_W] / acc[HEAD_W:HEAD_W + 1]
        od = o[:, :t] - lam * o[:, t:]
        ms = jnp.mean(od * od, axis=0, keepdims=True)
        y = od * lax.rsqrt(ms + LN_EPS) * sub_ref[...] * (1.0 - lam_init)
        o_ref[:, cols[hh]] = y.T.astype(BF16)


def _attn(proj, bias_tiles, lam_params, subln_col, *, q_blk, k_blk, v_blk, lam_init, t, hp=4):
    s = proj.shape[0]
    w = hp * HEAD_W
    assert q_blk % hp == 0 and k_blk % hp == 0 and v_blk % hp == 0
    once = pl.Buffered(1)
    return pl.pallas_call(
        functools.partial(_attn_kernel, t=t, hp=hp, lam_init=lam_init),
        grid=(HEADS // hp, s // t),
        in_specs=[pl.BlockSpec((t, w), lambda g, i: (i, q_blk // hp + g)),
                  pl.BlockSpec((s, w), lambda g, i: (0, k_blk // hp + g), pipeline_mode=once),
                  pl.BlockSpec((s, w), lambda g, i: (0, v_blk // hp + g), pipeline_mode=once),
                  pl.BlockSpec((hp, 2, t, 2 * t), lambda g, i: (g, 0, 0, 0), pipeline_mode=once),
                  pl.BlockSpec(lam_params.shape, lambda g, i: (0, 0)),
                  pl.BlockSpec((HEAD_W, 1), lambda g, i: (0, 0))],
        out_specs=pl.BlockSpec((t, w), lambda g, i: (i, g)),
        out_shape=jax.ShapeDtypeStruct((s, HEADS * HEAD_W), BF16),
        scratch_shapes=[pltpu.VMEM((hp, s // t, V_ROWS, t), BF16),
                        pltpu.VMEM((hp, HEAD_W, 2 * t), BF16),
                        pltpu.VMEM((2, hp, t, 2 * t), F32),
                        pltpu.VMEM((hp, 1, 2 * t), F32),
                        pltpu.VMEM((hp, V_ROWS, 2 * t), F32)],
        compiler_params=_params("arbitrary", "arbitrary"),
        name="diff_attn",
    )(proj, proj, proj, bias_tiles, lam_params, subln_col)


def _gates_kernel(sm_ref, alog_ref, dtb_ref, o_ref):
    tr = sm_ref[...].T
    beta = jax.nn.sigmoid(tr[0:HEADS])
    x = tr[HEADS:2 * HEADS] + dtb_ref[...]
    softplus = jnp.maximum(x, 0.0) + jnp.log1p(jnp.exp(-jnp.abs(x)))
    g = -jnp.exp(alog_ref[...]) * softplus
    pos = lax.broadcasted_iota(jnp.int32, g.shape, 1) % DN_CHUNK
    shift = 1
    while shift < DN_CHUNK:
        g = g + jnp.where(pos >= shift, pltpu.roll(g, shift, axis=1), 0.0)
        shift *= 2
    o_ref[0:HEADS] = beta
    o_ref[HEADS:2 * HEADS] = g


def _gates(small, alog_col, dtb_col, *, tb=1024):
    s, ns = small.shape
    col = pl.BlockSpec((HEADS, 1), lambda i: (0, 0))
    return pl.pallas_call(
        _gates_kernel,
        grid=(s // tb,),
        in_specs=[pl.BlockSpec((tb, ns), lambda i: (i, 0)), col, col],
        out_specs=pl.BlockSpec((2 * HEADS, tb), lambda i: (0, i)),
        out_shape=jax.ShapeDtypeStruct((2 * HEADS, s), F32),
        compiler_params=_params("parallel"),
        name="gdn_gates",
    )(small, alog_col, dtb_col)


GROUP = 2 * DN_CHUNK


def _gdn_kernel(q_ref, k_ref, v_ref, z_ref, gt_ref, cwq_ref, cwk_ref, cwv_ref, nw_ref, o_ref,
                pad_sc, state_sc, o_sc, *, tb):
    h = pl.program_id(0)
    ib = pl.program_id(1)
    halo = 8

    @pl.when(ib == 0)
    def _():
        state_sc[...] = jnp.zeros_like(state_sc)
        pad_sc[:, 0:halo, :] = jnp.zeros((3, halo, HEAD_W), F32)

    def conv_silu(a, x_ref, cw_ref):
        pad_sc[a, halo:halo + tb, :] = x_ref[...].astype(F32)
        cw = cw_ref[...]
        y = cw[0:1] * pad_sc[a, halo - 3:halo - 3 + tb, :]
        for j in range(1, DN_CONV):
            y = y + cw[j:j + 1] * pad_sc[a, halo - 3 + j:halo - 3 + j + tb, :]
        pad_sc[a, 0:halo, :] = pad_sc[a, tb:tb + halo, :]
        return _silu(y)

    def l2n(x):
        return x * lax.rsqrt(jnp.sum(x * x, axis=-1, keepdims=True) + RMS_EPS)

    q = l2n(conv_silu(0, q_ref, cwq_ref)) * (HEAD_W ** -0.5)
    k = l2n(conv_silu(1, k_ref, cwk_ref))
    v = conv_silu(2, v_ref, cwv_ref)

    beta_t = jnp.broadcast_to(gt_ref[pl.ds(h, 1), :], (HEAD_W, tb))
    cum_t = jnp.broadcast_to(gt_ref[pl.ds(HEADS + h, 1), :], (HEAD_W, tb))
    beta_c = beta_t.T
    cum_c = cum_t.T

    ii = lax.broadcasted_iota(jnp.int32, (GROUP, GROUP), 0)
    jj = lax.broadcasted_iota(jnp.int32, (GROUP, GROUP), 1)
    same = (ii >= DN_CHUNK) == (jj >= DN_CHUNK)
    tril = jnp.logical_and(same, ii >= jj)
    strict = jnp.logical_and(same, ii > jj)
    eye = (ii == jj).astype(F32)

    for r in range(tb // GROUP):
        rows = slice(r * GROUP, (r + 1) * GROUP)
        qg, kg, vg = q[rows], k[rows], v[rows]
        cg, bg = cum_c[rows], beta_c[rows]
        gdiff = cg - cum_t[:, rows]
        decay = jnp.where(tril, jnp.exp(jnp.where(tril, gdiff, 0.0)), 0.0)
        kb = kg * bg
        kg16 = kg.astype(BF16)
        a = jnp.where(strict, _dot_nt(kb.astype(BF16), kg16) * decay, 0.0)
        x = a.astype(BF16)
        tinv = eye - a
        power = 2
        while power < DN_CHUNK:
            x2 = _dot(x, x)
            x = x2.astype(BF16)
            tinv = tinv + _dot(tinv.astype(BF16), x)
            power *= 2
        eg = jnp.exp(cg)
        rhs = jnp.concatenate([vg * bg, kb * eg], axis=1).astype(BF16)
        uw = _dot(tinv.astype(BF16), rhs)
        u, w = uw[:, :HEAD_W], uw[:, HEAD_W:]
        qk = (_dot_nt(qg.astype(BF16), kg16) * decay).astype(BF16)
        qd = (qg * eg).astype(BF16)

        v_new, o_inter = [], []
        for c in range(2):
            cr = slice(c * DN_CHUNK, (c + 1) * DN_CHUNK)
            g_last = cg[c * DN_CHUNK + DN_CHUNK - 1:c * DN_CHUNK + DN_CHUNK, :]
            kd = (kg[cr] * jnp.exp(g_last - cg[cr])).astype(BF16)
            st = state_sc[...]
            st16 = st.astype(BF16)
            vn = u[cr] - _dot(w[cr].astype(BF16), st16)
            o_inter.append(_dot(qd[cr], st16))
            state_sc[...] = st * jnp.exp(g_last) + _dot_tn(kd, vn.astype(BF16))
            v_new.append(vn)
        vn_all = jnp.concatenate(v_new, axis=0).astype(BF16)
        o_sc[rows, :] = jnp.concatenate(o_inter, axis=0) + _dot(qk, vn_all)

    o = o_sc[...]
    on = o * lax.rsqrt(jnp.mean(o * o, axis=-1, keepdims=True) + RMS_EPS) * nw_ref[...]
    o_ref[...] = (on * _silu(z_ref[...].astype(F32))).astype(BF16)


def _gdn(proj, gates_t, conv_w, norm_w, *, q_blk, k_blk, v_blk, z_blk, tb=256):
    s = proj.shape[0]
    col = lambda blk: pl.BlockSpec((tb, HEAD_W), lambda h, i: (i, blk + h))
    cw = lambda blk: pl.BlockSpec((DN_CONV, HEAD_W), lambda h, i: (0, blk + h))
    return pl.pallas_call(
        functools.partial(_gdn_kernel, tb=tb),
        grid=(HEADS, s // tb),
        in_specs=[col(q_blk), col(k_blk), col(v_blk), col(z_blk),
                  pl.BlockSpec((2 * HEADS, tb), lambda h, i: (0, i)),
                  cw(0), cw(HEADS), cw(2 * HEADS),
                  pl.BlockSpec((1, HEAD_W), lambda h, i: (0, 0))],
        out_specs=pl.BlockSpec((tb, HEAD_W), lambda h, i: (i, h)),
        out_shape=jax.ShapeDtypeStruct((s, HEADS * HEAD_W), BF16),
        scratch_shapes=[pltpu.VMEM((3, tb + 8, HEAD_W), F32),
                        pltpu.VMEM((HEAD_W, HEAD_W), F32),
                        pltpu.VMEM((tb, HEAD_W), F32)],
        compiler_params=_params("arbitrary", "arbitrary"),
        name="gdn",
    )(proj, proj, proj, proj, gates_t, conv_w, conv_w, conv_w, norm_w)


def _merge_kernel(ya_ref, yb_ref, ga_ref, gb_ref, x_ref, gate_ref, lng_ref, lnb_ref,
                  wa_ref, wb_ref, wo_ref, o_ref, *, alpha):
    a = _dot(ya_ref[...], wa_ref[...])
    b = _dot(yb_ref[...], wb_ref[...])
    merged = (jax.nn.sigmoid(ga_ref[...].astype(F32)) * a
              + jax.nn.sigmoid(gb_ref[...].astype(F32)) * b).astype(BF16)
    y = _dot(merged, wo_ref[...])
    r = alpha * x_ref[...] + gate_ref[...] * y
    o_ref[...] = _layer_norm(r, lng_ref[...], lnb_ref[...])


def _merge(ya, yb, proj, x, gate, lng, lnb, w_a, w_b, w_o, *, ga_blk, gb_blk, alpha, tm=256):
    s, d = x.shape
    dv = ya.shape[1]
    vec = pl.BlockSpec((1, d), lambda i: (0, 0))
    const = lambda shape: pl.BlockSpec(shape, lambda i: (0, 0), pipeline_mode=pl.Buffered(1))
    return pl.pallas_call(
        functools.partial(_merge_kernel, alpha=alpha),
        grid=(s // tm,),
        in_specs=[pl.BlockSpec((tm, dv), lambda i: (i, 0)),
                  pl.BlockSpec((tm, dv), lambda i: (i, 0)),
                  pl.BlockSpec((tm, d), lambda i: (i, ga_blk)),
                  pl.BlockSpec((tm, d), lambda i: (i, gb_blk)),
                  pl.BlockSpec((tm, d), lambda i: (i, 0)), vec, vec, vec,
                  const(w_a.shape), const(w_b.shape), const(w_o.shape)],
        out_specs=pl.BlockSpec((tm, d), lambda i: (i, 0)),
        out_shape=jax.ShapeDtypeStruct((s, d), F32),
        compiler_params=_params("parallel"),
        name="merge",
    )(ya, yb, proj, proj, x, gate, lng, lnb, w_a, w_b, w_o)


def kernel(x, c, w_ada, b_ada, ln_g, ln_b, w_ffn_in, w_ffn_out, w_in, conv_w, dn_a_log, dn_dt_bias,
           dn_norm_w, diff_lambda, diff_subln_w, rel_bias, w_branch_a, w_branch_b, w_out):
    bsz, s, d = x.shape
    assert bsz == 1, "one sequence per call"
    depth = w_ada.shape[0]
    alpha = (2 * depth) ** 0.25
    hw = HEADS * HEAD_W
    attn_t = 256
    assert d % HEAD_W == 0 and s % 1024 == 0

    o_nb = 7 * hw
    o_ga = o_nb + 2 * HEADS
    nblk = d // HEAD_W
    blk = {"ga": 0, "gb": 1, "dq": 2 * nblk, "dk": 2 * nblk + HEADS, "dv": 2 * nblk + 2 * HEADS,
           "nq": 2 * nblk + 3 * HEADS, "nk": 2 * nblk + 4 * HEADS, "nv": 2 * nblk + 5 * HEADS,
           "nz": 2 * nblk + 6 * HEADS}

    bias_tiles = _bias_tiles(rel_bias, attn_t)
    x2 = x[0]
    for l in range(depth):
        lam_init = 0.8 - 0.6 * math.exp(-0.3 * l)
        ada = _ada(c.reshape(d, 1), w_ada[l], b_ada[l].reshape(1, -1)).reshape(N_SUB, 3, 1, d)
        shift, scale, gate = ada[:, 0], ada[:, 1], ada[:, 2]
        lng, lnb = ln_g[l].reshape(N_SUB, 1, d), ln_b[l].reshape(N_SUB, 1, d)
        wfi, wfo = w_ffn_in[l].astype(BF16), w_ffn_out[l].astype(BF16)
        wi = w_in[l]
        w_main = jnp.concatenate([wi[:, o_ga:], wi[:, :o_nb]], axis=1).astype(BF16)
        w_small = jnp.pad(wi[:, o_nb:o_ga], ((0, 0), (0, HEAD_W - 2 * HEADS))).astype(BF16)

        x2 = _ffn(x2, shift[0], scale[0], gate[0], lng[0], lnb[0], wfi[0], wfo[0], alpha=alpha)

        proj, small = _proj(x2, shift[1], scale[1], w_main, w_small)
        ya = _attn(proj, bias_tiles, diff_lambda[l], diff_subln_w[l].reshape(HEAD_W, 1),
                   q_blk=blk["dq"], k_blk=blk["dk"], v_blk=blk["dv"], lam_init=lam_init, t=attn_t)
        gates_t = _gates(small, dn_a_log[l].reshape(HEADS, 1), dn_dt_bias[l].reshape(HEADS, 1))
        yb = _gdn(proj, gates_t, conv_w[l], dn_norm_w[l].reshape(1, HEAD_W),
                  q_blk=blk["nq"], k_blk=blk["nk"], v_blk=blk["nv"], z_blk=blk["nz"])
        x2 = _merge(ya, yb, proj, x2, gate[1], lng[1], lnb[1], w_branch_a[l].astype(BF16),
                    w_branch_b[l].astype(BF16), w_out[l].astype(BF16),
                    ga_blk=blk["ga"], gb_blk=blk["gb"], alpha=alpha)

        x2 = _ffn(x2, shift[2], scale[2], gate[2], lng[2], lnb[2], wfi[1], wfo[1], alpha=alpha)
    return x2[None]
```

```python
import functools
import math

import numpy as np
import jax
import jax.numpy as jnp
from jax import lax
from jax.experimental import pallas as pl
from jax.experimental.pallas import tpu as pltpu

N_SUB = 3
HEADS = 8
HEAD_W = 128
DIFF_QK_DIM = 64
DN_CONV = 4
DN_CHUNK = 64
REL_BUCKETS = 32
REL_MAX_DIST = 128
LN_EPS = 1e-5
RMS_EPS = 1e-6
MASK_VALUE = -1e30
LOG2E = math.log2(math.e)
V_ROWS = HEAD_W + 16

F32 = jnp.float32
BF16 = jnp.bfloat16

V7X_VMEM_BYTES = 64 * 1024 * 1024
VMEM_LIMIT = 56 * 1024 * 1024


def _params(*sem):
    return pltpu.CompilerParams(dimension_semantics=sem, vmem_limit_bytes=VMEM_LIMIT)


def _tile(n, preferred):
    t = min(preferred, n)
    while n % t:
        t -= 128
    return t


def _silu(x):
    return x * jax.nn.sigmoid(x)


def _dot(a, b):
    return jnp.dot(a, b, preferred_element_type=F32)


def _dot_nt(a, b):
    return lax.dot_general(a, b, (((1,), (1,)), ((), ())), preferred_element_type=F32)


def _dot_tn(a, b):
    return lax.dot_general(a, b, (((0,), (0,)), ((), ())), preferred_element_type=F32)


def _layer_norm(r, g, b):
    mu = jnp.mean(r, axis=-1, keepdims=True)
    d = r - mu
    var = jnp.mean(d * d, axis=-1, keepdims=True)
    return d * lax.rsqrt(var + LN_EPS) * g + b


def _ada_kernel(c_ref, w_ref, b_ref, o_ref):
    sc = _silu(c_ref[...])
    o_ref[...] = jnp.sum(w_ref[...] * sc, axis=0, keepdims=True) + b_ref[...]


def _ada(c_col, w, b):
    d, n = w.shape
    tn = _tile(n, 1024)
    return pl.pallas_call(
        _ada_kernel,
        grid=(n // tn,),
        in_specs=[pl.BlockSpec((d, 1), lambda j: (0, 0)),
                  pl.BlockSpec((d, tn), lambda j: (0, j)),
                  pl.BlockSpec((1, tn), lambda j: (0, j))],
        out_specs=pl.BlockSpec((1, tn), lambda j: (0, j)),
        out_shape=jax.ShapeDtypeStruct((1, n), F32),
        compiler_params=_params("arbitrary"),
        name="ada",
    )(c_col, w, b)


def _ffn_kernel(x_ref, shift_ref, scale_ref, gate_ref, lng_ref, lnb_ref, wg_ref, wu_ref, wo_ref,
                o_ref, h_sc, acc_sc, *, alpha, gate_mul):
    j = pl.program_id(1)

    @pl.when(j == 0)
    def _():
        h_sc[...] = (x_ref[...] * (1.0 + scale_ref[...]) + shift_ref[...]).astype(BF16)
        acc_sc[...] = jnp.zeros_like(acc_sc)

    h = h_sc[...]
    g = _dot(h, wg_ref[...])
    u = _dot(h, wu_ref[...])
    a = (_silu(g) * u).astype(BF16)
    acc_sc[...] += _dot(a, wo_ref[...])

    @pl.when(j == pl.num_programs(1) - 1)
    def _():
        r = alpha * x_ref[...] + (gate_mul * gate_ref[...]) * acc_sc[...]
        o_ref[...] = _layer_norm(r, lng_ref[...], lnb_ref[...])


def _ffn(x, shift, scale, gate, lng, lnb, w_in, w_out, *, alpha, tm=512, tf=512):
    s, d = x.shape
    f = w_out.shape[0]
    tm, tf = _tile(s, tm), _tile(f, tf)
    nf = f // tf
    vec = pl.BlockSpec((1, d), lambda i, j: (0, 0))
    return pl.pallas_call(
        functools.partial(_ffn_kernel, alpha=alpha, gate_mul=0.5),
        grid=(s // tm, nf),
        in_specs=[pl.BlockSpec((tm, d), lambda i, j: (i, 0)), vec, vec, vec, vec, vec,
                  pl.BlockSpec((d, tf), lambda i, j: (0, j)),
                  pl.BlockSpec((d, tf), lambda i, j: (0, j + nf)),
                  pl.BlockSpec((tf, d), lambda i, j: (j, 0))],
        out_specs=pl.BlockSpec((tm, d), lambda i, j: (i, 0)),
        out_shape=jax.ShapeDtypeStruct((s, d), F32),
        scratch_shapes=[pltpu.VMEM((tm, d), BF16), pltpu.VMEM((tm, d), F32)],
        compiler_params=_params("parallel", "arbitrary"),
        name="ffn",
    )(x, shift, scale, gate, lng, lnb, w_in, w_in, w_out)


def _proj_kernel(x_ref, shift_ref, scale_ref, w_ref, ws_ref, o_ref, os_ref, h_sc):
    j = pl.program_id(1)

    @pl.when(j == 0)
    def _():
        h = (x_ref[...] * (1.0 + scale_ref[...]) + shift_ref[...]).astype(BF16)
        h_sc[...] = h
        os_ref[...] = _dot(h, ws_ref[...])

    o_ref[...] = _dot(h_sc[...], w_ref[...]).astype(BF16)


def _proj(x, shift, scale, w_main, w_small, *, tm=1024, tn=1024):
    s, d = x.shape
    n = w_main.shape[1]
    ns = w_small.shape[1]
    tm, tn = _tile(s, tm), _tile(n, tn)
    vec = pl.BlockSpec((1, d), lambda i, j: (0, 0))
    return pl.pallas_call(
        _proj_kernel,
        grid=(s // tm, n // tn),
        in_specs=[pl.BlockSpec((tm, d), lambda i, j: (i, 0)), vec, vec,
                  pl.BlockSpec((d, tn), lambda i, j: (0, j)),
                  pl.BlockSpec((d, ns), lambda i, j: (0, 0))],
        out_specs=[pl.BlockSpec((tm, tn), lambda i, j: (i, j)),
                   pl.BlockSpec((tm, ns), lambda i, j: (i, 0))],
        out_shape=[jax.ShapeDtypeStruct((s, n), BF16), jax.ShapeDtypeStruct((s, ns), F32)],
        scratch_shapes=[pltpu.VMEM((tm, d), BF16)],
        compiler_params=_params("parallel", "arbitrary"),
        name="proj",
    )(x, shift, scale, w_main, w_small)


def _bucket_starts():
    n = np.arange(0, 2 * REL_MAX_DIST)
    max_exact = REL_BUCKETS // 2
    nf = np.maximum(n, max_exact).astype(np.float32)
    large = max_exact + (np.log(nf / np.float32(max_exact)) / np.float32(math.log(REL_MAX_DIST / max_exact))
                         * np.float32(REL_BUCKETS - max_exact)).astype(np.int32)
    bucket = np.where(n < max_exact, n, np.minimum(large, REL_BUCKETS - 1))
    assert np.all(np.diff(bucket) >= 0) and bucket[-1] == REL_BUCKETS - 1
    return [int(np.min(n[bucket >= b])) for b in range(REL_BUCKETS)]


BUCKET_STARTS = _bucket_starts()


def _bias_kernel(rb_ref, o_ref, *, t):
    h = pl.program_id(0)
    jj = lax.broadcasted_iota(jnp.int32, (t, 2 * t), 0)
    ii = lax.broadcasted_iota(jnp.int32, (t, 2 * t), 1)
    ii = jnp.where(ii >= t, ii - t, ii)
    far = rb_ref[REL_BUCKETS - 1, h]
    for n in range(2):
        rel = ii - jj + n * t
        bias = jnp.full((t, 2 * t), (rb_ref[0, h] - far) * LOG2E, F32)
        for b in range(1, REL_BUCKETS):
            bias = jnp.where(rel >= BUCKET_STARTS[b], (rb_ref[b, h] - far) * LOG2E, bias)
        o_ref[0, n] = jnp.where(rel < 0, MASK_VALUE, bias)


def _bias_tiles(rel_bias, t):
    return pl.pallas_call(
        functools.partial(_bias_kernel, t=t),
        grid=(HEADS,),
        in_specs=[pl.BlockSpec(memory_space=pltpu.SMEM)],
        out_specs=pl.BlockSpec((1, 2, t, 2 * t), lambda h: (h, 0, 0, 0)),
        out_shape=jax.ShapeDtypeStruct((HEADS, 2, t, 2 * t), F32),
        compiler_params=_params("arbitrary"),
        name="bias_tiles",
    )(rel_bias)


def _attn_kernel(q_ref, k_ref, v_ref, bias_ref, lam_ref, sub_ref, o_ref,
                 vt_sc, qt_sc, s_sc, m_sc, acc_sc, *, t, hp, lam_init):
    qi = pl.program_id(1)
    n_kv = vt_sc.shape[1]
    heads = range(hp)
    cols = [slice(hh * HEAD_W, (hh + 1) * HEAD_W) for hh in heads]

    @pl.when(qi == 0)
    def _():
        ones_tile = (lax.broadcasted_iota(jnp.int32, (V_ROWS - HEAD_W, t), 0) == 0).astype(BF16)

        def body(c, carry):
            r0 = pl.multiple_of(c * t, t)
            for hh in heads:
                vt_sc[hh, c, 0:HEAD_W] = v_ref[pl.ds(r0, t), cols[hh]].astype(F32).T.astype(BF16)
                vt_sc[hh, c, HEAD_W:V_ROWS] = ones_tile
            return carry
        lax.fori_loop(0, n_kv, body, 0)

    row = lax.broadcasted_iota(jnp.int32, (HEAD_W, t), 0)
    for hh in heads:
        qt = (q_ref[:, cols[hh]].astype(F32) * (DIFF_QK_DIM ** -0.5 * LOG2E)).T
        zero = jnp.zeros_like(qt)
        qt_sc[hh] = jnp.concatenate([jnp.where(row < DIFF_QK_DIM, qt, zero),
                                     jnp.where(row >= DIFF_QK_DIM, qt, zero)], axis=1).astype(BF16)

    m_sc[...] = jnp.full_like(m_sc, -jnp.inf)
    acc_sc[...] = jnp.zeros_like(acc_sc)

    def scores(c, slot, biased):
        r0 = pl.multiple_of(c * t, t)
        for hh in heads:
            s = _dot(k_ref[pl.ds(r0, t), cols[hh]], qt_sc[hh])
            if biased:
                s = s + bias_ref[hh, qi - c]
            s_sc[slot, hh] = s

    def consume(c, slot):
        for hh in heads:
            s = s_sc[slot, hh]
            m_old = m_sc[hh]
            m_new = jnp.maximum(m_old, jnp.max(s, axis=0, keepdims=True))
            p = jnp.exp2(s - m_new).astype(BF16)
            acc_sc[hh] = jnp.exp2(m_old - m_new) * acc_sc[hh] + _dot(vt_sc[hh, c], p)
            m_sc[hh] = m_new

    pl.when(qi >= 2)(lambda: scores(0, 0, False))
    pl.when(qi < 2)(lambda: scores(0, 0, True))

    def body(c, carry):
        far_next = c + 1 <= qi - 2
        for slot in range(2):
            mine = (c & 1) == slot

            @pl.when(jnp.logical_and(mine, far_next))
            def _():
                scores(c + 1, 1 - slot, False)
                consume(c, slot)

            @pl.when(jnp.logical_and(mine, jnp.logical_not(far_next)))
            def _():
                scores(c + 1, 1 - slot, True)
                consume(c, slot)
        return carry
    lax.fori_loop(0, qi, body, 0)

    for slot in range(2):
        pl.when((qi & 1) == slot)(functools.partial(consume, qi, slot))

    lp = lam_ref[...]
    lam = (jnp.exp(jnp.sum(lp[0:1] * lp[1:2], axis=1, keepdims=True))
           - jnp.exp(jnp.sum(lp[2:3] * lp[3:4], axis=1, keepdims=True)) + lam_init)
    for hh in heads:
        acc = acc_sc[hh]
        o = acc[0:HEAD_W] / acc[HEAD_W:HEAD_W + 1]
        od = o[:, :t] - lam * o[:, t:]
        ms = jnp.mean(od * od, axis=0, keepdims=True)
        y = od * lax.rsqrt(ms + LN_EPS) * sub_ref[...] * (1.0 - lam_init)
        o_ref[:, cols[hh]] = y.T.astype(BF16)


def _attn(proj, bias_tiles, lam_params, subln_col, *, q_blk, k_blk, v_blk, lam_init, t, hp=4):
    s = proj.shape[0]
    w = hp * HEAD_W
    assert q_blk % hp == 0 and k_blk % hp == 0 and v_blk % hp == 0
    once = pl.Buffered(1)
    return pl.pallas_call(
        functools.partial(_attn_kernel, t=t, hp=hp, lam_init=lam_init),
        grid=(HEADS // hp, s // t),
        in_specs=[pl.BlockSpec((t, w), lambda g, i: (i, q_blk // hp + g)),
                  pl.BlockSpec((s, w), lambda g, i: (0, k_blk // hp + g), pipeline_mode=once),
                  pl.BlockSpec((s, w), lambda g, i: (0, v_blk // hp + g), pipeline_mode=once),
                  pl.BlockSpec((hp, 2, t, 2 * t), lambda g, i: (g, 0, 0, 0), pipeline_mode=once),
                  pl.BlockSpec(lam_params.shape, lambda g, i: (0, 0)),
                  pl.BlockSpec((HEAD_W, 1), lambda g, i: (0, 0))],
        out_specs=pl.BlockSpec((t, w), lambda g, i: (i, g)),
        out_shape=jax.ShapeDtypeStruct((s, HEADS * HEAD_W), BF16),
        scratch_shapes=[pltpu.VMEM((hp, s // t, V_ROWS, t), BF16),
                        pltpu.VMEM((hp, HEAD_W, 2 * t), BF16),
                        pltpu.VMEM((2, hp, t, 2 * t), F32),
                        pltpu.VMEM((hp, 1, 2 * t), F32),
                        pltpu.VMEM((hp, V_ROWS, 2 * t), F32)],
        compiler_params=_params("arbitrary", "arbitrary"),
        name="diff_attn",
    )(proj, proj, proj, bias_tiles, lam_params, subln_col)


def _gates_kernel(sm_ref, alog_ref, dtb_ref, o_ref):
    tr = sm_ref[...].T
    beta = jax.nn.sigmoid(tr[0:HEADS])
    x = tr[HEADS:2 * HEADS] + dtb_ref[...]
    softplus = jnp.maximum(x, 0.0) + jnp.log1p(jnp.exp(-jnp.abs(x)))
    g = -jnp.exp(alog_ref[...]) * softplus
    pos = lax.broadcasted_iota(jnp.int32, g.shape, 1) % DN_CHUNK
    shift = 1
    while shift < DN_CHUNK:
        g = g + jnp.where(pos >= shift, pltpu.roll(g, shift, axis=1), 0.0)
        shift *= 2
    o_ref[0:HEADS] = beta
    o_ref[HEADS:2 * HEADS] = g


def _gates(small, alog_col, dtb_col, *, tb=1024):
    s, ns = small.shape
    col = pl.BlockSpec((HEADS, 1), lambda i: (0, 0))
    return pl.pallas_call(
        _gates_kernel,
        grid=(s // tb,),
        in_specs=[pl.BlockSpec((tb, ns), lambda i: (i, 0)), col, col],
        out_specs=pl.BlockSpec((2 * HEADS, tb), lambda i: (0, i)),
        out_shape=jax.ShapeDtypeStruct((2 * HEADS, s), F32),
        compiler_params=_params("parallel"),
        name="gdn_gates",
    )(small, alog_col, dtb_col)


GROUP = 2 * DN_CHUNK


def _gdn_kernel(q_ref, k_ref, v_ref, z_ref, gt_ref, cwq_ref, cwk_ref, cwv_ref, nw_ref, o_ref,
                pad_sc, state_sc, o_sc, *, tb, hp):
    g = pl.program_id(0)
    ib = pl.program_id(1)
    halo = 8
    heads = range(hp)
    cols = [slice(hh * HEAD_W, (hh + 1) * HEAD_W) for hh in heads]

    @pl.when(ib == 0)
    def _():
        state_sc[...] = jnp.zeros_like(state_sc)
        pad_sc[:, 0:halo, :] = jnp.zeros((3 * hp, halo, HEAD_W), F32)

    def conv_silu(a, x_ref, cw_ref, hh):
        a = a * hp + hh
        pad_sc[a, halo:halo + tb, :] = x_ref[:, cols[hh]].astype(F32)
        cw = cw_ref[:, cols[hh]]
        y = cw[0:1] * pad_sc[a, halo - 3:halo - 3 + tb, :]
        for j in range(1, DN_CONV):
            y = y + cw[j:j + 1] * pad_sc[a, halo - 3 + j:halo - 3 + j + tb, :]
        pad_sc[a, 0:halo, :] = pad_sc[a, tb:tb + halo, :]
        return _silu(y)

    def l2n(x):
        return x * lax.rsqrt(jnp.sum(x * x, axis=-1, keepdims=True) + RMS_EPS)

    ii = lax.broadcasted_iota(jnp.int32, (GROUP, GROUP), 0)
    jj = lax.broadcasted_iota(jnp.int32, (GROUP, GROUP), 1)
    same = (ii >= DN_CHUNK) == (jj >= DN_CHUNK)
    tril = jnp.logical_and(same, ii >= jj)
    strict = jnp.logical_and(same, ii > jj)
    eye = (ii == jj).astype(F32)

    q, k, v, beta_c, cum_c, cum_t = [], [], [], [], [], []
    for hh in heads:
        q.append(l2n(conv_silu(0, q_ref, cwq_ref, hh)) * (HEAD_W ** -0.5))
        k.append(l2n(conv_silu(1, k_ref, cwk_ref, hh)))
        v.append(conv_silu(2, v_ref, cwv_ref, hh))
        head = g * hp + hh
        beta_t = jnp.broadcast_to(gt_ref[pl.ds(head, 1), :], (HEAD_W, tb))
        ct = jnp.broadcast_to(gt_ref[pl.ds(HEADS + head, 1), :], (HEAD_W, tb))
        beta_c.append(beta_t.T)
        cum_c.append(ct.T)
        cum_t.append(ct)

    groups = range(tb // GROUP)
    rows = [slice(r * GROUP, (r + 1) * GROUP) for r in groups]
    u, w, qk, qd, kg, cg = {}, {}, {}, {}, {}, {}

    def intra_chunk(r):
        kb, x, tinv, decay = {}, {}, {}, {}
        for hh in heads:
            sid = (r, hh)
            kg[sid], cg[sid] = k[hh][rows[r]], cum_c[hh][rows[r]]
            gdiff = cg[sid] - cum_t[hh][:, rows[r]]
            decay[hh] = jnp.where(tril, jnp.exp(jnp.where(tril, gdiff, 0.0)), 0.0)
            kb[hh] = kg[sid] * beta_c[hh][rows[r]]
            a = jnp.where(strict, _dot_nt(kb[hh].astype(BF16), kg[sid].astype(BF16)) * decay[hh], 0.0)
            x[hh] = a.astype(BF16)
            tinv[hh] = eye - a
        yield
        power = 2
        while power < DN_CHUNK:
            for hh in heads:
                x[hh] = _dot(x[hh], x[hh]).astype(BF16)
            yield
            for hh in heads:
                tinv[hh] = tinv[hh] + _dot(tinv[hh].astype(BF16), x[hh])
            yield
            power *= 2
        for hh in heads:
            sid = (r, hh)
            eg = jnp.exp(cg[sid])
            rhs = jnp.concatenate([v[hh][rows[r]] * beta_c[hh][rows[r]], kb[hh] * eg], axis=1).astype(BF16)
            uw = _dot(tinv[hh].astype(BF16), rhs)
            u[sid], w[sid] = uw[:, :HEAD_W], uw[:, HEAD_W:].astype(BF16)
            qg = q[hh][rows[r]]
            qk[sid] = (_dot_nt(qg.astype(BF16), kg[sid].astype(BF16)) * decay[hh]).astype(BF16)
            qd[sid] = (qg * eg).astype(BF16)
        yield

    def recurrence(r):
        v_new = {hh: [] for hh in heads}
        o_inter = {hh: [] for hh in heads}
        for c in range(2):
            cr = slice(c * DN_CHUNK, (c + 1) * DN_CHUNK)
            last = c * DN_CHUNK + DN_CHUNK - 1
            st, g_last = {}, {}
            for hh in heads:
                sid = (r, hh)
                g_last[hh] = cg[sid][last:last + 1, :]
                st[hh] = state_sc[hh]
                st16 = st[hh].astype(BF16)
                v_new[hh].append(u[sid][cr] - _dot(w[sid][cr], st16))
                o_inter[hh].append(_dot(qd[sid][cr], st16))
            yield
            for hh in heads:
                sid = (r, hh)
                kd = (kg[sid][cr] * jnp.exp(g_last[hh] - cg[sid][cr])).astype(BF16)
                state_sc[hh] = st[hh] * jnp.exp(g_last[hh]) + _dot_tn(kd, v_new[hh][c].astype(BF16))
            yield
        for hh in heads:
            vn_all = jnp.concatenate(v_new[hh], axis=0).astype(BF16)
            o_sc[hh, rows[r], :] = jnp.concatenate(o_inter[hh], axis=0) + _dot(qk[(r, hh)], vn_all)
        yield

    def interleave(slow, fast, ratio):
        slow_live = fast_live = True
        while slow_live or fast_live:
            if slow_live:
                slow_live = next(slow, "done") != "done"
            for _ in range(ratio):
                if fast_live:
                    fast_live = next(fast, "done") != "done"

    for _ in intra_chunk(0):
        pass
    for r in groups:
        nxt = intra_chunk(r + 1) if r + 1 < len(groups) else iter(())
        interleave(recurrence(r), nxt, 3)

    for hh in heads:
        o = o_sc[hh]
        on = o * lax.rsqrt(jnp.mean(o * o, axis=-1, keepdims=True) + RMS_EPS) * nw_ref[...]
        o_ref[:, cols[hh]] = (on * _silu(z_ref[:, cols[hh]].astype(F32))).astype(BF16)


def _gdn(proj, gates_t, conv_w, norm_w, *, q_blk, k_blk, v_blk, z_blk, tb=256, hp=8):
    s = proj.shape[0]
    w = hp * HEAD_W
    assert all(blk % hp == 0 for blk in (q_blk, k_blk, v_blk, z_blk))
    col = lambda blk: pl.BlockSpec((tb, w), lambda g, i: (i, blk // hp + g))
    cw = lambda blk: pl.BlockSpec((DN_CONV, w), lambda g, i: (0, blk // hp + g))
    return pl.pallas_call(
        functools.partial(_gdn_kernel, tb=tb, hp=hp),
        grid=(HEADS // hp, s // tb),
        in_specs=[col(q_blk), col(k_blk), col(v_blk), col(z_blk),
                  pl.BlockSpec((2 * HEADS, tb), lambda g, i: (0, i)),
                  cw(0), cw(HEADS), cw(2 * HEADS),
                  pl.BlockSpec((1, HEAD_W), lambda g, i: (0, 0))],
        out_specs=pl.BlockSpec((tb, w), lambda g, i: (i, g)),
        out_shape=jax.ShapeDtypeStruct((s, HEADS * HEAD_W), BF16),
        scratch_shapes=[pltpu.VMEM((3 * hp, tb + 8, HEAD_W), F32),
                        pltpu.VMEM((hp, HEAD_W, HEAD_W), F32),
                        pltpu.VMEM((hp, tb, HEAD_W), F32)],
        compiler_params=_params("arbitrary", "arbitrary"),
        name="gdn",
    )(proj, proj, proj, proj, gates_t, conv_w, conv_w, conv_w, norm_w)


def _merge_kernel(ya_ref, yb_ref, ga_ref, gb_ref, x_ref, gate_ref, lng_ref, lnb_ref,
                  wa_ref, wb_ref, wo_ref, o_ref, *, alpha):
    a = _dot(ya_ref[...], wa_ref[...])
    b = _dot(yb_ref[...], wb_ref[...])
    merged = (jax.nn.sigmoid(ga_ref[...].astype(F32)) * a
              + jax.nn.sigmoid(gb_ref[...].astype(F32)) * b).astype(BF16)
    y = _dot(merged, wo_ref[...])
    r = alpha * x_ref[...] + gate_ref[...] * y
    o_ref[...] = _layer_norm(r, lng_ref[...], lnb_ref[...])


def _merge(ya, yb, proj, x, gate, lng, lnb, w_a, w_b, w_o, *, ga_blk, gb_blk, alpha, tm=256):
    s, d = x.shape
    dv = ya.shape[1]
    vec = pl.BlockSpec((1, d), lambda i: (0, 0))
    const = lambda shape: pl.BlockSpec(shape, lambda i: (0, 0), pipeline_mode=pl.Buffered(1))
    return pl.pallas_call(
        functools.partial(_merge_kernel, alpha=alpha),
        grid=(s // tm,),
        in_specs=[pl.BlockSpec((tm, dv), lambda i: (i, 0)),
                  pl.BlockSpec((tm, dv), lambda i: (i, 0)),
                  pl.BlockSpec((tm, d), lambda i: (i, ga_blk)),
                  pl.BlockSpec((tm, d), lambda i: (i, gb_blk)),
                  pl.BlockSpec((tm, d), lambda i: (i, 0)), vec, vec, vec,
                  const(w_a.shape), const(w_b.shape), const(w_o.shape)],
        out_specs=pl.BlockSpec((tm, d), lambda i: (i, 0)),
        out_shape=jax.ShapeDtypeStruct((s, d), F32),
        compiler_params=_params("parallel"),
        name="merge",
    )(ya, yb, proj, proj, x, gate, lng, lnb, w_a, w_b, w_o)


def kernel(x, c, w_ada, b_ada, ln_g, ln_b, w_ffn_in, w_ffn_out, w_in, conv_w, dn_a_log, dn_dt_bias,
           dn_norm_w, diff_lambda, diff_subln_w, rel_bias, w_branch_a, w_branch_b, w_out):
    bsz, s, d = x.shape
    assert bsz == 1, "one sequence per call"
    depth = w_ada.shape[0]
    alpha = (2 * depth) ** 0.25
    hw = HEADS * HEAD_W
    attn_t = 256
    assert d % HEAD_W == 0 and s % 1024 == 0

    o_nb = 7 * hw
    o_ga = o_nb + 2 * HEADS
    nblk = d // HEAD_W
    blk = {"ga": 0, "gb": 1, "dq": 2 * nblk, "dk": 2 * nblk + HEADS, "dv": 2 * nblk + 2 * HEADS,
           "nq": 2 * nblk + 3 * HEADS, "nk": 2 * nblk + 4 * HEADS, "nv": 2 * nblk + 5 * HEADS,
           "nz": 2 * nblk + 6 * HEADS}

    bias_tiles = _bias_tiles(rel_bias, attn_t)
    x2 = x[0]
    for l in range(depth):
        lam_init = 0.8 - 0.6 * math.exp(-0.3 * l)
        ada = _ada(c.reshape(d, 1), w_ada[l], b_ada[l].reshape(1, -1)).reshape(N_SUB, 3, 1, d)
        shift, scale, gate = ada[:, 0], ada[:, 1], ada[:, 2]
        lng, lnb = ln_g[l].reshape(N_SUB, 1, d), ln_b[l].reshape(N_SUB, 1, d)
        wfi, wfo = w_ffn_in[l].astype(BF16), w_ffn_out[l].astype(BF16)
        wi = w_in[l]
        w_main = jnp.concatenate([wi[:, o_ga:], wi[:, :o_nb]], axis=1).astype(BF16)
        w_small = jnp.pad(wi[:, o_nb:o_ga], ((0, 0), (0, HEAD_W - 2 * HEADS))).astype(BF16)

        x2 = _ffn(x2, shift[0], scale[0], gate[0], lng[0], lnb[0], wfi[0], wfo[0], alpha=alpha)

        proj, small = _proj(x2, shift[1], scale[1], w_main, w_small)
        ya = _attn(proj, bias_tiles, diff_lambda[l], diff_subln_w[l].reshape(HEAD_W, 1),
                   q_blk=blk["dq"], k_blk=blk["dk"], v_blk=blk["dv"], lam_init=lam_init, t=attn_t)
        gates_t = _gates(small, dn_a_log[l].reshape(HEADS, 1), dn_dt_bias[l].reshape(HEADS, 1))
        yb = _gdn(proj, gates_t, conv_w[l], dn_norm_w[l].reshape(1, HEAD_W),
                  q_blk=blk["nq"], k_blk=blk["nk"], v_blk=blk["nv"], z_blk=blk["nz"])
        x2 = _merge(ya, yb, proj, x2, gate[1], lng[1], lnb[1], w_branch_a[l].astype(BF16),
                    w_branch_b[l].astype(BF16), w_out[l].astype(BF16),
                    ga_blk=blk["ga"], gb_blk=blk["gb"], alpha=alpha)

        x2 = _ffn(x2, shift[2], scale[2], gate[2], lng[2], lnb[2], wfi[1], wfo[1], alpha=alpha)
    return x2[None]
```

```python
import functools
import math

import numpy as np
import jax
import jax.numpy as jnp
from jax import lax
from jax.experimental import pallas as pl
from jax.experimental.pallas import tpu as pltpu

N_SUB = 3
HEADS = 8
HEAD_W = 128
DIFF_QK_DIM = 64
DN_CONV = 4
DN_CHUNK = 64
REL_BUCKETS = 32
REL_MAX_DIST = 128
LN_EPS = 1e-5
RMS_EPS = 1e-6
MASK_VALUE = -1e30
LOG2E = math.log2(math.e)
V_ROWS = HEAD_W + 16

F32 = jnp.float32
BF16 = jnp.bfloat16

V7X_VMEM_BYTES = 64 * 1024 * 1024
VMEM_LIMIT = 56 * 1024 * 1024


def _params(*sem):
    return pltpu.CompilerParams(dimension_semantics=sem, vmem_limit_bytes=VMEM_LIMIT)


def _tile(n, preferred):
    t = min(preferred, n)
    while n % t:
        t -= 128
    return t


def _silu(x):
    return x * jax.nn.sigmoid(x)


def _dot(a, b):
    return jnp.dot(a, b, preferred_element_type=F32)


def _dot_nt(a, b):
    return lax.dot_general(a, b, (((1,), (1,)), ((), ())), preferred_element_type=F32)


def _dot_tn(a, b):
    return lax.dot_general(a, b, (((0,), (0,)), ((), ())), preferred_element_type=F32)


def _layer_norm(r, g, b):
    mu = jnp.mean(r, axis=-1, keepdims=True)
    d = r - mu
    var = jnp.mean(d * d, axis=-1, keepdims=True)
    return d * lax.rsqrt(var + LN_EPS) * g + b


def _ada_kernel(c_ref, w_ref, b_ref, o_ref):
    sc = _silu(c_ref[...])
    o_ref[...] = jnp.sum(w_ref[...] * sc, axis=0, keepdims=True) + b_ref[...]


def _ada(c_col, w, b):
    d, n = w.shape
    tn = _tile(n, 1024)
    return pl.pallas_call(
        _ada_kernel,
        grid=(n // tn,),
        in_specs=[pl.BlockSpec((d, 1), lambda j: (0, 0)),
                  pl.BlockSpec((d, tn), lambda j: (0, j)),
                  pl.BlockSpec((1, tn), lambda j: (0, j))],
        out_specs=pl.BlockSpec((1, tn), lambda j: (0, j)),
        out_shape=jax.ShapeDtypeStruct((1, n), F32),
        compiler_params=_params("arbitrary"),
        name="ada",
    )(c_col, w, b)


def _ffn_kernel(x_ref, shift_ref, scale_ref, gate_ref, lng_ref, lnb_ref, wg_ref, wu_ref, wo_ref,
                o_ref, h_sc, acc_sc, *, alpha, gate_mul):
    j = pl.program_id(1)

    @pl.when(j == 0)
    def _():
        h_sc[...] = (x_ref[...] * (1.0 + scale_ref[...]) + shift_ref[...]).astype(BF16)
        acc_sc[...] = jnp.zeros_like(acc_sc)

    h = h_sc[...]
    g = _dot(h, wg_ref[...])
    u = _dot(h, wu_ref[...])
    a = (_silu(g) * u).astype(BF16)
    acc_sc[...] += _dot(a, wo_ref[...])

    @pl.when(j == pl.num_programs(1) - 1)
    def _():
        r = alpha * x_ref[...] + (gate_mul * gate_ref[...]) * acc_sc[...]
        o_ref[...] = _layer_norm(r, lng_ref[...], lnb_ref[...])


def _ffn(x, shift, scale, gate, lng, lnb, w_in, w_out, *, layer, which, alpha, tm=512, tf=512):
    s, d = x.shape
    f = w_out.shape[2]
    tm, tf = _tile(s, tm), _tile(f, tf)
    nf = f // tf
    vec = pl.BlockSpec((1, d), lambda i, j: (0, 0))
    return pl.pallas_call(
        functools.partial(_ffn_kernel, alpha=alpha, gate_mul=0.5),
        grid=(s // tm, nf),
        in_specs=[pl.BlockSpec((tm, d), lambda i, j: (i, 0)), vec, vec, vec, vec, vec,
                  pl.BlockSpec((None, None, d, tf), lambda i, j: (layer, which, 0, j)),
                  pl.BlockSpec((None, None, d, tf), lambda i, j: (layer, which, 0, j + nf)),
                  pl.BlockSpec((None, None, tf, d), lambda i, j: (layer, which, j, 0))],
        out_specs=pl.BlockSpec((tm, d), lambda i, j: (i, 0)),
        out_shape=jax.ShapeDtypeStruct((s, d), F32),
        scratch_shapes=[pltpu.VMEM((tm, d), BF16), pltpu.VMEM((tm, d), F32)],
        compiler_params=_params("parallel", "arbitrary"),
        name="ffn",
    )(x, shift, scale, gate, lng, lnb, w_in, w_in, w_out)


def _proj_kernel(x_ref, shift_ref, scale_ref, wg_ref, wh_ref, ws_ref, o_ref, os_ref, h_sc, *, n_gate):
    j = pl.program_id(1)

    @pl.when(j == 0)
    def _():
        h = (x_ref[...] * (1.0 + scale_ref[...]) + shift_ref[...]).astype(BF16)
        h_sc[...] = h
        os_ref[...] = _dot(h, ws_ref[...])

    @pl.when(j < n_gate)
    def _():
        o_ref[...] = _dot(h_sc[...], wg_ref[...]).astype(BF16)

    @pl.when(j >= n_gate)
    def _():
        o_ref[...] = _dot(h_sc[...], wh_ref[...]).astype(BF16)


def _proj(x, shift, scale, w_gate, w_all, w_small, *, layer, n_head_cols, tm=1024, tn=1024):
    s, d = x.shape
    ng_cols = w_gate.shape[1]
    ns = w_small.shape[1]
    tm = _tile(s, tm)
    tn = math.gcd(_tile(ng_cols, tn), _tile(n_head_cols, tn))
    n_gate, n_head = ng_cols // tn, n_head_cols // tn
    vec = pl.BlockSpec((1, d), lambda i, j: (0, 0))
    return pl.pallas_call(
        functools.partial(_proj_kernel, n_gate=n_gate),
        grid=(s // tm, n_gate + n_head),
        in_specs=[pl.BlockSpec((tm, d), lambda i, j: (i, 0)), vec, vec,
                  pl.BlockSpec((d, tn), lambda i, j: (0, jnp.minimum(j, n_gate - 1))),
                  pl.BlockSpec((None, d, tn), lambda i, j: (layer, 0, jnp.maximum(j - n_gate, 0))),
                  pl.BlockSpec((d, ns), lambda i, j: (0, 0))],
        out_specs=[pl.BlockSpec((tm, tn), lambda i, j: (i, j)),
                   pl.BlockSpec((tm, ns), lambda i, j: (i, 0))],
        out_shape=[jax.ShapeDtypeStruct((s, ng_cols + n_head_cols), BF16),
                   jax.ShapeDtypeStruct((s, ns), F32)],
        scratch_shapes=[pltpu.VMEM((tm, d), BF16)],
        compiler_params=_params("parallel", "arbitrary"),
        name="proj",
    )(x, shift, scale, w_gate, w_all, w_small)


def _bucket_starts():
    n = np.arange(0, 2 * REL_MAX_DIST)
    max_exact = REL_BUCKETS // 2
    nf = np.maximum(n, max_exact).astype(np.float32)
    large = max_exact + (np.log(nf / np.float32(max_exact)) / np.float32(math.log(REL_MAX_DIST / max_exact))
                         * np.float32(REL_BUCKETS - max_exact)).astype(np.int32)
    bucket = np.where(n < max_exact, n, np.minimum(large, REL_BUCKETS - 1))
    assert np.all(np.diff(bucket) >= 0) and bucket[-1] == REL_BUCKETS - 1
    return [int(np.min(n[bucket >= b])) for b in range(REL_BUCKETS)]


BUCKET_STARTS = _bucket_starts()


def _bias_kernel(rb_ref, o_ref, *, t):
    h = pl.program_id(0)
    jj = lax.broadcasted_iota(jnp.int32, (t, 2 * t), 0)
    ii = lax.broadcasted_iota(jnp.int32, (t, 2 * t), 1)
    ii = jnp.where(ii >= t, ii - t, ii)
    far = rb_ref[REL_BUCKETS - 1, h]
    for n in range(2):
        rel = ii - jj + n * t
        bias = jnp.full((t, 2 * t), (rb_ref[0, h] - far) * LOG2E, F32)
        for b in range(1, REL_BUCKETS):
            bias = jnp.where(rel >= BUCKET_STARTS[b], (rb_ref[b, h] - far) * LOG2E, bias)
        o_ref[0, n] = jnp.where(rel < 0, MASK_VALUE, bias)


def _bias_tiles(rel_bias, t):
    return pl.pallas_call(
        functools.partial(_bias_kernel, t=t),
        grid=(HEADS,),
        in_specs=[pl.BlockSpec(memory_space=pltpu.SMEM)],
        out_specs=pl.BlockSpec((1, 2, t, 2 * t), lambda h: (h, 0, 0, 0)),
        out_shape=jax.ShapeDtypeStruct((HEADS, 2, t, 2 * t), F32),
        compiler_params=_params("arbitrary"),
        name="bias_tiles",
    )(rel_bias)


def _attn_kernel(q_ref, k_ref, v_ref, bias_ref, lam_ref, sub_ref, o_ref,
                 vt_sc, qt_sc, s_sc, m_sc, acc_sc, *, t, hp, lam_init):
    qi = pl.program_id(1)
    n_kv = vt_sc.shape[1]
    heads = range(hp)
    cols = [slice(hh * HEAD_W, (hh + 1) * HEAD_W) for hh in heads]

    @pl.when(qi == 0)
    def _():
        ones_tile = (lax.broadcasted_iota(jnp.int32, (V_ROWS - HEAD_W, t), 0) == 0).astype(BF16)

        def body(c, carry):
            r0 = pl.multiple_of(c * t, t)
            for hh in heads:
                vt_sc[hh, c, 0:HEAD_W] = v_ref[pl.ds(r0, t), cols[hh]].astype(F32).T.astype(BF16)
                vt_sc[hh, c, HEAD_W:V_ROWS] = ones_tile
            return carry
        lax.fori_loop(0, n_kv, body, 0)

    row = lax.broadcasted_iota(jnp.int32, (HEAD_W, t), 0)
    for hh in heads:
        qt = (q_ref[:, cols[hh]].astype(F32) * (DIFF_QK_DIM ** -0.5 * LOG2E)).T
        zero = jnp.zeros_like(qt)
        qt_sc[hh] = jnp.concatenate([jnp.where(row < DIFF_QK_DIM, qt, zero),
                                     jnp.where(row >= DIFF_QK_DIM, qt, zero)], axis=1).astype(BF16)

    m_sc[...] = jnp.full_like(m_sc, -jnp.inf)
    acc_sc[...] = jnp.zeros_like(acc_sc)

    def scores(c, slot, biased):
        r0 = pl.multiple_of(c * t, t)
        for hh in heads:
            s = _dot(k_ref[pl.ds(r0, t), cols[hh]], qt_sc[hh])
            if biased:
                s = s + bias_ref[hh, qi - c]
            s_sc[slot, hh] = s

    def consume(c, slot):
        for hh in heads:
            s = s_sc[slot, hh]
            m_old = m_sc[hh]
            m_new = jnp.maximum(m_old, jnp.max(s, axis=0, keepdims=True))
            p = jnp.exp2(s - m_new).astype(BF16)
            acc_sc[hh] = jnp.exp2(m_old - m_new) * acc_sc[hh] + _dot(vt_sc[hh, c], p)
            m_sc[hh] = m_new

    pl.when(qi >= 2)(lambda: scores(0, 0, False))
    pl.when(qi < 2)(lambda: scores(0, 0, True))

    def body(c, carry):
        far_next = c + 1 <= qi - 2
        for slot in range(2):
            mine = (c & 1) == slot

            @pl.when(jnp.logical_and(mine, far_next))
            def _():
                scores(c + 1, 1 - slot, False)
                consume(c, slot)

            @pl.when(jnp.logical_and(mine, jnp.logical_not(far_next)))
            def _():
                scores(c + 1, 1 - slot, True)
                consume(c, slot)
        return carry
    lax.fori_loop(0, qi, body, 0)

    for slot in range(2):
        pl.when((qi & 1) == slot)(functools.partial(consume, qi, slot))

    lp = lam_ref[...]
    lam = (jnp.exp(jnp.sum(lp[0:1] * lp[1:2], axis=1, keepdims=True))
           - jnp.exp(jnp.sum(lp[2:3] * lp[3:4], axis=1, keepdims=True)) + lam_init)
    for hh in heads:
        acc = acc_sc[hh]
        o = acc[0:HEAD_W] / acc[HEAD_W:HEAD_W + 1]
        od = o[:, :t] - lam * o[:, t:]
        ms = jnp.mean(od * od, axis=0, keepdims=True)
        y = od * lax.rsqrt(ms + LN_EPS) * sub_ref[...] * (1.0 - lam_init)
        o_ref[:, cols[hh]] = y.T.astype(BF16)


def _attn(proj, bias_tiles, lam_params, subln_col, *, q_blk, k_blk, v_blk, lam_init, t, hp=4):
    s = proj.shape[0]
    w = hp * HEAD_W
    assert q_blk % hp == 0 and k_blk % hp == 0 and v_blk % hp == 0
    once = pl.Buffered(1)
    return pl.pallas_call(
        functools.partial(_attn_kernel, t=t, hp=hp, lam_init=lam_init),
        grid=(HEADS // hp, s // t),
        in_specs=[pl.BlockSpec((t, w), lambda g, i: (i, q_blk // hp + g)),
                  pl.BlockSpec((s, w), lambda g, i: (0, k_blk // hp + g), pipeline_mode=once),
                  pl.BlockSpec((s, w), lambda g, i: (0, v_blk // hp + g), pipeline_mode=once),
                  pl.BlockSpec((hp, 2, t, 2 * t), lambda g, i: (g, 0, 0, 0), pipeline_mode=once),
                  pl.BlockSpec(lam_params.shape, lambda g, i: (0, 0)),
                  pl.BlockSpec((HEAD_W, 1), lambda g, i: (0, 0))],
        out_specs=pl.BlockSpec((t, w), lambda g, i: (i, g)),
        out_shape=jax.ShapeDtypeStruct((s, HEADS * HEAD_W), BF16),
        scratch_shapes=[pltpu.VMEM((hp, s // t, V_ROWS, t), BF16),
                        pltpu.VMEM((hp, HEAD_W, 2 * t), BF16),
                        pltpu.VMEM((2, hp, t, 2 * t), F32),
                        pltpu.VMEM((hp, 1, 2 * t), F32),
                        pltpu.VMEM((hp, V_ROWS, 2 * t), F32)],
        compiler_params=_params("arbitrary", "arbitrary"),
        name="diff_attn",
    )(proj, proj, proj, bias_tiles, lam_params, subln_col)


def _gates_kernel(sm_ref, alog_ref, dtb_ref, o_ref):
    tr = sm_ref[...].T
    beta = jax.nn.sigmoid(tr[0:HEADS])
    x = tr[HEADS:2 * HEADS] + dtb_ref[...]
    softplus = jnp.maximum(x, 0.0) + jnp.log1p(jnp.exp(-jnp.abs(x)))
    g = -jnp.exp(alog_ref[...]) * softplus
    pos = lax.broadcasted_iota(jnp.int32, g.shape, 1) % DN_CHUNK
    shift = 1
    while shift < DN_CHUNK:
        g = g + jnp.where(pos >= shift, pltpu.roll(g, shift, axis=1), 0.0)
        shift *= 2
    o_ref[0:HEADS] = beta
    o_ref[HEADS:2 * HEADS] = g


def _gates(small, alog_col, dtb_col, *, tb=1024):
    s, ns = small.shape
    col = pl.BlockSpec((HEADS, 1), lambda i: (0, 0))
    return pl.pallas_call(
        _gates_kernel,
        grid=(s // tb,),
        in_specs=[pl.BlockSpec((tb, ns), lambda i: (i, 0)), col, col],
        out_specs=pl.BlockSpec((2 * HEADS, tb), lambda i: (0, i)),
        out_shape=jax.ShapeDtypeStruct((2 * HEADS, s), F32),
        compiler_params=_params("parallel"),
        name="gdn_gates",
    )(small, alog_col, dtb_col)


GROUP = 2 * DN_CHUNK


def _gdn_kernel(q_ref, k_ref, v_ref, z_ref, gt_ref, cwq_ref, cwk_ref, cwv_ref, nw_ref, o_ref,
                pad_sc, state_sc, o_sc, *, tb, hp):
    g = pl.program_id(0)
    ib = pl.program_id(1)
    halo = 8
    heads = range(hp)
    cols = [slice(hh * HEAD_W, (hh + 1) * HEAD_W) for hh in heads]

    @pl.when(ib == 0)
    def _():
        state_sc[...] = jnp.zeros_like(state_sc)
        pad_sc[:, 0:halo, :] = jnp.zeros((3 * hp, halo, HEAD_W), F32)

    def conv_silu(a, x_ref, cw_ref, hh):
        a = a * hp + hh
        pad_sc[a, halo:halo + tb, :] = x_ref[:, cols[hh]].astype(F32)
        cw = cw_ref[:, cols[hh]]
        y = cw[0:1] * pad_sc[a, halo - 3:halo - 3 + tb, :]
        for j in range(1, DN_CONV):
            y = y + cw[j:j + 1] * pad_sc[a, halo - 3 + j:halo - 3 + j + tb, :]
        pad_sc[a, 0:halo, :] = pad_sc[a, tb:tb + halo, :]
        return _silu(y)

    def l2n(x):
        return x * lax.rsqrt(jnp.sum(x * x, axis=-1, keepdims=True) + RMS_EPS)

    ii = lax.broadcasted_iota(jnp.int32, (GROUP, GROUP), 0)
    jj = lax.broadcasted_iota(jnp.int32, (GROUP, GROUP), 1)
    same = (ii >= DN_CHUNK) == (jj >= DN_CHUNK)
    tril = jnp.logical_and(same, ii >= jj)
    strict = jnp.logical_and(same, ii > jj)
    eye = (ii == jj).astype(F32)

    q, k, v, beta_c, cum_c, cum_t = [], [], [], [], [], []
    for hh in heads:
        q.append(l2n(conv_silu(0, q_ref, cwq_ref, hh)) * (HEAD_W ** -0.5))
        k.append(l2n(conv_silu(1, k_ref, cwk_ref, hh)))
        v.append(conv_silu(2, v_ref, cwv_ref, hh))
        head = g * hp + hh
        beta_t = jnp.broadcast_to(gt_ref[pl.ds(head, 1), :], (HEAD_W, tb))
        ct = jnp.broadcast_to(gt_ref[pl.ds(HEADS + head, 1), :], (HEAD_W, tb))
        beta_c.append(beta_t.T)
        cum_c.append(ct.T)
        cum_t.append(ct)

    groups = range(tb // GROUP)
    rows = [slice(r * GROUP, (r + 1) * GROUP) for r in groups]
    u, w, qk, qd, kg, cg = {}, {}, {}, {}, {}, {}

    def intra_chunk(r):
        kb, x, tinv, decay = {}, {}, {}, {}
        for hh in heads:
            sid = (r, hh)
            kg[sid], cg[sid] = k[hh][rows[r]], cum_c[hh][rows[r]]
            gdiff = cg[sid] - cum_t[hh][:, rows[r]]
            decay[hh] = jnp.where(tril, jnp.exp(jnp.where(tril, gdiff, 0.0)), 0.0)
            kb[hh] = kg[sid] * beta_c[hh][rows[r]]
            a = jnp.where(strict, _dot_nt(kb[hh].astype(BF16), kg[sid].astype(BF16)) * decay[hh], 0.0)
            x[hh] = a.astype(BF16)
            tinv[hh] = eye - a
        yield
        power = 2
        while power < DN_CHUNK:
            for hh in heads:
                x[hh] = _dot(x[hh], x[hh]).astype(BF16)
            yield
            for hh in heads:
                tinv[hh] = tinv[hh] + _dot(tinv[hh].astype(BF16), x[hh])
            yield
            power *= 2
        for hh in heads:
            sid = (r, hh)
            eg = jnp.exp(cg[sid])
            rhs = jnp.concatenate([v[hh][rows[r]] * beta_c[hh][rows[r]], kb[hh] * eg], axis=1).astype(BF16)
            uw = _dot(tinv[hh].astype(BF16), rhs)
            u[sid], w[sid] = uw[:, :HEAD_W], uw[:, HEAD_W:].astype(BF16)
            qg = q[hh][rows[r]]
            qk[sid] = (_dot_nt(qg.astype(BF16), kg[sid].astype(BF16)) * decay[hh]).astype(BF16)
            qd[sid] = (qg * eg).astype(BF16)
        yield

    def recurrence(r):
        v_new = {hh: [] for hh in heads}
        o_inter = {hh: [] for hh in heads}
        for c in range(2):
            cr = slice(c * DN_CHUNK, (c + 1) * DN_CHUNK)
            last = c * DN_CHUNK + DN_CHUNK - 1
            st, g_last = {}, {}
            for hh in heads:
                sid = (r, hh)
                g_last[hh] = cg[sid][last:last + 1, :]
                st[hh] = state_sc[hh]
                st16 = st[hh].astype(BF16)
                v_new[hh].append(u[sid][cr] - _dot(w[sid][cr], st16))
                o_inter[hh].append(_dot(qd[sid][cr], st16))
            yield
            for hh in heads:
                sid = (r, hh)
                kd = (kg[sid][cr] * jnp.exp(g_last[hh] - cg[sid][cr])).astype(BF16)
                state_sc[hh] = st[hh] * jnp.exp(g_last[hh]) + _dot_tn(kd, v_new[hh][c].astype(BF16))
            yield
        for hh in heads:
            vn_all = jnp.concatenate(v_new[hh], axis=0).astype(BF16)
            o_sc[hh, rows[r], :] = jnp.concatenate(o_inter[hh], axis=0) + _dot(qk[(r, hh)], vn_all)
        yield

    def interleave(slow, fast, ratio):
        slow_live = fast_live = True
        while slow_live or fast_live:
            if slow_live:
                slow_live = next(slow, "done") != "done"
            for _ in range(ratio):
                if fast_live:
                    fast_live = next(fast, "done") != "done"

    for _ in intra_chunk(0):
        pass
    for r in groups:
        nxt = intra_chunk(r + 1) if r + 1 < len(groups) else iter(())
        interleave(recurrence(r), nxt, 3)

    for hh in heads:
        o = o_sc[hh]
        on = o * lax.rsqrt(jnp.mean(o * o, axis=-1, keepdims=True) + RMS_EPS) * nw_ref[...]
        o_ref[:, cols[hh]] = (on * _silu(z_ref[:, cols[hh]].astype(F32))).astype(BF16)


def _gdn(proj, gates_t, conv_w, norm_w, *, q_blk, k_blk, v_blk, z_blk, tb=256, hp=8):
    s = proj.shape[0]
    w = hp * HEAD_W
    assert all(blk % hp == 0 for blk in (q_blk, k_blk, v_blk, z_blk))
    col = lambda blk: pl.BlockSpec((tb, w), lambda g, i: (i, blk // hp + g))
    cw = lambda blk: pl.BlockSpec((DN_CONV, w), lambda g, i: (0, blk // hp + g))
    return pl.pallas_call(
        functools.partial(_gdn_kernel, tb=tb, hp=hp),
        grid=(HEADS // hp, s // tb),
        in_specs=[col(q_blk), col(k_blk), col(v_blk), col(z_blk),
                  pl.BlockSpec((2 * HEADS, tb), lambda g, i: (0, i)),
                  cw(0), cw(HEADS), cw(2 * HEADS),
                  pl.BlockSpec((1, HEAD_W), lambda g, i: (0, 0))],
        out_specs=pl.BlockSpec((tb, w), lambda g, i: (i, g)),
        out_shape=jax.ShapeDtypeStruct((s, HEADS * HEAD_W), BF16),
        scratch_shapes=[pltpu.VMEM((3 * hp, tb + 8, HEAD_W), F32),
                        pltpu.VMEM((hp, HEAD_W, HEAD_W), F32),
                        pltpu.VMEM((hp, tb, HEAD_W), F32)],
        compiler_params=_params("arbitrary", "arbitrary"),
        name="gdn",
    )(proj, proj, proj, proj, gates_t, conv_w, conv_w, conv_w, norm_w)


def _merge_kernel(ya_ref, yb_ref, ga_ref, gb_ref, x_ref, gate_ref, lng_ref, lnb_ref,
                  wa_ref, wb_ref, wo_ref, o_ref, *, alpha):
    a = _dot(ya_ref[...], wa_ref[...])
    b = _dot(yb_ref[...], wb_ref[...])
    merged = (jax.nn.sigmoid(ga_ref[...].astype(F32)) * a
              + jax.nn.sigmoid(gb_ref[...].astype(F32)) * b).astype(BF16)
    y = _dot(merged, wo_ref[...])
    r = alpha * x_ref[...] + gate_ref[...] * y
    o_ref[...] = _layer_norm(r, lng_ref[...], lnb_ref[...])


def _merge(ya, yb, proj, x, gate, lng, lnb, w_a, w_b, w_o, *, ga_blk, gb_blk, alpha, tm=256):
    s, d = x.shape
    dv = ya.shape[1]
    vec = pl.BlockSpec((1, d), lambda i: (0, 0))
    const = lambda shape: pl.BlockSpec(shape, lambda i: (0, 0), pipeline_mode=pl.Buffered(1))
    return pl.pallas_call(
        functools.partial(_merge_kernel, alpha=alpha),
        grid=(s // tm,),
        in_specs=[pl.BlockSpec((tm, dv), lambda i: (i, 0)),
                  pl.BlockSpec((tm, dv), lambda i: (i, 0)),
                  pl.BlockSpec((tm, d), lambda i: (i, ga_blk)),
                  pl.BlockSpec((tm, d), lambda i: (i, gb_blk)),
                  pl.BlockSpec((tm, d), lambda i: (i, 0)), vec, vec, vec,
                  const(w_a.shape), const(w_b.shape), const(w_o.shape)],
        out_specs=pl.BlockSpec((tm, d), lambda i: (i, 0)),
        out_shape=jax.ShapeDtypeStruct((s, d), F32),
        compiler_params=_params("parallel"),
        name="merge",
    )(ya, yb, proj, proj, x, gate, lng, lnb, w_a, w_b, w_o)


def kernel(x, c, w_ada, b_ada, ln_g, ln_b, w_ffn_in, w_ffn_out, w_in, conv_w, dn_a_log, dn_dt_bias,
           dn_norm_w, diff_lambda, diff_subln_w, rel_bias, w_branch_a, w_branch_b, w_out):
    bsz, s, d = x.shape
    assert bsz == 1, "one sequence per call"
    depth = w_ada.shape[0]
    alpha = (2 * depth) ** 0.25
    hw = HEADS * HEAD_W
    attn_t = 256
    assert d % HEAD_W == 0 and s % 1024 == 0

    o_nb = 7 * hw
    o_ga = o_nb + 2 * HEADS
    nblk = d // HEAD_W
    blk = {"ga": 0, "gb": 1, "dq": 2 * nblk, "dk": 2 * nblk + HEADS, "dv": 2 * nblk + 2 * HEADS,
           "nq": 2 * nblk + 3 * HEADS, "nk": 2 * nblk + 4 * HEADS, "nv": 2 * nblk + 5 * HEADS,
           "nz": 2 * nblk + 6 * HEADS}

    bias_tiles = _bias_tiles(rel_bias, attn_t)
    wfi, wfo, wi16 = w_ffn_in.astype(BF16), w_ffn_out.astype(BF16), w_in.astype(BF16)
    x2 = x[0]
    for l in range(depth):
        lam_init = 0.8 - 0.6 * math.exp(-0.3 * l)
        ada = _ada(c.reshape(d, 1), w_ada[l], b_ada[l].reshape(1, -1)).reshape(N_SUB, 3, 1, d)
        shift, scale, gate = ada[:, 0], ada[:, 1], ada[:, 2]
        lng, lnb = ln_g[l].reshape(N_SUB, 1, d), ln_b[l].reshape(N_SUB, 1, d)
        w_gate = w_in[l, :, o_ga:].astype(BF16)
        w_small = jnp.pad(w_in[l, :, o_nb:o_ga], ((0, 0), (0, HEAD_W - 2 * HEADS))).astype(BF16)

        x2 = _ffn(x2, shift[0], scale[0], gate[0], lng[0], lnb[0], wfi, wfo, layer=l, which=0, alpha=alpha)

        proj, small = _proj(x2, shift[1], scale[1], w_gate, wi16, w_small, layer=l, n_head_cols=o_nb)
        ya = _attn(proj, bias_tiles, diff_lambda[l], diff_subln_w[l].reshape(HEAD_W, 1),
                   q_blk=blk["dq"], k_blk=blk["dk"], v_blk=blk["dv"], lam_init=lam_init, t=attn_t)
        gates_t = _gates(small, dn_a_log[l].reshape(HEADS, 1), dn_dt_bias[l].reshape(HEADS, 1))
        yb = _gdn(proj, gates_t, conv_w[l], dn_norm_w[l].reshape(1, HEAD_W),
                  q_blk=blk["nq"], k_blk=blk["nk"], v_blk=blk["nv"], z_blk=blk["nz"])
        x2 = _merge(ya, yb, proj, x2, gate[1], lng[1], lnb[1], w_branch_a[l].astype(BF16),
                    w_branch_b[l].astype(BF16), w_out[l].astype(BF16),
                    ga_blk=blk["ga"], gb_blk=blk["gb"], alpha=alpha)

        x2 = _ffn(x2, shift[2], scale[2], gate[2], lng[2], lnb[2], wfi, wfo, layer=l, which=1, alpha=alpha)
    return x2[None]
```

```python
import functools
import math

import numpy as np
import jax
import jax.numpy as jnp
from jax import lax
from jax.experimental import pallas as pl
from jax.experimental.pallas import tpu as pltpu

N_SUB = 3
HEADS = 8
HEAD_W = 128
DIFF_QK_DIM = 64
DN_CONV = 4
DN_CHUNK = 64
REL_BUCKETS = 32
REL_MAX_DIST = 128
LN_EPS = 1e-5
RMS_EPS = 1e-6
MASK_VALUE = -1e30
LOG2E = math.log2(math.e)
V_ROWS = HEAD_W + 16

F32 = jnp.float32
BF16 = jnp.bfloat16

V7X_VMEM_BYTES = 64 * 1024 * 1024
VMEM_LIMIT = V7X_VMEM_BYTES - 4 * 1024 * 1024


def _params(*sem):
    return pltpu.CompilerParams(dimension_semantics=sem, vmem_limit_bytes=VMEM_LIMIT)


def _tile(n, preferred):
    t = min(preferred, n)
    while n % t:
        t -= 128
    return t


def _silu(x):
    return x * jax.nn.sigmoid(x)


def _dot(a, b):
    return jnp.dot(a, b, preferred_element_type=F32)


def _dot_nt(a, b):
    return lax.dot_general(a, b, (((1,), (1,)), ((), ())), preferred_element_type=F32)


def _dot_tn(a, b):
    return lax.dot_general(a, b, (((0,), (0,)), ((), ())), preferred_element_type=F32)


def _layer_norm(r, g, b):
    mu = jnp.mean(r, axis=-1, keepdims=True)
    d = r - mu
    var = jnp.mean(d * d, axis=-1, keepdims=True)
    return d * lax.rsqrt(var + LN_EPS) * g + b


def _ada_kernel(c_ref, w_ref, b_ref, o_ref):
    sc = _silu(c_ref[...])
    o_ref[...] = jnp.sum(w_ref[...] * sc, axis=0, keepdims=True) + b_ref[...]


def _ada(c_col, w, b):
    d, n = w.shape
    tn = _tile(n, 1024)
    return pl.pallas_call(
        _ada_kernel,
        grid=(n // tn,),
        in_specs=[pl.BlockSpec((d, 1), lambda j: (0, 0)),
                  pl.BlockSpec((d, tn), lambda j: (0, j)),
                  pl.BlockSpec((1, tn), lambda j: (0, j))],
        out_specs=pl.BlockSpec((1, tn), lambda j: (0, j)),
        out_shape=jax.ShapeDtypeStruct((1, n), F32),
        compiler_params=_params("arbitrary"),
        name="ada",
    )(c_col, w, b)


def _ffn_kernel(x_ref, shift_ref, scale_ref, gate_ref, lng_ref, lnb_ref, wg_ref, wu_ref, wo_ref,
                o_ref, h_sc, a_sc, *, alpha, gate_mul):
    j = pl.program_id(1)
    n_chunks = pl.num_programs(1) - 1

    def drain(slot):
        return _dot(a_sc[slot], wo_ref[...])

    def activations(slot):
        h = h_sc[...]
        g = _dot(h, wg_ref[...])
        u = _dot(h, wu_ref[...])
        a_sc[slot] = (_silu(g) * u).astype(BF16)

    @pl.when(j == 0)
    def _():
        h_sc[...] = (x_ref[...] * (1.0 + scale_ref[...]) + shift_ref[...]).astype(BF16)
        o_ref[...] = jnp.zeros_like(o_ref)
        activations(0)

    for slot in range(2):
        @pl.when(jnp.logical_and(jnp.logical_and(j > 0, j < n_chunks), (j & 1) == slot))
        def _():
            o_ref[...] += drain(1 - slot)
            activations(slot)

        @pl.when(jnp.logical_and(j == n_chunks, ((j - 1) & 1) == slot))
        def _():
            r = alpha * x_ref[...] + (gate_mul * gate_ref[...]) * (o_ref[...] + drain(slot))
            o_ref[...] = _layer_norm(r, lng_ref[...], lnb_ref[...])


def _ffn(x, shift, scale, gate, lng, lnb, w_in, w_out, *, layer, which, alpha, tm=1024, tf=512):
    s, d = x.shape
    f = w_out.shape[2]
    tm, tf = _tile(s, tm), _tile(f, tf)
    nf = f // tf
    vec = pl.BlockSpec((1, d), lambda i, j: (0, 0))
    up = lambda j: jnp.minimum(j, nf - 1)
    down = lambda j: jnp.maximum(j - 1, 0)
    return pl.pallas_call(
        functools.partial(_ffn_kernel, alpha=alpha, gate_mul=0.5),
        grid=(s // tm, nf + 1),
        in_specs=[pl.BlockSpec((tm, d), lambda i, j: (i, 0)), vec, vec, vec, vec, vec,
                  pl.BlockSpec((None, None, d, tf), lambda i, j: (layer, which, 0, up(j))),
                  pl.BlockSpec((None, None, d, tf), lambda i, j: (layer, which, 0, up(j) + nf)),
                  pl.BlockSpec((None, None, tf, d), lambda i, j: (layer, which, down(j), 0))],
        out_specs=pl.BlockSpec((tm, d), lambda i, j: (i, 0)),
        out_shape=jax.ShapeDtypeStruct((s, d), F32),
        scratch_shapes=[pltpu.VMEM((tm, d), BF16), pltpu.VMEM((2, tm, tf), BF16)],
        compiler_params=_params("parallel", "arbitrary"),
        name="ffn",
    )(x, shift, scale, gate, lng, lnb, w_in, w_in, w_out)


def _proj_kernel(x_ref, shift_ref, scale_ref, wg_ref, wh_ref, ws_ref, o_ref, os_ref, h_sc, *, n_gate):
    j = pl.program_id(1)

    @pl.when(j == 0)
    def _():
        h = (x_ref[...] * (1.0 + scale_ref[...]) + shift_ref[...]).astype(BF16)
        h_sc[...] = h
        os_ref[...] = _dot(h, ws_ref[...])

    @pl.when(j < n_gate)
    def _():
        o_ref[...] = _dot(h_sc[...], wg_ref[...]).astype(BF16)

    @pl.when(j >= n_gate)
    def _():
        o_ref[...] = _dot(h_sc[...], wh_ref[...]).astype(BF16)


def _proj(x, shift, scale, w_gate, w_all, w_small, *, layer, n_head_cols, tm=1024, tn=1024):
    s, d = x.shape
    ng_cols = w_gate.shape[1]
    ns = w_small.shape[1]
    tm = _tile(s, tm)
    tn = math.gcd(_tile(ng_cols, tn), _tile(n_head_cols, tn))
    n_gate, n_head = ng_cols // tn, n_head_cols // tn
    vec = pl.BlockSpec((1, d), lambda i, j: (0, 0))
    return pl.pallas_call(
        functools.partial(_proj_kernel, n_gate=n_gate),
        grid=(s // tm, n_gate + n_head),
        in_specs=[pl.BlockSpec((tm, d), lambda i, j: (i, 0)), vec, vec,
                  pl.BlockSpec((d, tn), lambda i, j: (0, jnp.minimum(j, n_gate - 1))),
                  pl.BlockSpec((None, d, tn), lambda i, j: (layer, 0, jnp.maximum(j - n_gate, 0))),
                  pl.BlockSpec((d, ns), lambda i, j: (0, 0))],
        out_specs=[pl.BlockSpec((tm, tn), lambda i, j: (i, j)),
                   pl.BlockSpec((tm, ns), lambda i, j: (i, 0))],
        out_shape=[jax.ShapeDtypeStruct((s, ng_cols + n_head_cols), BF16),
                   jax.ShapeDtypeStruct((s, ns), F32)],
        scratch_shapes=[pltpu.VMEM((tm, d), BF16)],
        compiler_params=_params("parallel", "arbitrary"),
        name="proj",
    )(x, shift, scale, w_gate, w_all, w_small)


def _bucket_starts():
    n = np.arange(0, 2 * REL_MAX_DIST)
    max_exact = REL_BUCKETS // 2
    nf = np.maximum(n, max_exact).astype(np.float32)
    large = max_exact + (np.log(nf / np.float32(max_exact)) / np.float32(math.log(REL_MAX_DIST / max_exact))
                         * np.float32(REL_BUCKETS - max_exact)).astype(np.int32)
    bucket = np.where(n < max_exact, n, np.minimum(large, REL_BUCKETS - 1))
    assert np.all(np.diff(bucket) >= 0) and bucket[-1] == REL_BUCKETS - 1
    return [int(np.min(n[bucket >= b])) for b in range(REL_BUCKETS)]


BUCKET_STARTS = _bucket_starts()


def _bias_kernel(rb_ref, o_ref, *, t):
    h = pl.program_id(0)
    jj = lax.broadcasted_iota(jnp.int32, (t, 2 * t), 0)
    ii = lax.broadcasted_iota(jnp.int32, (t, 2 * t), 1)
    ii = jnp.where(ii >= t, ii - t, ii)
    far = rb_ref[REL_BUCKETS - 1, h]
    for n in range(2):
        rel = ii - jj + n * t
        bias = jnp.full((t, 2 * t), (rb_ref[0, h] - far) * LOG2E, F32)
        for b in range(1, REL_BUCKETS):
            bias = jnp.where(rel >= BUCKET_STARTS[b], (rb_ref[b, h] - far) * LOG2E, bias)
        o_ref[0, n] = jnp.where(rel < 0, MASK_VALUE, bias)


def _bias_tiles(rel_bias, t):
    return pl.pallas_call(
        functools.partial(_bias_kernel, t=t),
        grid=(HEADS,),
        in_specs=[pl.BlockSpec(memory_space=pltpu.SMEM)],
        out_specs=pl.BlockSpec((1, 2, t, 2 * t), lambda h: (h, 0, 0, 0)),
        out_shape=jax.ShapeDtypeStruct((HEADS, 2, t, 2 * t), F32),
        compiler_params=_params("arbitrary"),
        name="bias_tiles",
    )(rel_bias)


def _attn_kernel(q_ref, k_ref, v_ref, bias_ref, lam_ref, sub_ref, o_ref,
                 vt_sc, qt_sc, s_sc, m_sc, acc_sc, *, t, hp, lam_init):
    qi = pl.program_id(1)
    n_kv = vt_sc.shape[1]
    heads = range(hp)
    cols = [slice(hh * HEAD_W, (hh + 1) * HEAD_W) for hh in heads]

    @pl.when(qi == 0)
    def _():
        ones_tile = (lax.broadcasted_iota(jnp.int32, (V_ROWS - HEAD_W, t), 0) == 0).astype(BF16)

        def body(c, carry):
            r0 = pl.multiple_of(c * t, t)
            for hh in heads:
                vt_sc[hh, c, 0:HEAD_W] = v_ref[pl.ds(r0, t), cols[hh]].astype(F32).T.astype(BF16)
                vt_sc[hh, c, HEAD_W:V_ROWS] = ones_tile
            return carry
        lax.fori_loop(0, n_kv, body, 0)

    row = lax.broadcasted_iota(jnp.int32, (HEAD_W, t), 0)
    for hh in heads:
        qt = (q_ref[:, cols[hh]].astype(F32) * (DIFF_QK_DIM ** -0.5 * LOG2E)).T
        zero = jnp.zeros_like(qt)
        qt_sc[hh] = jnp.concatenate([jnp.where(row < DIFF_QK_DIM, qt, zero),
                                     jnp.where(row >= DIFF_QK_DIM, qt, zero)], axis=1).astype(BF16)

    m_sc[...] = jnp.full_like(m_sc, -jnp.inf)
    acc_sc[...] = jnp.zeros_like(acc_sc)

    def scores(c, slot, biased):
        r0 = pl.multiple_of(c * t, t)
        for hh in heads:
            s = _dot(k_ref[pl.ds(r0, t), cols[hh]], qt_sc[hh])
            if biased:
                s = s + bias_ref[hh, qi - c]
            s_sc[slot, hh] = s

    def consume(c, slot):
        for hh in heads:
            s = s_sc[slot, hh]
            m_old = m_sc[hh]
            m_new = jnp.maximum(m_old, jnp.max(s, axis=0, keepdims=True))
            p = jnp.exp2(s - m_new).astype(BF16)
            acc_sc[hh] = jnp.exp2(m_old - m_new) * acc_sc[hh] + _dot(vt_sc[hh, c], p)
            m_sc[hh] = m_new

    pl.when(qi >= 2)(lambda: scores(0, 0, False))
    pl.when(qi < 2)(lambda: scores(0, 0, True))

    def body(c, carry):
        far_next = c + 1 <= qi - 2
        for slot in range(2):
            mine = (c & 1) == slot

            @pl.when(jnp.logical_and(mine, far_next))
            def _():
                scores(c + 1, 1 - slot, False)
                consume(c, slot)

            @pl.when(jnp.logical_and(mine, jnp.logical_not(far_next)))
            def _():
                scores(c + 1, 1 - slot, True)
                consume(c, slot)
        return carry
    lax.fori_loop(0, qi, body, 0)

    for slot in range(2):
        pl.when((qi & 1) == slot)(functools.partial(consume, qi, slot))

    lp = lam_ref[...]
    lam = (jnp.exp(jnp.sum(lp[0:1] * lp[1:2], axis=1, keepdims=True))
           - jnp.exp(jnp.sum(lp[2:3] * lp[3:4], axis=1, keepdims=True)) + lam_init)
    for hh in heads:
        acc = acc_sc[hh]
        o = acc[0:HEAD_W] / acc[HEAD_W:HEAD_W + 1]
        od = o[:, :t] - lam * o[:, t:]
        ms = jnp.mean(od * od, axis=0, keepdims=True)
        y = od * lax.rsqrt(ms + LN_EPS) * sub_ref[...] * (1.0 - lam_init)
        o_ref[:, cols[hh]] = y.T.astype(BF16)


def _attn(proj, bias_tiles, lam_params, subln_col, *, q_blk, k_blk, v_blk, lam_init, t, hp=4):
    s = proj.shape[0]
    w = hp * HEAD_W
    assert q_blk % hp == 0 and k_blk % hp == 0 and v_blk % hp == 0
    once = pl.Buffered(1)
    return pl.pallas_call(
        functools.partial(_attn_kernel, t=t, hp=hp, lam_init=lam_init),
        grid=(HEADS // hp, s // t),
        in_specs=[pl.BlockSpec((t, w), lambda g, i: (i, q_blk // hp + g)),
                  pl.BlockSpec((s, w), lambda g, i: (0, k_blk // hp + g), pipeline_mode=once),
                  pl.BlockSpec((s, w), lambda g, i: (0, v_blk // hp + g), pipeline_mode=once),
                  pl.BlockSpec((hp, 2, t, 2 * t), lambda g, i: (g, 0, 0, 0), pipeline_mode=once),
                  pl.BlockSpec(lam_params.shape, lambda g, i: (0, 0)),
                  pl.BlockSpec((HEAD_W, 1), lambda g, i: (0, 0))],
        out_specs=pl.BlockSpec((t, w), lambda g, i: (i, g)),
        out_shape=jax.ShapeDtypeStruct((s, HEADS * HEAD_W), BF16),
        scratch_shapes=[pltpu.VMEM((hp, s // t, V_ROWS, t), BF16),
                        pltpu.VMEM((hp, HEAD_W, 2 * t), BF16),
                        pltpu.VMEM((2, hp, t, 2 * t), F32),
                        pltpu.VMEM((hp, 1, 2 * t), F32),
                        pltpu.VMEM((hp, V_ROWS, 2 * t), F32)],
        compiler_params=_params("arbitrary", "arbitrary"),
        name="diff_attn",
    )(proj, proj, proj, bias_tiles, lam_params, subln_col)


def _gates_kernel(sm_ref, alog_ref, dtb_ref, o_ref):
    tr = sm_ref[...].T
    beta = jax.nn.sigmoid(tr[0:HEADS])
    x = tr[HEADS:2 * HEADS] + dtb_ref[...]
    softplus = jnp.maximum(x, 0.0) + jnp.log1p(jnp.exp(-jnp.abs(x)))
    g = -jnp.exp(alog_ref[...]) * softplus
    pos = lax.broadcasted_iota(jnp.int32, g.shape, 1) % DN_CHUNK
    shift = 1
    while shift < DN_CHUNK:
        g = g + jnp.where(pos >= shift, pltpu.roll(g, shift, axis=1), 0.0)
        shift *= 2
    o_ref[0:HEADS] = beta
    o_ref[HEADS:2 * HEADS] = g


def _gates(small, alog_col, dtb_col, *, tb=1024):
    s, ns = small.shape
    col = pl.BlockSpec((HEADS, 1), lambda i: (0, 0))
    return pl.pallas_call(
        _gates_kernel,
        grid=(s // tb,),
        in_specs=[pl.BlockSpec((tb, ns), lambda i: (i, 0)), col, col],
        out_specs=pl.BlockSpec((2 * HEADS, tb), lambda i: (0, i)),
        out_shape=jax.ShapeDtypeStruct((2 * HEADS, s), F32),
        compiler_params=_params("parallel"),
        name="gdn_gates",
    )(small, alog_col, dtb_col)


GROUP = 2 * DN_CHUNK


def _gdn_kernel(q_ref, k_ref, v_ref, z_ref, gt_ref, cwq_ref, cwk_ref, cwv_ref, nw_ref, o_ref,
                pad_sc, state_sc, o_sc, *, tb, hp):
    g = pl.program_id(0)
    ib = pl.program_id(1)
    halo = 8
    heads = range(hp)
    cols = [slice(hh * HEAD_W, (hh + 1) * HEAD_W) for hh in heads]

    @pl.when(ib == 0)
    def _():
        state_sc[...] = jnp.zeros_like(state_sc)
        pad_sc[:, 0:halo, :] = jnp.zeros((3 * hp, halo, HEAD_W), F32)

    def conv_silu(a, x_ref, cw_ref, hh):
        a = a * hp + hh
        pad_sc[a, halo:halo + tb, :] = x_ref[:, cols[hh]].astype(F32)
        cw = cw_ref[:, cols[hh]]
        y = cw[0:1] * pad_sc[a, halo - 3:halo - 3 + tb, :]
        for j in range(1, DN_CONV):
            y = y + cw[j:j + 1] * pad_sc[a, halo - 3 + j:halo - 3 + j + tb, :]
        pad_sc[a, 0:halo, :] = pad_sc[a, tb:tb + halo, :]
        return _silu(y)

    def l2n(x):
        return x * lax.rsqrt(jnp.sum(x * x, axis=-1, keepdims=True) + RMS_EPS)

    ii = lax.broadcasted_iota(jnp.int32, (GROUP, GROUP), 0)
    jj = lax.broadcasted_iota(jnp.int32, (GROUP, GROUP), 1)
    same = (ii >= DN_CHUNK) == (jj >= DN_CHUNK)
    tril = jnp.logical_and(same, ii >= jj)
    strict = jnp.logical_and(same, ii > jj)
    eye = (ii == jj).astype(F32)

    q, k, v, beta_c, cum_c, cum_t = [], [], [], [], [], []
    for hh in heads:
        q.append(l2n(conv_silu(0, q_ref, cwq_ref, hh)) * (HEAD_W ** -0.5))
        k.append(l2n(conv_silu(1, k_ref, cwk_ref, hh)))
        v.append(conv_silu(2, v_ref, cwv_ref, hh))
        head = g * hp + hh
        beta_t = jnp.broadcast_to(gt_ref[pl.ds(head, 1), :], (HEAD_W, tb))
        ct = jnp.broadcast_to(gt_ref[pl.ds(HEADS + head, 1), :], (HEAD_W, tb))
        beta_c.append(beta_t.T)
        cum_c.append(ct.T)
        cum_t.append(ct)

    groups = range(tb // GROUP)
    rows = [slice(r * GROUP, (r + 1) * GROUP) for r in groups]
    u, w, qk, qd, kg, cg = {}, {}, {}, {}, {}, {}

    def intra_chunk(r):
        kb, x, tinv, decay = {}, {}, {}, {}
        for hh in heads:
            sid = (r, hh)
            kg[sid], cg[sid] = k[hh][rows[r]], cum_c[hh][rows[r]]
            gdiff = cg[sid] - cum_t[hh][:, rows[r]]
            decay[hh] = jnp.where(tril, jnp.exp(jnp.where(tril, gdiff, 0.0)), 0.0)
            kb[hh] = kg[sid] * beta_c[hh][rows[r]]
            a = jnp.where(strict, _dot_nt(kb[hh].astype(BF16), kg[sid].astype(BF16)) * decay[hh], 0.0)
            x[hh] = a.astype(BF16)
            tinv[hh] = eye - a
        yield
        power = 2
        while power < DN_CHUNK:
            for hh in heads:
                x[hh] = _dot(x[hh], x[hh]).astype(BF16)
            yield
            for hh in heads:
                tinv[hh] = tinv[hh] + _dot(tinv[hh].astype(BF16), x[hh])
            yield
            power *= 2
        for hh in heads:
            sid = (r, hh)
            eg = jnp.exp(cg[sid])
            rhs = jnp.concatenate([v[hh][rows[r]] * beta_c[hh][rows[r]], kb[hh] * eg], axis=1).astype(BF16)
            uw = _dot(tinv[hh].astype(BF16), rhs)
            u[sid], w[sid] = uw[:, :HEAD_W], uw[:, HEAD_W:].astype(BF16)
            qg = q[hh][rows[r]]
            qk[sid] = (_dot_nt(qg.astype(BF16), kg[sid].astype(BF16)) * decay[hh]).astype(BF16)
            qd[sid] = (qg * eg).astype(BF16)
        yield

    def recurrence(r):
        v_new = {hh: [] for hh in heads}
        o_inter = {hh: [] for hh in heads}
        for c in range(2):
            cr = slice(c * DN_CHUNK, (c + 1) * DN_CHUNK)
            last = c * DN_CHUNK + DN_CHUNK - 1
            st, g_last = {}, {}
            for hh in heads:
                sid = (r, hh)
                g_last[hh] = cg[sid][last:last + 1, :]
                st[hh] = state_sc[hh]
                st16 = st[hh].astype(BF16)
                v_new[hh].append(u[sid][cr] - _dot(w[sid][cr], st16))
                o_inter[hh].append(_dot(qd[sid][cr], st16))
            yield
            for hh in heads:
                sid = (r, hh)
                kd = (kg[sid][cr] * jnp.exp(g_last[hh] - cg[sid][cr])).astype(BF16)
                state_sc[hh] = st[hh] * jnp.exp(g_last[hh]) + _dot_tn(kd, v_new[hh][c].astype(BF16))
            yield
        for hh in heads:
            vn_all = jnp.concatenate(v_new[hh], axis=0).astype(BF16)
            o_sc[hh, rows[r], :] = jnp.concatenate(o_inter[hh], axis=0) + _dot(qk[(r, hh)], vn_all)
        yield

    def interleave(slow, fast, ratio):
        slow_live = fast_live = True
        while slow_live or fast_live:
            if slow_live:
                slow_live = next(slow, "done") != "done"
            for _ in range(ratio):
                if fast_live:
                    fast_live = next(fast, "done") != "done"

    for _ in intra_chunk(0):
        pass
    for r in groups:
        nxt = intra_chunk(r + 1) if r + 1 < len(groups) else iter(())
        interleave(recurrence(r), nxt, 3)

    for hh in heads:
        o = o_sc[hh]
        on = o * lax.rsqrt(jnp.mean(o * o, axis=-1, keepdims=True) + RMS_EPS) * nw_ref[...]
        o_ref[:, cols[hh]] = (on * _silu(z_ref[:, cols[hh]].astype(F32))).astype(BF16)


def _gdn(proj, gates_t, conv_w, norm_w, *, q_blk, k_blk, v_blk, z_blk, tb=256, hp=8):
    s = proj.shape[0]
    w = hp * HEAD_W
    assert all(blk % hp == 0 for blk in (q_blk, k_blk, v_blk, z_blk))
    col = lambda blk: pl.BlockSpec((tb, w), lambda g, i: (i, blk // hp + g))
    cw = lambda blk: pl.BlockSpec((DN_CONV, w), lambda g, i: (0, blk // hp + g))
    return pl.pallas_call(
        functools.partial(_gdn_kernel, tb=tb, hp=hp),
        grid=(HEADS // hp, s // tb),
        in_specs=[col(q_blk), col(k_blk), col(v_blk), col(z_blk),
                  pl.BlockSpec((2 * HEADS, tb), lambda g, i: (0, i)),
                  cw(0), cw(HEADS), cw(2 * HEADS),
                  pl.BlockSpec((1, HEAD_W), lambda g, i: (0, 0))],
        out_specs=pl.BlockSpec((tb, w), lambda g, i: (i, g)),
        out_shape=jax.ShapeDtypeStruct((s, HEADS * HEAD_W), BF16),
        scratch_shapes=[pltpu.VMEM((3 * hp, tb + 8, HEAD_W), F32),
                        pltpu.VMEM((hp, HEAD_W, HEAD_W), F32),
                        pltpu.VMEM((hp, tb, HEAD_W), F32)],
        compiler_params=_params("arbitrary", "arbitrary"),
        name="gdn",
    )(proj, proj, proj, proj, gates_t, conv_w, conv_w, conv_w, norm_w)


def _merge_kernel(ya_ref, yb_ref, ga_ref, gb_ref, x_ref, gate_ref, lng_ref, lnb_ref,
                  wa_ref, wb_ref, wo_ref, o_ref, *, alpha):
    a = _dot(ya_ref[...], wa_ref[...])
    b = _dot(yb_ref[...], wb_ref[...])
    merged = (jax.nn.sigmoid(ga_ref[...].astype(F32)) * a
              + jax.nn.sigmoid(gb_ref[...].astype(F32)) * b).astype(BF16)
    y = _dot(merged, wo_ref[...])
    r = alpha * x_ref[...] + gate_ref[...] * y
    o_ref[...] = _layer_norm(r, lng_ref[...], lnb_ref[...])


def _merge(ya, yb, proj, x, gate, lng, lnb, w_a, w_b, w_o, *, ga_blk, gb_blk, alpha, tm=256):
    s, d = x.shape
    dv = ya.shape[1]
    vec = pl.BlockSpec((1, d), lambda i: (0, 0))
    const = lambda shape: pl.BlockSpec(shape, lambda i: (0, 0), pipeline_mode=pl.Buffered(1))
    return pl.pallas_call(
        functools.partial(_merge_kernel, alpha=alpha),
        grid=(s // tm,),
        in_specs=[pl.BlockSpec((tm, dv), lambda i: (i, 0)),
                  pl.BlockSpec((tm, dv), lambda i: (i, 0)),
                  pl.BlockSpec((tm, d), lambda i: (i, ga_blk)),
                  pl.BlockSpec((tm, d), lambda i: (i, gb_blk)),
                  pl.BlockSpec((tm, d), lambda i: (i, 0)), vec, vec, vec,
                  const(w_a.shape), const(w_b.shape), const(w_o.shape)],
        out_specs=pl.BlockSpec((tm, d), lambda i: (i, 0)),
        out_shape=jax.ShapeDtypeStruct((s, d), F32),
        compiler_params=_params("parallel"),
        name="merge",
    )(ya, yb, proj, proj, x, gate, lng, lnb, w_a, w_b, w_o)


def kernel(x, c, w_ada, b_ada, ln_g, ln_b, w_ffn_in, w_ffn_out, w_in, conv_w, dn_a_log, dn_dt_bias,
           dn_norm_w, diff_lambda, diff_subln_w, rel_bias, w_branch_a, w_branch_b, w_out):
    bsz, s, d = x.shape
    assert bsz == 1, "one sequence per call"
    depth = w_ada.shape[0]
    alpha = (2 * depth) ** 0.25
    hw = HEADS * HEAD_W
    attn_t = 256
    assert d % HEAD_W == 0 and s % 1024 == 0

    o_nb = 7 * hw
    o_ga = o_nb + 2 * HEADS
    nblk = d // HEAD_W
    blk = {"ga": 0, "gb": 1, "dq": 2 * nblk, "dk": 2 * nblk + HEADS, "dv": 2 * nblk + 2 * HEADS,
           "nq": 2 * nblk + 3 * HEADS, "nk": 2 * nblk + 4 * HEADS, "nv": 2 * nblk + 5 * HEADS,
           "nz": 2 * nblk + 6 * HEADS}

    bias_tiles = _bias_tiles(rel_bias, attn_t)
    wfi, wfo, wi16 = w_ffn_in.astype(BF16), w_ffn_out.astype(BF16), w_in[:, :, :o_nb].astype(BF16)
    x2 = x[0]
    for l in range(depth):
        lam_init = 0.8 - 0.6 * math.exp(-0.3 * l)
        ada = _ada(c.reshape(d, 1), w_ada[l], b_ada[l].reshape(1, -1)).reshape(N_SUB, 3, 1, d)
        shift, scale, gate = ada[:, 0], ada[:, 1], ada[:, 2]
        lng, lnb = ln_g[l].reshape(N_SUB, 1, d), ln_b[l].reshape(N_SUB, 1, d)
        w_gate = w_in[l, :, o_ga:].astype(BF16)
        w_small = jnp.pad(w_in[l, :, o_nb:o_ga], ((0, 0), (0, HEAD_W - 2 * HEADS))).astype(BF16)

        x2 = _ffn(x2, shift[0], scale[0], gate[0], lng[0], lnb[0], wfi, wfo, layer=l, which=0, alpha=alpha)

        proj, small = _proj(x2, shift[1], scale[1], w_gate, wi16, w_small, layer=l, n_head_cols=o_nb)
        ya = _attn(proj, bias_tiles, diff_lambda[l], diff_subln_w[l].reshape(HEAD_W, 1),
                   q_blk=blk["dq"], k_blk=blk["dk"], v_blk=blk["dv"], lam_init=lam_init, t=attn_t)
        gates_t = _gates(small, dn_a_log[l].reshape(HEADS, 1), dn_dt_bias[l].reshape(HEADS, 1))
        yb = _gdn(proj, gates_t, conv_w[l], dn_norm_w[l].reshape(1, HEAD_W),
                  q_blk=blk["nq"], k_blk=blk["nk"], v_blk=blk["nv"], z_blk=blk["nz"])
        x2 = _merge(ya, yb, proj, x2, gate[1], lng[1], lnb[1], w_branch_a[l].astype(BF16),
                    w_branch_b[l].astype(BF16), w_out[l].astype(BF16),
                    ga_blk=blk["ga"], gb_blk=blk["gb"], alpha=alpha)

        x2 = _ffn(x2, shift[2], scale[2], gate[2], lng[2], lnb[2], wfi, wfo, layer=l, which=1, alpha=alpha)
    return x2[None]
```

```python
import functools
import math

import numpy as np
import jax
import jax.numpy as jnp
from jax import lax
from jax.experimental import pallas as pl
from jax.experimental.pallas import tpu as pltpu

N_SUB = 3
HEADS = 8
HEAD_W = 128
DIFF_QK_DIM = 64
DN_CONV = 4
DN_CHUNK = 64
REL_BUCKETS = 32
REL_MAX_DIST = 128
LN_EPS = 1e-5
RMS_EPS = 1e-6
MASK_VALUE = -1e30
LOG2E = math.log2(math.e)
V_ROWS = HEAD_W + 16

F32 = jnp.float32
BF16 = jnp.bfloat16

V7X_VMEM_BYTES = 64 * 1024 * 1024
VMEM_LIMIT = V7X_VMEM_BYTES - 4 * 1024 * 1024


def _params(*sem):
    return pltpu.CompilerParams(dimension_semantics=sem, vmem_limit_bytes=VMEM_LIMIT)


def _tile(n, preferred):
    t = min(preferred, n)
    while n % t:
        t -= 128
    return t


def _silu(x):
    return x * jax.nn.sigmoid(x)


def _dot(a, b):
    return jnp.dot(a, b, preferred_element_type=F32)


def _dot_nt(a, b):
    return lax.dot_general(a, b, (((1,), (1,)), ((), ())), preferred_element_type=F32)


def _dot_tn(a, b):
    return lax.dot_general(a, b, (((0,), (0,)), ((), ())), preferred_element_type=F32)


def _layer_norm(r, g, b):
    mu = jnp.mean(r, axis=-1, keepdims=True)
    d = r - mu
    var = jnp.mean(d * d, axis=-1, keepdims=True)
    return d * lax.rsqrt(var + LN_EPS) * g + b


def _ada_kernel(c_ref, w_ref, b_ref, o_ref):
    sc = _silu(c_ref[...])
    o_ref[...] = jnp.sum(w_ref[...] * sc, axis=0, keepdims=True) + b_ref[...]


def _ada(c_col, w, b):
    d, n = w.shape
    tn = _tile(n, 1024)
    return pl.pallas_call(
        _ada_kernel,
        grid=(n // tn,),
        in_specs=[pl.BlockSpec((d, 1), lambda j: (0, 0)),
                  pl.BlockSpec((d, tn), lambda j: (0, j)),
                  pl.BlockSpec((1, tn), lambda j: (0, j))],
        out_specs=pl.BlockSpec((1, tn), lambda j: (0, j)),
        out_shape=jax.ShapeDtypeStruct((1, n), F32),
        compiler_params=_params("arbitrary"),
        name="ada",
    )(c_col, w, b)


def _ffn_kernel(x_ref, shift_ref, scale_ref, gate_ref, lng_ref, lnb_ref, wg_ref, wu_ref, wo_ref,
                o_ref, h_sc, a_sc, *, alpha, gate_mul):
    j = pl.program_id(1)
    n_chunks = pl.num_programs(1) - 1

    def drain(slot):
        return _dot(a_sc[slot], wo_ref[...])

    def activations(slot):
        h = h_sc[...]
        g = _dot(h, wg_ref[...])
        u = _dot(h, wu_ref[...])
        a_sc[slot] = (_silu(g) * u).astype(BF16)

    @pl.when(j == 0)
    def _():
        h_sc[...] = (x_ref[...] * (1.0 + scale_ref[...]) + shift_ref[...]).astype(BF16)
        o_ref[...] = jnp.zeros_like(o_ref)
        activations(0)

    for slot in range(2):
        @pl.when(jnp.logical_and(jnp.logical_and(j > 0, j < n_chunks), (j & 1) == slot))
        def _():
            o_ref[...] += drain(1 - slot)
            activations(slot)

        @pl.when(jnp.logical_and(j == n_chunks, ((j - 1) & 1) == slot))
        def _():
            r = alpha * x_ref[...] + (gate_mul * gate_ref[...]) * (o_ref[...] + drain(slot))
            o_ref[...] = _layer_norm(r, lng_ref[...], lnb_ref[...])


def _ffn(x, shift, scale, gate, lng, lnb, w_in, w_out, *, layer, which, alpha, tm=1024, tf=512):
    s, d = x.shape
    f = w_out.shape[2]
    tm, tf = _tile(s, tm), _tile(f, tf)
    nf = f // tf
    vec = pl.BlockSpec((1, d), lambda i, j: (0, 0))
    up = lambda j: jnp.minimum(j, nf - 1)
    down = lambda j: jnp.maximum(j - 1, 0)
    return pl.pallas_call(
        functools.partial(_ffn_kernel, alpha=alpha, gate_mul=0.5),
        grid=(s // tm, nf + 1),
        in_specs=[pl.BlockSpec((tm, d), lambda i, j: (i, 0)), vec, vec, vec, vec, vec,
                  pl.BlockSpec((None, None, d, tf), lambda i, j: (layer, which, 0, up(j))),
                  pl.BlockSpec((None, None, d, tf), lambda i, j: (layer, which, 0, up(j) + nf)),
                  pl.BlockSpec((None, None, tf, d), lambda i, j: (layer, which, down(j), 0))],
        out_specs=pl.BlockSpec((tm, d), lambda i, j: (i, 0)),
        out_shape=jax.ShapeDtypeStruct((s, d), F32),
        scratch_shapes=[pltpu.VMEM((tm, d), BF16), pltpu.VMEM((2, tm, tf), BF16)],
        compiler_params=_params("parallel", "arbitrary"),
        name="ffn",
    )(x, shift, scale, gate, lng, lnb, w_in, w_in, w_out)


def _proj_kernel(x_ref, shift_ref, scale_ref, wg_ref, wh_ref, o_ref, w_sc, *, n_gate):
    j = pl.program_id(0)
    first_row = pl.program_id(1) == 0

    @pl.when(jnp.logical_and(first_row, j < n_gate))
    def _():
        w_sc[...] = wg_ref[...]

    @pl.when(jnp.logical_and(first_row, j >= n_gate))
    def _():
        w_sc[...] = wh_ref[...].astype(BF16)

    h = (x_ref[...] * (1.0 + scale_ref[...]) + shift_ref[...]).astype(BF16)
    o_ref[...] = _dot(h, w_sc[...]).astype(BF16)


def _proj(x, shift, scale, w_gate, w_all, *, layer, n_head_cols, tm=1024, tn=1024):
    s, d = x.shape
    ng_cols = w_gate.shape[1]
    tm = _tile(s, tm)
    vec = pl.BlockSpec((1, d), lambda j, i: (0, 0))
    tn = math.gcd(_tile(ng_cols, tn), _tile(n_head_cols, tn))
    n_gate, n_head = ng_cols // tn, n_head_cols // tn
    return pl.pallas_call(
        functools.partial(_proj_kernel, n_gate=n_gate),
        grid=(n_gate + n_head, s // tm),
        in_specs=[pl.BlockSpec((tm, d), lambda j, i: (i, 0)), vec, vec,
                  pl.BlockSpec((d, tn), lambda j, i: (0, jnp.minimum(j, n_gate - 1))),
                  pl.BlockSpec((None, d, tn), lambda j, i: (layer, 0, jnp.maximum(j - n_gate, 0)))],
        out_specs=pl.BlockSpec((tm, tn), lambda j, i: (i, j)),
        out_shape=jax.ShapeDtypeStruct((s, ng_cols + n_head_cols), BF16),
        scratch_shapes=[pltpu.VMEM((d, tn), BF16)],
        compiler_params=_params("arbitrary", "arbitrary"),
        name="proj",
    )(x, shift, scale, w_gate, w_all)


def _bucket_starts():
    n = np.arange(0, 2 * REL_MAX_DIST)
    max_exact = REL_BUCKETS // 2
    nf = np.maximum(n, max_exact).astype(np.float32)
    large = max_exact + (np.log(nf / np.float32(max_exact)) / np.float32(math.log(REL_MAX_DIST / max_exact))
                         * np.float32(REL_BUCKETS - max_exact)).astype(np.int32)
    bucket = np.where(n < max_exact, n, np.minimum(large, REL_BUCKETS - 1))
    assert np.all(np.diff(bucket) >= 0) and bucket[-1] == REL_BUCKETS - 1
    return [int(np.min(n[bucket >= b])) for b in range(REL_BUCKETS)]


BUCKET_STARTS = _bucket_starts()


def _bias_kernel(rb_ref, o_ref, *, t):
    h = pl.program_id(0)
    jj = lax.broadcasted_iota(jnp.int32, (t, 2 * t), 0)
    ii = lax.broadcasted_iota(jnp.int32, (t, 2 * t), 1)
    ii = jnp.where(ii >= t, ii - t, ii)
    far = rb_ref[REL_BUCKETS - 1, h]
    for n in range(2):
        rel = ii - jj + n * t
        bias = jnp.full((t, 2 * t), (rb_ref[0, h] - far) * LOG2E, F32)
        for b in range(1, REL_BUCKETS):
            bias = jnp.where(rel >= BUCKET_STARTS[b], (rb_ref[b, h] - far) * LOG2E, bias)
        o_ref[0, n] = jnp.where(rel < 0, MASK_VALUE, bias)


def _bias_tiles(rel_bias, t):
    return pl.pallas_call(
        functools.partial(_bias_kernel, t=t),
        grid=(HEADS,),
        in_specs=[pl.BlockSpec(memory_space=pltpu.SMEM)],
        out_specs=pl.BlockSpec((1, 2, t, 2 * t), lambda h: (h, 0, 0, 0)),
        out_shape=jax.ShapeDtypeStruct((HEADS, 2, t, 2 * t), F32),
        compiler_params=_params("arbitrary"),
        name="bias_tiles",
    )(rel_bias)


def _attn_kernel(q_ref, k_ref, v_ref, bias_ref, lam_ref, sub_ref, o_ref,
                 vt_sc, qt_sc, s_sc, m_sc, acc_sc, *, t, hp, lam_init):
    qi = pl.program_id(1)
    n_kv = vt_sc.shape[1]
    heads = range(hp)
    cols = [slice(hh * HEAD_W, (hh + 1) * HEAD_W) for hh in heads]

    @pl.when(qi == 0)
    def _():
        ones_tile = (lax.broadcasted_iota(jnp.int32, (V_ROWS - HEAD_W, t), 0) == 0).astype(BF16)

        def body(c, carry):
            r0 = pl.multiple_of(c * t, t)
            for hh in heads:
                vt_sc[hh, c, 0:HEAD_W] = v_ref[pl.ds(r0, t), cols[hh]].astype(F32).T.astype(BF16)
                vt_sc[hh, c, HEAD_W:V_ROWS] = ones_tile
            return carry
        lax.fori_loop(0, n_kv, body, 0)

    row = lax.broadcasted_iota(jnp.int32, (HEAD_W, t), 0)
    for hh in heads:
        qt = (q_ref[:, cols[hh]].astype(F32) * (DIFF_QK_DIM ** -0.5 * LOG2E)).T
        zero = jnp.zeros_like(qt)
        qt_sc[hh] = jnp.concatenate([jnp.where(row < DIFF_QK_DIM, qt, zero),
                                     jnp.where(row >= DIFF_QK_DIM, qt, zero)], axis=1).astype(BF16)

    m_sc[...] = jnp.full_like(m_sc, -jnp.inf)
    acc_sc[...] = jnp.zeros_like(acc_sc)

    def scores(c, slot, biased, which=heads):
        r0 = pl.multiple_of(c * t, t)
        for hh in which:
            s = _dot(k_ref[pl.ds(r0, t), cols[hh]], qt_sc[hh])
            if biased:
                s = s + bias_ref[hh, qi - c]
            s_sc[slot, hh] = s

    def consume(c, slot, which=heads):
        for hh in which:
            s = s_sc[slot, hh]
            m_old = m_sc[hh]
            m_new = jnp.maximum(m_old, jnp.max(s, axis=0, keepdims=True))
            p = jnp.exp2(s - m_new).astype(BF16)
            acc_sc[hh] = jnp.exp2(m_old - m_new) * acc_sc[hh] + _dot(vt_sc[hh, c], p)
            m_sc[hh] = m_new

    pl.when(qi >= 2)(lambda: scores(0, 0, False))
    pl.when(qi < 2)(lambda: scores(0, 0, True))

    n_pairs = jnp.maximum(qi - 2, 0) >> 1

    def pair_body(j, carry):
        c = 2 * j
        scores(c + 1, 1, False)
        for hh in heads:
            consume(c, 0, [hh])
            scores(c + 2, 0, False, [hh])
        consume(c + 1, 1)
        return carry
    lax.fori_loop(0, n_pairs, pair_body, 0)

    def body(c, carry):
        far_next = c + 1 <= qi - 2
        for slot in range(2):
            mine = (c & 1) == slot

            @pl.when(jnp.logical_and(mine, far_next))
            def _():
                scores(c + 1, 1 - slot, False)
                consume(c, slot)

            @pl.when(jnp.logical_and(mine, jnp.logical_not(far_next)))
            def _():
                scores(c + 1, 1 - slot, True)
                consume(c, slot)
        return carry
    lax.fori_loop(2 * n_pairs, qi, body, 0)

    for slot in range(2):
        pl.when((qi & 1) == slot)(functools.partial(consume, qi, slot))

    lp = lam_ref[...]
    lam = (jnp.exp(jnp.sum(lp[0:1] * lp[1:2], axis=1, keepdims=True))
           - jnp.exp(jnp.sum(lp[2:3] * lp[3:4], axis=1, keepdims=True)) + lam_init)
    for hh in heads:
        acc = acc_sc[hh]
        o = acc[0:HEAD_W] / acc[HEAD_W:HEAD_W + 1]
        od = o[:, :t] - lam * o[:, t:]
        ms = jnp.mean(od * od, axis=0, keepdims=True)
        y = od * lax.rsqrt(ms + LN_EPS) * sub_ref[...] * (1.0 - lam_init)
        o_ref[:, cols[hh]] = y.T.astype(BF16)


def _attn(proj, bias_tiles, lam_params, subln_col, *, q_blk, k_blk, v_blk, lam_init, t, hp=4):
    s = proj.shape[0]
    w = hp * HEAD_W
    assert q_blk % hp == 0 and k_blk % hp == 0 and v_blk % hp == 0
    once = pl.Buffered(1)
    return pl.pallas_call(
        functools.partial(_attn_kernel, t=t, hp=hp, lam_init=lam_init),
        grid=(HEADS // hp, s // t),
        in_specs=[pl.BlockSpec((t, w), lambda g, i: (i, q_blk // hp + g)),
                  pl.BlockSpec((s, w), lambda g, i: (0, k_blk // hp + g), pipeline_mode=once),
                  pl.BlockSpec((s, w), lambda g, i: (0, v_blk // hp + g), pipeline_mode=once),
                  pl.BlockSpec((hp, 2, t, 2 * t), lambda g, i: (g, 0, 0, 0), pipeline_mode=once),
                  pl.BlockSpec(lam_params.shape, lambda g, i: (0, 0)),
                  pl.BlockSpec((HEAD_W, 1), lambda g, i: (0, 0))],
        out_specs=pl.BlockSpec((t, w), lambda g, i: (i, g)),
        out_shape=jax.ShapeDtypeStruct((s, HEADS * HEAD_W), BF16),
        scratch_shapes=[pltpu.VMEM((hp, s // t, V_ROWS, t), BF16),
                        pltpu.VMEM((hp, HEAD_W, 2 * t), BF16),
                        pltpu.VMEM((2, hp, t, 2 * t), F32),
                        pltpu.VMEM((hp, 1, 2 * t), F32),
                        pltpu.VMEM((hp, V_ROWS, 2 * t), F32)],
        compiler_params=_params("arbitrary", "arbitrary"),
        name="diff_attn",
    )(proj, proj, proj, bias_tiles, lam_params, subln_col)


def _gates_kernel(x_ref, shift_ref, scale_ref, ws_ref, alog_ref, dtb_ref, o_ref):
    h = (x_ref[...] * (1.0 + scale_ref[...]) + shift_ref[...]).astype(BF16)
    tr = _dot(h.astype(F32), ws_ref[...].astype(F32)).T
    beta = jax.nn.sigmoid(tr[0:HEADS])
    x = tr[HEADS:2 * HEADS] + dtb_ref[...]
    softplus = jnp.maximum(x, 0.0) + jnp.log1p(jnp.exp(-jnp.abs(x)))
    g = -jnp.exp(alog_ref[...]) * softplus
    pos = lax.broadcasted_iota(jnp.int32, g.shape, 1) % DN_CHUNK
    shift = 1
    while shift < DN_CHUNK:
        g = g + jnp.where(pos >= shift, pltpu.roll(g, shift, axis=1), 0.0)
        shift *= 2
    o_ref[0:HEADS] = beta
    o_ref[HEADS:2 * HEADS] = g


def _gates(x, shift, scale, w_small, alog_col, dtb_col, *, tb=1024):
    s, d = x.shape
    col = pl.BlockSpec((HEADS, 1), lambda i: (0, 0))
    vec = pl.BlockSpec((1, d), lambda i: (0, 0))
    return pl.pallas_call(
        _gates_kernel,
        grid=(s // tb,),
        in_specs=[pl.BlockSpec((tb, d), lambda i: (i, 0)), vec, vec,
                  pl.BlockSpec(w_small.shape, lambda i: (0, 0)), col, col],
        out_specs=pl.BlockSpec((2 * HEADS, tb), lambda i: (0, i)),
        out_shape=jax.ShapeDtypeStruct((2 * HEADS, s), F32),
        compiler_params=_params("parallel"),
        name="gdn_gates",
    )(x, shift, scale, w_small, alog_col, dtb_col)


GROUP = 2 * DN_CHUNK


def _gdn_kernel(q_ref, k_ref, v_ref, z_ref, gt_ref, cwq_ref, cwk_ref, cwv_ref, nw_ref, o_ref,
                pad_sc, state_sc, o_sc, *, tb, hp):
    g = pl.program_id(0)
    ib = pl.program_id(1)
    halo = 8
    heads = range(hp)
    cols = [slice(hh * HEAD_W, (hh + 1) * HEAD_W) for hh in heads]

    @pl.when(ib == 0)
    def _():
        state_sc[...] = jnp.zeros_like(state_sc)
        pad_sc[:, 0:halo, :] = jnp.zeros((3 * hp, halo, HEAD_W), F32)

    def conv_silu(a, x_ref, cw_ref, hh):
        a = a * hp + hh
        pad_sc[a, halo:halo + tb, :] = x_ref[:, cols[hh]].astype(F32)
        cw = cw_ref[:, cols[hh]]
        y = cw[0:1] * pad_sc[a, halo - 3:halo - 3 + tb, :]
        for j in range(1, DN_CONV):
            y = y + cw[j:j + 1] * pad_sc[a, halo - 3 + j:halo - 3 + j + tb, :]
        pad_sc[a, 0:halo, :] = pad_sc[a, tb:tb + halo, :]
        return _silu(y)

    def l2n(x):
        return x * lax.rsqrt(jnp.sum(x * x, axis=-1, keepdims=True) + RMS_EPS)

    ii = lax.broadcasted_iota(jnp.int32, (GROUP, GROUP), 0)
    jj = lax.broadcasted_iota(jnp.int32, (GROUP, GROUP), 1)
    same = (ii >= DN_CHUNK) == (jj >= DN_CHUNK)
    tril = jnp.logical_and(same, ii >= jj)
    strict = jnp.logical_and(same, ii > jj)
    eye = (ii == jj).astype(F32)

    q, k, v, beta_c, cum_c, cum_t = [], [], [], [], [], []
    for hh in heads:
        q.append(l2n(conv_silu(0, q_ref, cwq_ref, hh)) * (HEAD_W ** -0.5))
        k.append(l2n(conv_silu(1, k_ref, cwk_ref, hh)))
        v.append(conv_silu(2, v_ref, cwv_ref, hh))
        head = g * hp + hh
        beta_t = jnp.broadcast_to(gt_ref[pl.ds(head, 1), :], (HEAD_W, tb))
        ct = jnp.broadcast_to(gt_ref[pl.ds(HEADS + head, 1), :], (HEAD_W, tb))
        beta_c.append(beta_t.T)
        cum_c.append(ct.T)
        cum_t.append(ct)

    groups = range(tb // GROUP)
    rows = [slice(r * GROUP, (r + 1) * GROUP) for r in groups]
    u, w, qk, qd, kg, cg = {}, {}, {}, {}, {}, {}

    def intra_chunk(r):
        kb, x, tinv, decay = {}, {}, {}, {}
        for hh in heads:
            sid = (r, hh)
            kg[sid], cg[sid] = k[hh][rows[r]], cum_c[hh][rows[r]]
            gdiff = cg[sid] - cum_t[hh][:, rows[r]]
            decay[hh] = jnp.where(tril, jnp.exp(jnp.where(tril, gdiff, 0.0)), 0.0)
            kb[hh] = kg[sid] * beta_c[hh][rows[r]]
            a = jnp.where(strict, _dot_nt(kb[hh].astype(BF16), kg[sid].astype(BF16)) * decay[hh], 0.0)
            x[hh] = a.astype(BF16)
            tinv[hh] = eye - a
        yield
        power = 2
        while power < DN_CHUNK:
            for hh in heads:
                x[hh] = _dot(x[hh], x[hh]).astype(BF16)
            yield
            for hh in heads:
                tinv[hh] = tinv[hh] + _dot(tinv[hh].astype(BF16), x[hh])
            yield
            power *= 2
        for hh in heads:
            sid = (r, hh)
            eg = jnp.exp(cg[sid])
            rhs = jnp.concatenate([v[hh][rows[r]] * beta_c[hh][rows[r]], kb[hh] * eg], axis=1).astype(BF16)
            uw = _dot(tinv[hh].astype(BF16), rhs)
            u[sid], w[sid] = uw[:, :HEAD_W], uw[:, HEAD_W:].astype(BF16)
            qg = q[hh][rows[r]]
            qk[sid] = (_dot_nt(qg.astype(BF16), kg[sid].astype(BF16)) * decay[hh]).astype(BF16)
            qd[sid] = (qg * eg).astype(BF16)
        yield

    def recurrence(r):
        v_new = {hh: [] for hh in heads}
        o_inter = {hh: [] for hh in heads}
        for c in range(2):
            cr = slice(c * DN_CHUNK, (c + 1) * DN_CHUNK)
            last = c * DN_CHUNK + DN_CHUNK - 1
            st, g_last = {}, {}
            for hh in heads:
                sid = (r, hh)
                g_last[hh] = cg[sid][last:last + 1, :]
                st[hh] = state_sc[hh]
                st16 = st[hh].astype(BF16)
                v_new[hh].append(u[sid][cr] - _dot(w[sid][cr], st16))
                o_inter[hh].append(_dot(qd[sid][cr], st16))
            yield
            for hh in heads:
                sid = (r, hh)
                kd = (kg[sid][cr] * jnp.exp(g_last[hh] - cg[sid][cr])).astype(BF16)
                state_sc[hh] = st[hh] * jnp.exp(g_last[hh]) + _dot_tn(kd, v_new[hh][c].astype(BF16))
            yield
        for hh in heads:
            vn_all = jnp.concatenate(v_new[hh], axis=0).astype(BF16)
            o_sc[hh, rows[r], :] = jnp.concatenate(o_inter[hh], axis=0) + _dot(qk[(r, hh)], vn_all)
        yield

    def interleave(slow, fast, ratio):
        slow_live = fast_live = True
        while slow_live or fast_live:
            if slow_live:
                slow_live = next(slow, "done") != "done"
            for _ in range(ratio):
                if fast_live:
                    fast_live = next(fast, "done") != "done"

    for _ in intra_chunk(0):
        pass
    for r in groups:
        nxt = intra_chunk(r + 1) if r + 1 < len(groups) else iter(())
        interleave(recurrence(r), nxt, 3)

    for hh in heads:
        o = o_sc[hh]
        on = o * lax.rsqrt(jnp.mean(o * o, axis=-1, keepdims=True) + RMS_EPS) * nw_ref[...]
        o_ref[:, cols[hh]] = (on * _silu(z_ref[:, cols[hh]].astype(F32))).astype(BF16)


def _gdn(proj, gates_t, conv_w, norm_w, *, q_blk, k_blk, v_blk, z_blk, tb=256, hp=8):
    s = proj.shape[0]
    w = hp * HEAD_W
    assert all(blk % hp == 0 for blk in (q_blk, k_blk, v_blk, z_blk))
    col = lambda blk: pl.BlockSpec((tb, w), lambda g, i: (i, blk // hp + g))
    cw = lambda blk: pl.BlockSpec((DN_CONV, w), lambda g, i: (0, blk // hp + g))
    return pl.pallas_call(
        functools.partial(_gdn_kernel, tb=tb, hp=hp),
        grid=(HEADS // hp, s // tb),
        in_specs=[col(q_blk), col(k_blk), col(v_blk), col(z_blk),
                  pl.BlockSpec((2 * HEADS, tb), lambda g, i: (0, i)),
                  cw(0), cw(HEADS), cw(2 * HEADS),
                  pl.BlockSpec((1, HEAD_W), lambda g, i: (0, 0))],
        out_specs=pl.BlockSpec((tb, w), lambda g, i: (i, g)),
        out_shape=jax.ShapeDtypeStruct((s, HEADS * HEAD_W), BF16),
        scratch_shapes=[pltpu.VMEM((3 * hp, tb + 8, HEAD_W), F32),
                        pltpu.VMEM((hp, HEAD_W, HEAD_W), F32),
                        pltpu.VMEM((hp, tb, HEAD_W), F32)],
        compiler_params=_params("arbitrary", "arbitrary"),
        name="gdn",
    )(proj, proj, proj, proj, gates_t, conv_w, conv_w, conv_w, norm_w)


def _merge_kernel(ya_ref, yb_ref, ga_ref, gb_ref, x_ref, gate_ref, lng_ref, lnb_ref,
                  wa_ref, wb_ref, wo_ref, o_ref, *, alpha):
    a = _dot(ya_ref[...], wa_ref[...])
    b = _dot(yb_ref[...], wb_ref[...])
    merged = (jax.nn.sigmoid(ga_ref[...].astype(F32)) * a
              + jax.nn.sigmoid(gb_ref[...].astype(F32)) * b).astype(BF16)
    y = _dot(merged, wo_ref[...])
    r = alpha * x_ref[...] + gate_ref[...] * y
    o_ref[...] = _layer_norm(r, lng_ref[...], lnb_ref[...])


def _merge(ya, yb, proj, x, gate, lng, lnb, w_a, w_b, w_o, *, ga_blk, gb_blk, alpha, tm=256):
    s, d = x.shape
    dv = ya.shape[1]
    vec = pl.BlockSpec((1, d), lambda i: (0, 0))
    const = lambda shape: pl.BlockSpec(shape, lambda i: (0, 0), pipeline_mode=pl.Buffered(1))
    return pl.pallas_call(
        functools.partial(_merge_kernel, alpha=alpha),
        grid=(s // tm,),
        in_specs=[pl.BlockSpec((tm, dv), lambda i: (i, 0)),
                  pl.BlockSpec((tm, dv), lambda i: (i, 0)),
                  pl.BlockSpec((tm, d), lambda i: (i, ga_blk)),
                  pl.BlockSpec((tm, d), lambda i: (i, gb_blk)),
                  pl.BlockSpec((tm, d), lambda i: (i, 0)), vec, vec, vec,
                  const(w_a.shape), const(w_b.shape), const(w_o.shape)],
        out_specs=pl.BlockSpec((tm, d), lambda i: (i, 0)),
        out_shape=jax.ShapeDtypeStruct((s, d), F32),
        compiler_params=_params("parallel"),
        name="merge",
    )(ya, yb, proj, proj, x, gate, lng, lnb, w_a, w_b, w_o)


def kernel(x, c, w_ada, b_ada, ln_g, ln_b, w_ffn_in, w_ffn_out, w_in, conv_w, dn_a_log, dn_dt_bias,
           dn_norm_w, diff_lambda, diff_subln_w, rel_bias, w_branch_a, w_branch_b, w_out):
    bsz, s, d = x.shape
    assert bsz == 1, "one sequence per call"
    depth = w_ada.shape[0]
    alpha = (2 * depth) ** 0.25
    hw = HEADS * HEAD_W
    attn_t = 256
    assert d % HEAD_W == 0 and s % 1024 == 0

    o_nb = 7 * hw
    o_ga = o_nb + 2 * HEADS
    nblk = d // HEAD_W
    blk = {"ga": 0, "gb": 1, "dq": 2 * nblk, "dk": 2 * nblk + HEADS, "dv": 2 * nblk + 2 * HEADS,
           "nq": 2 * nblk + 3 * HEADS, "nk": 2 * nblk + 4 * HEADS, "nv": 2 * nblk + 5 * HEADS,
           "nz": 2 * nblk + 6 * HEADS}

    bias_tiles = _bias_tiles(rel_bias, attn_t)
    wfi, wfo = w_ffn_in.astype(BF16), w_ffn_out.astype(BF16)
    x2 = x[0]
    for l in range(depth):
        lam_init = 0.8 - 0.6 * math.exp(-0.3 * l)
        ada = _ada(c.reshape(d, 1), w_ada[l], b_ada[l].reshape(1, -1)).reshape(N_SUB, 3, 1, d)
        shift, scale, gate = ada[:, 0], ada[:, 1], ada[:, 2]
        lng, lnb = ln_g[l].reshape(N_SUB, 1, d), ln_b[l].reshape(N_SUB, 1, d)
        w_gate = w_in[l, :, o_ga:].astype(BF16)
        w_small = jnp.pad(w_in[l, :, o_nb:o_ga], ((0, 0), (0, HEAD_W - 2 * HEADS))).astype(BF16)

        x2 = _ffn(x2, shift[0], scale[0], gate[0], lng[0], lnb[0], wfi, wfo, layer=l, which=0, alpha=alpha)

        proj = _proj(x2, shift[1], scale[1], w_gate, w_in, layer=l, n_head_cols=o_nb)
        ya = _attn(proj, bias_tiles, diff_lambda[l], diff_subln_w[l].reshape(HEAD_W, 1),
                   q_blk=blk["dq"], k_blk=blk["dk"], v_blk=blk["dv"], lam_init=lam_init, t=attn_t)
        gates_t = _gates(x2, shift[1], scale[1], w_small, dn_a_log[l].reshape(HEADS, 1),
                         dn_dt_bias[l].reshape(HEADS, 1))
        yb = _gdn(proj, gates_t, conv_w[l], dn_norm_w[l].reshape(1, HEAD_W),
                  q_blk=blk["nq"], k_blk=blk["nk"], v_blk=blk["nv"], z_blk=blk["nz"])
        x2 = _merge(ya, yb, proj, x2, gate[1], lng[1], lnb[1], w_branch_a[l].astype(BF16),
                    w_branch_b[l].astype(BF16), w_out[l].astype(BF16),
                    ga_blk=blk["ga"], gb_blk=blk["gb"], alpha=alpha)

        x2 = _ffn(x2, shift[2], scale[2], gate[2], lng[2], lnb[2], wfi, wfo, layer=l, which=1, alpha=alpha)
    return x2[None]
```

```python
import functools
import math

import numpy as np
import jax
import jax.numpy as jnp
from jax import lax
from jax.experimental import pallas as pl
from jax.experimental.pallas import tpu as pltpu

N_SUB = 3
HEADS = 8
HEAD_W = 128
DIFF_QK_DIM = 64
DN_CONV = 4
DN_CHUNK = 64
REL_BUCKETS = 32
REL_MAX_DIST = 128
LN_EPS = 1e-5
RMS_EPS = 1e-6
MASK_VALUE = -1e30
LOG2E = math.log2(math.e)
V_ROWS = HEAD_W + 16

F32 = jnp.float32
BF16 = jnp.bfloat16

V7X_VMEM_BYTES = 64 * 1024 * 1024
VMEM_LIMIT = V7X_VMEM_BYTES - 4 * 1024 * 1024


def _params(*sem):
    return pltpu.CompilerParams(dimension_semantics=sem, vmem_limit_bytes=VMEM_LIMIT)


def _tile(n, preferred):
    t = min(preferred, n)
    while n % t:
        t -= 128
    return t


def _silu(x):
    return x * jax.nn.sigmoid(x)


def _dot(a, b):
    return jnp.dot(a, b, preferred_element_type=F32)


def _dot_nt(a, b):
    return lax.dot_general(a, b, (((1,), (1,)), ((), ())), preferred_element_type=F32)


def _dot_tn(a, b):
    return lax.dot_general(a, b, (((0,), (0,)), ((), ())), preferred_element_type=F32)


def _layer_norm(r, g, b):
    mu = jnp.mean(r, axis=-1, keepdims=True)
    d = r - mu
    var = jnp.mean(d * d, axis=-1, keepdims=True)
    return d * lax.rsqrt(var + LN_EPS) * g + b


def _ada_kernel(c_ref, w_ref, b_ref, o_ref):
    sc = _silu(c_ref[...])
    o_ref[...] = jnp.sum(w_ref[...] * sc, axis=0, keepdims=True) + b_ref[...]


def _ada(c_col, w, b):
    d, n = w.shape
    tn = _tile(n, 1024)
    return pl.pallas_call(
        _ada_kernel,
        grid=(n // tn,),
        in_specs=[pl.BlockSpec((d, 1), lambda j: (0, 0)),
                  pl.BlockSpec((d, tn), lambda j: (0, j)),
                  pl.BlockSpec((1, tn), lambda j: (0, j))],
        out_specs=pl.BlockSpec((1, tn), lambda j: (0, j)),
        out_shape=jax.ShapeDtypeStruct((1, n), F32),
        compiler_params=_params("arbitrary"),
        name="ada",
    )(c_col, w, b)


def _ffn_kernel(x_ref, shift_ref, scale_ref, gate_ref, lng_ref, lnb_ref, wg_ref, wu_ref, wo_ref,
                o_ref, h_sc, a_sc, *, alpha, gate_mul):
    j = pl.program_id(1)
    n_chunks = pl.num_programs(1) - 1

    def drain(slot):
        return _dot(a_sc[slot], wo_ref[...])

    def activations(slot):
        h = h_sc[...]
        g = _dot(h, wg_ref[...])
        u = _dot(h, wu_ref[...])
        a_sc[slot] = (_silu(g) * u).astype(BF16)

    @pl.when(j == 0)
    def _():
        h_sc[...] = (x_ref[...] * (1.0 + scale_ref[...]) + shift_ref[...]).astype(BF16)
        o_ref[...] = jnp.zeros_like(o_ref)
        activations(0)

    for slot in range(2):
        @pl.when(jnp.logical_and(jnp.logical_and(j > 0, j < n_chunks), (j & 1) == slot))
        def _():
            o_ref[...] += drain(1 - slot)
            activations(slot)

        @pl.when(jnp.logical_and(j == n_chunks, ((j - 1) & 1) == slot))
        def _():
            r = alpha * x_ref[...] + (gate_mul * gate_ref[...]) * (o_ref[...] + drain(slot))
            o_ref[...] = _layer_norm(r, lng_ref[...], lnb_ref[...])


def _ffn(x, shift, scale, gate, lng, lnb, w_in, w_out, *, layer, which, alpha, tm=1024, tf=512):
    s, d = x.shape
    f = w_out.shape[2]
    tm, tf = _tile(s, tm), _tile(f, tf)
    nf = f // tf
    vec = pl.BlockSpec((1, d), lambda i, j: (0, 0))
    up = lambda j: jnp.minimum(j, nf - 1)
    down = lambda j: jnp.maximum(j - 1, 0)
    return pl.pallas_call(
        functools.partial(_ffn_kernel, alpha=alpha, gate_mul=0.5),
        grid=(s // tm, nf + 1),
        in_specs=[pl.BlockSpec((tm, d), lambda i, j: (i, 0)), vec, vec, vec, vec, vec,
                  pl.BlockSpec((None, None, d, tf), lambda i, j: (layer, which, 0, up(j))),
                  pl.BlockSpec((None, None, d, tf), lambda i, j: (layer, which, 0, up(j) + nf)),
                  pl.BlockSpec((None, None, tf, d), lambda i, j: (layer, which, down(j), 0))],
        out_specs=pl.BlockSpec((tm, d), lambda i, j: (i, 0)),
        out_shape=jax.ShapeDtypeStruct((s, d), F32),
        scratch_shapes=[pltpu.VMEM((tm, d), BF16), pltpu.VMEM((2, tm, tf), BF16)],
        compiler_params=_params("parallel", "arbitrary"),
        name="ffn",
    )(x, shift, scale, gate, lng, lnb, w_in, w_in, w_out)


def _proj_kernel(x_ref, shift_ref, scale_ref, wg_ref, wh_ref, o_ref, w_sc, *, n_gate):
    j = pl.program_id(0)
    first_row = pl.program_id(1) == 0

    @pl.when(jnp.logical_and(first_row, j < n_gate))
    def _():
        w_sc[...] = wg_ref[...]

    @pl.when(jnp.logical_and(first_row, j >= n_gate))
    def _():
        w_sc[...] = wh_ref[...].astype(BF16)

    h = (x_ref[...] * (1.0 + scale_ref[...]) + shift_ref[...]).astype(BF16)
    o_ref[...] = _dot_nt(h, w_sc[...]).astype(BF16)


def _proj(x, shift, scale, w_gate_t, w_all_t, *, layer, n_head_cols, tm=1024, tn=1024):
    s, d = x.shape
    ng_cols = w_gate_t.shape[0]
    tm = _tile(s, tm)
    vec = pl.BlockSpec((1, d), lambda j, i: (0, 0))
    tn = math.gcd(_tile(ng_cols, tn), _tile(n_head_cols, tn))
    n_gate, n_head = ng_cols // tn, n_head_cols // tn
    return pl.pallas_call(
        functools.partial(_proj_kernel, n_gate=n_gate),
        grid=(n_gate + n_head, s // tm),
        in_specs=[pl.BlockSpec((tm, d), lambda j, i: (i, 0)), vec, vec,
                  pl.BlockSpec((tn, d), lambda j, i: (jnp.minimum(j, n_gate - 1), 0)),
                  pl.BlockSpec((None, tn, d), lambda j, i: (layer, jnp.maximum(j - n_gate, 0), 0))],
        out_specs=pl.BlockSpec((tm, tn), lambda j, i: (i, j)),
        out_shape=jax.ShapeDtypeStruct((s, ng_cols + n_head_cols), BF16),
        scratch_shapes=[pltpu.VMEM((tn, d), BF16)],
        compiler_params=_params("arbitrary", "arbitrary"),
        name="proj",
    )(x, shift, scale, w_gate_t, w_all_t)


def _bucket_starts():
    n = np.arange(0, 2 * REL_MAX_DIST)
    max_exact = REL_BUCKETS // 2
    nf = np.maximum(n, max_exact).astype(np.float32)
    large = max_exact + (np.log(nf / np.float32(max_exact)) / np.float32(math.log(REL_MAX_DIST / max_exact))
                         * np.float32(REL_BUCKETS - max_exact)).astype(np.int32)
    bucket = np.where(n < max_exact, n, np.minimum(large, REL_BUCKETS - 1))
    assert np.all(np.diff(bucket) >= 0) and bucket[-1] == REL_BUCKETS - 1
    return [int(np.min(n[bucket >= b])) for b in range(REL_BUCKETS)]


BUCKET_STARTS = _bucket_starts()


def _bias_kernel(rb_ref, o_ref, *, t):
    h = pl.program_id(0)
    jj = lax.broadcasted_iota(jnp.int32, (t, 2 * t), 0)
    ii = lax.broadcasted_iota(jnp.int32, (t, 2 * t), 1)
    ii = jnp.where(ii >= t, ii - t, ii)
    far = rb_ref[REL_BUCKETS - 1, h]
    for n in range(2):
        rel = ii - jj + n * t
        bias = jnp.full((t, 2 * t), (rb_ref[0, h] - far) * LOG2E, F32)
        for b in range(1, REL_BUCKETS):
            bias = jnp.where(rel >= BUCKET_STARTS[b], (rb_ref[b, h] - far) * LOG2E, bias)
        o_ref[0, n] = jnp.where(rel < 0, MASK_VALUE, bias)


def _bias_tiles(rel_bias, t):
    return pl.pallas_call(
        functools.partial(_bias_kernel, t=t),
        grid=(HEADS,),
        in_specs=[pl.BlockSpec(memory_space=pltpu.SMEM)],
        out_specs=pl.BlockSpec((1, 2, t, 2 * t), lambda h: (h, 0, 0, 0)),
        out_shape=jax.ShapeDtypeStruct((HEADS, 2, t, 2 * t), F32),
        compiler_params=_params("arbitrary"),
        name="bias_tiles",
    )(rel_bias)


def _attn_kernel(q_ref, k_ref, v_ref, bias_ref, lam_ref, sub_ref, o_ref,
                 vt_sc, qt_sc, s_sc, m_sc, acc_sc, *, t, hp, lam_init):
    qi = pl.program_id(1)
    n_kv = vt_sc.shape[1]
    heads = range(hp)
    cols = [slice(hh * HEAD_W, (hh + 1) * HEAD_W) for hh in heads]

    @pl.when(qi == 0)
    def _():
        ones_tile = (lax.broadcasted_iota(jnp.int32, (V_ROWS - HEAD_W, t), 0) == 0).astype(BF16)

        def body(c, carry):
            r0 = pl.multiple_of(c * t, t)
            for hh in heads:
                vt_sc[hh, c, 0:HEAD_W] = v_ref[pl.ds(r0, t), cols[hh]].astype(F32).T.astype(BF16)
                vt_sc[hh, c, HEAD_W:V_ROWS] = ones_tile
            return carry
        lax.fori_loop(0, n_kv, body, 0)

    row = lax.broadcasted_iota(jnp.int32, (HEAD_W, t), 0)
    for hh in heads:
        qt = (q_ref[:, cols[hh]].astype(F32) * (DIFF_QK_DIM ** -0.5 * LOG2E)).T
        zero = jnp.zeros_like(qt)
        qt_sc[hh] = jnp.concatenate([jnp.where(row < DIFF_QK_DIM, qt, zero),
                                     jnp.where(row >= DIFF_QK_DIM, qt, zero)], axis=1).astype(BF16)

    m_sc[...] = jnp.full_like(m_sc, -jnp.inf)
    acc_sc[...] = jnp.zeros_like(acc_sc)

    def scores(c, slot, biased, which=heads):
        r0 = pl.multiple_of(c * t, t)
        for hh in which:
            s = _dot(k_ref[pl.ds(r0, t), cols[hh]], qt_sc[hh])
            if biased:
                s = s + bias_ref[hh, qi - c]
            s_sc[slot, hh] = s

    def consume(c, slot, which=heads):
        for hh in which:
            s = s_sc[slot, hh]
            m_old = m_sc[hh]
            m_new = jnp.maximum(m_old, jnp.max(s, axis=0, keepdims=True))
            p = jnp.exp2(s - m_new).astype(BF16)
            acc_sc[hh] = jnp.exp2(m_old - m_new) * acc_sc[hh] + _dot(vt_sc[hh, c], p)
            m_sc[hh] = m_new

    pl.when(qi >= 2)(lambda: scores(0, 0, False))
    pl.when(qi < 2)(lambda: scores(0, 0, True))

    n_pairs = jnp.maximum(qi - 2, 0) >> 1

    def pair_body(j, carry):
        c = 2 * j
        scores(c + 1, 1, False)
        for hh in heads:
            consume(c, 0, [hh])
            scores(c + 2, 0, False, [hh])
        consume(c + 1, 1)
        return carry
    lax.fori_loop(0, n_pairs, pair_body, 0)

    def body(c, carry):
        far_next = c + 1 <= qi - 2
        for slot in range(2):
            mine = (c & 1) == slot

            @pl.when(jnp.logical_and(mine, far_next))
            def _():
                scores(c + 1, 1 - slot, False)
                consume(c, slot)

            @pl.when(jnp.logical_and(mine, jnp.logical_not(far_next)))
            def _():
                scores(c + 1, 1 - slot, True)
                consume(c, slot)
        return carry
    lax.fori_loop(2 * n_pairs, qi, body, 0)

    for slot in range(2):
        pl.when((qi & 1) == slot)(functools.partial(consume, qi, slot))

    lp = lam_ref[...]
    lam = (jnp.exp(jnp.sum(lp[0:1] * lp[1:2], axis=1, keepdims=True))
           - jnp.exp(jnp.sum(lp[2:3] * lp[3:4], axis=1, keepdims=True)) + lam_init)
    for hh in heads:
        acc = acc_sc[hh]
        o = acc[0:HEAD_W] / acc[HEAD_W:HEAD_W + 1]
        od = o[:, :t] - lam * o[:, t:]
        ms = jnp.mean(od * od, axis=0, keepdims=True)
        y = od * lax.rsqrt(ms + LN_EPS) * sub_ref[...] * (1.0 - lam_init)
        o_ref[:, cols[hh]] = y.T.astype(BF16)


def _attn(proj, bias_tiles, lam_params, subln_col, *, q_blk, k_blk, v_blk, lam_init, t, hp=4):
    s = proj.shape[0]
    w = hp * HEAD_W
    assert q_blk % hp == 0 and k_blk % hp == 0 and v_blk % hp == 0
    once = pl.Buffered(1)
    return pl.pallas_call(
        functools.partial(_attn_kernel, t=t, hp=hp, lam_init=lam_init),
        grid=(HEADS // hp, s // t),
        in_specs=[pl.BlockSpec((t, w), lambda g, i: (i, q_blk // hp + g)),
                  pl.BlockSpec((s, w), lambda g, i: (0, k_blk // hp + g), pipeline_mode=once),
                  pl.BlockSpec((s, w), lambda g, i: (0, v_blk // hp + g), pipeline_mode=once),
                  pl.BlockSpec((hp, 2, t, 2 * t), lambda g, i: (g, 0, 0, 0), pipeline_mode=once),
                  pl.BlockSpec(lam_params.shape, lambda g, i: (0, 0)),
                  pl.BlockSpec((HEAD_W, 1), lambda g, i: (0, 0))],
        out_specs=pl.BlockSpec((t, w), lambda g, i: (i, g)),
        out_shape=jax.ShapeDtypeStruct((s, HEADS * HEAD_W), BF16),
        scratch_shapes=[pltpu.VMEM((hp, s // t, V_ROWS, t), BF16),
                        pltpu.VMEM((hp, HEAD_W, 2 * t), BF16),
                        pltpu.VMEM((2, hp, t, 2 * t), F32),
                        pltpu.VMEM((hp, 1, 2 * t), F32),
                        pltpu.VMEM((hp, V_ROWS, 2 * t), F32)],
        compiler_params=_params("arbitrary", "arbitrary"),
        name="diff_attn",
    )(proj, proj, proj, bias_tiles, lam_params, subln_col)


def _gates_kernel(x_ref, shift_ref, scale_ref, ws_ref, alog_ref, dtb_ref, o_ref):
    h = (x_ref[...] * (1.0 + scale_ref[...]) + shift_ref[...]).astype(BF16)
    tr = _dot_nt(ws_ref[...], h.astype(F32))
    beta = jax.nn.sigmoid(tr[0:HEADS])
    x = tr[HEADS:2 * HEADS] + dtb_ref[...]
    softplus = jnp.maximum(x, 0.0) + jnp.log1p(jnp.exp(-jnp.abs(x)))
    g = -jnp.exp(alog_ref[...]) * softplus
    pos = lax.broadcasted_iota(jnp.int32, g.shape, 1) % DN_CHUNK
    shift = 1
    while shift < DN_CHUNK:
        g = g + jnp.where(pos >= shift, pltpu.roll(g, shift, axis=1), 0.0)
        shift *= 2
    o_ref[0:HEADS] = beta
    o_ref[HEADS:2 * HEADS] = g


def _gates(x, shift, scale, w_small_t, alog_col, dtb_col, *, tb=1024):
    s, d = x.shape
    col = pl.BlockSpec((HEADS, 1), lambda i: (0, 0))
    vec = pl.BlockSpec((1, d), lambda i: (0, 0))
    return pl.pallas_call(
        _gates_kernel,
        grid=(s // tb,),
        in_specs=[pl.BlockSpec((tb, d), lambda i: (i, 0)), vec, vec,
                  pl.BlockSpec(w_small_t.shape, lambda i: (0, 0)), col, col],
        out_specs=pl.BlockSpec((2 * HEADS, tb), lambda i: (0, i)),
        out_shape=jax.ShapeDtypeStruct((2 * HEADS, s), F32),
        compiler_params=_params("parallel"),
        name="gdn_gates",
    )(x, shift, scale, w_small_t, alog_col, dtb_col)


GROUP = 2 * DN_CHUNK


def _gdn_kernel(q_ref, k_ref, v_ref, z_ref, gt_ref, cwq_ref, cwk_ref, cwv_ref, nw_ref, o_ref,
                pad_sc, state_sc, o_sc, *, tb, hp):
    g = pl.program_id(0)
    ib = pl.program_id(1)
    halo = 8
    heads = range(hp)
    cols = [slice(hh * HEAD_W, (hh + 1) * HEAD_W) for hh in heads]

    @pl.when(ib == 0)
    def _():
        state_sc[...] = jnp.zeros_like(state_sc)
        pad_sc[:, 0:halo, :] = jnp.zeros((3 * hp, halo, HEAD_W), F32)

    def conv_silu(a, x_ref, cw_ref, hh):
        a = a * hp + hh
        pad_sc[a, halo:halo + tb, :] = x_ref[:, cols[hh]].astype(F32)
        cw = cw_ref[:, cols[hh]]
        y = cw[0:1] * pad_sc[a, halo - 3:halo - 3 + tb, :]
        for j in range(1, DN_CONV):
            y = y + cw[j:j + 1] * pad_sc[a, halo - 3 + j:halo - 3 + j + tb, :]
        pad_sc[a, 0:halo, :] = pad_sc[a, tb:tb + halo, :]
        return _silu(y)

    def l2n(x):
        return x * lax.rsqrt(jnp.sum(x * x, axis=-1, keepdims=True) + RMS_EPS)

    ii = lax.broadcasted_iota(jnp.int32, (GROUP, GROUP), 0)
    jj = lax.broadcasted_iota(jnp.int32, (GROUP, GROUP), 1)
    same = (ii >= DN_CHUNK) == (jj >= DN_CHUNK)
    tril = jnp.logical_and(same, ii >= jj)
    strict = jnp.logical_and(same, ii > jj)
    eye = (ii == jj).astype(F32)

    q, k, v, beta_c, cum_c, cum_t = [], [], [], [], [], []
    for hh in heads:
        q.append(l2n(conv_silu(0, q_ref, cwq_ref, hh)) * (HEAD_W ** -0.5))
        k.append(l2n(conv_silu(1, k_ref, cwk_ref, hh)))
        v.append(conv_silu(2, v_ref, cwv_ref, hh))
        head = g * hp + hh
        beta_t = jnp.broadcast_to(gt_ref[pl.ds(head, 1), :], (HEAD_W, tb))
        ct = jnp.broadcast_to(gt_ref[pl.ds(HEADS + head, 1), :], (HEAD_W, tb))
        beta_c.append(beta_t.T)
        cum_c.append(ct.T)
        cum_t.append(ct)

    groups = range(tb // GROUP)
    rows = [slice(r * GROUP, (r + 1) * GROUP) for r in groups]
    u, w, qk, qd, kg, cg = {}, {}, {}, {}, {}, {}

    def intra_chunk(r):
        kb, x, tinv, decay = {}, {}, {}, {}
        for hh in heads:
            sid = (r, hh)
            kg[sid], cg[sid] = k[hh][rows[r]], cum_c[hh][rows[r]]
            gdiff = cg[sid] - cum_t[hh][:, rows[r]]
            decay[hh] = jnp.where(tril, jnp.exp(jnp.where(tril, gdiff, 0.0)), 0.0)
            kb[hh] = kg[sid] * beta_c[hh][rows[r]]
            a = jnp.where(strict, _dot_nt(kb[hh].astype(BF16), kg[sid].astype(BF16)) * decay[hh], 0.0)
            x[hh] = a.astype(BF16)
            tinv[hh] = eye - a
        yield
        power = 2
        while power < DN_CHUNK:
            for hh in heads:
                x[hh] = _dot(x[hh], x[hh]).astype(BF16)
            yield
            for hh in heads:
                tinv[hh] = tinv[hh] + _dot(tinv[hh].astype(BF16), x[hh])
            yield
            power *= 2
        for hh in heads:
            sid = (r, hh)
            eg = jnp.exp(cg[sid])
            rhs = jnp.concatenate([v[hh][rows[r]] * beta_c[hh][rows[r]], kb[hh] * eg], axis=1).astype(BF16)
            uw = _dot(tinv[hh].astype(BF16), rhs)
            u[sid], w[sid] = uw[:, :HEAD_W], uw[:, HEAD_W:].astype(BF16)
            qg = q[hh][rows[r]]
            qk[sid] = (_dot_nt(qg.astype(BF16), kg[sid].astype(BF16)) * decay[hh]).astype(BF16)
            qd[sid] = (qg * eg).astype(BF16)
        yield

    def recurrence(r):
        v_new = {hh: [] for hh in heads}
        o_inter = {hh: [] for hh in heads}
        for c in range(2):
            cr = slice(c * DN_CHUNK, (c + 1) * DN_CHUNK)
            last = c * DN_CHUNK + DN_CHUNK - 1
            st, g_last = {}, {}
            for hh in heads:
                sid = (r, hh)
                g_last[hh] = cg[sid][last:last + 1, :]
                st[hh] = state_sc[hh]
                st16 = st[hh].astype(BF16)
                v_new[hh].append(u[sid][cr] - _dot(w[sid][cr], st16))
                o_inter[hh].append(_dot(qd[sid][cr], st16))
            yield
            for hh in heads:
                sid = (r, hh)
                kd = (kg[sid][cr] * jnp.exp(g_last[hh] - cg[sid][cr])).astype(BF16)
                state_sc[hh] = st[hh] * jnp.exp(g_last[hh]) + _dot_tn(kd, v_new[hh][c].astype(BF16))
            yield
        for hh in heads:
            vn_all = jnp.concatenate(v_new[hh], axis=0).astype(BF16)
            o_sc[hh, rows[r], :] = jnp.concatenate(o_inter[hh], axis=0) + _dot(qk[(r, hh)], vn_all)
        yield

    def interleave(slow, fast, ratio):
        slow_live = fast_live = True
        while slow_live or fast_live:
            if slow_live:
                slow_live = next(slow, "done") != "done"
            for _ in range(ratio):
                if fast_live:
                    fast_live = next(fast, "done") != "done"

    for _ in intra_chunk(0):
        pass
    for r in groups:
        nxt = intra_chunk(r + 1) if r + 1 < len(groups) else iter(())
        interleave(recurrence(r), nxt, 3)

    for hh in heads:
        o = o_sc[hh]
        on = o * lax.rsqrt(jnp.mean(o * o, axis=-1, keepdims=True) + RMS_EPS) * nw_ref[...]
        o_ref[:, cols[hh]] = (on * _silu(z_ref[:, cols[hh]].astype(F32))).astype(BF16)


def _gdn(proj, gates_t, conv_w, norm_w, *, q_blk, k_blk, v_blk, z_blk, tb=256, hp=8):
    s = proj.shape[0]
    w = hp * HEAD_W
    assert all(blk % hp == 0 for blk in (q_blk, k_blk, v_blk, z_blk))
    col = lambda blk: pl.BlockSpec((tb, w), lambda g, i: (i, blk // hp + g))
    cw = lambda blk: pl.BlockSpec((DN_CONV, w), lambda g, i: (0, blk // hp + g))
    return pl.pallas_call(
        functools.partial(_gdn_kernel, tb=tb, hp=hp),
        grid=(HEADS // hp, s // tb),
        in_specs=[col(q_blk), col(k_blk), col(v_blk), col(z_blk),
                  pl.BlockSpec((2 * HEADS, tb), lambda g, i: (0, i)),
                  cw(0), cw(HEADS), cw(2 * HEADS),
                  pl.BlockSpec((1, HEAD_W), lambda g, i: (0, 0))],
        out_specs=pl.BlockSpec((tb, w), lambda g, i: (i, g)),
        out_shape=jax.ShapeDtypeStruct((s, HEADS * HEAD_W), BF16),
        scratch_shapes=[pltpu.VMEM((3 * hp, tb + 8, HEAD_W), F32),
                        pltpu.VMEM((hp, HEAD_W, HEAD_W), F32),
                        pltpu.VMEM((hp, tb, HEAD_W), F32)],
        compiler_params=_params("arbitrary", "arbitrary"),
        name="gdn",
    )(proj, proj, proj, proj, gates_t, conv_w, conv_w, conv_w, norm_w)


def _merge_kernel(ya_ref, yb_ref, ga_ref, gb_ref, x_ref, gate_ref, lng_ref, lnb_ref,
                  wa_ref, wb_ref, wo_ref, o_ref, *, alpha):
    a = _dot(ya_ref[...], wa_ref[...])
    b = _dot(yb_ref[...], wb_ref[...])
    merged = (jax.nn.sigmoid(ga_ref[...].astype(F32)) * a
              + jax.nn.sigmoid(gb_ref[...].astype(F32)) * b).astype(BF16)
    y = _dot(merged, wo_ref[...])
    r = alpha * x_ref[...] + gate_ref[...] * y
    o_ref[...] = _layer_norm(r, lng_ref[...], lnb_ref[...])


def _merge(ya, yb, proj, x, gate, lng, lnb, w_a, w_b, w_o, *, ga_blk, gb_blk, alpha, tm=256):
    s, d = x.shape
    dv = ya.shape[1]
    vec = pl.BlockSpec((1, d), lambda i: (0, 0))
    const = lambda shape: pl.BlockSpec(shape, lambda i: (0, 0), pipeline_mode=pl.Buffered(1))
    return pl.pallas_call(
        functools.partial(_merge_kernel, alpha=alpha),
        grid=(s // tm,),
        in_specs=[pl.BlockSpec((tm, dv), lambda i: (i, 0)),
                  pl.BlockSpec((tm, dv), lambda i: (i, 0)),
                  pl.BlockSpec((tm, d), lambda i: (i, ga_blk)),
                  pl.BlockSpec((tm, d), lambda i: (i, gb_blk)),
                  pl.BlockSpec((tm, d), lambda i: (i, 0)), vec, vec, vec,
                  const(w_a.shape), const(w_b.shape), const(w_o.shape)],
        out_specs=pl.BlockSpec((tm, d), lambda i: (i, 0)),
        out_shape=jax.ShapeDtypeStruct((s, d), F32),
        compiler_params=_params("parallel"),
        name="merge",
    )(ya, yb, proj, proj, x, gate, lng, lnb, w_a, w_b, w_o)


def kernel(x, c, w_ada, b_ada, ln_g, ln_b, w_ffn_in, w_ffn_out, w_in, conv_w, dn_a_log, dn_dt_bias,
           dn_norm_w, diff_lambda, diff_subln_w, rel_bias, w_branch_a, w_branch_b, w_out):
    bsz, s, d = x.shape
    assert bsz == 1, "one sequence per call"
    depth = w_ada.shape[0]
    alpha = (2 * depth) ** 0.25
    hw = HEADS * HEAD_W
    attn_t = 256
    assert d % HEAD_W == 0 and s % 1024 == 0

    o_nb = 7 * hw
    o_ga = o_nb + 2 * HEADS
    nblk = d // HEAD_W
    blk = {"ga": 0, "gb": 1, "dq": 2 * nblk, "dk": 2 * nblk + HEADS, "dv": 2 * nblk + 2 * HEADS,
           "nq": 2 * nblk + 3 * HEADS, "nk": 2 * nblk + 4 * HEADS, "nv": 2 * nblk + 5 * HEADS,
           "nz": 2 * nblk + 6 * HEADS}

    bias_tiles = _bias_tiles(rel_bias, attn_t)
    wfi, wfo = w_ffn_in.astype(BF16), w_ffn_out.astype(BF16)
    w_in_t = jnp.swapaxes(w_in, 1, 2)
    x2 = x[0]
    for l in range(depth):
        lam_init = 0.8 - 0.6 * math.exp(-0.3 * l)
        ada = _ada(c.reshape(d, 1), w_ada[l], b_ada[l].reshape(1, -1)).reshape(N_SUB, 3, 1, d)
        shift, scale, gate = ada[:, 0], ada[:, 1], ada[:, 2]
        lng, lnb = ln_g[l].reshape(N_SUB, 1, d), ln_b[l].reshape(N_SUB, 1, d)
        w_gate_t = w_in_t[l, o_ga:].astype(BF16)
        w_small_t = w_in_t[l, o_nb:o_ga]

        x2 = _ffn(x2, shift[0], scale[0], gate[0], lng[0], lnb[0], wfi, wfo, layer=l, which=0, alpha=alpha)

        proj = _proj(x2, shift[1], scale[1], w_gate_t, w_in_t, layer=l, n_head_cols=o_nb)
        ya = _attn(proj, bias_tiles, diff_lambda[l], diff_subln_w[l].reshape(HEAD_W, 1),
                   q_blk=blk["dq"], k_blk=blk["dk"], v_blk=blk["dv"], lam_init=lam_init, t=attn_t)
        gates_t = _gates(x2, shift[1], scale[1], w_small_t, dn_a_log[l].reshape(HEADS, 1),
                         dn_dt_bias[l].reshape(HEADS, 1))
        yb = _gdn(proj, gates_t, conv_w[l], dn_norm_w[l].reshape(1, HEAD_W),
                  q_blk=blk["nq"], k_blk=blk["nk"], v_blk=blk["nv"], z_blk=blk["nz"])
        x2 = _merge(ya, yb, proj, x2, gate[1], lng[1], lnb[1], w_branch_a[l].astype(BF16),
                    w_branch_b[l].astype(BF16), w_out[l].astype(BF16),
                    ga_blk=blk["ga"], gb_blk=blk["gb"], alpha=alpha)

        x2 = _ffn(x2, shift[2], scale[2], gate[2], lng[2], lnb[2], wfi, wfo, layer=l, which=1, alpha=alpha)
    return x2[None]
```

```python
import functools
import math

import numpy as np
import jax
import jax.numpy as jnp
from jax import lax
from jax.experimental import pallas as pl
from jax.experimental.pallas import tpu as pltpu

N_SUB = 3
HEADS = 8
HEAD_W = 128
DIFF_QK_DIM = 64
DN_CONV = 4
DN_CHUNK = 64
REL_BUCKETS = 32
REL_MAX_DIST = 128
LN_EPS = 1e-5
RMS_EPS = 1e-6
MASK_VALUE = -1e30
LOG2E = math.log2(math.e)
V_ROWS = HEAD_W + 16

F32 = jnp.float32
BF16 = jnp.bfloat16

V7X_VMEM_BYTES = 64 * 1024 * 1024
VMEM_LIMIT = V7X_VMEM_BYTES - 4 * 1024 * 1024


def _params(*sem):
    return pltpu.CompilerParams(dimension_semantics=sem, vmem_limit_bytes=VMEM_LIMIT)


def _tile(n, preferred):
    t = min(preferred, n)
    while n % t:
        t -= 128
    return t


def _silu(x):
    return x * jax.nn.sigmoid(x)


def _dot(a, b):
    return jnp.dot(a, b, preferred_element_type=F32)


def _dot_nt(a, b):
    return lax.dot_general(a, b, (((1,), (1,)), ((), ())), preferred_element_type=F32)


def _dot_tn(a, b):
    return lax.dot_general(a, b, (((0,), (0,)), ((), ())), preferred_element_type=F32)


def _layer_norm(r, g, b):
    mu = jnp.mean(r, axis=-1, keepdims=True)
    d = r - mu
    var = jnp.mean(d * d, axis=-1, keepdims=True)
    return d * lax.rsqrt(var + LN_EPS) * g + b


def _ada_kernel(c_ref, w_ref, b_ref, o_ref):
    sc = _silu(c_ref[...])
    o_ref[...] = jnp.sum(w_ref[...] * sc, axis=0, keepdims=True) + b_ref[...]


def _ada(c_col, w, b):
    d, n = w.shape
    tn = _tile(n, 1024)
    return pl.pallas_call(
        _ada_kernel,
        grid=(n // tn,),
        in_specs=[pl.BlockSpec((d, 1), lambda j: (0, 0)),
                  pl.BlockSpec((d, tn), lambda j: (0, j)),
                  pl.BlockSpec((1, tn), lambda j: (0, j))],
        out_specs=pl.BlockSpec((1, tn), lambda j: (0, j)),
        out_shape=jax.ShapeDtypeStruct((1, n), F32),
        compiler_params=_params("arbitrary"),
        name="ada",
    )(c_col, w, b)


def _ffn_kernel(x_ref, shift_ref, scale_ref, gate_ref, lng_ref, lnb_ref, wg_ref, wu_ref, wo_ref,
                o_ref, h_sc, a_sc, *, alpha, gate_mul):
    j = pl.program_id(1)
    n_chunks = pl.num_programs(1) - 1

    def drain(slot):
        return _dot(a_sc[slot], wo_ref[...])

    def activations(slot):
        h = h_sc[...]
        g = _dot(h, wg_ref[...])
        u = _dot(h, wu_ref[...])
        a_sc[slot] = (_silu(g) * u).astype(BF16)

    @pl.when(j == 0)
    def _():
        h_sc[...] = (x_ref[...] * (1.0 + scale_ref[...]) + shift_ref[...]).astype(BF16)
        o_ref[...] = jnp.zeros_like(o_ref)
        activations(0)

    for slot in range(2):
        @pl.when(jnp.logical_and(jnp.logical_and(j > 0, j < n_chunks), (j & 1) == slot))
        def _():
            o_ref[...] += drain(1 - slot)
            activations(slot)

        @pl.when(jnp.logical_and(j == n_chunks, ((j - 1) & 1) == slot))
        def _():
            r = alpha * x_ref[...] + (gate_mul * gate_ref[...]) * (o_ref[...] + drain(slot))
            o_ref[...] = _layer_norm(r, lng_ref[...], lnb_ref[...])


def _ffn(x, shift, scale, gate, lng, lnb, w_in, w_out, *, layer, which, alpha, tm=1024, tf=512):
    s, d = x.shape
    f = w_out.shape[2]
    tm, tf = _tile(s, tm), _tile(f, tf)
    nf = f // tf
    vec = pl.BlockSpec((1, d), lambda i, j: (0, 0))
    up = lambda j: jnp.minimum(j, nf - 1)
    down = lambda j: jnp.maximum(j - 1, 0)
    return pl.pallas_call(
        functools.partial(_ffn_kernel, alpha=alpha, gate_mul=0.5),
        grid=(s // tm, nf + 1),
        in_specs=[pl.BlockSpec((tm, d), lambda i, j: (i, 0)), vec, vec, vec, vec, vec,
                  pl.BlockSpec((None, None, d, tf), lambda i, j: (layer, which, 0, up(j))),
                  pl.BlockSpec((None, None, d, tf), lambda i, j: (layer, which, 0, up(j) + nf)),
                  pl.BlockSpec((None, None, tf, d), lambda i, j: (layer, which, down(j), 0))],
        out_specs=pl.BlockSpec((tm, d), lambda i, j: (i, 0)),
        out_shape=jax.ShapeDtypeStruct((s, d), F32),
        scratch_shapes=[pltpu.VMEM((tm, d), BF16), pltpu.VMEM((2, tm, tf), BF16)],
        compiler_params=_params("parallel", "arbitrary"),
        name="ffn",
    )(x, shift, scale, gate, lng, lnb, w_in, w_in, w_out)


def _proj_kernel(x_ref, shift_ref, scale_ref, wg_ref, wh_ref, o_ref, w_sc, *, n_gate):
    j = pl.program_id(0)
    first_row = pl.program_id(1) == 0

    @pl.when(jnp.logical_and(first_row, j < n_gate))
    def _():
        w_sc[...] = wg_ref[...]

    @pl.when(jnp.logical_and(first_row, j >= n_gate))
    def _():
        w_sc[...] = wh_ref[...].astype(BF16)

    h = (x_ref[...] * (1.0 + scale_ref[...]) + shift_ref[...]).astype(BF16)
    o_ref[...] = _dot_nt(h, w_sc[...]).astype(BF16)


def _proj(x, shift, scale, w_gate_t, w_all_t, *, layer, n_head_cols, tm=1024, tn=1024):
    s, d = x.shape
    ng_cols = w_gate_t.shape[0]
    tm = _tile(s, tm)
    vec = pl.BlockSpec((1, d), lambda j, i: (0, 0))
    tn = math.gcd(_tile(ng_cols, tn), _tile(n_head_cols, tn))
    n_gate, n_head = ng_cols // tn, n_head_cols // tn
    return pl.pallas_call(
        functools.partial(_proj_kernel, n_gate=n_gate),
        grid=(n_gate + n_head, s // tm),
        in_specs=[pl.BlockSpec((tm, d), lambda j, i: (i, 0)), vec, vec,
                  pl.BlockSpec((tn, d), lambda j, i: (jnp.minimum(j, n_gate - 1), 0)),
                  pl.BlockSpec((None, tn, d), lambda j, i: (layer, jnp.maximum(j - n_gate, 0), 0))],
        out_specs=pl.BlockSpec((tm, tn), lambda j, i: (i, j)),
        out_shape=jax.ShapeDtypeStruct((s, ng_cols + n_head_cols), BF16),
        scratch_shapes=[pltpu.VMEM((tn, d), BF16)],
        compiler_params=_params("arbitrary", "arbitrary"),
        name="proj",
    )(x, shift, scale, w_gate_t, w_all_t)


def _bucket_starts():
    n = np.arange(0, 2 * REL_MAX_DIST)
    max_exact = REL_BUCKETS // 2
    nf = np.maximum(n, max_exact).astype(np.float32)
    large = max_exact + (np.log(nf / np.float32(max_exact)) / np.float32(math.log(REL_MAX_DIST / max_exact))
                         * np.float32(REL_BUCKETS - max_exact)).astype(np.int32)
    bucket = np.where(n < max_exact, n, np.minimum(large, REL_BUCKETS - 1))
    assert np.all(np.diff(bucket) >= 0) and bucket[-1] == REL_BUCKETS - 1
    return [int(np.min(n[bucket >= b])) for b in range(REL_BUCKETS)]


BUCKET_STARTS = _bucket_starts()


def _bias_kernel(rb_ref, o_ref, *, t):
    h = pl.program_id(0)
    jj = lax.broadcasted_iota(jnp.int32, (t, 2 * t), 0)
    ii = lax.broadcasted_iota(jnp.int32, (t, 2 * t), 1)
    ii = jnp.where(ii >= t, ii - t, ii)
    far = rb_ref[REL_BUCKETS - 1, h]
    for n in range(2):
        rel = ii - jj + n * t
        bias = jnp.full((t, 2 * t), (rb_ref[0, h] - far) * LOG2E, F32)
        for b in range(1, REL_BUCKETS):
            bias = jnp.where(rel >= BUCKET_STARTS[b], (rb_ref[b, h] - far) * LOG2E, bias)
        o_ref[0, n] = jnp.where(rel < 0, MASK_VALUE, bias)


def _bias_tiles(rel_bias, t):
    return pl.pallas_call(
        functools.partial(_bias_kernel, t=t),
        grid=(HEADS,),
        in_specs=[pl.BlockSpec(memory_space=pltpu.SMEM)],
        out_specs=pl.BlockSpec((1, 2, t, 2 * t), lambda h: (h, 0, 0, 0)),
        out_shape=jax.ShapeDtypeStruct((HEADS, 2, t, 2 * t), F32),
        compiler_params=_params("arbitrary"),
        name="bias_tiles",
    )(rel_bias)


def _attn_kernel(q_ref, k_ref, v_ref, bias_ref, lam_ref, sub_ref, o_ref,
                 vt_sc, qt_sc, s_sc, m_sc, acc_sc, *, t, hp, lam_init):
    qi = pl.program_id(1)
    n_kv = vt_sc.shape[1]
    heads = range(hp)
    cols = [slice(hh * HEAD_W, (hh + 1) * HEAD_W) for hh in heads]

    @pl.when(qi == 0)
    def _():
        ones_tile = (lax.broadcasted_iota(jnp.int32, (V_ROWS - HEAD_W, t), 0) == 0).astype(BF16)

        def body(c, carry):
            r0 = pl.multiple_of(c * t, t)
            for hh in heads:
                vt_sc[hh, c, 0:HEAD_W] = v_ref[pl.ds(r0, t), cols[hh]].astype(F32).T.astype(BF16)
                vt_sc[hh, c, HEAD_W:V_ROWS] = ones_tile
            return carry
        lax.fori_loop(0, n_kv, body, 0)

    row = lax.broadcasted_iota(jnp.int32, (HEAD_W, t), 0)
    for hh in heads:
        qt = (q_ref[:, cols[hh]].astype(F32) * (DIFF_QK_DIM ** -0.5 * LOG2E)).T
        zero = jnp.zeros_like(qt)
        qt_sc[hh] = jnp.concatenate([jnp.where(row < DIFF_QK_DIM, qt, zero),
                                     jnp.where(row >= DIFF_QK_DIM, qt, zero)], axis=1).astype(BF16)

    m_sc[...] = jnp.full_like(m_sc, -jnp.inf)
    acc_sc[...] = jnp.zeros_like(acc_sc)

    def scores(c, slot, biased, which=heads):
        r0 = pl.multiple_of(c * t, t)
        for hh in which:
            s = _dot(k_ref[pl.ds(r0, t), cols[hh]], qt_sc[hh])
            if biased:
                s = s + bias_ref[hh, qi - c]
            s_sc[slot, hh] = s

    def consume(c, slot, which=heads):
        for hh in which:
            s = s_sc[slot, hh]
            m_old = m_sc[hh]
            m_new = jnp.maximum(m_old, jnp.max(s, axis=0, keepdims=True))
            p = jnp.exp2(s - m_new).astype(BF16)
            acc_sc[hh] = jnp.exp2(m_old - m_new) * acc_sc[hh] + _dot(vt_sc[hh, c], p)
            m_sc[hh] = m_new

    pl.when(qi >= 2)(lambda: scores(0, 0, False))
    pl.when(qi < 2)(lambda: scores(0, 0, True))

    n_pairs = jnp.maximum(qi - 2, 0) >> 1

    def pair_body(j, carry):
        c = 2 * j
        scores(c + 1, 1, False)
        for hh in heads:
            consume(c, 0, [hh])
            scores(c + 2, 0, False, [hh])
        consume(c + 1, 1)
        return carry
    lax.fori_loop(0, n_pairs, pair_body, 0)

    def body(c, carry):
        far_next = c + 1 <= qi - 2
        for slot in range(2):
            mine = (c & 1) == slot

            @pl.when(jnp.logical_and(mine, far_next))
            def _():
                scores(c + 1, 1 - slot, False)
                consume(c, slot)

            @pl.when(jnp.logical_and(mine, jnp.logical_not(far_next)))
            def _():
                scores(c + 1, 1 - slot, True)
                consume(c, slot)
        return carry
    lax.fori_loop(2 * n_pairs, qi, body, 0)

    for slot in range(2):
        pl.when((qi & 1) == slot)(functools.partial(consume, qi, slot))

    lp = lam_ref[...]
    lam = (jnp.exp(jnp.sum(lp[0:1] * lp[1:2], axis=1, keepdims=True))
           - jnp.exp(jnp.sum(lp[2:3] * lp[3:4], axis=1, keepdims=True)) + lam_init)
    for hh in heads:
        acc = acc_sc[hh]
        o = acc[0:HEAD_W] / acc[HEAD_W:HEAD_W + 1]
        od = o[:, :t] - lam * o[:, t:]
        ms = jnp.mean(od * od, axis=0, keepdims=True)
        y = od * lax.rsqrt(ms + LN_EPS) * sub_ref[...] * (1.0 - lam_init)
        o_ref[:, cols[hh]] = y.T.astype(BF16)


def _attn(proj, bias_tiles, lam_params, subln_col, *, q_blk, k_blk, v_blk, lam_init, t, hp=4):
    s = proj.shape[0]
    w = hp * HEAD_W
    assert q_blk % hp == 0 and k_blk % hp == 0 and v_blk % hp == 0
    once = pl.Buffered(1)
    return pl.pallas_call(
        functools.partial(_attn_kernel, t=t, hp=hp, lam_init=lam_init),
        grid=(HEADS // hp, s // t),
        in_specs=[pl.BlockSpec((t, w), lambda g, i: (i, q_blk // hp + g)),
                  pl.BlockSpec((s, w), lambda g, i: (0, k_blk // hp + g), pipeline_mode=once),
                  pl.BlockSpec((s, w), lambda g, i: (0, v_blk // hp + g), pipeline_mode=once),
                  pl.BlockSpec((hp, 2, t, 2 * t), lambda g, i: (g, 0, 0, 0), pipeline_mode=once),
                  pl.BlockSpec(lam_params.shape, lambda g, i: (0, 0)),
                  pl.BlockSpec((HEAD_W, 1), lambda g, i: (0, 0))],
        out_specs=pl.BlockSpec((t, w), lambda g, i: (i, g)),
        out_shape=jax.ShapeDtypeStruct((s, HEADS * HEAD_W), BF16),
        scratch_shapes=[pltpu.VMEM((hp, s // t, V_ROWS, t), BF16),
                        pltpu.VMEM((hp, HEAD_W, 2 * t), BF16),
                        pltpu.VMEM((2, hp, t, 2 * t), F32),
                        pltpu.VMEM((hp, 1, 2 * t), F32),
                        pltpu.VMEM((hp, V_ROWS, 2 * t), F32)],
        compiler_params=_params("arbitrary", "arbitrary"),
        name="diff_attn",
    )(proj, proj, proj, bias_tiles, lam_params, subln_col)


def _gates_kernel(x_ref, shift_ref, scale_ref, ws_ref, alog_ref, dtb_ref, o_ref):
    h = (x_ref[...] * (1.0 + scale_ref[...]) + shift_ref[...]).astype(BF16)
    tr = _dot_nt(ws_ref[...], h.astype(F32))
    beta = jax.nn.sigmoid(tr[0:HEADS])
    x = tr[HEADS:2 * HEADS] + dtb_ref[...]
    softplus = jnp.maximum(x, 0.0) + jnp.log1p(jnp.exp(-jnp.abs(x)))
    g = -jnp.exp(alog_ref[...]) * softplus
    pos = lax.broadcasted_iota(jnp.int32, g.shape, 1) % DN_CHUNK
    shift = 1
    while shift < DN_CHUNK:
        g = g + jnp.where(pos >= shift, pltpu.roll(g, shift, axis=1), 0.0)
        shift *= 2
    o_ref[0:HEADS] = beta
    o_ref[HEADS:2 * HEADS] = g


def _gates(x, shift, scale, w_small_t, alog_col, dtb_col, *, tb=1024):
    s, d = x.shape
    col = pl.BlockSpec((HEADS, 1), lambda i: (0, 0))
    vec = pl.BlockSpec((1, d), lambda i: (0, 0))
    return pl.pallas_call(
        _gates_kernel,
        grid=(s // tb,),
        in_specs=[pl.BlockSpec((tb, d), lambda i: (i, 0)), vec, vec,
                  pl.BlockSpec(w_small_t.shape, lambda i: (0, 0)), col, col],
        out_specs=pl.BlockSpec((2 * HEADS, tb), lambda i: (0, i)),
        out_shape=jax.ShapeDtypeStruct((2 * HEADS, s), F32),
        compiler_params=_params("parallel"),
        name="gdn_gates",
    )(x, shift, scale, w_small_t, alog_col, dtb_col)


GROUP = 2 * DN_CHUNK
INV_BASE = 16


def _gdn_kernel(q_ref, k_ref, v_ref, z_ref, gt_ref, cwq_ref, cwk_ref, cwv_ref, nw_ref, o_ref,
                pad_sc, state_sc, o_sc, *, tb, hp):
    g = pl.program_id(0)
    ib = pl.program_id(1)
    halo = 8
    heads = range(hp)
    cols = [slice(hh * HEAD_W, (hh + 1) * HEAD_W) for hh in heads]

    @pl.when(ib == 0)
    def _():
        state_sc[...] = jnp.zeros_like(state_sc)
        pad_sc[:, 0:halo, :] = jnp.zeros((3 * hp, halo, HEAD_W), F32)

    def conv_silu(a, x_ref, cw_ref, hh):
        a = a * hp + hh
        pad_sc[a, halo:halo + tb, :] = x_ref[:, cols[hh]].astype(F32)
        cw = cw_ref[:, cols[hh]]
        y = cw[0:1] * pad_sc[a, halo - 3:halo - 3 + tb, :]
        for j in range(1, DN_CONV):
            y = y + cw[j:j + 1] * pad_sc[a, halo - 3 + j:halo - 3 + j + tb, :]
        pad_sc[a, 0:halo, :] = pad_sc[a, tb:tb + halo, :]
        return _silu(y)

    def l2n(x):
        return x * lax.rsqrt(jnp.sum(x * x, axis=-1, keepdims=True) + RMS_EPS)

    ii = lax.broadcasted_iota(jnp.int32, (GROUP, GROUP), 0)
    jj = lax.broadcasted_iota(jnp.int32, (GROUP, GROUP), 1)
    same = (ii >= DN_CHUNK) == (jj >= DN_CHUNK)
    tril = jnp.logical_and(same, ii >= jj)
    eye = (ii == jj).astype(F32)
    bits = INV_BASE.bit_length() - 1
    in_base = jnp.logical_and(ii >> bits == jj >> bits, ii > jj)
    below = []
    while (1 << bits) < DN_CHUNK:
        below.append(jnp.logical_and(ii >> (bits + 1) == jj >> (bits + 1), ii >> bits == (jj >> bits) + 1))
        bits += 1

    q, k, v, beta_c, cum_c, cum_t = [], [], [], [], [], []
    for hh in heads:
        q.append(l2n(conv_silu(0, q_ref, cwq_ref, hh)) * (HEAD_W ** -0.5))
        k.append(l2n(conv_silu(1, k_ref, cwk_ref, hh)))
        v.append(conv_silu(2, v_ref, cwv_ref, hh))
        head = g * hp + hh
        beta_t = jnp.broadcast_to(gt_ref[pl.ds(head, 1), :], (HEAD_W, tb))
        ct = jnp.broadcast_to(gt_ref[pl.ds(HEADS + head, 1), :], (HEAD_W, tb))
        beta_c.append(beta_t.T)
        cum_c.append(ct.T)
        cum_t.append(ct)

    groups = range(tb // GROUP)
    rows = [slice(r * GROUP, (r + 1) * GROUP) for r in groups]
    u, w, qk, qd, kg, cg = {}, {}, {}, {}, {}, {}

    def intra_chunk(r):
        kb, x, tinv, decay, merges = {}, {}, {}, {}, {}
        for hh in heads:
            sid = (r, hh)
            kg[sid], cg[sid] = k[hh][rows[r]], cum_c[hh][rows[r]]
            gdiff = cg[sid] - cum_t[hh][:, rows[r]]
            decay[hh] = jnp.where(tril, jnp.exp(jnp.where(tril, gdiff, 0.0)), 0.0)
            kb[hh] = kg[sid] * beta_c[hh][rows[r]]
            a = _dot_nt(kb[hh].astype(BF16), kg[sid].astype(BF16)) * decay[hh]
            diag = jnp.where(in_base, a, 0.0)
            x[hh] = diag.astype(BF16)
            tinv[hh] = eye - diag
            merges[hh] = [jnp.where(m, a, 0.0).astype(BF16) for m in below]
        yield
        power = 2
        while power < INV_BASE:
            for hh in heads:
                x[hh] = _dot(x[hh], x[hh]).astype(BF16)
            yield
            for hh in heads:
                tinv[hh] = tinv[hh] + _dot(tinv[hh].astype(BF16), x[hh])
            yield
            power *= 2
        for level in range(len(below)):
            half = {}
            for hh in heads:
                half[hh] = _dot(tinv[hh].astype(BF16), merges[hh][level]).astype(BF16)
            yield
            for hh in heads:
                tinv[hh] = tinv[hh] - _dot(half[hh], tinv[hh].astype(BF16))
            yield
        for hh in heads:
            sid = (r, hh)
            eg = jnp.exp(cg[sid])
            rhs = jnp.concatenate([v[hh][rows[r]] * beta_c[hh][rows[r]], kb[hh] * eg], axis=1).astype(BF16)
            uw = _dot(tinv[hh].astype(BF16), rhs)
            u[sid], w[sid] = uw[:, :HEAD_W], uw[:, HEAD_W:].astype(BF16)
            qg = q[hh][rows[r]]
            qk[sid] = (_dot_nt(qg.astype(BF16), kg[sid].astype(BF16)) * decay[hh]).astype(BF16)
            qd[sid] = (qg * eg).astype(BF16)
        yield

    def recurrence(r):
        v_new = {hh: [] for hh in heads}
        o_inter = {hh: [] for hh in heads}
        for c in range(2):
            cr = slice(c * DN_CHUNK, (c + 1) * DN_CHUNK)
            last = c * DN_CHUNK + DN_CHUNK - 1
            st, g_last = {}, {}
            for hh in heads:
                sid = (r, hh)
                g_last[hh] = cg[sid][last:last + 1, :]
                st[hh] = state_sc[hh]
                st16 = st[hh].astype(BF16)
                v_new[hh].append(u[sid][cr] - _dot(w[sid][cr], st16))
                o_inter[hh].append(_dot(qd[sid][cr], st16))
            yield
            for hh in heads:
                sid = (r, hh)
                kd = (kg[sid][cr] * jnp.exp(g_last[hh] - cg[sid][cr])).astype(BF16)
                state_sc[hh] = st[hh] * jnp.exp(g_last[hh]) + _dot_tn(kd, v_new[hh][c].astype(BF16))
            yield
        for hh in heads:
            vn_all = jnp.concatenate(v_new[hh], axis=0).astype(BF16)
            o_sc[hh, rows[r], :] = jnp.concatenate(o_inter[hh], axis=0) + _dot(qk[(r, hh)], vn_all)
        yield

    def interleave(slow, fast, ratio):
        slow_live = fast_live = True
        while slow_live or fast_live:
            if slow_live:
                slow_live = next(slow, "done") != "done"
            for _ in range(ratio):
                if fast_live:
                    fast_live = next(fast, "done") != "done"

    for _ in intra_chunk(0):
        pass
    for r in groups:
        nxt = intra_chunk(r + 1) if r + 1 < len(groups) else iter(())
        interleave(recurrence(r), nxt, 3)

    for hh in heads:
        o = o_sc[hh]
        on = o * lax.rsqrt(jnp.mean(o * o, axis=-1, keepdims=True) + RMS_EPS) * nw_ref[...]
        o_ref[:, cols[hh]] = (on * _silu(z_ref[:, cols[hh]].astype(F32))).astype(BF16)


def _gdn(proj, gates_t, conv_w, norm_w, *, q_blk, k_blk, v_blk, z_blk, tb=256, hp=8):
    s = proj.shape[0]
    w = hp * HEAD_W
    assert all(blk % hp == 0 for blk in (q_blk, k_blk, v_blk, z_blk))
    col = lambda blk: pl.BlockSpec((tb, w), lambda g, i: (i, blk // hp + g))
    cw = lambda blk: pl.BlockSpec((DN_CONV, w), lambda g, i: (0, blk // hp + g))
    return pl.pallas_call(
        functools.partial(_gdn_kernel, tb=tb, hp=hp),
        grid=(HEADS // hp, s // tb),
        in_specs=[col(q_blk), col(k_blk), col(v_blk), col(z_blk),
                  pl.BlockSpec((2 * HEADS, tb), lambda g, i: (0, i)),
                  cw(0), cw(HEADS), cw(2 * HEADS),
                  pl.BlockSpec((1, HEAD_W), lambda g, i: (0, 0))],
        out_specs=pl.BlockSpec((tb, w), lambda g, i: (i, g)),
        out_shape=jax.ShapeDtypeStruct((s, HEADS * HEAD_W), BF16),
        scratch_shapes=[pltpu.VMEM((3 * hp, tb + 8, HEAD_W), F32),
                        pltpu.VMEM((hp, HEAD_W, HEAD_W), F32),
                        pltpu.VMEM((hp, tb, HEAD_W), F32)],
        compiler_params=_params("arbitrary", "arbitrary"),
        name="gdn",
    )(proj, proj, proj, proj, gates_t, conv_w, conv_w, conv_w, norm_w)


def _merge_kernel(ya_ref, yb_ref, ga_ref, gb_ref, x_ref, gate_ref, lng_ref, lnb_ref,
                  wa_ref, wb_ref, wo_ref, o_ref, *, alpha):
    a = _dot(ya_ref[...], wa_ref[...])
    b = _dot(yb_ref[...], wb_ref[...])
    merged = (jax.nn.sigmoid(ga_ref[...].astype(F32)) * a
              + jax.nn.sigmoid(gb_ref[...].astype(F32)) * b).astype(BF16)
    y = _dot(merged, wo_ref[...])
    r = alpha * x_ref[...] + gate_ref[...] * y
    o_ref[...] = _layer_norm(r, lng_ref[...], lnb_ref[...])


def _merge(ya, yb, proj, x, gate, lng, lnb, w_a, w_b, w_o, *, ga_blk, gb_blk, alpha, tm=256):
    s, d = x.shape
    dv = ya.shape[1]
    vec = pl.BlockSpec((1, d), lambda i: (0, 0))
    const = lambda shape: pl.BlockSpec(shape, lambda i: (0, 0), pipeline_mode=pl.Buffered(1))
    return pl.pallas_call(
        functools.partial(_merge_kernel, alpha=alpha),
        grid=(s // tm,),
        in_specs=[pl.BlockSpec((tm, dv), lambda i: (i, 0)),
                  pl.BlockSpec((tm, dv), lambda i: (i, 0)),
                  pl.BlockSpec((tm, d), lambda i: (i, ga_blk)),
                  pl.BlockSpec((tm, d), lambda i: (i, gb_blk)),
                  pl.BlockSpec((tm, d), lambda i: (i, 0)), vec, vec, vec,
                  const(w_a.shape), const(w_b.shape), const(w_o.shape)],
        out_specs=pl.BlockSpec((tm, d), lambda i: (i, 0)),
        out_shape=jax.ShapeDtypeStruct((s, d), F32),
        compiler_params=_params("parallel"),
        name="merge",
    )(ya, yb, proj, proj, x, gate, lng, lnb, w_a, w_b, w_o)


def kernel(x, c, w_ada, b_ada, ln_g, ln_b, w_ffn_in, w_ffn_out, w_in, conv_w, dn_a_log, dn_dt_bias,
           dn_norm_w, diff_lambda, diff_subln_w, rel_bias, w_branch_a, w_branch_b, w_out):
    bsz, s, d = x.shape
    assert bsz == 1, "one sequence per call"
    depth = w_ada.shape[0]
    alpha = (2 * depth) ** 0.25
    hw = HEADS * HEAD_W
    attn_t = 256
    assert d % HEAD_W == 0 and s % 1024 == 0

    o_nb = 7 * hw
    o_ga = o_nb + 2 * HEADS
    nblk = d // HEAD_W
    blk = {"ga": 0, "gb": 1, "dq": 2 * nblk, "dk": 2 * nblk + HEADS, "dv": 2 * nblk + 2 * HEADS,
           "nq": 2 * nblk + 3 * HEADS, "nk": 2 * nblk + 4 * HEADS, "nv": 2 * nblk + 5 * HEADS,
           "nz": 2 * nblk + 6 * HEADS}

    bias_tiles = _bias_tiles(rel_bias, attn_t)
    wfi, wfo = w_ffn_in.astype(BF16), w_ffn_out.astype(BF16)
    w_in_t = jnp.swapaxes(w_in, 1, 2)
    x2 = x[0]
    for l in range(depth):
        lam_init = 0.8 - 0.6 * math.exp(-0.3 * l)
        ada = _ada(c.reshape(d, 1), w_ada[l], b_ada[l].reshape(1, -1)).reshape(N_SUB, 3, 1, d)
        shift, scale, gate = ada[:, 0], ada[:, 1], ada[:, 2]
        lng, lnb = ln_g[l].reshape(N_SUB, 1, d), ln_b[l].reshape(N_SUB, 1, d)
        w_gate_t = w_in_t[l, o_ga:].astype(BF16)
        w_small_t = w_in_t[l, o_nb:o_ga]

        x2 = _ffn(x2, shift[0], scale[0], gate[0], lng[0], lnb[0], wfi, wfo, layer=l, which=0, alpha=alpha)

        proj = _proj(x2, shift[1], scale[1], w_gate_t, w_in_t, layer=l, n_head_cols=o_nb)
        ya = _attn(proj, bias_tiles, diff_lambda[l], diff_subln_w[l].reshape(HEAD_W, 1),
                   q_blk=blk["dq"], k_blk=blk["dk"], v_blk=blk["dv"], lam_init=lam_init, t=attn_t)
        gates_t = _gates(x2, shift[1], scale[1], w_small_t, dn_a_log[l].reshape(HEADS, 1),
                         dn_dt_bias[l].reshape(HEADS, 1))
        yb = _gdn(proj, gates_t, conv_w[l], dn_norm_w[l].reshape(1, HEAD_W),
                  q_blk=blk["nq"], k_blk=blk["nk"], v_blk=blk["nv"], z_blk=blk["nz"])
        x2 = _merge(ya, yb, proj, x2, gate[1], lng[1], lnb[1], w_branch_a[l].astype(BF16),
                    w_branch_b[l].astype(BF16), w_out[l].astype(BF16),
                    ga_blk=blk["ga"], gb_blk=blk["gb"], alpha=alpha)

        x2 = _ffn(x2, shift[2], scale[2], gate[2], lng[2], lnb[2], wfi, wfo, layer=l, which=1, alpha=alpha)
    return x2[None]
```

```python
import functools
import math

import numpy as np
import jax
import jax.numpy as jnp
from jax import lax
from jax.experimental import pallas as pl
from jax.experimental.pallas import tpu as pltpu

N_SUB = 3
HEADS = 8
HEAD_W = 128
DIFF_QK_DIM = 64
DN_CONV = 4
DN_CHUNK = 64
REL_BUCKETS = 32
REL_MAX_DIST = 128
LN_EPS = 1e-5
RMS_EPS = 1e-6
MASK_VALUE = -1e30
LOG2E = math.log2(math.e)
V_ROWS = HEAD_W + 16

F32 = jnp.float32
BF16 = jnp.bfloat16

V7X_VMEM_BYTES = 64 * 1024 * 1024
VMEM_LIMIT = V7X_VMEM_BYTES - 4 * 1024 * 1024


def _params(*sem):
    return pltpu.CompilerParams(dimension_semantics=sem, vmem_limit_bytes=VMEM_LIMIT)


def _tile(n, preferred):
    t = min(preferred, n)
    while n % t:
        t -= 128
    return t


def _silu(x):
    return x * jax.nn.sigmoid(x)


def _dot(a, b):
    return jnp.dot(a, b, preferred_element_type=F32)


def _dot_nt(a, b):
    return lax.dot_general(a, b, (((1,), (1,)), ((), ())), preferred_element_type=F32)


def _dot_tn(a, b):
    return lax.dot_general(a, b, (((0,), (0,)), ((), ())), preferred_element_type=F32)


def _layer_norm(r, g, b):
    mu = jnp.mean(r, axis=-1, keepdims=True)
    d = r - mu
    var = jnp.mean(d * d, axis=-1, keepdims=True)
    return d * lax.rsqrt(var + LN_EPS) * g + b


def _ada_kernel(c_ref, w_ref, b_ref, o_ref):
    sc = _silu(c_ref[...])
    o_ref[...] = jnp.sum(w_ref[...] * sc, axis=0, keepdims=True) + b_ref[...]


def _ada(c_col, w, b):
    d, n = w.shape
    tn = _tile(n, 1024)
    return pl.pallas_call(
        _ada_kernel,
        grid=(n // tn,),
        in_specs=[pl.BlockSpec((d, 1), lambda j: (0, 0)),
                  pl.BlockSpec((d, tn), lambda j: (0, j)),
                  pl.BlockSpec((1, tn), lambda j: (0, j))],
        out_specs=pl.BlockSpec((1, tn), lambda j: (0, j)),
        out_shape=jax.ShapeDtypeStruct((1, n), F32),
        compiler_params=_params("arbitrary"),
        name="ada",
    )(c_col, w, b)


def _ffn_kernel(x_ref, shift_ref, scale_ref, gate_ref, lng_ref, lnb_ref, wg_ref, wu_ref, wo_ref,
                o_ref, h_sc, a_sc, *, alpha, gate_mul):
    j = pl.program_id(1)
    n_chunks = pl.num_programs(1) - 1

    def drain(slot):
        return _dot(a_sc[slot], wo_ref[...])

    def activations(slot):
        h = h_sc[...]
        g = _dot(h, wg_ref[...])
        u = _dot(h, wu_ref[...])
        a_sc[slot] = (_silu(g) * u).astype(BF16)

    @pl.when(j == 0)
    def _():
        h_sc[...] = (x_ref[...] * (1.0 + scale_ref[...]) + shift_ref[...]).astype(BF16)
        o_ref[...] = jnp.zeros_like(o_ref)
        activations(0)

    for slot in range(2):
        @pl.when(jnp.logical_and(jnp.logical_and(j > 0, j < n_chunks), (j & 1) == slot))
        def _():
            o_ref[...] += drain(1 - slot)
            activations(slot)

        @pl.when(jnp.logical_and(j == n_chunks, ((j - 1) & 1) == slot))
        def _():
            r = alpha * x_ref[...] + (gate_mul * gate_ref[...]) * (o_ref[...] + drain(slot))
            o_ref[...] = _layer_norm(r, lng_ref[...], lnb_ref[...])


def _ffn(x, shift, scale, gate, lng, lnb, w_in, w_out, *, layer, which, alpha, tm=1024, tf=512):
    s, d = x.shape
    f = w_out.shape[2]
    tm, tf = _tile(s, tm), _tile(f, tf)
    nf = f // tf
    vec = pl.BlockSpec((1, d), lambda i, j: (0, 0))
    up = lambda j: jnp.minimum(j, nf - 1)
    down = lambda j: jnp.maximum(j - 1, 0)
    return pl.pallas_call(
        functools.partial(_ffn_kernel, alpha=alpha, gate_mul=0.5),
        grid=(s // tm, nf + 1),
        in_specs=[pl.BlockSpec((tm, d), lambda i, j: (i, 0)), vec, vec, vec, vec, vec,
                  pl.BlockSpec((None, None, d, tf), lambda i, j: (layer, which, 0, up(j))),
                  pl.BlockSpec((None, None, d, tf), lambda i, j: (layer, which, 0, up(j) + nf)),
                  pl.BlockSpec((None, None, tf, d), lambda i, j: (layer, which, down(j), 0))],
        out_specs=pl.BlockSpec((tm, d), lambda i, j: (i, 0)),
        out_shape=jax.ShapeDtypeStruct((s, d), F32),
        scratch_shapes=[pltpu.VMEM((tm, d), BF16), pltpu.VMEM((2, tm, tf), BF16)],
        compiler_params=_params("parallel", "arbitrary"),
        name="ffn",
    )(x, shift, scale, gate, lng, lnb, w_in, w_in, w_out)


def _proj_kernel(h_ref, wg_ref, wh_ref, o_ref, w_sc, *, n_gate):
    j = pl.program_id(0)
    first_row = pl.program_id(1) == 0

    @pl.when(jnp.logical_and(first_row, j < n_gate))
    def _():
        w_sc[...] = wg_ref[...]

    @pl.when(jnp.logical_and(first_row, j >= n_gate))
    def _():
        w_sc[...] = wh_ref[...].astype(BF16)

    o_ref[...] = _dot_nt(h_ref[...], w_sc[...]).astype(BF16)


def _proj(h, w_gate_t, w_all_t, *, layer, n_head_cols, tm=1024, tn=1024):
    s, d = h.shape
    ng_cols = w_gate_t.shape[0]
    tm = _tile(s, tm)
    tn = math.gcd(_tile(ng_cols, tn), _tile(n_head_cols, tn))
    n_gate, n_head = ng_cols // tn, n_head_cols // tn
    return pl.pallas_call(
        functools.partial(_proj_kernel, n_gate=n_gate),
        grid=(n_gate + n_head, s // tm),
        in_specs=[pl.BlockSpec((tm, d), lambda j, i: (i, 0)),
                  pl.BlockSpec((tn, d), lambda j, i: (jnp.minimum(j, n_gate - 1), 0)),
                  pl.BlockSpec((None, tn, d), lambda j, i: (layer, jnp.maximum(j - n_gate, 0), 0))],
        out_specs=pl.BlockSpec((tm, tn), lambda j, i: (i, j)),
        out_shape=jax.ShapeDtypeStruct((s, ng_cols + n_head_cols), BF16),
        scratch_shapes=[pltpu.VMEM((tn, d), BF16)],
        compiler_params=_params("arbitrary", "arbitrary"),
        name="proj",
    )(h, w_gate_t, w_all_t)


def _bucket_starts():
    n = np.arange(0, 2 * REL_MAX_DIST)
    max_exact = REL_BUCKETS // 2
    nf = np.maximum(n, max_exact).astype(np.float32)
    large = max_exact + (np.log(nf / np.float32(max_exact)) / np.float32(math.log(REL_MAX_DIST / max_exact))
                         * np.float32(REL_BUCKETS - max_exact)).astype(np.int32)
    bucket = np.where(n < max_exact, n, np.minimum(large, REL_BUCKETS - 1))
    assert np.all(np.diff(bucket) >= 0) and bucket[-1] == REL_BUCKETS - 1
    return [int(np.min(n[bucket >= b])) for b in range(REL_BUCKETS)]


BUCKET_STARTS = _bucket_starts()


def _bias_kernel(rb_ref, o_ref, *, t):
    h = pl.program_id(0)
    jj = lax.broadcasted_iota(jnp.int32, (t, 2 * t), 0)
    ii = lax.broadcasted_iota(jnp.int32, (t, 2 * t), 1)
    ii = jnp.where(ii >= t, ii - t, ii)
    far = rb_ref[REL_BUCKETS - 1, h]
    for n in range(2):
        rel = ii - jj + n * t
        bias = jnp.full((t, 2 * t), (rb_ref[0, h] - far) * LOG2E, F32)
        for b in range(1, REL_BUCKETS):
            bias = jnp.where(rel >= BUCKET_STARTS[b], (rb_ref[b, h] - far) * LOG2E, bias)
        o_ref[0, n] = jnp.where(rel < 0, MASK_VALUE, bias)


def _bias_tiles(rel_bias, t):
    return pl.pallas_call(
        functools.partial(_bias_kernel, t=t),
        grid=(HEADS,),
        in_specs=[pl.BlockSpec(memory_space=pltpu.SMEM)],
        out_specs=pl.BlockSpec((1, 2, t, 2 * t), lambda h: (h, 0, 0, 0)),
        out_shape=jax.ShapeDtypeStruct((HEADS, 2, t, 2 * t), F32),
        compiler_params=_params("arbitrary"),
        name="bias_tiles",
    )(rel_bias)


def _attn_kernel(q_ref, k_ref, v_ref, bias_ref, lam_ref, sub_ref, o_ref,
                 vt_sc, qt_sc, s_sc, m_sc, acc_sc, *, t, hp, lam_init):
    qi = pl.program_id(1)
    n_kv = vt_sc.shape[1]
    heads = range(hp)
    cols = [slice(hh * HEAD_W, (hh + 1) * HEAD_W) for hh in heads]

    @pl.when(qi == 0)
    def _():
        ones_tile = (lax.broadcasted_iota(jnp.int32, (V_ROWS - HEAD_W, t), 0) == 0).astype(BF16)

        def body(c, carry):
            r0 = pl.multiple_of(c * t, t)
            for hh in heads:
                vt_sc[hh, c, 0:HEAD_W] = v_ref[pl.ds(r0, t), cols[hh]].astype(F32).T.astype(BF16)
                vt_sc[hh, c, HEAD_W:V_ROWS] = ones_tile
            return carry
        lax.fori_loop(0, n_kv, body, 0)

    row = lax.broadcasted_iota(jnp.int32, (HEAD_W, t), 0)
    for hh in heads:
        qt = (q_ref[:, cols[hh]].astype(F32) * (DIFF_QK_DIM ** -0.5 * LOG2E)).T
        zero = jnp.zeros_like(qt)
        qt_sc[hh] = jnp.concatenate([jnp.where(row < DIFF_QK_DIM, qt, zero),
                                     jnp.where(row >= DIFF_QK_DIM, qt, zero)], axis=1).astype(BF16)

    m_sc[...] = jnp.full_like(m_sc, -jnp.inf)
    acc_sc[...] = jnp.zeros_like(acc_sc)

    def scores(c, slot, biased, which=heads):
        r0 = pl.multiple_of(c * t, t)
        for hh in which:
            s = _dot(k_ref[pl.ds(r0, t), cols[hh]], qt_sc[hh])
            if biased:
                s = s + bias_ref[hh, qi - c]
            s_sc[slot, hh] = s

    def consume(c, slot, which=heads):
        for hh in which:
            s = s_sc[slot, hh]
            m_old = m_sc[hh]
            m_new = jnp.maximum(m_old, jnp.max(s, axis=0, keepdims=True))
            p = jnp.exp2(s - m_new).astype(BF16)
            acc_sc[hh] = jnp.exp2(m_old - m_new) * acc_sc[hh] + _dot(vt_sc[hh, c], p)
            m_sc[hh] = m_new

    pl.when(qi >= 2)(lambda: scores(0, 0, False))
    pl.when(qi < 2)(lambda: scores(0, 0, True))

    n_pairs = jnp.maximum(qi - 2, 0) >> 1

    def pair_body(j, carry):
        c = 2 * j
        scores(c + 1, 1, False)
        for hh in heads:
            consume(c, 0, [hh])
            scores(c + 2, 0, False, [hh])
        consume(c + 1, 1)
        return carry
    lax.fori_loop(0, n_pairs, pair_body, 0)

    def body(c, carry):
        far_next = c + 1 <= qi - 2
        for slot in range(2):
            mine = (c & 1) == slot

            @pl.when(jnp.logical_and(mine, far_next))
            def _():
                scores(c + 1, 1 - slot, False)
                consume(c, slot)

            @pl.when(jnp.logical_and(mine, jnp.logical_not(far_next)))
            def _():
                scores(c + 1, 1 - slot, True)
                consume(c, slot)
        return carry
    lax.fori_loop(2 * n_pairs, qi, body, 0)

    for slot in range(2):
        pl.when((qi & 1) == slot)(functools.partial(consume, qi, slot))

    lp = lam_ref[...]
    lam = (jnp.exp(jnp.sum(lp[0:1] * lp[1:2], axis=1, keepdims=True))
           - jnp.exp(jnp.sum(lp[2:3] * lp[3:4], axis=1, keepdims=True)) + lam_init)
    for hh in heads:
        acc = acc_sc[hh]
        o = acc[0:HEAD_W] / acc[HEAD_W:HEAD_W + 1]
        od = o[:, :t] - lam * o[:, t:]
        ms = jnp.mean(od * od, axis=0, keepdims=True)
        y = od * lax.rsqrt(ms + LN_EPS) * sub_ref[...] * (1.0 - lam_init)
        o_ref[:, cols[hh]] = y.T.astype(BF16)


def _attn(proj, bias_tiles, lam_params, subln_col, *, q_blk, k_blk, v_blk, lam_init, t, hp=4):
    s = proj.shape[0]
    w = hp * HEAD_W
    assert q_blk % hp == 0 and k_blk % hp == 0 and v_blk % hp == 0
    once = pl.Buffered(1)
    return pl.pallas_call(
        functools.partial(_attn_kernel, t=t, hp=hp, lam_init=lam_init),
        grid=(HEADS // hp, s // t),
        in_specs=[pl.BlockSpec((t, w), lambda g, i: (i, q_blk // hp + g)),
                  pl.BlockSpec((s, w), lambda g, i: (0, k_blk // hp + g), pipeline_mode=once),
                  pl.BlockSpec((s, w), lambda g, i: (0, v_blk // hp + g), pipeline_mode=once),
                  pl.BlockSpec((hp, 2, t, 2 * t), lambda g, i: (g, 0, 0, 0), pipeline_mode=once),
                  pl.BlockSpec(lam_params.shape, lambda g, i: (0, 0)),
                  pl.BlockSpec((HEAD_W, 1), lambda g, i: (0, 0))],
        out_specs=pl.BlockSpec((t, w), lambda g, i: (i, g)),
        out_shape=jax.ShapeDtypeStruct((s, HEADS * HEAD_W), BF16),
        scratch_shapes=[pltpu.VMEM((hp, s // t, V_ROWS, t), BF16),
                        pltpu.VMEM((hp, HEAD_W, 2 * t), BF16),
                        pltpu.VMEM((2, hp, t, 2 * t), F32),
                        pltpu.VMEM((hp, 1, 2 * t), F32),
                        pltpu.VMEM((hp, V_ROWS, 2 * t), F32)],
        compiler_params=_params("arbitrary", "arbitrary"),
        name="diff_attn",
    )(proj, proj, proj, bias_tiles, lam_params, subln_col)


def _gates_kernel(x_ref, shift_ref, scale_ref, ws_ref, alog_ref, dtb_ref, o_ref, h_ref):
    h = (x_ref[...] * (1.0 + scale_ref[...]) + shift_ref[...]).astype(BF16)
    h_ref[...] = h
    tr = _dot_nt(ws_ref[...], h.astype(F32))
    beta = jax.nn.sigmoid(tr[0:HEADS])
    x = tr[HEADS:2 * HEADS] + dtb_ref[...]
    softplus = jnp.maximum(x, 0.0) + jnp.log1p(jnp.exp(-jnp.abs(x)))
    g = -jnp.exp(alog_ref[...]) * softplus
    pos = lax.broadcasted_iota(jnp.int32, g.shape, 1) % DN_CHUNK
    shift = 1
    while shift < DN_CHUNK:
        g = g + jnp.where(pos >= shift, pltpu.roll(g, shift, axis=1), 0.0)
        shift *= 2
    o_ref[0:HEADS] = beta
    o_ref[HEADS:2 * HEADS] = g


def _gates(x, shift, scale, w_small_t, alog_col, dtb_col, *, tb=1024):
    s, d = x.shape
    col = pl.BlockSpec((HEADS, 1), lambda i: (0, 0))
    vec = pl.BlockSpec((1, d), lambda i: (0, 0))
    return pl.pallas_call(
        _gates_kernel,
        grid=(s // tb,),
        in_specs=[pl.BlockSpec((tb, d), lambda i: (i, 0)), vec, vec,
                  pl.BlockSpec(w_small_t.shape, lambda i: (0, 0)), col, col],
        out_specs=[pl.BlockSpec((2 * HEADS, tb), lambda i: (0, i)), pl.BlockSpec((tb, d), lambda i: (i, 0))],
        out_shape=[jax.ShapeDtypeStruct((2 * HEADS, s), F32), jax.ShapeDtypeStruct((s, d), BF16)],
        compiler_params=_params("parallel"),
        name="gdn_gates",
    )(x, shift, scale, w_small_t, alog_col, dtb_col)


GROUP = 2 * DN_CHUNK
INV_BASE = 16


def _gdn_kernel(q_ref, k_ref, v_ref, z_ref, gt_ref, cwq_ref, cwk_ref, cwv_ref, nw_ref, o_ref,
                pad_sc, state_sc, o_sc, *, tb, hp):
    g = pl.program_id(0)
    ib = pl.program_id(1)
    halo = 8
    heads = range(hp)
    cols = [slice(hh * HEAD_W, (hh + 1) * HEAD_W) for hh in heads]

    @pl.when(ib == 0)
    def _():
        state_sc[...] = jnp.zeros_like(state_sc)
        pad_sc[:, 0:halo, :] = jnp.zeros((3 * hp, halo, HEAD_W), F32)

    def conv_silu(a, x_ref, cw_ref, hh):
        a = a * hp + hh
        pad_sc[a, halo:halo + tb, :] = x_ref[:, cols[hh]].astype(F32)
        cw = cw_ref[:, cols[hh]]
        y = cw[0:1] * pad_sc[a, halo - 3:halo - 3 + tb, :]
        for j in range(1, DN_CONV):
            y = y + cw[j:j + 1] * pad_sc[a, halo - 3 + j:halo - 3 + j + tb, :]
        pad_sc[a, 0:halo, :] = pad_sc[a, tb:tb + halo, :]
        return _silu(y)

    def l2n(x):
        return x * lax.rsqrt(jnp.sum(x * x, axis=-1, keepdims=True) + RMS_EPS)

    ii = lax.broadcasted_iota(jnp.int32, (GROUP, GROUP), 0)
    jj = lax.broadcasted_iota(jnp.int32, (GROUP, GROUP), 1)
    same = (ii >= DN_CHUNK) == (jj >= DN_CHUNK)
    tril = jnp.logical_and(same, ii >= jj)
    eye = (ii == jj).astype(F32)
    bits = INV_BASE.bit_length() - 1
    in_base = jnp.logical_and(ii >> bits == jj >> bits, ii > jj)
    below = []
    while (1 << bits) < DN_CHUNK:
        below.append(jnp.logical_and(ii >> (bits + 1) == jj >> (bits + 1), ii >> bits == (jj >> bits) + 1))
        bits += 1

    q, k, v, beta_c, cum_c, cum_t = [], [], [], [], [], []
    for hh in heads:
        q.append(l2n(conv_silu(0, q_ref, cwq_ref, hh)) * (HEAD_W ** -0.5))
        k.append(l2n(conv_silu(1, k_ref, cwk_ref, hh)))
        v.append(conv_silu(2, v_ref, cwv_ref, hh))
        head = g * hp + hh
        beta_t = jnp.broadcast_to(gt_ref[pl.ds(head, 1), :], (HEAD_W, tb))
        ct = jnp.broadcast_to(gt_ref[pl.ds(HEADS + head, 1), :], (HEAD_W, tb))
        beta_c.append(beta_t.T)
        cum_c.append(ct.T)
        cum_t.append(ct)

    groups = range(tb // GROUP)
    rows = [slice(r * GROUP, (r + 1) * GROUP) for r in groups]
    u, w, qk, qd, kg, cg = {}, {}, {}, {}, {}, {}

    def intra_chunk(r):
        kb, x, tinv, decay, merges = {}, {}, {}, {}, {}
        for hh in heads:
            sid = (r, hh)
            kg[sid], cg[sid] = k[hh][rows[r]], cum_c[hh][rows[r]]
            gdiff = cg[sid] - cum_t[hh][:, rows[r]]
            decay[hh] = jnp.where(tril, jnp.exp(jnp.where(tril, gdiff, 0.0)), 0.0)
            kb[hh] = kg[sid] * beta_c[hh][rows[r]]
            a = _dot_nt(kb[hh].astype(BF16), kg[sid].astype(BF16)) * decay[hh]
            diag = jnp.where(in_base, a, 0.0)
            x[hh] = diag.astype(BF16)
            tinv[hh] = eye - diag
            merges[hh] = [jnp.where(m, a, 0.0).astype(BF16) for m in below]
        yield
        power = 2
        while power < INV_BASE:
            for hh in heads:
                x[hh] = _dot(x[hh], x[hh]).astype(BF16)
            yield
            for hh in heads:
                tinv[hh] = tinv[hh] + _dot(tinv[hh].astype(BF16), x[hh])
            yield
            power *= 2
        for level in range(len(below)):
            half = {}
            for hh in heads:
                half[hh] = _dot(tinv[hh].astype(BF16), merges[hh][level]).astype(BF16)
            yield
            for hh in heads:
                tinv[hh] = tinv[hh] - _dot(half[hh], tinv[hh].astype(BF16))
            yield
        for hh in heads:
            sid = (r, hh)
            eg = jnp.exp(cg[sid])
            rhs = jnp.concatenate([v[hh][rows[r]] * beta_c[hh][rows[r]], kb[hh] * eg], axis=1).astype(BF16)
            uw = _dot(tinv[hh].astype(BF16), rhs)
            u[sid], w[sid] = uw[:, :HEAD_W], uw[:, HEAD_W:].astype(BF16)
            qg = q[hh][rows[r]]
            qk[sid] = (_dot_nt(qg.astype(BF16), kg[sid].astype(BF16)) * decay[hh]).astype(BF16)
            qd[sid] = (qg * eg).astype(BF16)
        yield

    def recurrence(r):
        v_new = {hh: [] for hh in heads}
        o_inter = {hh: [] for hh in heads}
        for c in range(2):
            cr = slice(c * DN_CHUNK, (c + 1) * DN_CHUNK)
            last = c * DN_CHUNK + DN_CHUNK - 1
            st, g_last = {}, {}
            for hh in heads:
                sid = (r, hh)
                g_last[hh] = cg[sid][last:last + 1, :]
                st[hh] = state_sc[hh]
                st16 = st[hh].astype(BF16)
                v_new[hh].append(u[sid][cr] - _dot(w[sid][cr], st16))
                o_inter[hh].append(_dot(qd[sid][cr], st16))
            yield
            for hh in heads:
                sid = (r, hh)
                kd = (kg[sid][cr] * jnp.exp(g_last[hh] - cg[sid][cr])).astype(BF16)
                state_sc[hh] = st[hh] * jnp.exp(g_last[hh]) + _dot_tn(kd, v_new[hh][c].astype(BF16))
            yield
        for hh in heads:
            vn_all = jnp.concatenate(v_new[hh], axis=0).astype(BF16)
            o_sc[hh, rows[r], :] = jnp.concatenate(o_inter[hh], axis=0) + _dot(qk[(r, hh)], vn_all)
        yield

    def interleave(slow, fast, ratio):
        slow_live = fast_live = True
        while slow_live or fast_live:
            if slow_live:
                slow_live = next(slow, "done") != "done"
            for _ in range(ratio):
                if fast_live:
                    fast_live = next(fast, "done") != "done"

    for _ in intra_chunk(0):
        pass
    for r in groups:
        nxt = intra_chunk(r + 1) if r + 1 < len(groups) else iter(())
        interleave(recurrence(r), nxt, 3)

    for hh in heads:
        o = o_sc[hh]
        on = o * lax.rsqrt(jnp.mean(o * o, axis=-1, keepdims=True) + RMS_EPS) * nw_ref[...]
        o_ref[:, cols[hh]] = (on * _silu(z_ref[:, cols[hh]].astype(F32))).astype(BF16)


def _gdn(proj, gates_t, conv_w, norm_w, *, q_blk, k_blk, v_blk, z_blk, tb=256, hp=8):
    s = proj.shape[0]
    w = hp * HEAD_W
    assert all(blk % hp == 0 for blk in (q_blk, k_blk, v_blk, z_blk))
    col = lambda blk: pl.BlockSpec((tb, w), lambda g, i: (i, blk // hp + g))
    cw = lambda blk: pl.BlockSpec((DN_CONV, w), lambda g, i: (0, blk // hp + g))
    return pl.pallas_call(
        functools.partial(_gdn_kernel, tb=tb, hp=hp),
        grid=(HEADS // hp, s // tb),
        in_specs=[col(q_blk), col(k_blk), col(v_blk), col(z_blk),
                  pl.BlockSpec((2 * HEADS, tb), lambda g, i: (0, i)),
                  cw(0), cw(HEADS), cw(2 * HEADS),
                  pl.BlockSpec((1, HEAD_W), lambda g, i: (0, 0))],
        out_specs=pl.BlockSpec((tb, w), lambda g, i: (i, g)),
        out_shape=jax.ShapeDtypeStruct((s, HEADS * HEAD_W), BF16),
        scratch_shapes=[pltpu.VMEM((3 * hp, tb + 8, HEAD_W), F32),
                        pltpu.VMEM((hp, HEAD_W, HEAD_W), F32),
                        pltpu.VMEM((hp, tb, HEAD_W), F32)],
        compiler_params=_params("arbitrary", "arbitrary"),
        name="gdn",
    )(proj, proj, proj, proj, gates_t, conv_w, conv_w, conv_w, norm_w)


def _merge_kernel(ya_ref, yb_ref, ga_ref, gb_ref, x_ref, gate_ref, lng_ref, lnb_ref,
                  wa_ref, wb_ref, wo_ref, o_ref, m_sc, *, alpha):
    i = pl.program_id(0)
    n_tiles = pl.num_programs(0) - 1

    def merge(slot):
        a = _dot(ya_ref[...], wa_ref[...])
        b = _dot(yb_ref[...], wb_ref[...])
        m_sc[slot] = (jax.nn.sigmoid(ga_ref[...].astype(F32)) * a
                      + jax.nn.sigmoid(gb_ref[...].astype(F32)) * b).astype(BF16)

    def project(slot):
        y = _dot(m_sc[slot], wo_ref[...])
        r = alpha * x_ref[...] + gate_ref[...] * y
        o_ref[...] = _layer_norm(r, lng_ref[...], lnb_ref[...])

    pl.when(i == 0)(lambda: merge(0))
    for slot in range(2):
        @pl.when(jnp.logical_and(jnp.logical_and(i > 0, i < n_tiles), (i & 1) == slot))
        def _():
            project(1 - slot)
            merge(slot)

        pl.when(jnp.logical_and(i == n_tiles, ((i - 1) & 1) == slot))(functools.partial(project, slot))


def _merge(ya, yb, proj, x, gate, lng, lnb, w_a, w_b, w_o, *, ga_blk, gb_blk, alpha, tm=256):
    s, d = x.shape
    dv = ya.shape[1]
    n = s // tm
    vec = pl.BlockSpec((1, d), lambda i: (0, 0))
    const = lambda shape: pl.BlockSpec(shape, lambda i: (0, 0), pipeline_mode=pl.Buffered(1))
    cur = lambda i: jnp.minimum(i, n - 1)
    prev = lambda i: jnp.maximum(i - 1, 0)
    return pl.pallas_call(
        functools.partial(_merge_kernel, alpha=alpha),
        grid=(n + 1,),
        in_specs=[pl.BlockSpec((tm, dv), lambda i: (cur(i), 0)),
                  pl.BlockSpec((tm, dv), lambda i: (cur(i), 0)),
                  pl.BlockSpec((tm, d), lambda i: (cur(i), ga_blk)),
                  pl.BlockSpec((tm, d), lambda i: (cur(i), gb_blk)),
                  pl.BlockSpec((tm, d), lambda i: (prev(i), 0)), vec, vec, vec,
                  const(w_a.shape), const(w_b.shape), const(w_o.shape)],
        out_specs=pl.BlockSpec((tm, d), lambda i: (prev(i), 0)),
        out_shape=jax.ShapeDtypeStruct((s, d), F32),
        scratch_shapes=[pltpu.VMEM((2, tm, d), BF16)],
        compiler_params=_params("arbitrary"),
        name="merge",
    )(ya, yb, proj, proj, x, gate, lng, lnb, w_a, w_b, w_o)


def kernel(x, c, w_ada, b_ada, ln_g, ln_b, w_ffn_in, w_ffn_out, w_in, conv_w, dn_a_log, dn_dt_bias,
           dn_norm_w, diff_lambda, diff_subln_w, rel_bias, w_branch_a, w_branch_b, w_out):
    bsz, s, d = x.shape
    assert bsz == 1, "one sequence per call"
    depth = w_ada.shape[0]
    alpha = (2 * depth) ** 0.25
    hw = HEADS * HEAD_W
    attn_t = 256
    assert d % HEAD_W == 0 and s % 1024 == 0

    o_nb = 7 * hw
    o_ga = o_nb + 2 * HEADS
    nblk = d // HEAD_W
    blk = {"ga": 0, "gb": 1, "dq": 2 * nblk, "dk": 2 * nblk + HEADS, "dv": 2 * nblk + 2 * HEADS,
           "nq": 2 * nblk + 3 * HEADS, "nk": 2 * nblk + 4 * HEADS, "nv": 2 * nblk + 5 * HEADS,
           "nz": 2 * nblk + 6 * HEADS}

    bias_tiles = _bias_tiles(rel_bias, attn_t)
    wfi, wfo = w_ffn_in.astype(BF16), w_ffn_out.astype(BF16)
    w_in_t = jnp.swapaxes(w_in, 1, 2)
    x2 = x[0]
    for l in range(depth):
        lam_init = 0.8 - 0.6 * math.exp(-0.3 * l)
        ada = _ada(c.reshape(d, 1), w_ada[l], b_ada[l].reshape(1, -1)).reshape(N_SUB, 3, 1, d)
        shift, scale, gate = ada[:, 0], ada[:, 1], ada[:, 2]
        lng, lnb = ln_g[l].reshape(N_SUB, 1, d), ln_b[l].reshape(N_SUB, 1, d)
        w_gate_t = w_in_t[l, o_ga:].astype(BF16)
        w_small_t = w_in_t[l, o_nb:o_ga]

        x2 = _ffn(x2, shift[0], scale[0], gate[0], lng[0], lnb[0], wfi, wfo, layer=l, which=0, alpha=alpha)

        gates_t, h = _gates(x2, shift[1], scale[1], w_small_t, dn_a_log[l].reshape(HEADS, 1),
                            dn_dt_bias[l].reshape(HEADS, 1))
        proj = _proj(h, w_gate_t, w_in_t, layer=l, n_head_cols=o_nb)
        ya = _attn(proj, bias_tiles, diff_lambda[l], diff_subln_w[l].reshape(HEAD_W, 1),
                   q_blk=blk["dq"], k_blk=blk["dk"], v_blk=blk["dv"], lam_init=lam_init, t=attn_t)
        yb = _gdn(proj, gates_t, conv_w[l], dn_norm_w[l].reshape(1, HEAD_W),
                  q_blk=blk["nq"], k_blk=blk["nk"], v_blk=blk["nv"], z_blk=blk["nz"])
        x2 = _merge(ya, yb, proj, x2, gate[1], lng[1], lnb[1], w_branch_a[l].astype(BF16),
                    w_branch_b[l].astype(BF16), w_out[l].astype(BF16),
                    ga_blk=blk["ga"], gb_blk=blk["gb"], alpha=alpha)

        x2 = _ffn(x2, shift[2], scale[2], gate[2], lng[2], lnb[2], wfi, wfo, layer=l, which=1, alpha=alpha)
    return x2[None]
```

```python
import functools
import math

import numpy as np
import jax
import jax.numpy as jnp
from jax import lax
from jax.experimental import pallas as pl
from jax.experimental.pallas import tpu as pltpu

N_SUB = 3
HEADS = 8
HEAD_W = 128
DIFF_QK_DIM = 64
DN_CONV = 4
DN_CHUNK = 64
REL_BUCKETS = 32
REL_MAX_DIST = 128
LN_EPS = 1e-5
RMS_EPS = 1e-6
MASK_VALUE = -1e30
LOG2E = math.log2(math.e)
V_ROWS = HEAD_W + 16

F32 = jnp.float32
BF16 = jnp.bfloat16

V7X_VMEM_BYTES = 64 * 1024 * 1024
VMEM_LIMIT = V7X_VMEM_BYTES - 4 * 1024 * 1024


def _params(*sem):
    return pltpu.CompilerParams(dimension_semantics=sem, vmem_limit_bytes=VMEM_LIMIT)


def _tile(n, preferred):
    t = min(preferred, n)
    while n % t:
        t -= 128
    return t


def _silu(x):
    return x * jax.nn.sigmoid(x)


def _dot(a, b):
    return jnp.dot(a, b, preferred_element_type=F32)


def _dot_nt(a, b):
    return lax.dot_general(a, b, (((1,), (1,)), ((), ())), preferred_element_type=F32)


def _dot_tn(a, b):
    return lax.dot_general(a, b, (((0,), (0,)), ((), ())), preferred_element_type=F32)


def _layer_norm(r, g, b):
    mu = jnp.mean(r, axis=-1, keepdims=True)
    d = r - mu
    var = jnp.mean(d * d, axis=-1, keepdims=True)
    return d * lax.rsqrt(var + LN_EPS) * g + b


def _ada_kernel(c_ref, w_ref, b_ref, o_ref):
    sc = _silu(c_ref[...])
    o_ref[...] = jnp.sum(w_ref[...] * sc, axis=0, keepdims=True) + b_ref[...]


def _ada(c_col, w, b):
    d, n = w.shape
    tn = _tile(n, 1024)
    return pl.pallas_call(
        _ada_kernel,
        grid=(n // tn,),
        in_specs=[pl.BlockSpec((d, 1), lambda j: (0, 0)),
                  pl.BlockSpec((d, tn), lambda j: (0, j)),
                  pl.BlockSpec((1, tn), lambda j: (0, j))],
        out_specs=pl.BlockSpec((1, tn), lambda j: (0, j)),
        out_shape=jax.ShapeDtypeStruct((1, n), F32),
        compiler_params=_params("arbitrary"),
        name="ada",
    )(c_col, w, b)


def _ffn_kernel(x_ref, shift_ref, scale_ref, gate_ref, lng_ref, lnb_ref, wg_ref, wu_ref, wo_ref,
                o_ref, h_sc, a_sc, *, alpha, gate_mul):
    j = pl.program_id(1)
    n_chunks = pl.num_programs(1) - 1

    def drain(slot):
        return _dot(a_sc[slot], wo_ref[...])

    def activations(slot):
        h = h_sc[...]
        g = _dot(h, wg_ref[...])
        u = _dot(h, wu_ref[...])
        a_sc[slot] = (_silu(g) * u).astype(BF16)

    @pl.when(j == 0)
    def _():
        h_sc[...] = (x_ref[...] * (1.0 + scale_ref[...]) + shift_ref[...]).astype(BF16)
        o_ref[...] = jnp.zeros_like(o_ref)
        activations(0)

    for slot in range(2):
        @pl.when(jnp.logical_and(jnp.logical_and(j > 0, j < n_chunks), (j & 1) == slot))
        def _():
            o_ref[...] += drain(1 - slot)
            activations(slot)

        @pl.when(jnp.logical_and(j == n_chunks, ((j - 1) & 1) == slot))
        def _():
            r = alpha * x_ref[...] + (gate_mul * gate_ref[...]) * (o_ref[...] + drain(slot))
            o_ref[...] = _layer_norm(r, lng_ref[...], lnb_ref[...])


def _ffn(x, shift, scale, gate, lng, lnb, w_in, w_out, *, layer, which, alpha, tm=1024, tf=512):
    s, d = x.shape
    f = w_out.shape[2]
    tm, tf = _tile(s, tm), _tile(f, tf)
    nf = f // tf
    vec = pl.BlockSpec((1, d), lambda i, j: (0, 0))
    up = lambda j: jnp.minimum(j, nf - 1)
    down = lambda j: jnp.maximum(j - 1, 0)
    return pl.pallas_call(
        functools.partial(_ffn_kernel, alpha=alpha, gate_mul=0.5),
        grid=(s // tm, nf + 1),
        in_specs=[pl.BlockSpec((tm, d), lambda i, j: (i, 0)), vec, vec, vec, vec, vec,
                  pl.BlockSpec((None, None, d, tf), lambda i, j: (layer, which, 0, up(j))),
                  pl.BlockSpec((None, None, d, tf), lambda i, j: (layer, which, 0, up(j) + nf)),
                  pl.BlockSpec((None, None, tf, d), lambda i, j: (layer, which, down(j), 0))],
        out_specs=pl.BlockSpec((tm, d), lambda i, j: (i, 0)),
        out_shape=jax.ShapeDtypeStruct((s, d), F32),
        scratch_shapes=[pltpu.VMEM((tm, d), BF16), pltpu.VMEM((2, tm, tf), BF16)],
        compiler_params=_params("parallel", "arbitrary"),
        name="ffn",
    )(x, shift, scale, gate, lng, lnb, w_in, w_in, w_out)


def _proj_kernel(h_ref, wg_ref, wh_ref, o_ref, w_sc, *, n_gate):
    j = pl.program_id(0)
    first_row = pl.program_id(1) == 0

    @pl.when(jnp.logical_and(first_row, j < n_gate))
    def _():
        w_sc[...] = wg_ref[...]

    @pl.when(jnp.logical_and(first_row, j >= n_gate))
    def _():
        w_sc[...] = wh_ref[...].astype(BF16)

    o_ref[...] = _dot_nt(h_ref[...], w_sc[...]).astype(BF16)


def _proj(h, w_gate_t, w_all_t, *, layer, n_head_cols, tm=1024, tn=1024):
    s, d = h.shape
    ng_cols = w_gate_t.shape[0]
    tm = _tile(s, tm)
    tn = math.gcd(_tile(ng_cols, tn), _tile(n_head_cols, tn))
    n_gate, n_head = ng_cols // tn, n_head_cols // tn
    return pl.pallas_call(
        functools.partial(_proj_kernel, n_gate=n_gate),
        grid=(n_gate + n_head, s // tm),
        in_specs=[pl.BlockSpec((tm, d), lambda j, i: (i, 0)),
                  pl.BlockSpec((tn, d), lambda j, i: (jnp.minimum(j, n_gate - 1), 0)),
                  pl.BlockSpec((None, tn, d), lambda j, i: (layer, jnp.maximum(j - n_gate, 0), 0))],
        out_specs=pl.BlockSpec((tm, tn), lambda j, i: (i, j)),
        out_shape=jax.ShapeDtypeStruct((s, ng_cols + n_head_cols), BF16),
        scratch_shapes=[pltpu.VMEM((tn, d), BF16)],
        compiler_params=_params("arbitrary", "arbitrary"),
        name="proj",
    )(h, w_gate_t, w_all_t)


def _bucket_starts():
    n = np.arange(0, 2 * REL_MAX_DIST)
    max_exact = REL_BUCKETS // 2
    nf = np.maximum(n, max_exact).astype(np.float32)
    large = max_exact + (np.log(nf / np.float32(max_exact)) / np.float32(math.log(REL_MAX_DIST / max_exact))
                         * np.float32(REL_BUCKETS - max_exact)).astype(np.int32)
    bucket = np.where(n < max_exact, n, np.minimum(large, REL_BUCKETS - 1))
    assert np.all(np.diff(bucket) >= 0) and bucket[-1] == REL_BUCKETS - 1
    return [int(np.min(n[bucket >= b])) for b in range(REL_BUCKETS)]


BUCKET_STARTS = _bucket_starts()


def _bias_kernel(rb_ref, o_ref, *, t):
    h = pl.program_id(0)
    jj = lax.broadcasted_iota(jnp.int32, (t, 2 * t), 0)
    ii = lax.broadcasted_iota(jnp.int32, (t, 2 * t), 1)
    ii = jnp.where(ii >= t, ii - t, ii)
    far = rb_ref[REL_BUCKETS - 1, h]
    for n in range(2):
        rel = ii - jj + n * t
        bias = jnp.full((t, 2 * t), (rb_ref[0, h] - far) * LOG2E, F32)
        for b in range(1, REL_BUCKETS):
            bias = jnp.where(rel >= BUCKET_STARTS[b], (rb_ref[b, h] - far) * LOG2E, bias)
        o_ref[0, n] = jnp.where(rel < 0, MASK_VALUE, bias)


def _bias_tiles(rel_bias, t):
    return pl.pallas_call(
        functools.partial(_bias_kernel, t=t),
        grid=(HEADS,),
        in_specs=[pl.BlockSpec(memory_space=pltpu.SMEM)],
        out_specs=pl.BlockSpec((1, 2, t, 2 * t), lambda h: (h, 0, 0, 0)),
        out_shape=jax.ShapeDtypeStruct((HEADS, 2, t, 2 * t), F32),
        compiler_params=_params("arbitrary"),
        name="bias_tiles",
    )(rel_bias)


def _attn_kernel(q_ref, k_ref, v_ref, bias_ref, lam_ref, sub_ref, o_ref,
                 vt_sc, qt_sc, s_sc, m_sc, acc_sc, *, t, hp, lam_init):
    qi = pl.program_id(1)
    n_kv = vt_sc.shape[1]
    heads = range(hp)
    cols = [slice(hh * HEAD_W, (hh + 1) * HEAD_W) for hh in heads]

    @pl.when(qi == 0)
    def _():
        ones_tile = (lax.broadcasted_iota(jnp.int32, (V_ROWS - HEAD_W, t), 0) == 0).astype(BF16)

        def body(c, carry):
            r0 = pl.multiple_of(c * t, t)
            for hh in heads:
                vt_sc[hh, c, 0:HEAD_W] = v_ref[pl.ds(r0, t), cols[hh]].astype(F32).T.astype(BF16)
                vt_sc[hh, c, HEAD_W:V_ROWS] = ones_tile
            return carry
        lax.fori_loop(0, n_kv, body, 0)

    row = lax.broadcasted_iota(jnp.int32, (HEAD_W, t), 0)
    for hh in heads:
        qt = (q_ref[:, cols[hh]].astype(F32) * (DIFF_QK_DIM ** -0.5 * LOG2E)).T
        zero = jnp.zeros_like(qt)
        qt_sc[hh] = jnp.concatenate([jnp.where(row < DIFF_QK_DIM, qt, zero),
                                     jnp.where(row >= DIFF_QK_DIM, qt, zero)], axis=1).astype(BF16)

    m_sc[...] = jnp.full_like(m_sc, -jnp.inf)
    acc_sc[...] = jnp.zeros_like(acc_sc)

    def scores(c, slot, biased, which=heads):
        r0 = pl.multiple_of(c * t, t)
        for hh in which:
            s = _dot(k_ref[pl.ds(r0, t), cols[hh]], qt_sc[hh])
            if biased:
                s = s + bias_ref[hh, qi - c]
            s_sc[slot, hh] = s

    def consume(c, slot, which=heads):
        for hh in which:
            s = s_sc[slot, hh]
            m_old = m_sc[hh]
            m_new = jnp.maximum(m_old, jnp.max(s, axis=0, keepdims=True))
            p = jnp.exp2(s - m_new).astype(BF16)
            acc_sc[hh] = jnp.exp2(m_old - m_new) * acc_sc[hh] + _dot(vt_sc[hh, c], p)
            m_sc[hh] = m_new

    pl.when(qi >= 2)(lambda: scores(0, 0, False))
    pl.when(qi < 2)(lambda: scores(0, 0, True))

    n_pairs = jnp.maximum(qi - 2, 0) >> 1

    def pair_body(j, carry):
        c = 2 * j
        scores(c + 1, 1, False)
        for hh in heads:
            consume(c, 0, [hh])
            scores(c + 2, 0, False, [hh])
        consume(c + 1, 1)
        return carry
    lax.fori_loop(0, n_pairs, pair_body, 0)

    def body(c, carry):
        far_next = c + 1 <= qi - 2
        for slot in range(2):
            mine = (c & 1) == slot

            @pl.when(jnp.logical_and(mine, far_next))
            def _():
                scores(c + 1, 1 - slot, False)
                consume(c, slot)

            @pl.when(jnp.logical_and(mine, jnp.logical_not(far_next)))
            def _():
                scores(c + 1, 1 - slot, True)
                consume(c, slot)
        return carry
    lax.fori_loop(2 * n_pairs, qi, body, 0)

    for slot in range(2):
        pl.when((qi & 1) == slot)(functools.partial(consume, qi, slot))

    lp = lam_ref[...]
    lam = (jnp.exp(jnp.sum(lp[0:1] * lp[1:2], axis=1, keepdims=True))
           - jnp.exp(jnp.sum(lp[2:3] * lp[3:4], axis=1, keepdims=True)) + lam_init)
    for hh in heads:
        acc = acc_sc[hh]
        o = acc[0:HEAD_W] / acc[HEAD_W:HEAD_W + 1]
        od = o[:, :t] - lam * o[:, t:]
        ms = jnp.mean(od * od, axis=0, keepdims=True)
        y = od * lax.rsqrt(ms + LN_EPS) * sub_ref[...] * (1.0 - lam_init)
        o_ref[:, cols[hh]] = y.T.astype(BF16)


def _attn(proj, bias_tiles, lam_params, subln_col, *, q_blk, k_blk, v_blk, lam_init, t, hp=4):
    s = proj.shape[0]
    w = hp * HEAD_W
    assert q_blk % hp == 0 and k_blk % hp == 0 and v_blk % hp == 0
    once = pl.Buffered(1)
    return pl.pallas_call(
        functools.partial(_attn_kernel, t=t, hp=hp, lam_init=lam_init),
        grid=(HEADS // hp, s // t),
        in_specs=[pl.BlockSpec((t, w), lambda g, i: (i, q_blk // hp + g)),
                  pl.BlockSpec((s, w), lambda g, i: (0, k_blk // hp + g), pipeline_mode=once),
                  pl.BlockSpec((s, w), lambda g, i: (0, v_blk // hp + g), pipeline_mode=once),
                  pl.BlockSpec((hp, 2, t, 2 * t), lambda g, i: (g, 0, 0, 0), pipeline_mode=once),
                  pl.BlockSpec(lam_params.shape, lambda g, i: (0, 0)),
                  pl.BlockSpec((HEAD_W, 1), lambda g, i: (0, 0))],
        out_specs=pl.BlockSpec((t, w), lambda g, i: (i, g)),
        out_shape=jax.ShapeDtypeStruct((s, HEADS * HEAD_W), BF16),
        scratch_shapes=[pltpu.VMEM((hp, s // t, V_ROWS, t), BF16),
                        pltpu.VMEM((hp, HEAD_W, 2 * t), BF16),
                        pltpu.VMEM((2, hp, t, 2 * t), F32),
                        pltpu.VMEM((hp, 1, 2 * t), F32),
                        pltpu.VMEM((hp, V_ROWS, 2 * t), F32)],
        compiler_params=_params("arbitrary", "arbitrary"),
        name="diff_attn",
    )(proj, proj, proj, bias_tiles, lam_params, subln_col)


def _gates_kernel(x_ref, shift_ref, scale_ref, ws_ref, alog_ref, dtb_ref, o_ref, h_ref):
    h = (x_ref[...] * (1.0 + scale_ref[...]) + shift_ref[...]).astype(BF16)
    h_ref[...] = h
    tr = _dot_nt(ws_ref[...], h.astype(F32))
    beta = jax.nn.sigmoid(tr[0:HEADS])
    x = tr[HEADS:2 * HEADS] + dtb_ref[...]
    softplus = jnp.maximum(x, 0.0) + jnp.log1p(jnp.exp(-jnp.abs(x)))
    g = -jnp.exp(alog_ref[...]) * softplus
    pos = lax.broadcasted_iota(jnp.int32, g.shape, 1) % DN_CHUNK
    shift = 1
    while shift < DN_CHUNK:
        g = g + jnp.where(pos >= shift, pltpu.roll(g, shift, axis=1), 0.0)
        shift *= 2
    o_ref[0:HEADS] = beta
    o_ref[HEADS:2 * HEADS] = g


def _gates(x, shift, scale, w_small_t, alog_col, dtb_col, *, tb=1024):
    s, d = x.shape
    col = pl.BlockSpec((HEADS, 1), lambda i: (0, 0))
    vec = pl.BlockSpec((1, d), lambda i: (0, 0))
    return pl.pallas_call(
        _gates_kernel,
        grid=(s // tb,),
        in_specs=[pl.BlockSpec((tb, d), lambda i: (i, 0)), vec, vec,
                  pl.BlockSpec(w_small_t.shape, lambda i: (0, 0)), col, col],
        out_specs=[pl.BlockSpec((2 * HEADS, tb), lambda i: (0, i)), pl.BlockSpec((tb, d), lambda i: (i, 0))],
        out_shape=[jax.ShapeDtypeStruct((2 * HEADS, s), F32), jax.ShapeDtypeStruct((s, d), BF16)],
        compiler_params=_params("parallel"),
        name="gdn_gates",
    )(x, shift, scale, w_small_t, alog_col, dtb_col)


GROUP = 2 * DN_CHUNK
INV_BASE = 16


def _gdn_kernel(q_ref, k_ref, v_ref, z_ref, gt_ref, cwq_ref, cwk_ref, cwv_ref, nw_ref, o_ref,
                pad_sc, state_sc, o_sc, *, tb, hp):
    g = pl.program_id(0)
    ib = pl.program_id(1)
    halo = 8
    heads = range(hp)
    cols = [slice(hh * HEAD_W, (hh + 1) * HEAD_W) for hh in heads]

    @pl.when(ib == 0)
    def _():
        state_sc[...] = jnp.zeros_like(state_sc)
        pad_sc[:, 0:halo, :] = jnp.zeros((3 * hp, halo, HEAD_W), F32)

    def conv_silu(a, x_ref, cw_ref, hh):
        a = a * hp + hh
        pad_sc[a, halo:halo + tb, :] = x_ref[:, cols[hh]].astype(F32)
        cw = cw_ref[:, cols[hh]]
        y = cw[0:1] * pad_sc[a, halo - 3:halo - 3 + tb, :]
        for j in range(1, DN_CONV):
            y = y + cw[j:j + 1] * pad_sc[a, halo - 3 + j:halo - 3 + j + tb, :]
        pad_sc[a, 0:halo, :] = pad_sc[a, tb:tb + halo, :]
        return _silu(y)

    def l2n(x):
        return x * lax.rsqrt(jnp.sum(x * x, axis=-1, keepdims=True) + RMS_EPS)

    ii = lax.broadcasted_iota(jnp.int32, (GROUP, GROUP), 0)
    jj = lax.broadcasted_iota(jnp.int32, (GROUP, GROUP), 1)
    same = (ii >= DN_CHUNK) == (jj >= DN_CHUNK)
    tril = jnp.logical_and(same, ii >= jj)
    eye = (ii == jj).astype(F32)
    bits = INV_BASE.bit_length() - 1
    in_base = jnp.logical_and(ii >> bits == jj >> bits, ii > jj)
    below = []
    while (1 << bits) < DN_CHUNK:
        below.append(jnp.logical_and(ii >> (bits + 1) == jj >> (bits + 1), ii >> bits == (jj >> bits) + 1))
        bits += 1

    q, k, v, beta_c, cum_c, cum_t = [], [], [], [], [], []
    for hh in heads:
        q.append(l2n(conv_silu(0, q_ref, cwq_ref, hh)) * (HEAD_W ** -0.5))
        k.append(l2n(conv_silu(1, k_ref, cwk_ref, hh)))
        v.append(conv_silu(2, v_ref, cwv_ref, hh))
        head = g * hp + hh
        beta_t = jnp.broadcast_to(gt_ref[pl.ds(head, 1), :], (HEAD_W, tb))
        ct = jnp.broadcast_to(gt_ref[pl.ds(HEADS + head, 1), :], (HEAD_W, tb))
        beta_c.append(beta_t.T)
        cum_c.append(ct.T)
        cum_t.append(ct)

    groups = range(tb // GROUP)
    rows = [slice(r * GROUP, (r + 1) * GROUP) for r in groups]
    u, w, qk, qd, kg, cg = {}, {}, {}, {}, {}, {}

    def intra_chunk(r):
        kb, x, tinv, decay, merges = {}, {}, {}, {}, {}
        for hh in heads:
            sid = (r, hh)
            kg[sid], cg[sid] = k[hh][rows[r]], cum_c[hh][rows[r]]
            gdiff = cg[sid] - cum_t[hh][:, rows[r]]
            decay[hh] = jnp.where(tril, jnp.exp(jnp.where(tril, gdiff, 0.0)), 0.0)
            kb[hh] = kg[sid] * beta_c[hh][rows[r]]
            a = _dot_nt(kb[hh].astype(BF16), kg[sid].astype(BF16)) * decay[hh]
            diag = jnp.where(in_base, a, 0.0)
            x[hh] = diag.astype(BF16)
            tinv[hh] = eye - diag
            merges[hh] = [jnp.where(m, a, 0.0).astype(BF16) for m in below]
        yield
        power = 2
        while power < INV_BASE:
            for hh in heads:
                x[hh] = _dot(x[hh], x[hh]).astype(BF16)
            yield
            for hh in heads:
                tinv[hh] = tinv[hh] + _dot(tinv[hh].astype(BF16), x[hh])
            yield
            power *= 2
        for level in range(len(below)):
            half = {}
            for hh in heads:
                half[hh] = _dot(tinv[hh].astype(BF16), merges[hh][level]).astype(BF16)
            yield
            for hh in heads:
                tinv[hh] = tinv[hh] - _dot(half[hh], tinv[hh].astype(BF16))
            yield
        for hh in heads:
            sid = (r, hh)
            eg = jnp.exp(cg[sid])
            rhs = jnp.concatenate([v[hh][rows[r]] * beta_c[hh][rows[r]], kb[hh] * eg], axis=1).astype(BF16)
            uw = _dot(tinv[hh].astype(BF16), rhs)
            u[sid], w[sid] = uw[:, :HEAD_W], uw[:, HEAD_W:].astype(BF16)
            qg = q[hh][rows[r]]
            qk[sid] = (_dot_nt(qg.astype(BF16), kg[sid].astype(BF16)) * decay[hh]).astype(BF16)
            qd[sid] = (qg * eg).astype(BF16)
        yield

    def recurrence(r):
        v_new = {hh: [] for hh in heads}
        o_inter = {hh: [] for hh in heads}
        for c in range(2):
            cr = slice(c * DN_CHUNK, (c + 1) * DN_CHUNK)
            last = c * DN_CHUNK + DN_CHUNK - 1
            st, g_last = {}, {}
            for hh in heads:
                sid = (r, hh)
                g_last[hh] = cg[sid][last:last + 1, :]
                st[hh] = state_sc[hh]
                st16 = st[hh].astype(BF16)
                v_new[hh].append(u[sid][cr] - _dot(w[sid][cr], st16))
                o_inter[hh].append(_dot(qd[sid][cr], st16))
            yield
            for hh in heads:
                sid = (r, hh)
                kd = (kg[sid][cr] * jnp.exp(g_last[hh] - cg[sid][cr])).astype(BF16)
                state_sc[hh] = st[hh] * jnp.exp(g_last[hh]) + _dot_tn(kd, v_new[hh][c].astype(BF16))
            yield
        for hh in heads:
            vn_all = jnp.concatenate(v_new[hh], axis=0).astype(BF16)
            o_sc[hh, rows[r], :] = jnp.concatenate(o_inter[hh], axis=0) + _dot(qk[(r, hh)], vn_all)
        yield

    def interleave(slow, fast, ratio):
        slow_live = fast_live = True
        while slow_live or fast_live:
            if slow_live:
                slow_live = next(slow, "done") != "done"
            for _ in range(ratio):
                if fast_live:
                    fast_live = next(fast, "done") != "done"

    for _ in intra_chunk(0):
        pass
    for r in groups:
        nxt = intra_chunk(r + 1) if r + 1 < len(groups) else iter(())
        interleave(recurrence(r), nxt, 3)

    for hh in heads:
        o = o_sc[hh]
        on = o * lax.rsqrt(jnp.mean(o * o, axis=-1, keepdims=True) + RMS_EPS) * nw_ref[...]
        o_ref[:, cols[hh]] = (on * _silu(z_ref[:, cols[hh]].astype(F32))).astype(BF16)


def _gdn(proj, gates_t, conv_w, norm_w, *, q_blk, k_blk, v_blk, z_blk, tb=256, hp=8):
    s = proj.shape[0]
    w = hp * HEAD_W
    assert all(blk % hp == 0 for blk in (q_blk, k_blk, v_blk, z_blk))
    col = lambda blk: pl.BlockSpec((tb, w), lambda g, i: (i, blk // hp + g))
    cw = lambda blk: pl.BlockSpec((DN_CONV, w), lambda g, i: (0, blk // hp + g))
    return pl.pallas_call(
        functools.partial(_gdn_kernel, tb=tb, hp=hp),
        grid=(HEADS // hp, s // tb),
        in_specs=[col(q_blk), col(k_blk), col(v_blk), col(z_blk),
                  pl.BlockSpec((2 * HEADS, tb), lambda g, i: (0, i)),
                  cw(0), cw(HEADS), cw(2 * HEADS),
                  pl.BlockSpec((1, HEAD_W), lambda g, i: (0, 0))],
        out_specs=pl.BlockSpec((tb, w), lambda g, i: (i, g)),
        out_shape=jax.ShapeDtypeStruct((s, HEADS * HEAD_W), BF16),
        scratch_shapes=[pltpu.VMEM((3 * hp, tb + 8, HEAD_W), F32),
                        pltpu.VMEM((hp, HEAD_W, HEAD_W), F32),
                        pltpu.VMEM((hp, tb, HEAD_W), F32)],
        compiler_params=_params("arbitrary", "arbitrary"),
        name="gdn",
    )(proj, proj, proj, proj, gates_t, conv_w, conv_w, conv_w, norm_w)


def _merge_kernel(ya_ref, yb_ref, ga_ref, gb_ref, x_ref, gate_ref, lng_ref, lnb_ref,
                  wa_ref, wb_ref, wo_ref, o_ref, m_sc, *, alpha):
    i = pl.program_id(0)
    n_tiles = pl.num_programs(0) - 1

    def merge(slot):
        a = _dot(ya_ref[...], wa_ref[...])
        b = _dot(yb_ref[...], wb_ref[...])
        m_sc[slot] = (jax.nn.sigmoid(ga_ref[...].astype(F32)) * a
                      + jax.nn.sigmoid(gb_ref[...].astype(F32)) * b).astype(BF16)

    def project(slot):
        y = _dot(m_sc[slot], wo_ref[...])
        r = alpha * x_ref[...] + gate_ref[...] * y
        o_ref[...] = _layer_norm(r, lng_ref[...], lnb_ref[...])

    pl.when(i == 0)(lambda: merge(0))
    for slot in range(2):
        @pl.when(jnp.logical_and(jnp.logical_and(i > 0, i < n_tiles), (i & 1) == slot))
        def _():
            project(1 - slot)
            merge(slot)

        pl.when(jnp.logical_and(i == n_tiles, ((i - 1) & 1) == slot))(functools.partial(project, slot))


def _merge(ya, yb, proj, x, gate, lng, lnb, w_a, w_b, w_o, *, ga_blk, gb_blk, alpha, tm=512):
    s, d = x.shape
    dv = ya.shape[1]
    n = s // tm
    vec = pl.BlockSpec((1, d), lambda i: (0, 0))
    const = lambda shape: pl.BlockSpec(shape, lambda i: (0, 0), pipeline_mode=pl.Buffered(1))
    cur = lambda i: jnp.minimum(i, n - 1)
    prev = lambda i: jnp.maximum(i - 1, 0)
    return pl.pallas_call(
        functools.partial(_merge_kernel, alpha=alpha),
        grid=(n + 1,),
        in_specs=[pl.BlockSpec((tm, dv), lambda i: (cur(i), 0)),
                  pl.BlockSpec((tm, dv), lambda i: (cur(i), 0)),
                  pl.BlockSpec((tm, d), lambda i: (cur(i), ga_blk)),
                  pl.BlockSpec((tm, d), lambda i: (cur(i), gb_blk)),
                  pl.BlockSpec((tm, d), lambda i: (prev(i), 0)), vec, vec, vec,
                  const(w_a.shape), const(w_b.shape), const(w_o.shape)],
        out_specs=pl.BlockSpec((tm, d), lambda i: (prev(i), 0)),
        out_shape=jax.ShapeDtypeStruct((s, d), F32),
        scratch_shapes=[pltpu.VMEM((2, tm, d), BF16)],
        compiler_params=_params("arbitrary"),
        name="merge",
    )(ya, yb, proj, proj, x, gate, lng, lnb, w_a, w_b, w_o)


def kernel(x, c, w_ada, b_ada, ln_g, ln_b, w_ffn_in, w_ffn_out, w_in, conv_w, dn_a_log, dn_dt_bias,
           dn_norm_w, diff_lambda, diff_subln_w, rel_bias, w_branch_a, w_branch_b, w_out):
    bsz, s, d = x.shape
    assert bsz == 1, "one sequence per call"
    depth = w_ada.shape[0]
    alpha = (2 * depth) ** 0.25
    hw = HEADS * HEAD_W
    attn_t = 256
    assert d % HEAD_W == 0 and s % 1024 == 0

    o_nb = 7 * hw
    o_ga = o_nb + 2 * HEADS
    nblk = d // HEAD_W
    blk = {"ga": 0, "gb": 1, "dq": 2 * nblk, "dk": 2 * nblk + HEADS, "dv": 2 * nblk + 2 * HEADS,
           "nq": 2 * nblk + 3 * HEADS, "nk": 2 * nblk + 4 * HEADS, "nv": 2 * nblk + 5 * HEADS,
           "nz": 2 * nblk + 6 * HEADS}

    bias_tiles = _bias_tiles(rel_bias, attn_t)
    wfi, wfo = w_ffn_in.astype(BF16), w_ffn_out.astype(BF16)
    w_in_t = jnp.swapaxes(w_in, 1, 2)
    x2 = x[0]
    for l in range(depth):
        lam_init = 0.8 - 0.6 * math.exp(-0.3 * l)
        ada = _ada(c.reshape(d, 1), w_ada[l], b_ada[l].reshape(1, -1)).reshape(N_SUB, 3, 1, d)
        shift, scale, gate = ada[:, 0], ada[:, 1], ada[:, 2]
        lng, lnb = ln_g[l].reshape(N_SUB, 1, d), ln_b[l].reshape(N_SUB, 1, d)
        w_gate_t = w_in_t[l, o_ga:].astype(BF16)
        w_small_t = w_in_t[l, o_nb:o_ga]

        x2 = _ffn(x2, shift[0], scale[0], gate[0], lng[0], lnb[0], wfi, wfo, layer=l, which=0, alpha=alpha)

        gates_t, h = _gates(x2, shift[1], scale[1], w_small_t, dn_a_log[l].reshape(HEADS, 1),
                            dn_dt_bias[l].reshape(HEADS, 1))
        proj = _proj(h, w_gate_t, w_in_t, layer=l, n_head_cols=o_nb)
        ya = _attn(proj, bias_tiles, diff_lambda[l], diff_subln_w[l].reshape(HEAD_W, 1),
                   q_blk=blk["dq"], k_blk=blk["dk"], v_blk=blk["dv"], lam_init=lam_init, t=attn_t)
        yb = _gdn(proj, gates_t, conv_w[l], dn_norm_w[l].reshape(1, HEAD_W),
                  q_blk=blk["nq"], k_blk=blk["nk"], v_blk=blk["nv"], z_blk=blk["nz"])
        x2 = _merge(ya, yb, proj, x2, gate[1], lng[1], lnb[1], w_branch_a[l].astype(BF16),
                    w_branch_b[l].astype(BF16), w_out[l].astype(BF16),
                    ga_blk=blk["ga"], gb_blk=blk["gb"], alpha=alpha)

        x2 = _ffn(x2, shift[2], scale[2], gate[2], lng[2], lnb[2], wfi, wfo, layer=l, which=1, alpha=alpha)
    return x2[None]
```

```python
import functools
import math

import numpy as np
import jax
import jax.numpy as jnp
from jax import lax
from jax.experimental import pallas as pl
from jax.experimental.pallas import tpu as pltpu

N_SUB = 3
HEADS = 8
HEAD_W = 128
DIFF_QK_DIM = 64
DN_CONV = 4
DN_CHUNK = 64
REL_BUCKETS = 32
REL_MAX_DIST = 128
LN_EPS = 1e-5
RMS_EPS = 1e-6
MASK_VALUE = -1e30
LOG2E = math.log2(math.e)
V_ROWS = HEAD_W + 16

F32 = jnp.float32
BF16 = jnp.bfloat16

V7X_VMEM_BYTES = 64 * 1024 * 1024
VMEM_LIMIT = V7X_VMEM_BYTES - 4 * 1024 * 1024


def _params(*sem):
    return pltpu.CompilerParams(dimension_semantics=sem, vmem_limit_bytes=VMEM_LIMIT)


def _tile(n, preferred):
    t = min(preferred, n)
    while n % t:
        t -= 128
    return t


def _silu(x):
    return x * jax.nn.sigmoid(x)


def _dot(a, b):
    return jnp.dot(a, b, preferred_element_type=F32)


def _dot_nt(a, b):
    return lax.dot_general(a, b, (((1,), (1,)), ((), ())), preferred_element_type=F32)


def _dot_tn(a, b):
    return lax.dot_general(a, b, (((0,), (0,)), ((), ())), preferred_element_type=F32)


def _layer_norm(r, g, b):
    mu = jnp.mean(r, axis=-1, keepdims=True)
    d = r - mu
    var = jnp.mean(d * d, axis=-1, keepdims=True)
    return d * lax.rsqrt(var + LN_EPS) * g + b


def _ada_kernel(c_ref, w_ref, b_ref, o_ref):
    sc = _silu(c_ref[...])
    o_ref[...] = jnp.sum(w_ref[...] * sc, axis=0, keepdims=True) + b_ref[...]


def _ada(c_col, w, b):
    d, n = w.shape
    tn = _tile(n, 1024)
    return pl.pallas_call(
        _ada_kernel,
        grid=(n // tn,),
        in_specs=[pl.BlockSpec((d, 1), lambda j: (0, 0)),
                  pl.BlockSpec((d, tn), lambda j: (0, j)),
                  pl.BlockSpec((1, tn), lambda j: (0, j))],
        out_specs=pl.BlockSpec((1, tn), lambda j: (0, j)),
        out_shape=jax.ShapeDtypeStruct((1, n), F32),
        compiler_params=_params("arbitrary"),
        name="ada",
    )(c_col, w, b)


def _ffn_kernel(x_ref, shift_ref, scale_ref, gate_ref, lng_ref, lnb_ref, wg_ref, wu_ref, wo_ref,
                o_ref, h_sc, a_sc, *, alpha, gate_mul):
    j = pl.program_id(1)
    n_chunks = pl.num_programs(1) - 1

    def drain(slot):
        return _dot(a_sc[slot], wo_ref[...].astype(BF16))

    def activations(slot):
        h = h_sc[...]
        g = _dot(h, wg_ref[...].astype(BF16))
        u = _dot(h, wu_ref[...].astype(BF16))
        a_sc[slot] = (_silu(g) * u).astype(BF16)

    @pl.when(j == 0)
    def _():
        h_sc[...] = (x_ref[...] * (1.0 + scale_ref[...]) + shift_ref[...]).astype(BF16)
        o_ref[...] = jnp.zeros_like(o_ref)
        activations(0)

    for slot in range(2):
        @pl.when(jnp.logical_and(jnp.logical_and(j > 0, j < n_chunks), (j & 1) == slot))
        def _():
            o_ref[...] += drain(1 - slot)
            activations(slot)

        @pl.when(jnp.logical_and(j == n_chunks, ((j - 1) & 1) == slot))
        def _():
            r = alpha * x_ref[...] + (gate_mul * gate_ref[...]) * (o_ref[...] + drain(slot))
            o_ref[...] = _layer_norm(r, lng_ref[...], lnb_ref[...])


def _ffn(x, shift, scale, gate, lng, lnb, w_in, w_out, *, layer, which, alpha, tm=1024, tf=256):
    s, d = x.shape
    f = w_out.shape[2]
    tm, tf = _tile(s, tm), _tile(f, tf)
    nf = f // tf
    vec = pl.BlockSpec((1, d), lambda i, j: (0, 0))
    up = lambda j: jnp.minimum(j, nf - 1)
    down = lambda j: jnp.maximum(j - 1, 0)
    return pl.pallas_call(
        functools.partial(_ffn_kernel, alpha=alpha, gate_mul=0.5),
        grid=(s // tm, nf + 1),
        in_specs=[pl.BlockSpec((tm, d), lambda i, j: (i, 0)), vec, vec, vec, vec, vec,
                  pl.BlockSpec((None, None, d, tf), lambda i, j: (layer, which, 0, up(j))),
                  pl.BlockSpec((None, None, d, tf), lambda i, j: (layer, which, 0, up(j) + nf)),
                  pl.BlockSpec((None, None, tf, d), lambda i, j: (layer, which, down(j), 0))],
        out_specs=pl.BlockSpec((tm, d), lambda i, j: (i, 0)),
        out_shape=jax.ShapeDtypeStruct((s, d), F32),
        scratch_shapes=[pltpu.VMEM((tm, d), BF16), pltpu.VMEM((2, tm, tf), BF16)],
        compiler_params=_params("parallel", "arbitrary"),
        name="ffn",
    )(x, shift, scale, gate, lng, lnb, w_in, w_in, w_out)


def _proj_kernel(h_ref, wg_ref, wh_ref, o_ref, w_sc, *, n_gate):
    j = pl.program_id(0)
    first_row = pl.program_id(1) == 0

    @pl.when(jnp.logical_and(first_row, j < n_gate))
    def _():
        w_sc[...] = wg_ref[...]

    @pl.when(jnp.logical_and(first_row, j >= n_gate))
    def _():
        w_sc[...] = wh_ref[...].astype(BF16)

    o_ref[...] = _dot_nt(h_ref[...], w_sc[...]).astype(BF16)


def _proj(h, w_gate_t, w_all_t, *, layer, n_head_cols, tm=1024, tn=1024):
    s, d = h.shape
    ng_cols = w_gate_t.shape[0]
    tm = _tile(s, tm)
    tn = math.gcd(_tile(ng_cols, tn), _tile(n_head_cols, tn))
    n_gate, n_head = ng_cols // tn, n_head_cols // tn
    return pl.pallas_call(
        functools.partial(_proj_kernel, n_gate=n_gate),
        grid=(n_gate + n_head, s // tm),
        in_specs=[pl.BlockSpec((tm, d), lambda j, i: (i, 0)),
                  pl.BlockSpec((tn, d), lambda j, i: (jnp.minimum(j, n_gate - 1), 0)),
                  pl.BlockSpec((None, tn, d), lambda j, i: (layer, jnp.maximum(j - n_gate, 0), 0))],
        out_specs=pl.BlockSpec((tm, tn), lambda j, i: (i, j)),
        out_shape=jax.ShapeDtypeStruct((s, ng_cols + n_head_cols), BF16),
        scratch_shapes=[pltpu.VMEM((tn, d), BF16)],
        compiler_params=_params("arbitrary", "arbitrary"),
        name="proj",
    )(h, w_gate_t, w_all_t)


def _bucket_starts():
    n = np.arange(0, 2 * REL_MAX_DIST)
    max_exact = REL_BUCKETS // 2
    nf = np.maximum(n, max_exact).astype(np.float32)
    large = max_exact + (np.log(nf / np.float32(max_exact)) / np.float32(math.log(REL_MAX_DIST / max_exact))
                         * np.float32(REL_BUCKETS - max_exact)).astype(np.int32)
    bucket = np.where(n < max_exact, n, np.minimum(large, REL_BUCKETS - 1))
    assert np.all(np.diff(bucket) >= 0) and bucket[-1] == REL_BUCKETS - 1
    return [int(np.min(n[bucket >= b])) for b in range(REL_BUCKETS)]


BUCKET_STARTS = _bucket_starts()


def _bias_kernel(rb_ref, o_ref, *, t):
    h = pl.program_id(0)
    jj = lax.broadcasted_iota(jnp.int32, (t, 2 * t), 0)
    ii = lax.broadcasted_iota(jnp.int32, (t, 2 * t), 1)
    ii = jnp.where(ii >= t, ii - t, ii)
    far = rb_ref[REL_BUCKETS - 1, h]
    for n in range(2):
        rel = ii - jj + n * t
        bias = jnp.full((t, 2 * t), (rb_ref[0, h] - far) * LOG2E, F32)
        for b in range(1, REL_BUCKETS):
            bias = jnp.where(rel >= BUCKET_STARTS[b], (rb_ref[b, h] - far) * LOG2E, bias)
        o_ref[0, n] = jnp.where(rel < 0, MASK_VALUE, bias)


def _bias_tiles(rel_bias, t):
    return pl.pallas_call(
        functools.partial(_bias_kernel, t=t),
        grid=(HEADS,),
        in_specs=[pl.BlockSpec(memory_space=pltpu.SMEM)],
        out_specs=pl.BlockSpec((1, 2, t, 2 * t), lambda h: (h, 0, 0, 0)),
        out_shape=jax.ShapeDtypeStruct((HEADS, 2, t, 2 * t), F32),
        compiler_params=_params("arbitrary"),
        name="bias_tiles",
    )(rel_bias)


def _attn_kernel(q_ref, k_ref, v_ref, bias_ref, lam_ref, sub_ref, o_ref,
                 vt_sc, qt_sc, s_sc, m_sc, acc_sc, *, t, hp, lam_init):
    qi = pl.program_id(1)
    n_kv = vt_sc.shape[1]
    heads = range(hp)
    cols = [slice(hh * HEAD_W, (hh + 1) * HEAD_W) for hh in heads]

    @pl.when(qi == 0)
    def _():
        ones_tile = (lax.broadcasted_iota(jnp.int32, (V_ROWS - HEAD_W, t), 0) == 0).astype(BF16)

        def body(c, carry):
            r0 = pl.multiple_of(c * t, t)
            for hh in heads:
                vt_sc[hh, c, 0:HEAD_W] = v_ref[pl.ds(r0, t), cols[hh]].astype(F32).T.astype(BF16)
                vt_sc[hh, c, HEAD_W:V_ROWS] = ones_tile
            return carry
        lax.fori_loop(0, n_kv, body, 0)

    row = lax.broadcasted_iota(jnp.int32, (HEAD_W, t), 0)
    for hh in heads:
        qt = (q_ref[:, cols[hh]].astype(F32) * (DIFF_QK_DIM ** -0.5 * LOG2E)).T
        zero = jnp.zeros_like(qt)
        qt_sc[hh] = jnp.concatenate([jnp.where(row < DIFF_QK_DIM, qt, zero),
                                     jnp.where(row >= DIFF_QK_DIM, qt, zero)], axis=1).astype(BF16)

    m_sc[...] = jnp.full_like(m_sc, -jnp.inf)
    acc_sc[...] = jnp.zeros_like(acc_sc)

    def scores(c, slot, biased, which=heads):
        r0 = pl.multiple_of(c * t, t)
        for hh in which:
            s = _dot(k_ref[pl.ds(r0, t), cols[hh]], qt_sc[hh])
            if biased:
                s = s + bias_ref[hh, qi - c]
            s_sc[slot, hh] = s

    def consume(c, slot, which=heads):
        for hh in which:
            s = s_sc[slot, hh]
            m_old = m_sc[hh]
            m_new = jnp.maximum(m_old, jnp.max(s, axis=0, keepdims=True))
            p = jnp.exp2(s - m_new).astype(BF16)
            acc_sc[hh] = jnp.exp2(m_old - m_new) * acc_sc[hh] + _dot(vt_sc[hh, c], p)
            m_sc[hh] = m_new

    pl.when(qi >= 2)(lambda: scores(0, 0, False))
    pl.when(qi < 2)(lambda: scores(0, 0, True))

    n_pairs = jnp.maximum(qi - 2, 0) >> 1

    def pair_body(j, carry):
        c = 2 * j
        scores(c + 1, 1, False)
        for hh in heads:
            consume(c, 0, [hh])
            scores(c + 2, 0, False, [hh])
        consume(c + 1, 1)
        return carry
    lax.fori_loop(0, n_pairs, pair_body, 0)

    def body(c, carry):
        far_next = c + 1 <= qi - 2
        for slot in range(2):
            mine = (c & 1) == slot

            @pl.when(jnp.logical_and(mine, far_next))
            def _():
                scores(c + 1, 1 - slot, False)
                consume(c, slot)

            @pl.when(jnp.logical_and(mine, jnp.logical_not(far_next)))
            def _():
                scores(c + 1, 1 - slot, True)
                consume(c, slot)
        return carry
    lax.fori_loop(2 * n_pairs, qi, body, 0)

    for slot in range(2):
        pl.when((qi & 1) == slot)(functools.partial(consume, qi, slot))

    lp = lam_ref[...]
    lam = (jnp.exp(jnp.sum(lp[0:1] * lp[1:2], axis=1, keepdims=True))
           - jnp.exp(jnp.sum(lp[2:3] * lp[3:4], axis=1, keepdims=True)) + lam_init)
    for hh in heads:
        acc = acc_sc[hh]
        o = acc[0:HEAD_W] / acc[HEAD_W:HEAD_W + 1]
        od = o[:, :t] - lam * o[:, t:]
        ms = jnp.mean(od * od, axis=0, keepdims=True)
        y = od * lax.rsqrt(ms + LN_EPS) * sub_ref[...] * (1.0 - lam_init)
        o_ref[:, cols[hh]] = y.T.astype(BF16)


def _attn(proj, bias_tiles, lam_params, subln_col, *, q_blk, k_blk, v_blk, lam_init, t, hp=4):
    s = proj.shape[0]
    w = hp * HEAD_W
    assert q_blk % hp == 0 and k_blk % hp == 0 and v_blk % hp == 0
    once = pl.Buffered(1)
    return pl.pallas_call(
        functools.partial(_attn_kernel, t=t, hp=hp, lam_init=lam_init),
        grid=(HEADS // hp, s // t),
        in_specs=[pl.BlockSpec((t, w), lambda g, i: (i, q_blk // hp + g)),
                  pl.BlockSpec((s, w), lambda g, i: (0, k_blk // hp + g), pipeline_mode=once),
                  pl.BlockSpec((s, w), lambda g, i: (0, v_blk // hp + g), pipeline_mode=once),
                  pl.BlockSpec((hp, 2, t, 2 * t), lambda g, i: (g, 0, 0, 0), pipeline_mode=once),
                  pl.BlockSpec(lam_params.shape, lambda g, i: (0, 0)),
                  pl.BlockSpec((HEAD_W, 1), lambda g, i: (0, 0))],
        out_specs=pl.BlockSpec((t, w), lambda g, i: (i, g)),
        out_shape=jax.ShapeDtypeStruct((s, HEADS * HEAD_W), BF16),
        scratch_shapes=[pltpu.VMEM((hp, s // t, V_ROWS, t), BF16),
                        pltpu.VMEM((hp, HEAD_W, 2 * t), BF16),
                        pltpu.VMEM((2, hp, t, 2 * t), F32),
                        pltpu.VMEM((hp, 1, 2 * t), F32),
                        pltpu.VMEM((hp, V_ROWS, 2 * t), F32)],
        compiler_params=_params("arbitrary", "arbitrary"),
        name="diff_attn",
    )(proj, proj, proj, bias_tiles, lam_params, subln_col)


def _gates_kernel(x_ref, shift_ref, scale_ref, ws_ref, alog_ref, dtb_ref, o_ref, h_ref):
    h = (x_ref[...] * (1.0 + scale_ref[...]) + shift_ref[...]).astype(BF16)
    h_ref[...] = h
    tr = _dot_nt(ws_ref[...], h.astype(F32))
    beta = jax.nn.sigmoid(tr[0:HEADS])
    x = tr[HEADS:2 * HEADS] + dtb_ref[...]
    softplus = jnp.maximum(x, 0.0) + jnp.log1p(jnp.exp(-jnp.abs(x)))
    g = -jnp.exp(alog_ref[...]) * softplus
    pos = lax.broadcasted_iota(jnp.int32, g.shape, 1) % DN_CHUNK
    shift = 1
    while shift < DN_CHUNK:
        g = g + jnp.where(pos >= shift, pltpu.roll(g, shift, axis=1), 0.0)
        shift *= 2
    o_ref[0:HEADS] = beta
    o_ref[HEADS:2 * HEADS] = g


def _gates(x, shift, scale, w_small_t, alog_col, dtb_col, *, tb=1024):
    s, d = x.shape
    col = pl.BlockSpec((HEADS, 1), lambda i: (0, 0))
    vec = pl.BlockSpec((1, d), lambda i: (0, 0))
    return pl.pallas_call(
        _gates_kernel,
        grid=(s // tb,),
        in_specs=[pl.BlockSpec((tb, d), lambda i: (i, 0)), vec, vec,
                  pl.BlockSpec(w_small_t.shape, lambda i: (0, 0)), col, col],
        out_specs=[pl.BlockSpec((2 * HEADS, tb), lambda i: (0, i)), pl.BlockSpec((tb, d), lambda i: (i, 0))],
        out_shape=[jax.ShapeDtypeStruct((2 * HEADS, s), F32), jax.ShapeDtypeStruct((s, d), BF16)],
        compiler_params=_params("parallel"),
        name="gdn_gates",
    )(x, shift, scale, w_small_t, alog_col, dtb_col)


GROUP = 2 * DN_CHUNK
INV_BASE = 16


def _gdn_kernel(q_ref, k_ref, v_ref, z_ref, gt_ref, cwq_ref, cwk_ref, cwv_ref, nw_ref, o_ref,
                pad_sc, state_sc, o_sc, *, tb, hp):
    g = pl.program_id(0)
    ib = pl.program_id(1)
    halo = 8
    heads = range(hp)
    cols = [slice(hh * HEAD_W, (hh + 1) * HEAD_W) for hh in heads]

    @pl.when(ib == 0)
    def _():
        state_sc[...] = jnp.zeros_like(state_sc)
        pad_sc[:, 0:halo, :] = jnp.zeros((3 * hp, halo, HEAD_W), F32)

    def conv_silu(a, x_ref, cw_ref, hh):
        a = a * hp + hh
        pad_sc[a, halo:halo + tb, :] = x_ref[:, cols[hh]].astype(F32)
        cw = cw_ref[:, cols[hh]]
        y = cw[0:1] * pad_sc[a, halo - 3:halo - 3 + tb, :]
        for j in range(1, DN_CONV):
            y = y + cw[j:j + 1] * pad_sc[a, halo - 3 + j:halo - 3 + j + tb, :]
        pad_sc[a, 0:halo, :] = pad_sc[a, tb:tb + halo, :]
        return _silu(y)

    def l2n(x):
        return x * lax.rsqrt(jnp.sum(x * x, axis=-1, keepdims=True) + RMS_EPS)

    ii = lax.broadcasted_iota(jnp.int32, (GROUP, GROUP), 0)
    jj = lax.broadcasted_iota(jnp.int32, (GROUP, GROUP), 1)
    same = (ii >= DN_CHUNK) == (jj >= DN_CHUNK)
    tril = jnp.logical_and(same, ii >= jj)
    eye = (ii == jj).astype(F32)
    bits = INV_BASE.bit_length() - 1
    in_base = jnp.logical_and(ii >> bits == jj >> bits, ii > jj)
    below = []
    while (1 << bits) < DN_CHUNK:
        below.append(jnp.logical_and(ii >> (bits + 1) == jj >> (bits + 1), ii >> bits == (jj >> bits) + 1))
        bits += 1

    q, k, v, beta_c, cum_c, cum_t = [], [], [], [], [], []
    for hh in heads:
        q.append(l2n(conv_silu(0, q_ref, cwq_ref, hh)) * (HEAD_W ** -0.5))
        k.append(l2n(conv_silu(1, k_ref, cwk_ref, hh)))
        v.append(conv_silu(2, v_ref, cwv_ref, hh))
        head = g * hp + hh
        beta_t = jnp.broadcast_to(gt_ref[pl.ds(head, 1), :], (HEAD_W, tb))
        ct = jnp.broadcast_to(gt_ref[pl.ds(HEADS + head, 1), :], (HEAD_W, tb))
        beta_c.append(beta_t.T)
        cum_c.append(ct.T)
        cum_t.append(ct)

    groups = range(tb // GROUP)
    rows = [slice(r * GROUP, (r + 1) * GROUP) for r in groups]
    u, w, qk, qd, kg, cg = {}, {}, {}, {}, {}, {}

    def intra_chunk(r):
        kb, x, tinv, decay, merges = {}, {}, {}, {}, {}
        for hh in heads:
            sid = (r, hh)
            kg[sid], cg[sid] = k[hh][rows[r]], cum_c[hh][rows[r]]
            gdiff = cg[sid] - cum_t[hh][:, rows[r]]
            decay[hh] = jnp.where(tril, jnp.exp(jnp.where(tril, gdiff, 0.0)), 0.0)
            kb[hh] = kg[sid] * beta_c[hh][rows[r]]
            a = _dot_nt(kb[hh].astype(BF16), kg[sid].astype(BF16)) * decay[hh]
            diag = jnp.where(in_base, a, 0.0)
            x[hh] = diag.astype(BF16)
            tinv[hh] = eye - diag
            merges[hh] = [jnp.where(m, a, 0.0).astype(BF16) for m in below]
        yield
        power = 2
        while power < INV_BASE:
            for hh in heads:
                x[hh] = _dot(x[hh], x[hh]).astype(BF16)
            yield
            for hh in heads:
                tinv[hh] = tinv[hh] + _dot(tinv[hh].astype(BF16), x[hh])
            yield
            power *= 2
        for level in range(len(below)):
            half = {}
            for hh in heads:
                half[hh] = _dot(tinv[hh].astype(BF16), merges[hh][level]).astype(BF16)
            yield
            for hh in heads:
                tinv[hh] = tinv[hh] - _dot(half[hh], tinv[hh].astype(BF16))
            yield
        for hh in heads:
            sid = (r, hh)
            eg = jnp.exp(cg[sid])
            rhs = jnp.concatenate([v[hh][rows[r]] * beta_c[hh][rows[r]], kb[hh] * eg], axis=1).astype(BF16)
            uw = _dot(tinv[hh].astype(BF16), rhs)
            u[sid], w[sid] = uw[:, :HEAD_W], uw[:, HEAD_W:].astype(BF16)
            qg = q[hh][rows[r]]
            qk[sid] = (_dot_nt(qg.astype(BF16), kg[sid].astype(BF16)) * decay[hh]).astype(BF16)
            qd[sid] = (qg * eg).astype(BF16)
        yield

    def recurrence(r):
        v_new = {hh: [] for hh in heads}
        o_inter = {hh: [] for hh in heads}
        for c in range(2):
            cr = slice(c * DN_CHUNK, (c + 1) * DN_CHUNK)
            last = c * DN_CHUNK + DN_CHUNK - 1
            st, g_last = {}, {}
            for hh in heads:
                sid = (r, hh)
                g_last[hh] = cg[sid][last:last + 1, :]
                st[hh] = state_sc[hh]
                st16 = st[hh].astype(BF16)
                v_new[hh].append(u[sid][cr] - _dot(w[sid][cr], st16))
                o_inter[hh].append(_dot(qd[sid][cr], st16))
            yield
            for hh in heads:
                sid = (r, hh)
                kd = (kg[sid][cr] * jnp.exp(g_last[hh] - cg[sid][cr])).astype(BF16)
                state_sc[hh] = st[hh] * jnp.exp(g_last[hh]) + _dot_tn(kd, v_new[hh][c].astype(BF16))
            yield
        for hh in heads:
            vn_all = jnp.concatenate(v_new[hh], axis=0).astype(BF16)
            o_sc[hh, rows[r], :] = jnp.concatenate(o_inter[hh], axis=0) + _dot(qk[(r, hh)], vn_all)
        yield

    def interleave(slow, fast, ratio):
        slow_live = fast_live = True
        while slow_live or fast_live:
            if slow_live:
                slow_live = next(slow, "done") != "done"
            for _ in range(ratio):
                if fast_live:
                    fast_live = next(fast, "done") != "done"

    for _ in intra_chunk(0):
        pass
    for r in groups:
        nxt = intra_chunk(r + 1) if r + 1 < len(groups) else iter(())
        interleave(recurrence(r), nxt, 3)

    for hh in heads:
        o = o_sc[hh]
        on = o * lax.rsqrt(jnp.mean(o * o, axis=-1, keepdims=True) + RMS_EPS) * nw_ref[...]
        o_ref[:, cols[hh]] = (on * _silu(z_ref[:, cols[hh]].astype(F32))).astype(BF16)


def _gdn(proj, gates_t, conv_w, norm_w, *, q_blk, k_blk, v_blk, z_blk, tb=256, hp=8):
    s = proj.shape[0]
    w = hp * HEAD_W
    assert all(blk % hp == 0 for blk in (q_blk, k_blk, v_blk, z_blk))
    col = lambda blk: pl.BlockSpec((tb, w), lambda g, i: (i, blk // hp + g))
    cw = lambda blk: pl.BlockSpec((DN_CONV, w), lambda g, i: (0, blk // hp + g))
    return pl.pallas_call(
        functools.partial(_gdn_kernel, tb=tb, hp=hp),
        grid=(HEADS // hp, s // tb),
        in_specs=[col(q_blk), col(k_blk), col(v_blk), col(z_blk),
                  pl.BlockSpec((2 * HEADS, tb), lambda g, i: (0, i)),
                  cw(0), cw(HEADS), cw(2 * HEADS),
                  pl.BlockSpec((1, HEAD_W), lambda g, i: (0, 0))],
        out_specs=pl.BlockSpec((tb, w), lambda g, i: (i, g)),
        out_shape=jax.ShapeDtypeStruct((s, HEADS * HEAD_W), BF16),
        scratch_shapes=[pltpu.VMEM((3 * hp, tb + 8, HEAD_W), F32),
                        pltpu.VMEM((hp, HEAD_W, HEAD_W), F32),
                        pltpu.VMEM((hp, tb, HEAD_W), F32)],
        compiler_params=_params("arbitrary", "arbitrary"),
        name="gdn",
    )(proj, proj, proj, proj, gates_t, conv_w, conv_w, conv_w, norm_w)


def _merge_kernel(ya_ref, yb_ref, ga_ref, gb_ref, x_ref, gate_ref, lng_ref, lnb_ref,
                  wa_ref, wb_ref, wo_ref, o_ref, m_sc, *, alpha):
    i = pl.program_id(0)
    n_tiles = pl.num_programs(0) - 1

    def merge(slot):
        a = _dot(ya_ref[...], wa_ref[...])
        b = _dot(yb_ref[...], wb_ref[...])
        m_sc[slot] = (jax.nn.sigmoid(ga_ref[...].astype(F32)) * a
                      + jax.nn.sigmoid(gb_ref[...].astype(F32)) * b).astype(BF16)

    def project(slot):
        y = _dot(m_sc[slot], wo_ref[...])
        r = alpha * x_ref[...] + gate_ref[...] * y
        o_ref[...] = _layer_norm(r, lng_ref[...], lnb_ref[...])

    pl.when(i == 0)(lambda: merge(0))
    for slot in range(2):
        @pl.when(jnp.logical_and(jnp.logical_and(i > 0, i < n_tiles), (i & 1) == slot))
        def _():
            project(1 - slot)
            merge(slot)

        pl.when(jnp.logical_and(i == n_tiles, ((i - 1) & 1) == slot))(functools.partial(project, slot))


def _merge(ya, yb, proj, x, gate, lng, lnb, w_a, w_b, w_o, *, ga_blk, gb_blk, alpha, tm=256):
    s, d = x.shape
    dv = ya.shape[1]
    n = s // tm
    vec = pl.BlockSpec((1, d), lambda i: (0, 0))
    const = lambda shape: pl.BlockSpec(shape, lambda i: (0, 0), pipeline_mode=pl.Buffered(1))
    cur = lambda i: jnp.minimum(i, n - 1)
    prev = lambda i: jnp.maximum(i - 1, 0)
    return pl.pallas_call(
        functools.partial(_merge_kernel, alpha=alpha),
        grid=(n + 1,),
        in_specs=[pl.BlockSpec((tm, dv), lambda i: (cur(i), 0)),
                  pl.BlockSpec((tm, dv), lambda i: (cur(i), 0)),
                  pl.BlockSpec((tm, d), lambda i: (cur(i), ga_blk)),
                  pl.BlockSpec((tm, d), lambda i: (cur(i), gb_blk)),
                  pl.BlockSpec((tm, d), lambda i: (prev(i), 0)), vec, vec, vec,
                  const(w_a.shape), const(w_b.shape), const(w_o.shape)],
        out_specs=pl.BlockSpec((tm, d), lambda i: (prev(i), 0)),
        out_shape=jax.ShapeDtypeStruct((s, d), F32),
        scratch_shapes=[pltpu.VMEM((2, tm, d), BF16)],
        compiler_params=_params("arbitrary"),
        name="merge",
    )(ya, yb, proj, proj, x, gate, lng, lnb, w_a, w_b, w_o)


def kernel(x, c, w_ada, b_ada, ln_g, ln_b, w_ffn_in, w_ffn_out, w_in, conv_w, dn_a_log, dn_dt_bias,
           dn_norm_w, diff_lambda, diff_subln_w, rel_bias, w_branch_a, w_branch_b, w_out):
    bsz, s, d = x.shape
    assert bsz == 1, "one sequence per call"
    depth = w_ada.shape[0]
    alpha = (2 * depth) ** 0.25
    hw = HEADS * HEAD_W
    attn_t = 256
    assert d % HEAD_W == 0 and s % 1024 == 0

    o_nb = 7 * hw
    o_ga = o_nb + 2 * HEADS
    nblk = d // HEAD_W
    blk = {"ga": 0, "gb": 1, "dq": 2 * nblk, "dk": 2 * nblk + HEADS, "dv": 2 * nblk + 2 * HEADS,
           "nq": 2 * nblk + 3 * HEADS, "nk": 2 * nblk + 4 * HEADS, "nv": 2 * nblk + 5 * HEADS,
           "nz": 2 * nblk + 6 * HEADS}

    bias_tiles = _bias_tiles(rel_bias, attn_t)
    wfi, wfo = w_ffn_in, w_ffn_out
    w_in_t = jnp.swapaxes(w_in, 1, 2)
    x2 = x[0]
    for l in range(depth):
        lam_init = 0.8 - 0.6 * math.exp(-0.3 * l)
        ada = _ada(c.reshape(d, 1), w_ada[l], b_ada[l].reshape(1, -1)).reshape(N_SUB, 3, 1, d)
        shift, scale, gate = ada[:, 0], ada[:, 1], ada[:, 2]
        lng, lnb = ln_g[l].reshape(N_SUB, 1, d), ln_b[l].reshape(N_SUB, 1, d)
        w_gate_t = w_in_t[l, o_ga:].astype(BF16)
        w_small_t = w_in_t[l, o_nb:o_ga]

        x2 = _ffn(x2, shift[0], scale[0], gate[0], lng[0], lnb[0], wfi, wfo, layer=l, which=0, alpha=alpha)

        gates_t, h = _gates(x2, shift[1], scale[1], w_small_t, dn_a_log[l].reshape(HEADS, 1),
                            dn_dt_bias[l].reshape(HEADS, 1))
        proj = _proj(h, w_gate_t, w_in_t, layer=l, n_head_cols=o_nb)
        ya = _attn(proj, bias_tiles, diff_lambda[l], diff_subln_w[l].reshape(HEAD_W, 1),
                   q_blk=blk["dq"], k_blk=blk["dk"], v_blk=blk["dv"], lam_init=lam_init, t=attn_t)
        yb = _gdn(proj, gates_t, conv_w[l], dn_norm_w[l].reshape(1, HEAD_W),
                  q_blk=blk["nq"], k_blk=blk["nk"], v_blk=blk["nv"], z_blk=blk["nz"])
        x2 = _merge(ya, yb, proj, x2, gate[1], lng[1], lnb[1], w_branch_a[l].astype(BF16),
                    w_branch_b[l].astype(BF16), w_out[l].astype(BF16),
                    ga_blk=blk["ga"], gb_blk=blk["gb"], alpha=alpha)

        x2 = _ffn(x2, shift[2], scale[2], gate[2], lng[2], lnb[2], wfi, wfo, layer=l, which=1, alpha=alpha)
    return x2[None]
```

```python
import functools
import math

import numpy as np
import jax
import jax.numpy as jnp
from jax import lax
from jax.experimental import pallas as pl
from jax.experimental.pallas import tpu as pltpu

N_SUB = 3
HEADS = 8
HEAD_W = 128
DIFF_QK_DIM = 64
DN_CONV = 4
DN_CHUNK = 64
REL_BUCKETS = 32
REL_MAX_DIST = 128
LN_EPS = 1e-5
RMS_EPS = 1e-6
MASK_VALUE = -1e30
LOG2E = math.log2(math.e)
V_ROWS = HEAD_W + 16

F32 = jnp.float32
BF16 = jnp.bfloat16

V7X_VMEM_BYTES = 64 * 1024 * 1024
VMEM_LIMIT = V7X_VMEM_BYTES - 4 * 1024 * 1024


def _params(*sem):
    return pltpu.CompilerParams(dimension_semantics=sem, vmem_limit_bytes=VMEM_LIMIT)


def _tile(n, preferred):
    t = min(preferred, n)
    while n % t:
        t -= 128
    return t


def _silu(x):
    return x * jax.nn.sigmoid(x)


def _dot(a, b):
    return jnp.dot(a, b, preferred_element_type=F32)


def _dot_nt(a, b):
    return lax.dot_general(a, b, (((1,), (1,)), ((), ())), preferred_element_type=F32)


def _dot_tn(a, b):
    return lax.dot_general(a, b, (((0,), (0,)), ((), ())), preferred_element_type=F32)


def _layer_norm(r, g, b):
    mu = jnp.mean(r, axis=-1, keepdims=True)
    d = r - mu
    var = jnp.mean(d * d, axis=-1, keepdims=True)
    return d * lax.rsqrt(var + LN_EPS) * g + b


def _ada_kernel(c_ref, w_ref, b_ref, o_ref):
    sc = _silu(c_ref[...])
    o_ref[...] = jnp.sum(w_ref[...] * sc, axis=0, keepdims=True) + b_ref[...]


def _ada(c_col, w, b):
    d, n = w.shape
    tn = _tile(n, 1024)
    return pl.pallas_call(
        _ada_kernel,
        grid=(n // tn,),
        in_specs=[pl.BlockSpec((d, 1), lambda j: (0, 0)),
                  pl.BlockSpec((d, tn), lambda j: (0, j)),
                  pl.BlockSpec((1, tn), lambda j: (0, j))],
        out_specs=pl.BlockSpec((1, tn), lambda j: (0, j)),
        out_shape=jax.ShapeDtypeStruct((1, n), F32),
        compiler_params=_params("arbitrary"),
        name="ada",
    )(c_col, w, b)


def _ffn_kernel(x_ref, shift_ref, scale_ref, gate_ref, lng_ref, lnb_ref, wg_ref, wu_ref, wo_ref,
                o_ref, h_sc, a_sc, *, alpha, gate_mul):
    j = pl.program_id(1)
    n_chunks = pl.num_programs(1) - 1

    def drain(slot):
        return _dot(a_sc[slot], wo_ref[...])

    def activations(slot):
        h = h_sc[...]
        g = _dot(h, wg_ref[...].astype(BF16))
        u = _dot(h, wu_ref[...].astype(BF16))
        a_sc[slot] = (_silu(g) * u).astype(BF16)

    @pl.when(j == 0)
    def _():
        h_sc[...] = (x_ref[...] * (1.0 + scale_ref[...]) + shift_ref[...]).astype(BF16)
        o_ref[...] = jnp.zeros_like(o_ref)
        activations(0)

    for slot in range(2):
        @pl.when(jnp.logical_and(jnp.logical_and(j > 0, j < n_chunks), (j & 1) == slot))
        def _():
            o_ref[...] += drain(1 - slot)
            activations(slot)

        @pl.when(jnp.logical_and(j == n_chunks, ((j - 1) & 1) == slot))
        def _():
            r = alpha * x_ref[...] + (gate_mul * gate_ref[...]) * (o_ref[...] + drain(slot))
            o_ref[...] = _layer_norm(r, lng_ref[...], lnb_ref[...])


def _ffn(x, shift, scale, gate, lng, lnb, w_in, w_out, *, layer, which, alpha, tm=1024, tf=256):
    s, d = x.shape
    f = w_out.shape[2]
    tm, tf = _tile(s, tm), _tile(f, tf)
    nf = f // tf
    vec = pl.BlockSpec((1, d), lambda i, j: (0, 0))
    up = lambda j: jnp.minimum(j, nf - 1)
    down = lambda j: jnp.maximum(j - 1, 0)
    return pl.pallas_call(
        functools.partial(_ffn_kernel, alpha=alpha, gate_mul=0.5),
        grid=(s // tm, nf + 1),
        in_specs=[pl.BlockSpec((tm, d), lambda i, j: (i, 0)), vec, vec, vec, vec, vec,
                  pl.BlockSpec((None, None, d, tf), lambda i, j: (layer, which, 0, up(j))),
                  pl.BlockSpec((None, None, d, tf), lambda i, j: (layer, which, 0, up(j) + nf)),
                  pl.BlockSpec((None, None, tf, d), lambda i, j: (layer, which, down(j), 0))],
        out_specs=pl.BlockSpec((tm, d), lambda i, j: (i, 0)),
        out_shape=jax.ShapeDtypeStruct((s, d), F32),
        scratch_shapes=[pltpu.VMEM((tm, d), BF16), pltpu.VMEM((2, tm, tf), BF16)],
        compiler_params=_params("parallel", "arbitrary"),
        name="ffn",
    )(x, shift, scale, gate, lng, lnb, w_in, w_in, w_out)


def _proj_kernel(h_ref, wg_ref, wh_ref, o_ref, w_sc, *, n_gate):
    j = pl.program_id(0)
    first_row = pl.program_id(1) == 0

    @pl.when(jnp.logical_and(first_row, j < n_gate))
    def _():
        w_sc[...] = wg_ref[...]

    @pl.when(jnp.logical_and(first_row, j >= n_gate))
    def _():
        w_sc[...] = wh_ref[...].astype(BF16)

    o_ref[...] = _dot_nt(h_ref[...], w_sc[...]).astype(BF16)


def _proj(h, w_gate_t, w_all_t, *, layer, n_head_cols, tm=1024, tn=1024):
    s, d = h.shape
    ng_cols = w_gate_t.shape[0]
    tm = _tile(s, tm)
    tn = math.gcd(_tile(ng_cols, tn), _tile(n_head_cols, tn))
    n_gate, n_head = ng_cols // tn, n_head_cols // tn
    return pl.pallas_call(
        functools.partial(_proj_kernel, n_gate=n_gate),
        grid=(n_gate + n_head, s // tm),
        in_specs=[pl.BlockSpec((tm, d), lambda j, i: (i, 0)),
                  pl.BlockSpec((tn, d), lambda j, i: (jnp.minimum(j, n_gate - 1), 0)),
                  pl.BlockSpec((None, tn, d), lambda j, i: (layer, jnp.maximum(j - n_gate, 0), 0))],
        out_specs=pl.BlockSpec((tm, tn), lambda j, i: (i, j)),
        out_shape=jax.ShapeDtypeStruct((s, ng_cols + n_head_cols), BF16),
        scratch_shapes=[pltpu.VMEM((tn, d), BF16)],
        compiler_params=_params("arbitrary", "arbitrary"),
        name="proj",
    )(h, w_gate_t, w_all_t)


def _bucket_starts():
    n = np.arange(0, 2 * REL_MAX_DIST)
    max_exact = REL_BUCKETS // 2
    nf = np.maximum(n, max_exact).astype(np.float32)
    large = max_exact + (np.log(nf / np.float32(max_exact)) / np.float32(math.log(REL_MAX_DIST / max_exact))
                         * np.float32(REL_BUCKETS - max_exact)).astype(np.int32)
    bucket = np.where(n < max_exact, n, np.minimum(large, REL_BUCKETS - 1))
    assert np.all(np.diff(bucket) >= 0) and bucket[-1] == REL_BUCKETS - 1
    return [int(np.min(n[bucket >= b])) for b in range(REL_BUCKETS)]


BUCKET_STARTS = _bucket_starts()


def _bias_kernel(rb_ref, o_ref, *, t):
    h = pl.program_id(0)
    jj = lax.broadcasted_iota(jnp.int32, (t, 2 * t), 0)
    ii = lax.broadcasted_iota(jnp.int32, (t, 2 * t), 1)
    ii = jnp.where(ii >= t, ii - t, ii)
    far = rb_ref[REL_BUCKETS - 1, h]
    for n in range(2):
        rel = ii - jj + n * t
        bias = jnp.full((t, 2 * t), (rb_ref[0, h] - far) * LOG2E, F32)
        for b in range(1, REL_BUCKETS):
            bias = jnp.where(rel >= BUCKET_STARTS[b], (rb_ref[b, h] - far) * LOG2E, bias)
        o_ref[0, n] = jnp.where(rel < 0, MASK_VALUE, bias)


def _bias_tiles(rel_bias, t):
    return pl.pallas_call(
        functools.partial(_bias_kernel, t=t),
        grid=(HEADS,),
        in_specs=[pl.BlockSpec(memory_space=pltpu.SMEM)],
        out_specs=pl.BlockSpec((1, 2, t, 2 * t), lambda h: (h, 0, 0, 0)),
        out_shape=jax.ShapeDtypeStruct((HEADS, 2, t, 2 * t), F32),
        compiler_params=_params("arbitrary"),
        name="bias_tiles",
    )(rel_bias)


def _attn_kernel(q_ref, k_ref, v_ref, bias_ref, lam_ref, sub_ref, o_ref,
                 vt_sc, qt_sc, s_sc, m_sc, acc_sc, *, t, hp, lam_init):
    qi = pl.program_id(1)
    n_kv = vt_sc.shape[1]
    heads = range(hp)
    cols = [slice(hh * HEAD_W, (hh + 1) * HEAD_W) for hh in heads]

    @pl.when(qi == 0)
    def _():
        ones_tile = (lax.broadcasted_iota(jnp.int32, (V_ROWS - HEAD_W, t), 0) == 0).astype(BF16)

        def body(c, carry):
            r0 = pl.multiple_of(c * t, t)
            for hh in heads:
                vt_sc[hh, c, 0:HEAD_W] = v_ref[pl.ds(r0, t), cols[hh]].astype(F32).T.astype(BF16)
                vt_sc[hh, c, HEAD_W:V_ROWS] = ones_tile
            return carry
        lax.fori_loop(0, n_kv, body, 0)

    row = lax.broadcasted_iota(jnp.int32, (HEAD_W, t), 0)
    for hh in heads:
        qt = (q_ref[:, cols[hh]].astype(F32) * (DIFF_QK_DIM ** -0.5 * LOG2E)).T
        zero = jnp.zeros_like(qt)
        qt_sc[hh] = jnp.concatenate([jnp.where(row < DIFF_QK_DIM, qt, zero),
                                     jnp.where(row >= DIFF_QK_DIM, qt, zero)], axis=1).astype(BF16)

    m_sc[...] = jnp.full_like(m_sc, -jnp.inf)
    acc_sc[...] = jnp.zeros_like(acc_sc)

    def scores(c, slot, biased, which=heads):
        r0 = pl.multiple_of(c * t, t)
        for hh in which:
            s = _dot(k_ref[pl.ds(r0, t), cols[hh]], qt_sc[hh])
            if biased:
                s = s + bias_ref[hh, qi - c]
            s_sc[slot, hh] = s

    def consume(c, slot, which=heads):
        for hh in which:
            s = s_sc[slot, hh]
            m_old = m_sc[hh]
            m_new = jnp.maximum(m_old, jnp.max(s, axis=0, keepdims=True))
            p = jnp.exp2(s - m_new).astype(BF16)
            acc_sc[hh] = jnp.exp2(m_old - m_new) * acc_sc[hh] + _dot(vt_sc[hh, c], p)
            m_sc[hh] = m_new

    pl.when(qi >= 2)(lambda: scores(0, 0, False))
    pl.when(qi < 2)(lambda: scores(0, 0, True))

    n_pairs = jnp.maximum(qi - 2, 0) >> 1

    def pair_body(j, carry):
        c = 2 * j
        scores(c + 1, 1, False)
        for hh in heads:
            consume(c, 0, [hh])
            scores(c + 2, 0, False, [hh])
        consume(c + 1, 1)
        return carry
    lax.fori_loop(0, n_pairs, pair_body, 0)

    def body(c, carry):
        far_next = c + 1 <= qi - 2
        for slot in range(2):
            mine = (c & 1) == slot

            @pl.when(jnp.logical_and(mine, far_next))
            def _():
                scores(c + 1, 1 - slot, False)
                consume(c, slot)

            @pl.when(jnp.logical_and(mine, jnp.logical_not(far_next)))
            def _():
                scores(c + 1, 1 - slot, True)
                consume(c, slot)
        return carry
    lax.fori_loop(2 * n_pairs, qi, body, 0)

    for slot in range(2):
        pl.when((qi & 1) == slot)(functools.partial(consume, qi, slot))

    lp = lam_ref[...]
    lam = (jnp.exp(jnp.sum(lp[0:1] * lp[1:2], axis=1, keepdims=True))
           - jnp.exp(jnp.sum(lp[2:3] * lp[3:4], axis=1, keepdims=True)) + lam_init)
    for hh in heads:
        acc = acc_sc[hh]
        o = acc[0:HEAD_W] / acc[HEAD_W:HEAD_W + 1]
        od = o[:, :t] - lam * o[:, t:]
        ms = jnp.mean(od * od, axis=0, keepdims=True)
        y = od * lax.rsqrt(ms + LN_EPS) * sub_ref[...] * (1.0 - lam_init)
        o_ref[:, cols[hh]] = y.T.astype(BF16)


def _attn(proj, bias_tiles, lam_params, subln_col, *, q_blk, k_blk, v_blk, lam_init, t, hp=4):
    s = proj.shape[0]
    w = hp * HEAD_W
    assert q_blk % hp == 0 and k_blk % hp == 0 and v_blk % hp == 0
    once = pl.Buffered(1)
    return pl.pallas_call(
        functools.partial(_attn_kernel, t=t, hp=hp, lam_init=lam_init),
        grid=(HEADS // hp, s // t),
        in_specs=[pl.BlockSpec((t, w), lambda g, i: (i, q_blk // hp + g)),
                  pl.BlockSpec((s, w), lambda g, i: (0, k_blk // hp + g), pipeline_mode=once),
                  pl.BlockSpec((s, w), lambda g, i: (0, v_blk // hp + g), pipeline_mode=once),
                  pl.BlockSpec((hp, 2, t, 2 * t), lambda g, i: (g, 0, 0, 0), pipeline_mode=once),
                  pl.BlockSpec(lam_params.shape, lambda g, i: (0, 0)),
                  pl.BlockSpec((HEAD_W, 1), lambda g, i: (0, 0))],
        out_specs=pl.BlockSpec((t, w), lambda g, i: (i, g)),
        out_shape=jax.ShapeDtypeStruct((s, HEADS * HEAD_W), BF16),
        scratch_shapes=[pltpu.VMEM((hp, s // t, V_ROWS, t), BF16),
                        pltpu.VMEM((hp, HEAD_W, 2 * t), BF16),
                        pltpu.VMEM((2, hp, t, 2 * t), F32),
                        pltpu.VMEM((hp, 1, 2 * t), F32),
                        pltpu.VMEM((hp, V_ROWS, 2 * t), F32)],
        compiler_params=_params("arbitrary", "arbitrary"),
        name="diff_attn",
    )(proj, proj, proj, bias_tiles, lam_params, subln_col)


def _gates_kernel(x_ref, shift_ref, scale_ref, ws_ref, alog_ref, dtb_ref, o_ref, h_ref):
    h = (x_ref[...] * (1.0 + scale_ref[...]) + shift_ref[...]).astype(BF16)
    h_ref[...] = h
    tr = _dot_nt(ws_ref[...], h.astype(F32))
    beta = jax.nn.sigmoid(tr[0:HEADS])
    x = tr[HEADS:2 * HEADS] + dtb_ref[...]
    softplus = jnp.maximum(x, 0.0) + jnp.log1p(jnp.exp(-jnp.abs(x)))
    g = -jnp.exp(alog_ref[...]) * softplus
    pos = lax.broadcasted_iota(jnp.int32, g.shape, 1) % DN_CHUNK
    shift = 1
    while shift < DN_CHUNK:
        g = g + jnp.where(pos >= shift, pltpu.roll(g, shift, axis=1), 0.0)
        shift *= 2
    o_ref[0:HEADS] = beta
    o_ref[HEADS:2 * HEADS] = g


def _gates(x, shift, scale, w_small_t, alog_col, dtb_col, *, tb=1024):
    s, d = x.shape
    col = pl.BlockSpec((HEADS, 1), lambda i: (0, 0))
    vec = pl.BlockSpec((1, d), lambda i: (0, 0))
    return pl.pallas_call(
        _gates_kernel,
        grid=(s // tb,),
        in_specs=[pl.BlockSpec((tb, d), lambda i: (i, 0)), vec, vec,
                  pl.BlockSpec(w_small_t.shape, lambda i: (0, 0)), col, col],
        out_specs=[pl.BlockSpec((2 * HEADS, tb), lambda i: (0, i)), pl.BlockSpec((tb, d), lambda i: (i, 0))],
        out_shape=[jax.ShapeDtypeStruct((2 * HEADS, s), F32), jax.ShapeDtypeStruct((s, d), BF16)],
        compiler_params=_params("parallel"),
        name="gdn_gates",
    )(x, shift, scale, w_small_t, alog_col, dtb_col)


GROUP = 2 * DN_CHUNK
INV_BASE = 16


def _gdn_kernel(q_ref, k_ref, v_ref, z_ref, gt_ref, cwq_ref, cwk_ref, cwv_ref, nw_ref, o_ref,
                pad_sc, state_sc, o_sc, *, tb, hp):
    g = pl.program_id(0)
    ib = pl.program_id(1)
    halo = 8
    heads = range(hp)
    cols = [slice(hh * HEAD_W, (hh + 1) * HEAD_W) for hh in heads]

    @pl.when(ib == 0)
    def _():
        state_sc[...] = jnp.zeros_like(state_sc)
        pad_sc[:, 0:halo, :] = jnp.zeros((3 * hp, halo, HEAD_W), F32)

    def conv_silu(a, x_ref, cw_ref, hh):
        a = a * hp + hh
        pad_sc[a, halo:halo + tb, :] = x_ref[:, cols[hh]].astype(F32)
        cw = cw_ref[:, cols[hh]]
        y = cw[0:1] * pad_sc[a, halo - 3:halo - 3 + tb, :]
        for j in range(1, DN_CONV):
            y = y + cw[j:j + 1] * pad_sc[a, halo - 3 + j:halo - 3 + j + tb, :]
        pad_sc[a, 0:halo, :] = pad_sc[a, tb:tb + halo, :]
        return _silu(y)

    def l2n(x):
        return x * lax.rsqrt(jnp.sum(x * x, axis=-1, keepdims=True) + RMS_EPS)

    ii = lax.broadcasted_iota(jnp.int32, (GROUP, GROUP), 0)
    jj = lax.broadcasted_iota(jnp.int32, (GROUP, GROUP), 1)
    same = (ii >= DN_CHUNK) == (jj >= DN_CHUNK)
    tril = jnp.logical_and(same, ii >= jj)
    eye = (ii == jj).astype(F32)
    bits = INV_BASE.bit_length() - 1
    in_base = jnp.logical_and(ii >> bits == jj >> bits, ii > jj)
    below = []
    while (1 << bits) < DN_CHUNK:
        below.append(jnp.logical_and(ii >> (bits + 1) == jj >> (bits + 1), ii >> bits == (jj >> bits) + 1))
        bits += 1

    q, k, v, beta_c, cum_c, cum_t = [], [], [], [], [], []
    for hh in heads:
        q.append(l2n(conv_silu(0, q_ref, cwq_ref, hh)) * (HEAD_W ** -0.5))
        k.append(l2n(conv_silu(1, k_ref, cwk_ref, hh)))
        v.append(conv_silu(2, v_ref, cwv_ref, hh))
        head = g * hp + hh
        beta_t = jnp.broadcast_to(gt_ref[pl.ds(head, 1), :], (HEAD_W, tb))
        ct = jnp.broadcast_to(gt_ref[pl.ds(HEADS + head, 1), :], (HEAD_W, tb))
        beta_c.append(beta_t.T)
        cum_c.append(ct.T)
        cum_t.append(ct)

    groups = range(tb // GROUP)
    rows = [slice(r * GROUP, (r + 1) * GROUP) for r in groups]
    u, w, qk, qd, kg, cg = {}, {}, {}, {}, {}, {}

    def intra_chunk(r):
        kb, x, tinv, decay, merges = {}, {}, {}, {}, {}
        for hh in heads:
            sid = (r, hh)
            kg[sid], cg[sid] = k[hh][rows[r]], cum_c[hh][rows[r]]
            gdiff = cg[sid] - cum_t[hh][:, rows[r]]
            decay[hh] = jnp.where(tril, jnp.exp(jnp.where(tril, gdiff, 0.0)), 0.0)
            kb[hh] = kg[sid] * beta_c[hh][rows[r]]
            a = _dot_nt(kb[hh].astype(BF16), kg[sid].astype(BF16)) * decay[hh]
            diag = jnp.where(in_base, a, 0.0)
            x[hh] = diag.astype(BF16)
            tinv[hh] = eye - diag
            merges[hh] = [jnp.where(m, a, 0.0).astype(BF16) for m in below]
        yield
        power = 2
        while power < INV_BASE:
            for hh in heads:
                x[hh] = _dot(x[hh], x[hh]).astype(BF16)
            yield
            for hh in heads:
                tinv[hh] = tinv[hh] + _dot(tinv[hh].astype(BF16), x[hh])
            yield
            power *= 2
        for level in range(len(below)):
            half = {}
            for hh in heads:
                half[hh] = _dot(tinv[hh].astype(BF16), merges[hh][level]).astype(BF16)
            yield
            for hh in heads:
                tinv[hh] = tinv[hh] - _dot(half[hh], tinv[hh].astype(BF16))
            yield
        for hh in heads:
            sid = (r, hh)
            eg = jnp.exp(cg[sid])
            rhs = jnp.concatenate([v[hh][rows[r]] * beta_c[hh][rows[r]], kb[hh] * eg], axis=1).astype(BF16)
            uw = _dot(tinv[hh].astype(BF16), rhs)
            u[sid], w[sid] = uw[:, :HEAD_W], uw[:, HEAD_W:].astype(BF16)
            qg = q[hh][rows[r]]
            qk[sid] = (_dot_nt(qg.astype(BF16), kg[sid].astype(BF16)) * decay[hh]).astype(BF16)
            qd[sid] = (qg * eg).astype(BF16)
        yield

    def recurrence(r):
        v_new = {hh: [] for hh in heads}
        o_inter = {hh: [] for hh in heads}
        for c in range(2):
            cr = slice(c * DN_CHUNK, (c + 1) * DN_CHUNK)
            last = c * DN_CHUNK + DN_CHUNK - 1
            st, g_last = {}, {}
            for hh in heads:
                sid = (r, hh)
                g_last[hh] = cg[sid][last:last + 1, :]
                st[hh] = state_sc[hh]
                st16 = st[hh].astype(BF16)
                v_new[hh].append(u[sid][cr] - _dot(w[sid][cr], st16))
                o_inter[hh].append(_dot(qd[sid][cr], st16))
            yield
            for hh in heads:
                sid = (r, hh)
                kd = (kg[sid][cr] * jnp.exp(g_last[hh] - cg[sid][cr])).astype(BF16)
                state_sc[hh] = st[hh] * jnp.exp(g_last[hh]) + _dot_tn(kd, v_new[hh][c].astype(BF16))
            yield
        for hh in heads:
            vn_all = jnp.concatenate(v_new[hh], axis=0).astype(BF16)
            o_sc[hh, rows[r], :] = jnp.concatenate(o_inter[hh], axis=0) + _dot(qk[(r, hh)], vn_all)
        yield

    def interleave(slow, fast, ratio):
        slow_live = fast_live = True
        while slow_live or fast_live:
            if slow_live:
                slow_live = next(slow, "done") != "done"
            for _ in range(ratio):
                if fast_live:
                    fast_live = next(fast, "done") != "done"

    for _ in intra_chunk(0):
        pass
    for r in groups:
        nxt = intra_chunk(r + 1) if r + 1 < len(groups) else iter(())
        interleave(recurrence(r), nxt, 3)

    for hh in heads:
        o = o_sc[hh]
        on = o * lax.rsqrt(jnp.mean(o * o, axis=-1, keepdims=True) + RMS_EPS) * nw_ref[...]
        o_ref[:, cols[hh]] = (on * _silu(z_ref[:, cols[hh]].astype(F32))).astype(BF16)


def _gdn(proj, gates_t, conv_w, norm_w, *, q_blk, k_blk, v_blk, z_blk, tb=512, hp=8):
    s = proj.shape[0]
    w = hp * HEAD_W
    assert all(blk % hp == 0 for blk in (q_blk, k_blk, v_blk, z_blk))
    col = lambda blk: pl.BlockSpec((tb, w), lambda g, i: (i, blk // hp + g))
    cw = lambda blk: pl.BlockSpec((DN_CONV, w), lambda g, i: (0, blk // hp + g))
    return pl.pallas_call(
        functools.partial(_gdn_kernel, tb=tb, hp=hp),
        grid=(HEADS // hp, s // tb),
        in_specs=[col(q_blk), col(k_blk), col(v_blk), col(z_blk),
                  pl.BlockSpec((2 * HEADS, tb), lambda g, i: (0, i)),
                  cw(0), cw(HEADS), cw(2 * HEADS),
                  pl.BlockSpec((1, HEAD_W), lambda g, i: (0, 0))],
        out_specs=pl.BlockSpec((tb, w), lambda g, i: (i, g)),
        out_shape=jax.ShapeDtypeStruct((s, HEADS * HEAD_W), BF16),
        scratch_shapes=[pltpu.VMEM((3 * hp, tb + 8, HEAD_W), F32),
                        pltpu.VMEM((hp, HEAD_W, HEAD_W), F32),
                        pltpu.VMEM((hp, tb, HEAD_W), F32)],
        compiler_params=_params("arbitrary", "arbitrary"),
        name="gdn",
    )(proj, proj, proj, proj, gates_t, conv_w, conv_w, conv_w, norm_w)


def _merge_kernel(ya_ref, yb_ref, ga_ref, gb_ref, x_ref, gate_ref, lng_ref, lnb_ref,
                  wa_ref, wb_ref, wo_ref, o_ref, m_sc, *, alpha):
    i = pl.program_id(0)
    n_tiles = pl.num_programs(0) - 1

    def merge(slot):
        a = _dot(ya_ref[...], wa_ref[...])
        b = _dot(yb_ref[...], wb_ref[...])
        m_sc[slot] = (jax.nn.sigmoid(ga_ref[...].astype(F32)) * a
                      + jax.nn.sigmoid(gb_ref[...].astype(F32)) * b).astype(BF16)

    def project(slot):
        y = _dot(m_sc[slot], wo_ref[...])
        r = alpha * x_ref[...] + gate_ref[...] * y
        o_ref[...] = _layer_norm(r, lng_ref[...], lnb_ref[...])

    pl.when(i == 0)(lambda: merge(0))
    for slot in range(2):
        @pl.when(jnp.logical_and(jnp.logical_and(i > 0, i < n_tiles), (i & 1) == slot))
        def _():
            project(1 - slot)
            merge(slot)

        pl.when(jnp.logical_and(i == n_tiles, ((i - 1) & 1) == slot))(functools.partial(project, slot))


def _merge(ya, yb, proj, x, gate, lng, lnb, w_a, w_b, w_o, *, ga_blk, gb_blk, alpha, tm=256):
    s, d = x.shape
    dv = ya.shape[1]
    n = s // tm
    vec = pl.BlockSpec((1, d), lambda i: (0, 0))
    const = lambda shape: pl.BlockSpec(shape, lambda i: (0, 0), pipeline_mode=pl.Buffered(1))
    cur = lambda i: jnp.minimum(i, n - 1)
    prev = lambda i: jnp.maximum(i - 1, 0)
    return pl.pallas_call(
        functools.partial(_merge_kernel, alpha=alpha),
        grid=(n + 1,),
        in_specs=[pl.BlockSpec((tm, dv), lambda i: (cur(i), 0)),
                  pl.BlockSpec((tm, dv), lambda i: (cur(i), 0)),
                  pl.BlockSpec((tm, d), lambda i: (cur(i), ga_blk)),
                  pl.BlockSpec((tm, d), lambda i: (cur(i), gb_blk)),
                  pl.BlockSpec((tm, d), lambda i: (prev(i), 0)), vec, vec, vec,
                  const(w_a.shape), const(w_b.shape), const(w_o.shape)],
        out_specs=pl.BlockSpec((tm, d), lambda i: (prev(i), 0)),
        out_shape=jax.ShapeDtypeStruct((s, d), F32),
        scratch_shapes=[pltpu.VMEM((2, tm, d), BF16)],
        compiler_params=_params("arbitrary"),
        name="merge",
    )(ya, yb, proj, proj, x, gate, lng, lnb, w_a, w_b, w_o)


def kernel(x, c, w_ada, b_ada, ln_g, ln_b, w_ffn_in, w_ffn_out, w_in, conv_w, dn_a_log, dn_dt_bias,
           dn_norm_w, diff_lambda, diff_subln_w, rel_bias, w_branch_a, w_branch_b, w_out):
    bsz, s, d = x.shape
    assert bsz == 1, "one sequence per call"
    depth = w_ada.shape[0]
    alpha = (2 * depth) ** 0.25
    hw = HEADS * HEAD_W
    attn_t = 256
    assert d % HEAD_W == 0 and s % 1024 == 0

    o_nb = 7 * hw
    o_ga = o_nb + 2 * HEADS
    nblk = d // HEAD_W
    blk = {"ga": 0, "gb": 1, "dq": 2 * nblk, "dk": 2 * nblk + HEADS, "dv": 2 * nblk + 2 * HEADS,
           "nq": 2 * nblk + 3 * HEADS, "nk": 2 * nblk + 4 * HEADS, "nv": 2 * nblk + 5 * HEADS,
           "nz": 2 * nblk + 6 * HEADS}

    bias_tiles = _bias_tiles(rel_bias, attn_t)
    wfi, wfo = w_ffn_in, w_ffn_out.astype(BF16)
    w_in_t = jnp.swapaxes(w_in, 1, 2)
    x2 = x[0]
    for l in range(depth):
        lam_init = 0.8 - 0.6 * math.exp(-0.3 * l)
        ada = _ada(c.reshape(d, 1), w_ada[l], b_ada[l].reshape(1, -1)).reshape(N_SUB, 3, 1, d)
        shift, scale, gate = ada[:, 0], ada[:, 1], ada[:, 2]
        lng, lnb = ln_g[l].reshape(N_SUB, 1, d), ln_b[l].reshape(N_SUB, 1, d)
        w_gate_t = w_in_t[l, o_ga:].astype(BF16)
        w_small_t = w_in_t[l, o_nb:o_ga]

        x2 = _ffn(x2, shift[0], scale[0], gate[0], lng[0], lnb[0], wfi, wfo, layer=l, which=0, alpha=alpha)

        gates_t, h = _gates(x2, shift[1], scale[1], w_small_t, dn_a_log[l].reshape(HEADS, 1),
                            dn_dt_bias[l].reshape(HEADS, 1))
        proj = _proj(h, w_gate_t, w_in_t, layer=l, n_head_cols=o_nb)
        ya = _attn(proj, bias_tiles, diff_lambda[l], diff_subln_w[l].reshape(HEAD_W, 1),
                   q_blk=blk["dq"], k_blk=blk["dk"], v_blk=blk["dv"], lam_init=lam_init, t=attn_t)
        yb = _gdn(proj, gates_t, conv_w[l], dn_norm_w[l].reshape(1, HEAD_W),
                  q_blk=blk["nq"], k_blk=blk["nk"], v_blk=blk["nv"], z_blk=blk["nz"])
        x2 = _merge(ya, yb, proj, x2, gate[1], lng[1], lnb[1], w_branch_a[l].astype(BF16),
                    w_branch_b[l].astype(BF16), w_out[l].astype(BF16),
                    ga_blk=blk["ga"], gb_blk=blk["gb"], alpha=alpha)

        x2 = _ffn(x2, shift[2], scale[2], gate[2], lng[2], lnb[2], wfi, wfo, layer=l, which=1, alpha=alpha)
    return x2[None]
```

```python
import functools
import math

import numpy as np
import jax
import jax.numpy as jnp
from jax import lax
from jax.experimental import pallas as pl
from jax.experimental.pallas import tpu as pltpu

N_SUB = 3
HEADS = 8
HEAD_W = 128
DIFF_QK_DIM = 64
DN_CONV = 4
DN_CHUNK = 64
REL_BUCKETS = 32
REL_MAX_DIST = 128
LN_EPS = 1e-5
RMS_EPS = 1e-6
MASK_VALUE = -1e30
LOG2E = math.log2(math.e)
V_ROWS = HEAD_W + 16

F32 = jnp.float32
BF16 = jnp.bfloat16

V7X_VMEM_BYTES = 64 * 1024 * 1024
VMEM_LIMIT = V7X_VMEM_BYTES - 4 * 1024 * 1024


def _params(*sem):
    return pltpu.CompilerParams(dimension_semantics=sem, vmem_limit_bytes=VMEM_LIMIT)


def _tile(n, preferred):
    t = min(preferred, n)
    while n % t:
        t -= 128
    return t


def _silu(x):
    return x * jax.nn.sigmoid(x)


def _dot(a, b):
    return jnp.dot(a, b, preferred_element_type=F32)


def _dot_nt(a, b):
    return lax.dot_general(a, b, (((1,), (1,)), ((), ())), preferred_element_type=F32)


def _dot_tn(a, b):
    return lax.dot_general(a, b, (((0,), (0,)), ((), ())), preferred_element_type=F32)


def _layer_norm(r, g, b):
    mu = jnp.mean(r, axis=-1, keepdims=True)
    d = r - mu
    var = jnp.mean(d * d, axis=-1, keepdims=True)
    return d * lax.rsqrt(var + LN_EPS) * g + b


def _ada_kernel(c_ref, w_ref, b_ref, o_ref):
    sc = _silu(c_ref[...])
    o_ref[...] = jnp.sum(w_ref[...] * sc, axis=0, keepdims=True) + b_ref[...]


def _ada(c_col, w, b):
    d, n = w.shape
    tn = _tile(n, 1024)
    return pl.pallas_call(
        _ada_kernel,
        grid=(n // tn,),
        in_specs=[pl.BlockSpec((d, 1), lambda j: (0, 0)),
                  pl.BlockSpec((d, tn), lambda j: (0, j)),
                  pl.BlockSpec((1, tn), lambda j: (0, j))],
        out_specs=pl.BlockSpec((1, tn), lambda j: (0, j)),
        out_shape=jax.ShapeDtypeStruct((1, n), F32),
        compiler_params=_params("arbitrary"),
        name="ada",
    )(c_col, w, b)


def _ffn_kernel(x_ref, shift_ref, scale_ref, gate_ref, lng_ref, lnb_ref, wg_ref, wu_ref, wo_ref,
                o_ref, h_sc, a_sc, *, alpha, gate_mul):
    j = pl.program_id(1)
    n_chunks = pl.num_programs(1) - 1

    def drain(slot):
        return _dot(a_sc[slot], wo_ref[...].astype(BF16))

    def activations(slot):
        h = h_sc[...]
        g = _dot(h, wg_ref[...].astype(BF16))
        u = _dot(h, wu_ref[...].astype(BF16))
        a_sc[slot] = (_silu(g) * u).astype(BF16)

    @pl.when(j == 0)
    def _():
        h_sc[...] = (x_ref[...] * (1.0 + scale_ref[...]) + shift_ref[...]).astype(BF16)
        o_ref[...] = jnp.zeros_like(o_ref)
        activations(0)

    for slot in range(2):
        @pl.when(jnp.logical_and(jnp.logical_and(j > 0, j < n_chunks), (j & 1) == slot))
        def _():
            o_ref[...] += drain(1 - slot)
            activations(slot)

        @pl.when(jnp.logical_and(j == n_chunks, ((j - 1) & 1) == slot))
        def _():
            r = alpha * x_ref[...] + (gate_mul * gate_ref[...]) * (o_ref[...] + drain(slot))
            o_ref[...] = _layer_norm(r, lng_ref[...], lnb_ref[...])


def _ffn(x, shift, scale, gate, lng, lnb, w_in, w_out, *, layer, which, alpha, tm=1024, tf=256):
    s, d = x.shape
    f = w_out.shape[2]
    tm, tf = _tile(s, tm), _tile(f, tf)
    nf = f // tf
    vec = pl.BlockSpec((1, d), lambda i, j: (0, 0))
    up = lambda j: jnp.minimum(j, nf - 1)
    down = lambda j: jnp.maximum(j - 1, 0)
    return pl.pallas_call(
        functools.partial(_ffn_kernel, alpha=alpha, gate_mul=0.5),
        grid=(s // tm, nf + 1),
        in_specs=[pl.BlockSpec((tm, d), lambda i, j: (i, 0)), vec, vec, vec, vec, vec,
                  pl.BlockSpec((None, None, d, tf), lambda i, j: (layer, which, 0, up(j))),
                  pl.BlockSpec((None, None, d, tf), lambda i, j: (layer, which, 0, up(j) + nf)),
                  pl.BlockSpec((None, None, tf, d), lambda i, j: (layer, which, down(j), 0))],
        out_specs=pl.BlockSpec((tm, d), lambda i, j: (i, 0)),
        out_shape=jax.ShapeDtypeStruct((s, d), F32),
        scratch_shapes=[pltpu.VMEM((tm, d), BF16), pltpu.VMEM((2, tm, tf), BF16)],
        compiler_params=_params("parallel", "arbitrary"),
        name="ffn",
    )(x, shift, scale, gate, lng, lnb, w_in, w_in, w_out)


def _proj_kernel(h_ref, w_ref, o_ref, w_sc):
    @pl.when(pl.program_id(1) == 0)
    def _():
        w_sc[...] = w_ref[...].astype(BF16)

    o_ref[...] = _dot_nt(h_ref[...], w_sc[...]).astype(BF16)


def _proj(h, w_all_t, *, gate_row0, n_gate_cols, n_head_cols, tm=1024, tn=1024):
    s, d = h.shape
    tm = _tile(s, tm)
    tn = math.gcd(_tile(n_gate_cols, tn), _tile(n_head_cols, tn))
    n_gate, n_head = n_gate_cols // tn, n_head_cols // tn
    assert gate_row0 % 8 == 0
    first_row = lambda j: pl.multiple_of(jnp.where(j < n_gate, gate_row0 + j * tn, (j - n_gate) * tn), 8)
    return pl.pallas_call(
        _proj_kernel,
        grid=(n_gate + n_head, s // tm),
        in_specs=[pl.BlockSpec((tm, d), lambda j, i: (i, 0)),
                  pl.BlockSpec((pl.Element(tn), pl.Element(d)), lambda j, i: (first_row(j), 0))],
        out_specs=pl.BlockSpec((tm, tn), lambda j, i: (i, j)),
        out_shape=jax.ShapeDtypeStruct((s, n_gate_cols + n_head_cols), BF16),
        scratch_shapes=[pltpu.VMEM((tn, d), BF16)],
        compiler_params=_params("arbitrary", "arbitrary"),
        name="proj",
    )(h, w_all_t)


def _bucket_starts():
    n = np.arange(0, 2 * REL_MAX_DIST)
    max_exact = REL_BUCKETS // 2
    nf = np.maximum(n, max_exact).astype(np.float32)
    large = max_exact + (np.log(nf / np.float32(max_exact)) / np.float32(math.log(REL_MAX_DIST / max_exact))
                         * np.float32(REL_BUCKETS - max_exact)).astype(np.int32)
    bucket = np.where(n < max_exact, n, np.minimum(large, REL_BUCKETS - 1))
    assert np.all(np.diff(bucket) >= 0) and bucket[-1] == REL_BUCKETS - 1
    return [int(np.min(n[bucket >= b])) for b in range(REL_BUCKETS)]


BUCKET_STARTS = _bucket_starts()


def _bias_kernel(rb_ref, o_ref, *, t):
    h = pl.program_id(0)
    jj = lax.broadcasted_iota(jnp.int32, (t, 2 * t), 0)
    ii = lax.broadcasted_iota(jnp.int32, (t, 2 * t), 1)
    ii = jnp.where(ii >= t, ii - t, ii)
    far = rb_ref[REL_BUCKETS - 1, h]
    for n in range(2):
        rel = ii - jj + n * t
        bias = jnp.full((t, 2 * t), (rb_ref[0, h] - far) * LOG2E, F32)
        for b in range(1, REL_BUCKETS):
            bias = jnp.where(rel >= BUCKET_STARTS[b], (rb_ref[b, h] - far) * LOG2E, bias)
        o_ref[0, n] = jnp.where(rel < 0, MASK_VALUE, bias)


def _bias_tiles(rel_bias, t):
    return pl.pallas_call(
        functools.partial(_bias_kernel, t=t),
        grid=(HEADS,),
        in_specs=[pl.BlockSpec(memory_space=pltpu.SMEM)],
        out_specs=pl.BlockSpec((1, 2, t, 2 * t), lambda h: (h, 0, 0, 0)),
        out_shape=jax.ShapeDtypeStruct((HEADS, 2, t, 2 * t), F32),
        compiler_params=_params("arbitrary"),
        name="bias_tiles",
    )(rel_bias)


def _attn_kernel(q_ref, k_ref, v_ref, bias_ref, lam_ref, sub_ref, o_ref,
                 vt_sc, qt_sc, s_sc, m_sc, acc_sc, *, t, hp, lam_init):
    qi = pl.program_id(1)
    n_kv = vt_sc.shape[1]
    heads = range(hp)
    cols = [slice(hh * HEAD_W, (hh + 1) * HEAD_W) for hh in heads]

    @pl.when(qi == 0)
    def _():
        ones_tile = (lax.broadcasted_iota(jnp.int32, (V_ROWS - HEAD_W, t), 0) == 0).astype(BF16)

        def body(c, carry):
            r0 = pl.multiple_of(c * t, t)
            for hh in heads:
                vt_sc[hh, c, 0:HEAD_W] = v_ref[pl.ds(r0, t), cols[hh]].astype(F32).T.astype(BF16)
                vt_sc[hh, c, HEAD_W:V_ROWS] = ones_tile
            return carry
        lax.fori_loop(0, n_kv, body, 0)

    row = lax.broadcasted_iota(jnp.int32, (HEAD_W, t), 0)
    for hh in heads:
        qt = (q_ref[:, cols[hh]].astype(F32) * (DIFF_QK_DIM ** -0.5 * LOG2E)).T
        zero = jnp.zeros_like(qt)
        qt_sc[hh] = jnp.concatenate([jnp.where(row < DIFF_QK_DIM, qt, zero),
                                     jnp.where(row >= DIFF_QK_DIM, qt, zero)], axis=1).astype(BF16)

    m_sc[...] = jnp.full_like(m_sc, -jnp.inf)
    acc_sc[...] = jnp.zeros_like(acc_sc)

    def scores(c, slot, biased, which=heads):
        r0 = pl.multiple_of(c * t, t)
        for hh in which:
            s = _dot(k_ref[pl.ds(r0, t), cols[hh]], qt_sc[hh])
            if biased:
                s = s + bias_ref[hh, qi - c]
            s_sc[slot, hh] = s

    def consume(c, slot, which=heads):
        for hh in which:
            s = s_sc[slot, hh]
            m_old = m_sc[hh]
            m_new = jnp.maximum(m_old, jnp.max(s, axis=0, keepdims=True))
            p = jnp.exp2(s - m_new).astype(BF16)
            acc_sc[hh] = jnp.exp2(m_old - m_new) * acc_sc[hh] + _dot(vt_sc[hh, c], p)
            m_sc[hh] = m_new

    pl.when(qi >= 2)(lambda: scores(0, 0, False))
    pl.when(qi < 2)(lambda: scores(0, 0, True))

    n_pairs = jnp.maximum(qi - 2, 0) >> 1

    def pair_body(j, carry):
        c = 2 * j
        scores(c + 1, 1, False)
        for hh in heads:
            consume(c, 0, [hh])
            scores(c + 2, 0, False, [hh])
        consume(c + 1, 1)
        return carry
    lax.fori_loop(0, n_pairs, pair_body, 0)

    def body(c, carry):
        far_next = c + 1 <= qi - 2
        for slot in range(2):
            mine = (c & 1) == slot

            @pl.when(jnp.logical_and(mine, far_next))
            def _():
                scores(c + 1, 1 - slot, False)
                consume(c, slot)

            @pl.when(jnp.logical_and(mine, jnp.logical_not(far_next)))
            def _():
                scores(c + 1, 1 - slot, True)
                consume(c, slot)
        return carry
    lax.fori_loop(2 * n_pairs, qi, body, 0)

    for slot in range(2):
        pl.when((qi & 1) == slot)(functools.partial(consume, qi, slot))

    lp = lam_ref[...]
    lam = (jnp.exp(jnp.sum(lp[0:1] * lp[1:2], axis=1, keepdims=True))
           - jnp.exp(jnp.sum(lp[2:3] * lp[3:4], axis=1, keepdims=True)) + lam_init)
    for hh in heads:
        acc = acc_sc[hh]
        o = acc[0:HEAD_W] / acc[HEAD_W:HEAD_W + 1]
        od = o[:, :t] - lam * o[:, t:]
        ms = jnp.mean(od * od, axis=0, keepdims=True)
        y = od * lax.rsqrt(ms + LN_EPS) * sub_ref[...] * (1.0 - lam_init)
        o_ref[:, cols[hh]] = y.T.astype(BF16)


def _attn(proj, bias_tiles, lam_params, subln_col, *, q_blk, k_blk, v_blk, lam_init, t, hp=4):
    s = proj.shape[0]
    w = hp * HEAD_W
    assert q_blk % hp == 0 and k_blk % hp == 0 and v_blk % hp == 0
    once = pl.Buffered(1)
    return pl.pallas_call(
        functools.partial(_attn_kernel, t=t, hp=hp, lam_init=lam_init),
        grid=(HEADS // hp, s // t),
        in_specs=[pl.BlockSpec((t, w), lambda g, i: (i, q_blk // hp + g)),
                  pl.BlockSpec((s, w), lambda g, i: (0, k_blk // hp + g), pipeline_mode=once),
                  pl.BlockSpec((s, w), lambda g, i: (0, v_blk // hp + g), pipeline_mode=once),
                  pl.BlockSpec((hp, 2, t, 2 * t), lambda g, i: (g, 0, 0, 0), pipeline_mode=once),
                  pl.BlockSpec(lam_params.shape, lambda g, i: (0, 0)),
                  pl.BlockSpec((HEAD_W, 1), lambda g, i: (0, 0))],
        out_specs=pl.BlockSpec((t, w), lambda g, i: (i, g)),
        out_shape=jax.ShapeDtypeStruct((s, HEADS * HEAD_W), BF16),
        scratch_shapes=[pltpu.VMEM((hp, s // t, V_ROWS, t), BF16),
                        pltpu.VMEM((hp, HEAD_W, 2 * t), BF16),
                        pltpu.VMEM((2, hp, t, 2 * t), F32),
                        pltpu.VMEM((hp, 1, 2 * t), F32),
                        pltpu.VMEM((hp, V_ROWS, 2 * t), F32)],
        compiler_params=_params("arbitrary", "arbitrary"),
        name="diff_attn",
    )(proj, proj, proj, bias_tiles, lam_params, subln_col)


def _gates_kernel(x_ref, shift_ref, scale_ref, ws_ref, alog_ref, dtb_ref, o_ref, h_ref):
    h = (x_ref[...] * (1.0 + scale_ref[...]) + shift_ref[...]).astype(BF16)
    h_ref[...] = h
    tr = _dot_nt(ws_ref[...], h.astype(F32))
    beta = jax.nn.sigmoid(tr[0:HEADS])
    x = tr[HEADS:2 * HEADS] + dtb_ref[...]
    softplus = jnp.maximum(x, 0.0) + jnp.log1p(jnp.exp(-jnp.abs(x)))
    g = -jnp.exp(alog_ref[...]) * softplus
    pos = lax.broadcasted_iota(jnp.int32, g.shape, 1) % DN_CHUNK
    shift = 1
    while shift < DN_CHUNK:
        g = g + jnp.where(pos >= shift, pltpu.roll(g, shift, axis=1), 0.0)
        shift *= 2
    o_ref[0:HEADS] = beta
    o_ref[HEADS:2 * HEADS] = g


def _gates(x, shift, scale, w_small_t, alog_col, dtb_col, *, tb=1024):
    s, d = x.shape
    col = pl.BlockSpec((HEADS, 1), lambda i: (0, 0))
    vec = pl.BlockSpec((1, d), lambda i: (0, 0))
    return pl.pallas_call(
        _gates_kernel,
        grid=(s // tb,),
        in_specs=[pl.BlockSpec((tb, d), lambda i: (i, 0)), vec, vec,
                  pl.BlockSpec(w_small_t.shape, lambda i: (0, 0)), col, col],
        out_specs=[pl.BlockSpec((2 * HEADS, tb), lambda i: (0, i)), pl.BlockSpec((tb, d), lambda i: (i, 0))],
        out_shape=[jax.ShapeDtypeStruct((2 * HEADS, s), F32), jax.ShapeDtypeStruct((s, d), BF16)],
        compiler_params=_params("parallel"),
        name="gdn_gates",
    )(x, shift, scale, w_small_t, alog_col, dtb_col)


GROUP = 2 * DN_CHUNK
INV_BASE = 16


def _gdn_kernel(q_ref, k_ref, v_ref, z_ref, gt_ref, cwq_ref, cwk_ref, cwv_ref, nw_ref, o_ref,
                pad_sc, state_sc, o_sc, *, tb, hp):
    g = pl.program_id(0)
    ib = pl.program_id(1)
    halo = 8
    heads = range(hp)
    cols = [slice(hh * HEAD_W, (hh + 1) * HEAD_W) for hh in heads]

    @pl.when(ib == 0)
    def _():
        state_sc[...] = jnp.zeros_like(state_sc)
        pad_sc[:, 0:halo, :] = jnp.zeros((3 * hp, halo, HEAD_W), F32)

    def conv_silu(a, x_ref, cw_ref, hh):
        a = a * hp + hh
        pad_sc[a, halo:halo + tb, :] = x_ref[:, cols[hh]].astype(F32)
        cw = cw_ref[:, cols[hh]]
        y = cw[0:1] * pad_sc[a, halo - 3:halo - 3 + tb, :]
        for j in range(1, DN_CONV):
            y = y + cw[j:j + 1] * pad_sc[a, halo - 3 + j:halo - 3 + j + tb, :]
        pad_sc[a, 0:halo, :] = pad_sc[a, tb:tb + halo, :]
        return _silu(y)

    def l2n(x):
        return x * lax.rsqrt(jnp.sum(x * x, axis=-1, keepdims=True) + RMS_EPS)

    ii = lax.broadcasted_iota(jnp.int32, (GROUP, GROUP), 0)
    jj = lax.broadcasted_iota(jnp.int32, (GROUP, GROUP), 1)
    same = (ii >= DN_CHUNK) == (jj >= DN_CHUNK)
    tril = jnp.logical_and(same, ii >= jj)
    eye = (ii == jj).astype(F32)
    bits = INV_BASE.bit_length() - 1
    in_base = jnp.logical_and(ii >> bits == jj >> bits, ii > jj)
    below = []
    while (1 << bits) < DN_CHUNK:
        below.append(jnp.logical_and(ii >> (bits + 1) == jj >> (bits + 1), ii >> bits == (jj >> bits) + 1))
        bits += 1

    q, k, v, beta_c, cum_c, cum_t = [], [], [], [], [], []
    for hh in heads:
        q.append(l2n(conv_silu(0, q_ref, cwq_ref, hh)) * (HEAD_W ** -0.5))
        k.append(l2n(conv_silu(1, k_ref, cwk_ref, hh)))
        v.append(conv_silu(2, v_ref, cwv_ref, hh))
        head = g * hp + hh
        beta_t = jnp.broadcast_to(gt_ref[pl.ds(head, 1), :], (HEAD_W, tb))
        ct = jnp.broadcast_to(gt_ref[pl.ds(HEADS + head, 1), :], (HEAD_W, tb))
        beta_c.append(beta_t.T)
        cum_c.append(ct.T)
        cum_t.append(ct)

    groups = range(tb // GROUP)
    rows = [slice(r * GROUP, (r + 1) * GROUP) for r in groups]
    u, w, qk, qd, kg, cg = {}, {}, {}, {}, {}, {}

    def intra_chunk(r):
        kb, x, tinv, decay, merges = {}, {}, {}, {}, {}
        for hh in heads:
            sid = (r, hh)
            kg[sid], cg[sid] = k[hh][rows[r]], cum_c[hh][rows[r]]
            gdiff = cg[sid] - cum_t[hh][:, rows[r]]
            decay[hh] = jnp.where(tril, jnp.exp(jnp.where(tril, gdiff, 0.0)), 0.0)
            kb[hh] = kg[sid] * beta_c[hh][rows[r]]
            a = _dot_nt(kb[hh].astype(BF16), kg[sid].astype(BF16)) * decay[hh]
            diag = jnp.where(in_base, a, 0.0)
            x[hh] = diag.astype(BF16)
            tinv[hh] = eye - diag
            merges[hh] = [jnp.where(m, a, 0.0).astype(BF16) for m in below]
        yield
        power = 2
        while power < INV_BASE:
            for hh in heads:
                x[hh] = _dot(x[hh], x[hh]).astype(BF16)
            yield
            for hh in heads:
                tinv[hh] = tinv[hh] + _dot(tinv[hh].astype(BF16), x[hh])
            yield
            power *= 2
        for level in range(len(below)):
            half = {}
            for hh in heads:
                half[hh] = _dot(tinv[hh].astype(BF16), merges[hh][level]).astype(BF16)
            yield
            for hh in heads:
                tinv[hh] = tinv[hh] - _dot(half[hh], tinv[hh].astype(BF16))
            yield
        for hh in heads:
            sid = (r, hh)
            eg = jnp.exp(cg[sid])
            rhs = jnp.concatenate([v[hh][rows[r]] * beta_c[hh][rows[r]], kb[hh] * eg], axis=1).astype(BF16)
            uw = _dot(tinv[hh].astype(BF16), rhs)
            u[sid], w[sid] = uw[:, :HEAD_W], uw[:, HEAD_W:].astype(BF16)
            qg = q[hh][rows[r]]
            qk[sid] = (_dot_nt(qg.astype(BF16), kg[sid].astype(BF16)) * decay[hh]).astype(BF16)
            qd[sid] = (qg * eg).astype(BF16)
        yield

    def recurrence(r):
        v_new = {hh: [] for hh in heads}
        o_inter = {hh: [] for hh in heads}
        for c in range(2):
            cr = slice(c * DN_CHUNK, (c + 1) * DN_CHUNK)
            last = c * DN_CHUNK + DN_CHUNK - 1
            st, g_last = {}, {}
            for hh in heads:
                sid = (r, hh)
                g_last[hh] = cg[sid][last:last + 1, :]
                st[hh] = state_sc[hh]
                st16 = st[hh].astype(BF16)
                v_new[hh].append(u[sid][cr] - _dot(w[sid][cr], st16))
                o_inter[hh].append(_dot(qd[sid][cr], st16))
            yield
            for hh in heads:
                sid = (r, hh)
                kd = (kg[sid][cr] * jnp.exp(g_last[hh] - cg[sid][cr])).astype(BF16)
                state_sc[hh] = st[hh] * jnp.exp(g_last[hh]) + _dot_tn(kd, v_new[hh][c].astype(BF16))
            yield
        for hh in heads:
            vn_all = jnp.concatenate(v_new[hh], axis=0).astype(BF16)
            o_sc[hh, rows[r], :] = jnp.concatenate(o_inter[hh], axis=0) + _dot(qk[(r, hh)], vn_all)
        yield

    def interleave(slow, fast, ratio):
        slow_live = fast_live = True
        while slow_live or fast_live:
            if slow_live:
                slow_live = next(slow, "done") != "done"
            for _ in range(ratio):
                if fast_live:
                    fast_live = next(fast, "done") != "done"

    for _ in intra_chunk(0):
        pass
    for r in groups:
        nxt = intra_chunk(r + 1) if r + 1 < len(groups) else iter(())
        interleave(recurrence(r), nxt, 3)

    for hh in heads:
        o = o_sc[hh]
        on = o * lax.rsqrt(jnp.mean(o * o, axis=-1, keepdims=True) + RMS_EPS) * nw_ref[...]
        o_ref[:, cols[hh]] = (on * _silu(z_ref[:, cols[hh]].astype(F32))).astype(BF16)


def _gdn(proj, gates_t, conv_w, norm_w, *, q_blk, k_blk, v_blk, z_blk, tb=512, hp=8):
    s = proj.shape[0]
    w = hp * HEAD_W
    assert all(blk % hp == 0 for blk in (q_blk, k_blk, v_blk, z_blk))
    col = lambda blk: pl.BlockSpec((tb, w), lambda g, i: (i, blk // hp + g))
    cw = lambda blk: pl.BlockSpec((DN_CONV, w), lambda g, i: (0, blk // hp + g))
    return pl.pallas_call(
        functools.partial(_gdn_kernel, tb=tb, hp=hp),
        grid=(HEADS // hp, s // tb),
        in_specs=[col(q_blk), col(k_blk), col(v_blk), col(z_blk),
                  pl.BlockSpec((2 * HEADS, tb), lambda g, i: (0, i)),
                  cw(0), cw(HEADS), cw(2 * HEADS),
                  pl.BlockSpec((1, HEAD_W), lambda g, i: (0, 0))],
        out_specs=pl.BlockSpec((tb, w), lambda g, i: (i, g)),
        out_shape=jax.ShapeDtypeStruct((s, HEADS * HEAD_W), BF16),
        scratch_shapes=[pltpu.VMEM((3 * hp, tb + 8, HEAD_W), F32),
                        pltpu.VMEM((hp, HEAD_W, HEAD_W), F32),
                        pltpu.VMEM((hp, tb, HEAD_W), F32)],
        compiler_params=_params("arbitrary", "arbitrary"),
        name="gdn",
    )(proj, proj, proj, proj, gates_t, conv_w, conv_w, conv_w, norm_w)


def _merge_kernel(ya_ref, yb_ref, ga_ref, gb_ref, x_ref, gate_ref, lng_ref, lnb_ref,
                  wa_ref, wb_ref, wo_ref, o_ref, m_sc, *, alpha):
    i = pl.program_id(0)
    n_tiles = pl.num_programs(0) - 1

    def merge(slot):
        a = _dot(ya_ref[...], wa_ref[...])
        b = _dot(yb_ref[...], wb_ref[...])
        m_sc[slot] = (jax.nn.sigmoid(ga_ref[...].astype(F32)) * a
                      + jax.nn.sigmoid(gb_ref[...].astype(F32)) * b).astype(BF16)

    def project(slot):
        y = _dot(m_sc[slot], wo_ref[...])
        r = alpha * x_ref[...] + gate_ref[...] * y
        o_ref[...] = _layer_norm(r, lng_ref[...], lnb_ref[...])

    pl.when(i == 0)(lambda: merge(0))
    for slot in range(2):
        @pl.when(jnp.logical_and(jnp.logical_and(i > 0, i < n_tiles), (i & 1) == slot))
        def _():
            project(1 - slot)
            merge(slot)

        pl.when(jnp.logical_and(i == n_tiles, ((i - 1) & 1) == slot))(functools.partial(project, slot))


def _merge(ya, yb, proj, x, gate, lng, lnb, w_a, w_b, w_o, *, ga_blk, gb_blk, alpha, tm=256):
    s, d = x.shape
    dv = ya.shape[1]
    n = s // tm
    vec = pl.BlockSpec((1, d), lambda i: (0, 0))
    const = lambda shape: pl.BlockSpec(shape, lambda i: (0, 0), pipeline_mode=pl.Buffered(1))
    cur = lambda i: jnp.minimum(i, n - 1)
    prev = lambda i: jnp.maximum(i - 1, 0)
    return pl.pallas_call(
        functools.partial(_merge_kernel, alpha=alpha),
        grid=(n + 1,),
        in_specs=[pl.BlockSpec((tm, dv), lambda i: (cur(i), 0)),
                  pl.BlockSpec((tm, dv), lambda i: (cur(i), 0)),
                  pl.BlockSpec((tm, d), lambda i: (cur(i), ga_blk)),
                  pl.BlockSpec((tm, d), lambda i: (cur(i), gb_blk)),
                  pl.BlockSpec((tm, d), lambda i: (prev(i), 0)), vec, vec, vec,
                  const(w_a.shape), const(w_b.shape), const(w_o.shape)],
        out_specs=pl.BlockSpec((tm, d), lambda i: (prev(i), 0)),
        out_shape=jax.ShapeDtypeStruct((s, d), F32),
        scratch_shapes=[pltpu.VMEM((2, tm, d), BF16)],
        compiler_params=_params("arbitrary"),
        name="merge",
    )(ya, yb, proj, proj, x, gate, lng, lnb, w_a, w_b, w_o)


def kernel(x, c, w_ada, b_ada, ln_g, ln_b, w_ffn_in, w_ffn_out, w_in, conv_w, dn_a_log, dn_dt_bias,
           dn_norm_w, diff_lambda, diff_subln_w, rel_bias, w_branch_a, w_branch_b, w_out):
    bsz, s, d = x.shape
    assert bsz == 1, "one sequence per call"
    depth = w_ada.shape[0]
    alpha = (2 * depth) ** 0.25
    hw = HEADS * HEAD_W
    attn_t = 256
    assert d % HEAD_W == 0 and s % 1024 == 0

    o_nb = 7 * hw
    o_ga = o_nb + 2 * HEADS
    nblk = d // HEAD_W
    blk = {"ga": 0, "gb": 1, "dq": 2 * nblk, "dk": 2 * nblk + HEADS, "dv": 2 * nblk + 2 * HEADS,
           "nq": 2 * nblk + 3 * HEADS, "nk": 2 * nblk + 4 * HEADS, "nv": 2 * nblk + 5 * HEADS,
           "nz": 2 * nblk + 6 * HEADS}

    bias_tiles = _bias_tiles(rel_bias, attn_t)
    wfi, wfo = w_ffn_in, w_ffn_out
    w_in_t = jnp.swapaxes(w_in, 1, 2)
    x2 = x[0]
    for l in range(depth):
        lam_init = 0.8 - 0.6 * math.exp(-0.3 * l)
        ada = _ada(c.reshape(d, 1), w_ada[l], b_ada[l].reshape(1, -1)).reshape(N_SUB, 3, 1, d)
        shift, scale, gate = ada[:, 0], ada[:, 1], ada[:, 2]
        lng, lnb = ln_g[l].reshape(N_SUB, 1, d), ln_b[l].reshape(N_SUB, 1, d)
        w_small_t = w_in_t[l, o_nb:o_ga]

        x2 = _ffn(x2, shift[0], scale[0], gate[0], lng[0], lnb[0], wfi, wfo, layer=l, which=0, alpha=alpha)

        gates_t, h = _gates(x2, shift[1], scale[1], w_small_t, dn_a_log[l].reshape(HEADS, 1),
                            dn_dt_bias[l].reshape(HEADS, 1))
        proj = _proj(h, w_in_t[l], gate_row0=o_ga, n_gate_cols=2 * d, n_head_cols=o_nb)
        ya = _attn(proj, bias_tiles, diff_lambda[l], diff_subln_w[l].reshape(HEAD_W, 1),
                   q_blk=blk["dq"], k_blk=blk["dk"], v_blk=blk["dv"], lam_init=lam_init, t=attn_t)
        yb = _gdn(proj, gates_t, conv_w[l], dn_norm_w[l].reshape(1, HEAD_W),
                  q_blk=blk["nq"], k_blk=blk["nk"], v_blk=blk["nv"], z_blk=blk["nz"])
        x2 = _merge(ya, yb, proj, x2, gate[1], lng[1], lnb[1], w_branch_a[l].astype(BF16),
                    w_branch_b[l].astype(BF16), w_out[l].astype(BF16),
                    ga_blk=blk["ga"], gb_blk=blk["gb"], alpha=alpha)

        x2 = _ffn(x2, shift[2], scale[2], gate[2], lng[2], lnb[2], wfi, wfo, layer=l, which=1, alpha=alpha)
    return x2[None]
```

```python
import functools
import math

import numpy as np
import jax
import jax.numpy as jnp
from jax import lax
from jax.experimental import pallas as pl
from jax.experimental.pallas import tpu as pltpu

N_SUB = 3
HEADS = 8
HEAD_W = 128
DIFF_QK_DIM = 64
DN_CONV = 4
DN_CHUNK = 64
REL_BUCKETS = 32
REL_MAX_DIST = 128
LN_EPS = 1e-5
RMS_EPS = 1e-6
MASK_VALUE = -1e30
LOG2E = math.log2(math.e)
V_ROWS = HEAD_W + 16

F32 = jnp.float32
BF16 = jnp.bfloat16

V7X_VMEM_BYTES = 64 * 1024 * 1024
VMEM_LIMIT = V7X_VMEM_BYTES - 4 * 1024 * 1024


def _params(*sem):
    return pltpu.CompilerParams(dimension_semantics=sem, vmem_limit_bytes=VMEM_LIMIT)


def _tile(n, preferred):
    t = min(preferred, n)
    while n % t:
        t -= 128
    return t


def _silu(x):
    return x * jax.nn.sigmoid(x)


def _dot(a, b):
    return jnp.dot(a, b, preferred_element_type=F32)


def _dot_nt(a, b):
    return lax.dot_general(a, b, (((1,), (1,)), ((), ())), preferred_element_type=F32)


def _dot_tn(a, b):
    return lax.dot_general(a, b, (((0,), (0,)), ((), ())), preferred_element_type=F32)


def _layer_norm(r, g, b):
    mu = jnp.mean(r, axis=-1, keepdims=True)
    d = r - mu
    var = jnp.mean(d * d, axis=-1, keepdims=True)
    return d * lax.rsqrt(var + LN_EPS) * g + b


def _ada_kernel(c_ref, w_ref, b_ref, o_ref):
    sc = _silu(c_ref[...])
    o_ref[...] = jnp.sum(w_ref[...] * sc, axis=0, keepdims=True) + b_ref[...]


def _ada(c_col, w, b):
    d, n = w.shape
    tn = _tile(n, 1024)
    return pl.pallas_call(
        _ada_kernel,
        grid=(n // tn,),
        in_specs=[pl.BlockSpec((d, 1), lambda j: (0, 0)),
                  pl.BlockSpec((d, tn), lambda j: (0, j)),
                  pl.BlockSpec((1, tn), lambda j: (0, j))],
        out_specs=pl.BlockSpec((1, tn), lambda j: (0, j)),
        out_shape=jax.ShapeDtypeStruct((1, n), F32),
        compiler_params=_params("arbitrary"),
        name="ada",
    )(c_col, w, b)


def _ffn_kernel(x_ref, shift_ref, scale_ref, gate_ref, lng_ref, lnb_ref, wg_ref, wu_ref, wo_ref,
                o_ref, h_sc, a_sc, *, alpha, gate_mul):
    j = pl.program_id(1)
    n_chunks = pl.num_programs(1) - 1

    def drain(slot):
        return _dot(a_sc[slot], wo_ref[...].astype(BF16))

    def activations(slot):
        h = h_sc[...]
        g = _dot(h, wg_ref[...].astype(BF16))
        u = _dot(h, wu_ref[...].astype(BF16))
        a_sc[slot] = (_silu(g) * u).astype(BF16)

    @pl.when(j == 0)
    def _():
        h_sc[...] = (x_ref[...] * (1.0 + scale_ref[...]) + shift_ref[...]).astype(BF16)
        o_ref[...] = jnp.zeros_like(o_ref)
        activations(0)

    for slot in range(2):
        @pl.when(jnp.logical_and(jnp.logical_and(j > 0, j < n_chunks), (j & 1) == slot))
        def _():
            o_ref[...] += drain(1 - slot)
            activations(slot)

        @pl.when(jnp.logical_and(j == n_chunks, ((j - 1) & 1) == slot))
        def _():
            r = alpha * x_ref[...] + (gate_mul * gate_ref[...]) * (o_ref[...] + drain(slot))
            o_ref[...] = _layer_norm(r, lng_ref[...], lnb_ref[...])


def _ffn(x, shift, scale, gate, lng, lnb, w_in, w_out, *, layer, which, alpha, tm=1024, tf=256):
    s, d = x.shape
    f = w_out.shape[2]
    tm, tf = _tile(s, tm), _tile(f, tf)
    nf = f // tf
    vec = pl.BlockSpec((1, d), lambda i, j: (0, 0))
    up = lambda j: jnp.minimum(j, nf - 1)
    down = lambda j: jnp.maximum(j - 1, 0)
    return pl.pallas_call(
        functools.partial(_ffn_kernel, alpha=alpha, gate_mul=0.5),
        grid=(s // tm, nf + 1),
        in_specs=[pl.BlockSpec((tm, d), lambda i, j: (i, 0)), vec, vec, vec, vec, vec,
                  pl.BlockSpec((None, None, d, tf), lambda i, j: (layer, which, 0, up(j))),
                  pl.BlockSpec((None, None, d, tf), lambda i, j: (layer, which, 0, up(j) + nf)),
                  pl.BlockSpec((None, None, tf, d), lambda i, j: (layer, which, down(j), 0))],
        out_specs=pl.BlockSpec((tm, d), lambda i, j: (i, 0)),
        out_shape=jax.ShapeDtypeStruct((s, d), F32),
        scratch_shapes=[pltpu.VMEM((tm, d), BF16), pltpu.VMEM((2, tm, tf), BF16)],
        compiler_params=_params("parallel", "arbitrary"),
        name="ffn",
    )(x, shift, scale, gate, lng, lnb, w_in, w_in, w_out)


def _proj_kernel(h_ref, w_ref, o_ref, w_sc):
    @pl.when(pl.program_id(1) == 0)
    def _():
        w_sc[...] = w_ref[...].astype(BF16)

    o_ref[...] = _dot_nt(h_ref[...], w_sc[...]).astype(BF16)


def _proj(h, w_all_t, *, gate_row0, n_gate_cols, n_head_cols, tm=1024, tn=1024):
    s, d = h.shape
    tm = _tile(s, tm)
    tn = math.gcd(_tile(n_gate_cols, tn), _tile(n_head_cols, tn))
    n_gate, n_head = n_gate_cols // tn, n_head_cols // tn
    assert gate_row0 % 8 == 0
    first_row = lambda j: pl.multiple_of(jnp.where(j < n_gate, gate_row0 + j * tn, (j - n_gate) * tn), 8)
    return pl.pallas_call(
        _proj_kernel,
        grid=(n_gate + n_head, s // tm),
        in_specs=[pl.BlockSpec((tm, d), lambda j, i: (i, 0)),
                  pl.BlockSpec((pl.Element(tn), pl.Element(d)), lambda j, i: (first_row(j), 0))],
        out_specs=pl.BlockSpec((tm, tn), lambda j, i: (i, j)),
        out_shape=jax.ShapeDtypeStruct((s, n_gate_cols + n_head_cols), BF16),
        scratch_shapes=[pltpu.VMEM((tn, d), BF16)],
        compiler_params=_params("arbitrary", "arbitrary"),
        name="proj",
    )(h, w_all_t)


def _bucket_starts():
    n = np.arange(0, 2 * REL_MAX_DIST)
    max_exact = REL_BUCKETS // 2
    nf = np.maximum(n, max_exact).astype(np.float32)
    large = max_exact + (np.log(nf / np.float32(max_exact)) / np.float32(math.log(REL_MAX_DIST / max_exact))
                         * np.float32(REL_BUCKETS - max_exact)).astype(np.int32)
    bucket = np.where(n < max_exact, n, np.minimum(large, REL_BUCKETS - 1))
    assert np.all(np.diff(bucket) >= 0) and bucket[-1] == REL_BUCKETS - 1
    return [int(np.min(n[bucket >= b])) for b in range(REL_BUCKETS)]


BUCKET_STARTS = _bucket_starts()


def _bias_kernel(rb_ref, o_ref, *, t):
    h = pl.program_id(0)
    jj = lax.broadcasted_iota(jnp.int32, (t, 2 * t), 0)
    ii = lax.broadcasted_iota(jnp.int32, (t, 2 * t), 1)
    ii = jnp.where(ii >= t, ii - t, ii)
    far = rb_ref[REL_BUCKETS - 1, h]
    for n in range(2):
        rel = ii - jj + n * t
        bias = jnp.full((t, 2 * t), (rb_ref[0, h] - far) * LOG2E, F32)
        for b in range(1, REL_BUCKETS):
            bias = jnp.where(rel >= BUCKET_STARTS[b], (rb_ref[b, h] - far) * LOG2E, bias)
        o_ref[0, n] = jnp.where(rel < 0, MASK_VALUE, bias)


def _bias_tiles(rel_bias, t):
    return pl.pallas_call(
        functools.partial(_bias_kernel, t=t),
        grid=(HEADS,),
        in_specs=[pl.BlockSpec(memory_space=pltpu.SMEM)],
        out_specs=pl.BlockSpec((1, 2, t, 2 * t), lambda h: (h, 0, 0, 0)),
        out_shape=jax.ShapeDtypeStruct((HEADS, 2, t, 2 * t), F32),
        compiler_params=_params("arbitrary"),
        name="bias_tiles",
    )(rel_bias)


def _attn_kernel(q_ref, k_ref, v_ref, bias_ref, lam_ref, sub_ref, wa_ref, wb_ref, o_ref, wa16_ref, wb16_ref,
                 vt_sc, qt_sc, s_sc, m_sc, acc_sc, *, t, hp, lam_init):
    qi = pl.program_id(1)
    wa16_ref[...] = wa_ref[...].astype(BF16)
    wb16_ref[...] = wb_ref[...].astype(BF16)
    n_kv = vt_sc.shape[1]
    heads = range(hp)
    cols = [slice(hh * HEAD_W, (hh + 1) * HEAD_W) for hh in heads]

    @pl.when(qi == 0)
    def _():
        ones_tile = (lax.broadcasted_iota(jnp.int32, (V_ROWS - HEAD_W, t), 0) == 0).astype(BF16)

        def body(c, carry):
            r0 = pl.multiple_of(c * t, t)
            for hh in heads:
                vt_sc[hh, c, 0:HEAD_W] = v_ref[pl.ds(r0, t), cols[hh]].astype(F32).T.astype(BF16)
                vt_sc[hh, c, HEAD_W:V_ROWS] = ones_tile
            return carry
        lax.fori_loop(0, n_kv, body, 0)

    row = lax.broadcasted_iota(jnp.int32, (HEAD_W, t), 0)
    for hh in heads:
        qt = (q_ref[:, cols[hh]].astype(F32) * (DIFF_QK_DIM ** -0.5 * LOG2E)).T
        zero = jnp.zeros_like(qt)
        qt_sc[hh] = jnp.concatenate([jnp.where(row < DIFF_QK_DIM, qt, zero),
                                     jnp.where(row >= DIFF_QK_DIM, qt, zero)], axis=1).astype(BF16)

    m_sc[...] = jnp.full_like(m_sc, -jnp.inf)
    acc_sc[...] = jnp.zeros_like(acc_sc)

    def scores(c, slot, biased, which=heads):
        r0 = pl.multiple_of(c * t, t)
        for hh in which:
            s = _dot(k_ref[pl.ds(r0, t), cols[hh]], qt_sc[hh])
            if biased:
                s = s + bias_ref[hh, qi - c]
            s_sc[slot, hh] = s

    def consume(c, slot, which=heads):
        for hh in which:
            s = s_sc[slot, hh]
            m_old = m_sc[hh]
            m_new = jnp.maximum(m_old, jnp.max(s, axis=0, keepdims=True))
            p = jnp.exp2(s - m_new).astype(BF16)
            acc_sc[hh] = jnp.exp2(m_old - m_new) * acc_sc[hh] + _dot(vt_sc[hh, c], p)
            m_sc[hh] = m_new

    pl.when(qi >= 2)(lambda: scores(0, 0, False))
    pl.when(qi < 2)(lambda: scores(0, 0, True))

    n_pairs = jnp.maximum(qi - 2, 0) >> 1

    def pair_body(j, carry):
        c = 2 * j
        scores(c + 1, 1, False)
        for hh in heads:
            consume(c, 0, [hh])
            scores(c + 2, 0, False, [hh])
        consume(c + 1, 1)
        return carry
    lax.fori_loop(0, n_pairs, pair_body, 0)

    def body(c, carry):
        far_next = c + 1 <= qi - 2
        for slot in range(2):
            mine = (c & 1) == slot

            @pl.when(jnp.logical_and(mine, far_next))
            def _():
                scores(c + 1, 1 - slot, False)
                consume(c, slot)

            @pl.when(jnp.logical_and(mine, jnp.logical_not(far_next)))
            def _():
                scores(c + 1, 1 - slot, True)
                consume(c, slot)
        return carry
    lax.fori_loop(2 * n_pairs, qi, body, 0)

    for slot in range(2):
        pl.when((qi & 1) == slot)(functools.partial(consume, qi, slot))

    lp = lam_ref[...]
    lam = (jnp.exp(jnp.sum(lp[0:1] * lp[1:2], axis=1, keepdims=True))
           - jnp.exp(jnp.sum(lp[2:3] * lp[3:4], axis=1, keepdims=True)) + lam_init)
    for hh in heads:
        acc = acc_sc[hh]
        o = acc[0:HEAD_W] / acc[HEAD_W:HEAD_W + 1]
        od = o[:, :t] - lam * o[:, t:]
        ms = jnp.mean(od * od, axis=0, keepdims=True)
        y = od * lax.rsqrt(ms + LN_EPS) * sub_ref[...] * (1.0 - lam_init)
        o_ref[:, cols[hh]] = y.T.astype(BF16)


def _attn(proj, bias_tiles, lam_params, subln_col, cast_a, cast_b, cast_idx, *, q_blk, k_blk, v_blk, lam_init,
          t, hp=4):
    s = proj.shape[0]
    w = hp * HEAD_W
    assert q_blk % hp == 0 and k_blk % hp == 0 and v_blk % hp == 0
    once = pl.Buffered(1)
    n_q = s // t
    n_steps = (HEADS // hp) * n_q
    ra, rb = cast_a.shape[2] // n_steps, cast_b.shape[2] // n_steps
    assert ra * n_steps == cast_a.shape[2] and rb * n_steps == cast_b.shape[2] and ra % 16 == 0 and rb % 16 == 0
    slab_in = lambda rows, arr: pl.BlockSpec((None, None, rows, arr.shape[3]),
                                             lambda g, i: (*cast_idx, g * n_q + i, 0))
    slab_out = lambda rows, arr: pl.BlockSpec((rows, arr.shape[3]), lambda g, i: (g * n_q + i, 0))
    return pl.pallas_call(
        functools.partial(_attn_kernel, t=t, hp=hp, lam_init=lam_init),
        grid=(HEADS // hp, s // t),
        in_specs=[pl.BlockSpec((t, w), lambda g, i: (i, q_blk // hp + g)),
                  pl.BlockSpec((s, w), lambda g, i: (0, k_blk // hp + g), pipeline_mode=once),
                  pl.BlockSpec((s, w), lambda g, i: (0, v_blk // hp + g), pipeline_mode=once),
                  pl.BlockSpec((hp, 2, t, 2 * t), lambda g, i: (g, 0, 0, 0), pipeline_mode=once),
                  pl.BlockSpec(lam_params.shape, lambda g, i: (0, 0)),
                  pl.BlockSpec((HEAD_W, 1), lambda g, i: (0, 0)),
                  slab_in(ra, cast_a), slab_in(rb, cast_b)],
        out_specs=[pl.BlockSpec((t, w), lambda g, i: (i, g)), slab_out(ra, cast_a), slab_out(rb, cast_b)],
        out_shape=[jax.ShapeDtypeStruct((s, HEADS * HEAD_W), BF16),
                   jax.ShapeDtypeStruct(cast_a.shape[2:], BF16), jax.ShapeDtypeStruct(cast_b.shape[2:], BF16)],
        scratch_shapes=[pltpu.VMEM((hp, s // t, V_ROWS, t), BF16),
                        pltpu.VMEM((hp, HEAD_W, 2 * t), BF16),
                        pltpu.VMEM((2, hp, t, 2 * t), F32),
                        pltpu.VMEM((hp, 1, 2 * t), F32),
                        pltpu.VMEM((hp, V_ROWS, 2 * t), F32)],
        compiler_params=_params("arbitrary", "arbitrary"),
        name="diff_attn",
    )(proj, proj, proj, bias_tiles, lam_params, subln_col, cast_a, cast_b)


def _gates_kernel(x_ref, shift_ref, scale_ref, ws_ref, alog_ref, dtb_ref, o_ref, h_ref):
    h = (x_ref[...] * (1.0 + scale_ref[...]) + shift_ref[...]).astype(BF16)
    h_ref[...] = h
    tr = _dot_nt(ws_ref[...], h.astype(F32))
    beta = jax.nn.sigmoid(tr[0:HEADS])
    x = tr[HEADS:2 * HEADS] + dtb_ref[...]
    softplus = jnp.maximum(x, 0.0) + jnp.log1p(jnp.exp(-jnp.abs(x)))
    g = -jnp.exp(alog_ref[...]) * softplus
    pos = lax.broadcasted_iota(jnp.int32, g.shape, 1) % DN_CHUNK
    shift = 1
    while shift < DN_CHUNK:
        g = g + jnp.where(pos >= shift, pltpu.roll(g, shift, axis=1), 0.0)
        shift *= 2
    o_ref[0:HEADS] = beta
    o_ref[HEADS:2 * HEADS] = g


def _gates(x, shift, scale, w_small_t, alog_col, dtb_col, *, tb=1024):
    s, d = x.shape
    col = pl.BlockSpec((HEADS, 1), lambda i: (0, 0))
    vec = pl.BlockSpec((1, d), lambda i: (0, 0))
    return pl.pallas_call(
        _gates_kernel,
        grid=(s // tb,),
        in_specs=[pl.BlockSpec((tb, d), lambda i: (i, 0)), vec, vec,
                  pl.BlockSpec(w_small_t.shape, lambda i: (0, 0)), col, col],
        out_specs=[pl.BlockSpec((2 * HEADS, tb), lambda i: (0, i)), pl.BlockSpec((tb, d), lambda i: (i, 0))],
        out_shape=[jax.ShapeDtypeStruct((2 * HEADS, s), F32), jax.ShapeDtypeStruct((s, d), BF16)],
        compiler_params=_params("parallel"),
        name="gdn_gates",
    )(x, shift, scale, w_small_t, alog_col, dtb_col)


GROUP = 2 * DN_CHUNK
INV_BASE = 16


def _gdn_kernel(q_ref, k_ref, v_ref, z_ref, gt_ref, cwq_ref, cwk_ref, cwv_ref, nw_ref, o_ref,
                pad_sc, state_sc, o_sc, *, tb, hp):
    g = pl.program_id(0)
    ib = pl.program_id(1)
    halo = 8
    heads = range(hp)
    cols = [slice(hh * HEAD_W, (hh + 1) * HEAD_W) for hh in heads]

    @pl.when(ib == 0)
    def _():
        state_sc[...] = jnp.zeros_like(state_sc)
        pad_sc[:, 0:halo, :] = jnp.zeros((3 * hp, halo, HEAD_W), F32)

    def conv_silu(a, x_ref, cw_ref, hh):
        a = a * hp + hh
        pad_sc[a, halo:halo + tb, :] = x_ref[:, cols[hh]].astype(F32)
        cw = cw_ref[:, cols[hh]]
        y = cw[0:1] * pad_sc[a, halo - 3:halo - 3 + tb, :]
        for j in range(1, DN_CONV):
            y = y + cw[j:j + 1] * pad_sc[a, halo - 3 + j:halo - 3 + j + tb, :]
        pad_sc[a, 0:halo, :] = pad_sc[a, tb:tb + halo, :]
        return _silu(y)

    def l2n(x):
        return x * lax.rsqrt(jnp.sum(x * x, axis=-1, keepdims=True) + RMS_EPS)

    ii = lax.broadcasted_iota(jnp.int32, (GROUP, GROUP), 0)
    jj = lax.broadcasted_iota(jnp.int32, (GROUP, GROUP), 1)
    same = (ii >= DN_CHUNK) == (jj >= DN_CHUNK)
    tril = jnp.logical_and(same, ii >= jj)
    eye = (ii == jj).astype(F32)
    bits = INV_BASE.bit_length() - 1
    in_base = jnp.logical_and(ii >> bits == jj >> bits, ii > jj)
    below = []
    while (1 << bits) < DN_CHUNK:
        below.append(jnp.logical_and(ii >> (bits + 1) == jj >> (bits + 1), ii >> bits == (jj >> bits) + 1))
        bits += 1

    q, k, v, beta_c, cum_c, cum_t = [], [], [], [], [], []
    for hh in heads:
        q.append(l2n(conv_silu(0, q_ref, cwq_ref, hh)) * (HEAD_W ** -0.5))
        k.append(l2n(conv_silu(1, k_ref, cwk_ref, hh)))
        v.append(conv_silu(2, v_ref, cwv_ref, hh))
        head = g * hp + hh
        beta_t = jnp.broadcast_to(gt_ref[pl.ds(head, 1), :], (HEAD_W, tb))
        ct = jnp.broadcast_to(gt_ref[pl.ds(HEADS + head, 1), :], (HEAD_W, tb))
        beta_c.append(beta_t.T)
        cum_c.append(ct.T)
        cum_t.append(ct)

    groups = range(tb // GROUP)
    rows = [slice(r * GROUP, (r + 1) * GROUP) for r in groups]
    u, w, qk, qd, kg, cg = {}, {}, {}, {}, {}, {}

    def intra_chunk(r):
        kb, x, tinv, decay, merges = {}, {}, {}, {}, {}
        for hh in heads:
            sid = (r, hh)
            kg[sid], cg[sid] = k[hh][rows[r]], cum_c[hh][rows[r]]
            gdiff = cg[sid] - cum_t[hh][:, rows[r]]
            decay[hh] = jnp.where(tril, jnp.exp(jnp.where(tril, gdiff, 0.0)), 0.0)
            kb[hh] = kg[sid] * beta_c[hh][rows[r]]
            a = _dot_nt(kb[hh].astype(BF16), kg[sid].astype(BF16)) * decay[hh]
            diag = jnp.where(in_base, a, 0.0)
            x[hh] = diag.astype(BF16)
            tinv[hh] = eye - diag
            merges[hh] = [jnp.where(m, a, 0.0).astype(BF16) for m in below]
        yield
        power = 2
        while power < INV_BASE:
            for hh in heads:
                x[hh] = _dot(x[hh], x[hh]).astype(BF16)
            yield
            for hh in heads:
                tinv[hh] = tinv[hh] + _dot(tinv[hh].astype(BF16), x[hh])
            yield
            power *= 2
        for level in range(len(below)):
            half = {}
            for hh in heads:
                half[hh] = _dot(tinv[hh].astype(BF16), merges[hh][level]).astype(BF16)
            yield
            for hh in heads:
                tinv[hh] = tinv[hh] - _dot(half[hh], tinv[hh].astype(BF16))
            yield
        for hh in heads:
            sid = (r, hh)
            eg = jnp.exp(cg[sid])
            rhs = jnp.concatenate([v[hh][rows[r]] * beta_c[hh][rows[r]], kb[hh] * eg], axis=1).astype(BF16)
            uw = _dot(tinv[hh].astype(BF16), rhs)
            u[sid], w[sid] = uw[:, :HEAD_W], uw[:, HEAD_W:].astype(BF16)
            qg = q[hh][rows[r]]
            qk[sid] = (_dot_nt(qg.astype(BF16), kg[sid].astype(BF16)) * decay[hh]).astype(BF16)
            qd[sid] = (qg * eg).astype(BF16)
        yield

    def recurrence(r):
        v_new = {hh: [] for hh in heads}
        o_inter = {hh: [] for hh in heads}
        for c in range(2):
            cr = slice(c * DN_CHUNK, (c + 1) * DN_CHUNK)
            last = c * DN_CHUNK + DN_CHUNK - 1
            st, g_last = {}, {}
            for hh in heads:
                sid = (r, hh)
                g_last[hh] = cg[sid][last:last + 1, :]
                st[hh] = state_sc[hh]
                st16 = st[hh].astype(BF16)
                v_new[hh].append(u[sid][cr] - _dot(w[sid][cr], st16))
                o_inter[hh].append(_dot(qd[sid][cr], st16))
            yield
            for hh in heads:
                sid = (r, hh)
                kd = (kg[sid][cr] * jnp.exp(g_last[hh] - cg[sid][cr])).astype(BF16)
                state_sc[hh] = st[hh] * jnp.exp(g_last[hh]) + _dot_tn(kd, v_new[hh][c].astype(BF16))
            yield
        for hh in heads:
            vn_all = jnp.concatenate(v_new[hh], axis=0).astype(BF16)
            o_sc[hh, rows[r], :] = jnp.concatenate(o_inter[hh], axis=0) + _dot(qk[(r, hh)], vn_all)
        yield

    def interleave(slow, fast, ratio):
        slow_live = fast_live = True
        while slow_live or fast_live:
            if slow_live:
                slow_live = next(slow, "done") != "done"
            for _ in range(ratio):
                if fast_live:
                    fast_live = next(fast, "done") != "done"

    for _ in intra_chunk(0):
        pass
    for r in groups:
        nxt = intra_chunk(r + 1) if r + 1 < len(groups) else iter(())
        interleave(recurrence(r), nxt, 3)

    for hh in heads:
        o = o_sc[hh]
        on = o * lax.rsqrt(jnp.mean(o * o, axis=-1, keepdims=True) + RMS_EPS) * nw_ref[...]
        o_ref[:, cols[hh]] = (on * _silu(z_ref[:, cols[hh]].astype(F32))).astype(BF16)


def _gdn(proj, gates_t, conv_w, norm_w, *, q_blk, k_blk, v_blk, z_blk, tb=512, hp=8):
    s = proj.shape[0]
    w = hp * HEAD_W
    assert all(blk % hp == 0 for blk in (q_blk, k_blk, v_blk, z_blk))
    col = lambda blk: pl.BlockSpec((tb, w), lambda g, i: (i, blk // hp + g))
    cw = lambda blk: pl.BlockSpec((DN_CONV, w), lambda g, i: (0, blk // hp + g))
    return pl.pallas_call(
        functools.partial(_gdn_kernel, tb=tb, hp=hp),
        grid=(HEADS // hp, s // tb),
        in_specs=[col(q_blk), col(k_blk), col(v_blk), col(z_blk),
                  pl.BlockSpec((2 * HEADS, tb), lambda g, i: (0, i)),
                  cw(0), cw(HEADS), cw(2 * HEADS),
                  pl.BlockSpec((1, HEAD_W), lambda g, i: (0, 0))],
        out_specs=pl.BlockSpec((tb, w), lambda g, i: (i, g)),
        out_shape=jax.ShapeDtypeStruct((s, HEADS * HEAD_W), BF16),
        scratch_shapes=[pltpu.VMEM((3 * hp, tb + 8, HEAD_W), F32),
                        pltpu.VMEM((hp, HEAD_W, HEAD_W), F32),
                        pltpu.VMEM((hp, tb, HEAD_W), F32)],
        compiler_params=_params("arbitrary", "arbitrary"),
        name="gdn",
    )(proj, proj, proj, proj, gates_t, conv_w, conv_w, conv_w, norm_w)


def _merge_kernel(ya_ref, yb_ref, ga_ref, gb_ref, x_ref, gate_ref, lng_ref, lnb_ref,
                  wa_ref, wb_ref, wo_ref, o_ref, m_sc, *, alpha):
    i = pl.program_id(0)
    n_tiles = pl.num_programs(0) - 1

    def merge(slot):
        a = _dot(ya_ref[...], wa_ref[...])
        b = _dot(yb_ref[...], wb_ref[...])
        m_sc[slot] = (jax.nn.sigmoid(ga_ref[...].astype(F32)) * a
                      + jax.nn.sigmoid(gb_ref[...].astype(F32)) * b).astype(BF16)

    def project(slot):
        y = _dot(m_sc[slot], wo_ref[...])
        r = alpha * x_ref[...] + gate_ref[...] * y
        o_ref[...] = _layer_norm(r, lng_ref[...], lnb_ref[...])

    pl.when(i == 0)(lambda: merge(0))
    for slot in range(2):
        @pl.when(jnp.logical_and(jnp.logical_and(i > 0, i < n_tiles), (i & 1) == slot))
        def _():
            project(1 - slot)
            merge(slot)

        pl.when(jnp.logical_and(i == n_tiles, ((i - 1) & 1) == slot))(functools.partial(project, slot))


def _merge(ya, yb, proj, x, gate, lng, lnb, w_a, w_b, w_o, *, ga_blk, gb_blk, alpha, tm=256):
    s, d = x.shape
    dv = ya.shape[1]
    n = s // tm
    vec = pl.BlockSpec((1, d), lambda i: (0, 0))
    const = lambda shape: pl.BlockSpec(shape, lambda i: (0, 0), pipeline_mode=pl.Buffered(1))
    cur = lambda i: jnp.minimum(i, n - 1)
    prev = lambda i: jnp.maximum(i - 1, 0)
    return pl.pallas_call(
        functools.partial(_merge_kernel, alpha=alpha),
        grid=(n + 1,),
        in_specs=[pl.BlockSpec((tm, dv), lambda i: (cur(i), 0)),
                  pl.BlockSpec((tm, dv), lambda i: (cur(i), 0)),
                  pl.BlockSpec((tm, d), lambda i: (cur(i), ga_blk)),
                  pl.BlockSpec((tm, d), lambda i: (cur(i), gb_blk)),
                  pl.BlockSpec((tm, d), lambda i: (prev(i), 0)), vec, vec, vec,
                  const(w_a.shape), const(w_b.shape), const(w_o.shape)],
        out_specs=pl.BlockSpec((tm, d), lambda i: (prev(i), 0)),
        out_shape=jax.ShapeDtypeStruct((s, d), F32),
        scratch_shapes=[pltpu.VMEM((2, tm, d), BF16)],
        compiler_params=_params("arbitrary"),
        name="merge",
    )(ya, yb, proj, proj, x, gate, lng, lnb, w_a, w_b, w_o)


def kernel(x, c, w_ada, b_ada, ln_g, ln_b, w_ffn_in, w_ffn_out, w_in, conv_w, dn_a_log, dn_dt_bias,
           dn_norm_w, diff_lambda, diff_subln_w, rel_bias, w_branch_a, w_branch_b, w_out):
    bsz, s, d = x.shape
    assert bsz == 1, "one sequence per call"
    depth = w_ada.shape[0]
    alpha = (2 * depth) ** 0.25
    hw = HEADS * HEAD_W
    attn_t = 256
    assert d % HEAD_W == 0 and s % 1024 == 0

    o_nb = 7 * hw
    o_ga = o_nb + 2 * HEADS
    nblk = d // HEAD_W
    blk = {"ga": 0, "gb": 1, "dq": 2 * nblk, "dk": 2 * nblk + HEADS, "dv": 2 * nblk + 2 * HEADS,
           "nq": 2 * nblk + 3 * HEADS, "nk": 2 * nblk + 4 * HEADS, "nv": 2 * nblk + 5 * HEADS,
           "nz": 2 * nblk + 6 * HEADS}

    bias_tiles = _bias_tiles(rel_bias, attn_t)
    wfi, wfo = w_ffn_in, w_ffn_out
    w_in_t = jnp.swapaxes(w_in, 1, 2)
    x2 = x[0]
    for l in range(depth):
        lam_init = 0.8 - 0.6 * math.exp(-0.3 * l)
        ada = _ada(c.reshape(d, 1), w_ada[l], b_ada[l].reshape(1, -1)).reshape(N_SUB, 3, 1, d)
        shift, scale, gate = ada[:, 0], ada[:, 1], ada[:, 2]
        lng, lnb = ln_g[l].reshape(N_SUB, 1, d), ln_b[l].reshape(N_SUB, 1, d)
        w_small_t = w_in_t[l, o_nb:o_ga]

        x2 = _ffn(x2, shift[0], scale[0], gate[0], lng[0], lnb[0], wfi, wfo, layer=l, which=0, alpha=alpha)

        gates_t, h = _gates(x2, shift[1], scale[1], w_small_t, dn_a_log[l].reshape(HEADS, 1),
                            dn_dt_bias[l].reshape(HEADS, 1))
        proj = _proj(h, w_in_t[l], gate_row0=o_ga, n_gate_cols=2 * d, n_head_cols=o_nb)
        f2 = w_ffn_in.shape[-1]
        ya, wfi2, wfo2 = _attn(proj, bias_tiles, diff_lambda[l], diff_subln_w[l].reshape(HEAD_W, 1),
                               w_ffn_in, w_ffn_out.reshape(depth, 2, -1, f2), (l, 1),
                               q_blk=blk["dq"], k_blk=blk["dk"], v_blk=blk["dv"], lam_init=lam_init, t=attn_t)
        yb = _gdn(proj, gates_t, conv_w[l], dn_norm_w[l].reshape(1, HEAD_W),
                  q_blk=blk["nq"], k_blk=blk["nk"], v_blk=blk["nv"], z_blk=blk["nz"])
        x2 = _merge(ya, yb, proj, x2, gate[1], lng[1], lnb[1], w_branch_a[l].astype(BF16),
                    w_branch_b[l].astype(BF16), w_out[l].astype(BF16),
                    ga_blk=blk["ga"], gb_blk=blk["gb"], alpha=alpha)

        x2 = _ffn(x2, shift[2], scale[2], gate[2], lng[2], lnb[2], wfi2[None, None],
                  wfo2.reshape(w_ffn_out.shape[2:])[None, None], layer=0, which=0, alpha=alpha, tf=512)
    return x2[None]
```

```python
import functools
import math

import numpy as np
import jax
import jax.numpy as jnp
from jax import lax
from jax.experimental import pallas as pl
from jax.experimental.pallas import tpu as pltpu

N_SUB = 3
HEADS = 8
HEAD_W = 128
DIFF_QK_DIM = 64
DN_CONV = 4
DN_CHUNK = 64
REL_BUCKETS = 32
REL_MAX_DIST = 128
LN_EPS = 1e-5
RMS_EPS = 1e-6
MASK_VALUE = -1e30
LOG2E = math.log2(math.e)
V_ROWS = HEAD_W + 16

F32 = jnp.float32
BF16 = jnp.bfloat16

V7X_VMEM_BYTES = 64 * 1024 * 1024
VMEM_LIMIT = V7X_VMEM_BYTES - 4 * 1024 * 1024


def _params(*sem):
    return pltpu.CompilerParams(dimension_semantics=sem, vmem_limit_bytes=VMEM_LIMIT)


def _tile(n, preferred):
    t = min(preferred, n)
    while n % t:
        t -= 128
    return t


def _silu(x):
    return x * jax.nn.sigmoid(x)


def _dot(a, b):
    return jnp.dot(a, b, preferred_element_type=F32)


def _dot_nt(a, b):
    return lax.dot_general(a, b, (((1,), (1,)), ((), ())), preferred_element_type=F32)


def _dot_tn(a, b):
    return lax.dot_general(a, b, (((0,), (0,)), ((), ())), preferred_element_type=F32)


def _layer_norm(r, g, b):
    mu = jnp.mean(r, axis=-1, keepdims=True)
    d = r - mu
    var = jnp.mean(d * d, axis=-1, keepdims=True)
    return d * lax.rsqrt(var + LN_EPS) * g + b


def _ada_kernel(c_ref, w_ref, b_ref, o_ref):
    sc = _silu(c_ref[...])
    o_ref[...] = jnp.sum(w_ref[...] * sc, axis=0, keepdims=True) + b_ref[...]


def _ada(c_col, w, b):
    d, n = w.shape
    tn = _tile(n, 1024)
    return pl.pallas_call(
        _ada_kernel,
        grid=(n // tn,),
        in_specs=[pl.BlockSpec((d, 1), lambda j: (0, 0)),
                  pl.BlockSpec((d, tn), lambda j: (0, j)),
                  pl.BlockSpec((1, tn), lambda j: (0, j))],
        out_specs=pl.BlockSpec((1, tn), lambda j: (0, j)),
        out_shape=jax.ShapeDtypeStruct((1, n), F32),
        compiler_params=_params("arbitrary"),
        name="ada",
    )(c_col, w, b)


def _ffn_kernel(x_ref, shift_ref, scale_ref, gate_ref, lng_ref, lnb_ref, wg_ref, wu_ref, wo_ref,
                o_ref, h_sc, a_sc, *, alpha, gate_mul):
    j = pl.program_id(1)
    n_chunks = pl.num_programs(1) - 1

    def drain(slot):
        return _dot(a_sc[slot], wo_ref[...].astype(BF16))

    def activations(slot):
        h = h_sc[...]
        g = _dot(h, wg_ref[...].astype(BF16))
        u = _dot(h, wu_ref[...].astype(BF16))
        a_sc[slot] = (_silu(g) * u).astype(BF16)

    @pl.when(j == 0)
    def _():
        h_sc[...] = (x_ref[...] * (1.0 + scale_ref[...]) + shift_ref[...]).astype(BF16)
        o_ref[...] = jnp.zeros_like(o_ref)
        activations(0)

    for slot in range(2):
        @pl.when(jnp.logical_and(jnp.logical_and(j > 0, j < n_chunks), (j & 1) == slot))
        def _():
            o_ref[...] += drain(1 - slot)
            activations(slot)

        @pl.when(jnp.logical_and(j == n_chunks, ((j - 1) & 1) == slot))
        def _():
            r = alpha * x_ref[...] + (gate_mul * gate_ref[...]) * (o_ref[...] + drain(slot))
            o_ref[...] = _layer_norm(r, lng_ref[...], lnb_ref[...])


def _ffn(x, shift, scale, gate, lng, lnb, w_in, w_out, *, layer, which, alpha, tm=1024, tf=256):
    s, d = x.shape
    f = w_out.shape[2]
    tm, tf = _tile(s, tm), _tile(f, tf)
    nf = f // tf
    vec = pl.BlockSpec((1, d), lambda i, j: (0, 0))
    up = lambda j: jnp.minimum(j, nf - 1)
    down = lambda j: jnp.maximum(j - 1, 0)
    return pl.pallas_call(
        functools.partial(_ffn_kernel, alpha=alpha, gate_mul=0.5),
        grid=(s // tm, nf + 1),
        in_specs=[pl.BlockSpec((tm, d), lambda i, j: (i, 0)), vec, vec, vec, vec, vec,
                  pl.BlockSpec((None, None, d, tf), lambda i, j: (layer, which, 0, up(j))),
                  pl.BlockSpec((None, None, d, tf), lambda i, j: (layer, which, 0, up(j) + nf)),
                  pl.BlockSpec((None, None, tf, d), lambda i, j: (layer, which, down(j), 0))],
        out_specs=pl.BlockSpec((tm, d), lambda i, j: (i, 0)),
        out_shape=jax.ShapeDtypeStruct((s, d), F32),
        scratch_shapes=[pltpu.VMEM((tm, d), BF16), pltpu.VMEM((2, tm, tf), BF16)],
        compiler_params=_params("parallel", "arbitrary"),
        name="ffn",
    )(x, shift, scale, gate, lng, lnb, w_in, w_in, w_out)


def _proj_kernel(h_ref, w_ref, ca_ref, cb_ref, o_ref, ca16_ref, cb16_ref, w_sc):
    @pl.when(pl.program_id(1) == 0)
    def _():
        w_sc[...] = w_ref[...].astype(BF16)

    o_ref[...] = _dot_nt(h_ref[...], w_sc[...]).astype(BF16)
    ca16_ref[...] = ca_ref[...].astype(BF16)
    cb16_ref[...] = cb_ref[...].astype(BF16)


def _slab_rows(rows, n_steps):
    r = 16
    while rows % r or rows // r > n_steps:
        r += 16
    return r


def _proj(h, w_all_t, cast_a, cast_b, cast_idx, *, gate_row0, n_gate_cols, n_head_cols, tm=1024, tn=1024):
    s, d = h.shape
    tm = _tile(s, tm)
    tn = math.gcd(_tile(n_gate_cols, tn), _tile(n_head_cols, tn))
    n_gate, n_head = n_gate_cols // tn, n_head_cols // tn
    assert gate_row0 % 8 == 0
    first_row = lambda j: pl.multiple_of(jnp.where(j < n_gate, gate_row0 + j * tn, (j - n_gate) * tn), 8)
    n_rows = s // tm
    n_steps = (n_gate + n_head) * n_rows

    def slabs(arr):
        rows, cols = arr.shape[2:]
        r = _slab_rows(rows, n_steps)
        step = lambda j, i: jnp.minimum(j * n_rows + i, rows // r - 1)
        return (pl.BlockSpec((None, None, r, cols), lambda j, i: (*cast_idx, step(j, i), 0)),
                pl.BlockSpec((r, cols), lambda j, i: (step(j, i), 0)), jax.ShapeDtypeStruct((rows, cols), BF16))
    (a_in, a_out, a_shape), (b_in, b_out, b_shape) = slabs(cast_a), slabs(cast_b)
    return pl.pallas_call(
        _proj_kernel,
        grid=(n_gate + n_head, n_rows),
        in_specs=[pl.BlockSpec((tm, d), lambda j, i: (i, 0)),
                  pl.BlockSpec((pl.Element(tn), pl.Element(d)), lambda j, i: (first_row(j), 0)), a_in, b_in],
        out_specs=[pl.BlockSpec((tm, tn), lambda j, i: (i, j)), a_out, b_out],
        out_shape=[jax.ShapeDtypeStruct((s, n_gate_cols + n_head_cols), BF16), a_shape, b_shape],
        scratch_shapes=[pltpu.VMEM((tn, d), BF16)],
        compiler_params=_params("arbitrary", "arbitrary"),
        name="proj",
    )(h, w_all_t, cast_a, cast_b)


def _bucket_starts():
    n = np.arange(0, 2 * REL_MAX_DIST)
    max_exact = REL_BUCKETS // 2
    nf = np.maximum(n, max_exact).astype(np.float32)
    large = max_exact + (np.log(nf / np.float32(max_exact)) / np.float32(math.log(REL_MAX_DIST / max_exact))
                         * np.float32(REL_BUCKETS - max_exact)).astype(np.int32)
    bucket = np.where(n < max_exact, n, np.minimum(large, REL_BUCKETS - 1))
    assert np.all(np.diff(bucket) >= 0) and bucket[-1] == REL_BUCKETS - 1
    return [int(np.min(n[bucket >= b])) for b in range(REL_BUCKETS)]


BUCKET_STARTS = _bucket_starts()


def _bias_kernel(rb_ref, o_ref, *, t):
    h = pl.program_id(0)
    jj = lax.broadcasted_iota(jnp.int32, (t, 2 * t), 0)
    ii = lax.broadcasted_iota(jnp.int32, (t, 2 * t), 1)
    ii = jnp.where(ii >= t, ii - t, ii)
    far = rb_ref[REL_BUCKETS - 1, h]
    for n in range(2):
        rel = ii - jj + n * t
        bias = jnp.full((t, 2 * t), (rb_ref[0, h] - far) * LOG2E, F32)
        for b in range(1, REL_BUCKETS):
            bias = jnp.where(rel >= BUCKET_STARTS[b], (rb_ref[b, h] - far) * LOG2E, bias)
        o_ref[0, n] = jnp.where(rel < 0, MASK_VALUE, bias)


def _bias_tiles(rel_bias, t):
    return pl.pallas_call(
        functools.partial(_bias_kernel, t=t),
        grid=(HEADS,),
        in_specs=[pl.BlockSpec(memory_space=pltpu.SMEM)],
        out_specs=pl.BlockSpec((1, 2, t, 2 * t), lambda h: (h, 0, 0, 0)),
        out_shape=jax.ShapeDtypeStruct((HEADS, 2, t, 2 * t), F32),
        compiler_params=_params("arbitrary"),
        name="bias_tiles",
    )(rel_bias)


def _attn_kernel(q_ref, k_ref, v_ref, bias_ref, lam_ref, sub_ref, o_ref,
                 vt_sc, qt_sc, s_sc, m_sc, acc_sc, *, t, hp, lam_init):
    qi = pl.program_id(1)
    n_kv = vt_sc.shape[1]
    heads = range(hp)
    cols = [slice(hh * HEAD_W, (hh + 1) * HEAD_W) for hh in heads]

    @pl.when(qi == 0)
    def _():
        ones_tile = (lax.broadcasted_iota(jnp.int32, (V_ROWS - HEAD_W, t), 0) == 0).astype(BF16)

        def body(c, carry):
            r0 = pl.multiple_of(c * t, t)
            for hh in heads:
                vt_sc[hh, c, 0:HEAD_W] = v_ref[pl.ds(r0, t), cols[hh]].astype(F32).T.astype(BF16)
                vt_sc[hh, c, HEAD_W:V_ROWS] = ones_tile
            return carry
        lax.fori_loop(0, n_kv, body, 0)

    row = lax.broadcasted_iota(jnp.int32, (HEAD_W, t), 0)
    for hh in heads:
        qt = (q_ref[:, cols[hh]].astype(F32) * (DIFF_QK_DIM ** -0.5 * LOG2E)).T
        zero = jnp.zeros_like(qt)
        qt_sc[hh] = jnp.concatenate([jnp.where(row < DIFF_QK_DIM, qt, zero),
                                     jnp.where(row >= DIFF_QK_DIM, qt, zero)], axis=1).astype(BF16)

    m_sc[...] = jnp.full_like(m_sc, -jnp.inf)
    acc_sc[...] = jnp.zeros_like(acc_sc)

    def scores(c, slot, biased, which=heads):
        r0 = pl.multiple_of(c * t, t)
        for hh in which:
            s = _dot(k_ref[pl.ds(r0, t), cols[hh]], qt_sc[hh])
            if biased:
                s = s + bias_ref[hh, qi - c]
            s_sc[slot, hh] = s

    def consume(c, slot, which=heads):
        for hh in which:
            s = s_sc[slot, hh]
            m_old = m_sc[hh]
            m_new = jnp.maximum(m_old, jnp.max(s, axis=0, keepdims=True))
            p = jnp.exp2(s - m_new).astype(BF16)
            acc_sc[hh] = jnp.exp2(m_old - m_new) * acc_sc[hh] + _dot(vt_sc[hh, c], p)
            m_sc[hh] = m_new

    pl.when(qi >= 2)(lambda: scores(0, 0, False))
    pl.when(qi < 2)(lambda: scores(0, 0, True))

    n_pairs = jnp.maximum(qi - 2, 0) >> 1

    def pair_body(j, carry):
        c = 2 * j
        scores(c + 1, 1, False)
        for hh in heads:
            consume(c, 0, [hh])
            scores(c + 2, 0, False, [hh])
        consume(c + 1, 1)
        return carry
    lax.fori_loop(0, n_pairs, pair_body, 0)

    def body(c, carry):
        far_next = c + 1 <= qi - 2
        for slot in range(2):
            mine = (c & 1) == slot

            @pl.when(jnp.logical_and(mine, far_next))
            def _():
                scores(c + 1, 1 - slot, False)
                consume(c, slot)

            @pl.when(jnp.logical_and(mine, jnp.logical_not(far_next)))
            def _():
                scores(c + 1, 1 - slot, True)
                consume(c, slot)
        return carry
    lax.fori_loop(2 * n_pairs, qi, body, 0)

    for slot in range(2):
        pl.when((qi & 1) == slot)(functools.partial(consume, qi, slot))

    lp = lam_ref[...]
    lam = (jnp.exp(jnp.sum(lp[0:1] * lp[1:2], axis=1, keepdims=True))
           - jnp.exp(jnp.sum(lp[2:3] * lp[3:4], axis=1, keepdims=True)) + lam_init)
    for hh in heads:
        acc = acc_sc[hh]
        o = acc[0:HEAD_W] / acc[HEAD_W:HEAD_W + 1]
        od = o[:, :t] - lam * o[:, t:]
        ms = jnp.mean(od * od, axis=0, keepdims=True)
        y = od * lax.rsqrt(ms + LN_EPS) * sub_ref[...] * (1.0 - lam_init)
        o_ref[:, cols[hh]] = y.T.astype(BF16)


def _attn(proj, bias_tiles, lam_params, subln_col, *, q_blk, k_blk, v_blk, lam_init, t, hp=4):
    s = proj.shape[0]
    w = hp * HEAD_W
    assert q_blk % hp == 0 and k_blk % hp == 0 and v_blk % hp == 0
    once = pl.Buffered(1)
    return pl.pallas_call(
        functools.partial(_attn_kernel, t=t, hp=hp, lam_init=lam_init),
        grid=(HEADS // hp, s // t),
        in_specs=[pl.BlockSpec((t, w), lambda g, i: (i, q_blk // hp + g)),
                  pl.BlockSpec((s, w), lambda g, i: (0, k_blk // hp + g), pipeline_mode=once),
                  pl.BlockSpec((s, w), lambda g, i: (0, v_blk // hp + g), pipeline_mode=once),
                  pl.BlockSpec((hp, 2, t, 2 * t), lambda g, i: (g, 0, 0, 0), pipeline_mode=once),
                  pl.BlockSpec(lam_params.shape, lambda g, i: (0, 0)),
                  pl.BlockSpec((HEAD_W, 1), lambda g, i: (0, 0))],
        out_specs=pl.BlockSpec((t, w), lambda g, i: (i, g)),
        out_shape=jax.ShapeDtypeStruct((s, HEADS * HEAD_W), BF16),
        scratch_shapes=[pltpu.VMEM((hp, s // t, V_ROWS, t), BF16),
                        pltpu.VMEM((hp, HEAD_W, 2 * t), BF16),
                        pltpu.VMEM((2, hp, t, 2 * t), F32),
                        pltpu.VMEM((hp, 1, 2 * t), F32),
                        pltpu.VMEM((hp, V_ROWS, 2 * t), F32)],
        compiler_params=_params("arbitrary", "arbitrary"),
        name="diff_attn",
    )(proj, proj, proj, bias_tiles, lam_params, subln_col)


def _gates_kernel(x_ref, shift_ref, scale_ref, ws_ref, alog_ref, dtb_ref, o_ref, h_ref):
    h = (x_ref[...] * (1.0 + scale_ref[...]) + shift_ref[...]).astype(BF16)
    h_ref[...] = h
    tr = _dot_nt(ws_ref[...], h.astype(F32))
    beta = jax.nn.sigmoid(tr[0:HEADS])
    x = tr[HEADS:2 * HEADS] + dtb_ref[...]
    softplus = jnp.maximum(x, 0.0) + jnp.log1p(jnp.exp(-jnp.abs(x)))
    g = -jnp.exp(alog_ref[...]) * softplus
    pos = lax.broadcasted_iota(jnp.int32, g.shape, 1) % DN_CHUNK
    shift = 1
    while shift < DN_CHUNK:
        g = g + jnp.where(pos >= shift, pltpu.roll(g, shift, axis=1), 0.0)
        shift *= 2
    o_ref[0:HEADS] = beta
    o_ref[HEADS:2 * HEADS] = g


def _gates(x, shift, scale, w_small_t, alog_col, dtb_col, *, tb=1024):
    s, d = x.shape
    col = pl.BlockSpec((HEADS, 1), lambda i: (0, 0))
    vec = pl.BlockSpec((1, d), lambda i: (0, 0))
    return pl.pallas_call(
        _gates_kernel,
        grid=(s // tb,),
        in_specs=[pl.BlockSpec((tb, d), lambda i: (i, 0)), vec, vec,
                  pl.BlockSpec(w_small_t.shape, lambda i: (0, 0)), col, col],
        out_specs=[pl.BlockSpec((2 * HEADS, tb), lambda i: (0, i)), pl.BlockSpec((tb, d), lambda i: (i, 0))],
        out_shape=[jax.ShapeDtypeStruct((2 * HEADS, s), F32), jax.ShapeDtypeStruct((s, d), BF16)],
        compiler_params=_params("parallel"),
        name="gdn_gates",
    )(x, shift, scale, w_small_t, alog_col, dtb_col)


GROUP = 2 * DN_CHUNK
INV_BASE = 16


def _gdn_kernel(q_ref, k_ref, v_ref, z_ref, gt_ref, cwq_ref, cwk_ref, cwv_ref, nw_ref, o_ref,
                pad_sc, state_sc, o_sc, *, tb, hp):
    g = pl.program_id(0)
    ib = pl.program_id(1)
    halo = 8
    heads = range(hp)
    cols = [slice(hh * HEAD_W, (hh + 1) * HEAD_W) for hh in heads]

    @pl.when(ib == 0)
    def _():
        state_sc[...] = jnp.zeros_like(state_sc)
        pad_sc[:, 0:halo, :] = jnp.zeros((3 * hp, halo, HEAD_W), F32)

    def conv_silu(a, x_ref, cw_ref, hh):
        a = a * hp + hh
        pad_sc[a, halo:halo + tb, :] = x_ref[:, cols[hh]].astype(F32)
        cw = cw_ref[:, cols[hh]]
        y = cw[0:1] * pad_sc[a, halo - 3:halo - 3 + tb, :]
        for j in range(1, DN_CONV):
            y = y + cw[j:j + 1] * pad_sc[a, halo - 3 + j:halo - 3 + j + tb, :]
        pad_sc[a, 0:halo, :] = pad_sc[a, tb:tb + halo, :]
        return _silu(y)

    def l2n(x):
        return x * lax.rsqrt(jnp.sum(x * x, axis=-1, keepdims=True) + RMS_EPS)

    ii = lax.broadcasted_iota(jnp.int32, (GROUP, GROUP), 0)
    jj = lax.broadcasted_iota(jnp.int32, (GROUP, GROUP), 1)
    same = (ii >= DN_CHUNK) == (jj >= DN_CHUNK)
    tril = jnp.logical_and(same, ii >= jj)
    eye = (ii == jj).astype(F32)
    bits = INV_BASE.bit_length() - 1
    in_base = jnp.logical_and(ii >> bits == jj >> bits, ii > jj)
    below = []
    while (1 << bits) < DN_CHUNK:
        below.append(jnp.logical_and(ii >> (bits + 1) == jj >> (bits + 1), ii >> bits == (jj >> bits) + 1))
        bits += 1

    q, k, v, beta_c, cum_c, cum_t = [], [], [], [], [], []
    for hh in heads:
        q.append(l2n(conv_silu(0, q_ref, cwq_ref, hh)) * (HEAD_W ** -0.5))
        k.append(l2n(conv_silu(1, k_ref, cwk_ref, hh)))
        v.append(conv_silu(2, v_ref, cwv_ref, hh))
        head = g * hp + hh
        beta_t = jnp.broadcast_to(gt_ref[pl.ds(head, 1), :], (HEAD_W, tb))
        ct = jnp.broadcast_to(gt_ref[pl.ds(HEADS + head, 1), :], (HEAD_W, tb))
        beta_c.append(beta_t.T)
        cum_c.append(ct.T)
        cum_t.append(ct)

    groups = range(tb // GROUP)
    rows = [slice(r * GROUP, (r + 1) * GROUP) for r in groups]
    u, w, qk, qd, kg, cg = {}, {}, {}, {}, {}, {}

    def intra_chunk(r):
        kb, x, tinv, decay, merges = {}, {}, {}, {}, {}
        for hh in heads:
            sid = (r, hh)
            kg[sid], cg[sid] = k[hh][rows[r]], cum_c[hh][rows[r]]
            gdiff = cg[sid] - cum_t[hh][:, rows[r]]
            decay[hh] = jnp.where(tril, jnp.exp(jnp.where(tril, gdiff, 0.0)), 0.0)
            kb[hh] = kg[sid] * beta_c[hh][rows[r]]
            a = _dot_nt(kb[hh].astype(BF16), kg[sid].astype(BF16)) * decay[hh]
            diag = jnp.where(in_base, a, 0.0)
            x[hh] = diag.astype(BF16)
            tinv[hh] = eye - diag
            merges[hh] = [jnp.where(m, a, 0.0).astype(BF16) for m in below]
        yield
        power = 2
        while power < INV_BASE:
            for hh in heads:
                x[hh] = _dot(x[hh], x[hh]).astype(BF16)
            yield
            for hh in heads:
                tinv[hh] = tinv[hh] + _dot(tinv[hh].astype(BF16), x[hh])
            yield
            power *= 2
        for level in range(len(below)):
            half = {}
            for hh in heads:
                half[hh] = _dot(tinv[hh].astype(BF16), merges[hh][level]).astype(BF16)
            yield
            for hh in heads:
                tinv[hh] = tinv[hh] - _dot(half[hh], tinv[hh].astype(BF16))
            yield
        for hh in heads:
            sid = (r, hh)
            eg = jnp.exp(cg[sid])
            rhs = jnp.concatenate([v[hh][rows[r]] * beta_c[hh][rows[r]], kb[hh] * eg], axis=1).astype(BF16)
            uw = _dot(tinv[hh].astype(BF16), rhs)
            u[sid], w[sid] = uw[:, :HEAD_W], uw[:, HEAD_W:].astype(BF16)
            qg = q[hh][rows[r]]
            qk[sid] = (_dot_nt(qg.astype(BF16), kg[sid].astype(BF16)) * decay[hh]).astype(BF16)
            qd[sid] = (qg * eg).astype(BF16)
        yield

    def recurrence(r):
        v_new = {hh: [] for hh in heads}
        o_inter = {hh: [] for hh in heads}
        for c in range(2):
            cr = slice(c * DN_CHUNK, (c + 1) * DN_CHUNK)
            last = c * DN_CHUNK + DN_CHUNK - 1
            st, g_last = {}, {}
            for hh in heads:
                sid = (r, hh)
                g_last[hh] = cg[sid][last:last + 1, :]
                st[hh] = state_sc[hh]
                st16 = st[hh].astype(BF16)
                v_new[hh].append(u[sid][cr] - _dot(w[sid][cr], st16))
                o_inter[hh].append(_dot(qd[sid][cr], st16))
            yield
            for hh in heads:
                sid = (r, hh)
                kd = (kg[sid][cr] * jnp.exp(g_last[hh] - cg[sid][cr])).astype(BF16)
                state_sc[hh] = st[hh] * jnp.exp(g_last[hh]) + _dot_tn(kd, v_new[hh][c].astype(BF16))
            yield
        for hh in heads:
            vn_all = jnp.concatenate(v_new[hh], axis=0).astype(BF16)
            o_sc[hh, rows[r], :] = jnp.concatenate(o_inter[hh], axis=0) + _dot(qk[(r, hh)], vn_all)
        yield

    def interleave(slow, fast, ratio):
        slow_live = fast_live = True
        while slow_live or fast_live:
            if slow_live:
                slow_live = next(slow, "done") != "done"
            for _ in range(ratio):
                if fast_live:
                    fast_live = next(fast, "done") != "done"

    for _ in intra_chunk(0):
        pass
    for r in groups:
        nxt = intra_chunk(r + 1) if r + 1 < len(groups) else iter(())
        interleave(recurrence(r), nxt, 3)

    for hh in heads:
        o = o_sc[hh]
        on = o * lax.rsqrt(jnp.mean(o * o, axis=-1, keepdims=True) + RMS_EPS) * nw_ref[...]
        o_ref[:, cols[hh]] = (on * _silu(z_ref[:, cols[hh]].astype(F32))).astype(BF16)


def _gdn(proj, gates_t, conv_w, norm_w, *, q_blk, k_blk, v_blk, z_blk, tb=512, hp=8):
    s = proj.shape[0]
    w = hp * HEAD_W
    assert all(blk % hp == 0 for blk in (q_blk, k_blk, v_blk, z_blk))
    col = lambda blk: pl.BlockSpec((tb, w), lambda g, i: (i, blk // hp + g))
    cw = lambda blk: pl.BlockSpec((DN_CONV, w), lambda g, i: (0, blk // hp + g))
    return pl.pallas_call(
        functools.partial(_gdn_kernel, tb=tb, hp=hp),
        grid=(HEADS // hp, s // tb),
        in_specs=[col(q_blk), col(k_blk), col(v_blk), col(z_blk),
                  pl.BlockSpec((2 * HEADS, tb), lambda g, i: (0, i)),
                  cw(0), cw(HEADS), cw(2 * HEADS),
                  pl.BlockSpec((1, HEAD_W), lambda g, i: (0, 0))],
        out_specs=pl.BlockSpec((tb, w), lambda g, i: (i, g)),
        out_shape=jax.ShapeDtypeStruct((s, HEADS * HEAD_W), BF16),
        scratch_shapes=[pltpu.VMEM((3 * hp, tb + 8, HEAD_W), F32),
                        pltpu.VMEM((hp, HEAD_W, HEAD_W), F32),
                        pltpu.VMEM((hp, tb, HEAD_W), F32)],
        compiler_params=_params("arbitrary", "arbitrary"),
        name="gdn",
    )(proj, proj, proj, proj, gates_t, conv_w, conv_w, conv_w, norm_w)


def _merge_kernel(ya_ref, yb_ref, ga_ref, gb_ref, x_ref, gate_ref, lng_ref, lnb_ref,
                  wa_ref, wb_ref, wo_ref, o_ref, m_sc, *, alpha):
    i = pl.program_id(0)
    n_tiles = pl.num_programs(0) - 1

    def merge(slot):
        a = _dot(ya_ref[...], wa_ref[...])
        b = _dot(yb_ref[...], wb_ref[...])
        m_sc[slot] = (jax.nn.sigmoid(ga_ref[...].astype(F32)) * a
                      + jax.nn.sigmoid(gb_ref[...].astype(F32)) * b).astype(BF16)

    def project(slot):
        y = _dot(m_sc[slot], wo_ref[...])
        r = alpha * x_ref[...] + gate_ref[...] * y
        o_ref[...] = _layer_norm(r, lng_ref[...], lnb_ref[...])

    pl.when(i == 0)(lambda: merge(0))
    for slot in range(2):
        @pl.when(jnp.logical_and(jnp.logical_and(i > 0, i < n_tiles), (i & 1) == slot))
        def _():
            project(1 - slot)
            merge(slot)

        pl.when(jnp.logical_and(i == n_tiles, ((i - 1) & 1) == slot))(functools.partial(project, slot))


def _merge(ya, yb, proj, x, gate, lng, lnb, w_a, w_b, w_o, *, ga_blk, gb_blk, alpha, tm=256):
    s, d = x.shape
    dv = ya.shape[1]
    n = s // tm
    vec = pl.BlockSpec((1, d), lambda i: (0, 0))
    const = lambda shape: pl.BlockSpec(shape, lambda i: (0, 0), pipeline_mode=pl.Buffered(1))
    cur = lambda i: jnp.minimum(i, n - 1)
    prev = lambda i: jnp.maximum(i - 1, 0)
    return pl.pallas_call(
        functools.partial(_merge_kernel, alpha=alpha),
        grid=(n + 1,),
        in_specs=[pl.BlockSpec((tm, dv), lambda i: (cur(i), 0)),
                  pl.BlockSpec((tm, dv), lambda i: (cur(i), 0)),
                  pl.BlockSpec((tm, d), lambda i: (cur(i), ga_blk)),
                  pl.BlockSpec((tm, d), lambda i: (cur(i), gb_blk)),
                  pl.BlockSpec((tm, d), lambda i: (prev(i), 0)), vec, vec, vec,
                  const(w_a.shape), const(w_b.shape), const(w_o.shape)],
        out_specs=pl.BlockSpec((tm, d), lambda i: (prev(i), 0)),
        out_shape=jax.ShapeDtypeStruct((s, d), F32),
        scratch_shapes=[pltpu.VMEM((2, tm, d), BF16)],
        compiler_params=_params("arbitrary"),
        name="merge",
    )(ya, yb, proj, proj, x, gate, lng, lnb, w_a, w_b, w_o)


def kernel(x, c, w_ada, b_ada, ln_g, ln_b, w_ffn_in, w_ffn_out, w_in, conv_w, dn_a_log, dn_dt_bias,
           dn_norm_w, diff_lambda, diff_subln_w, rel_bias, w_branch_a, w_branch_b, w_out):
    bsz, s, d = x.shape
    assert bsz == 1, "one sequence per call"
    depth = w_ada.shape[0]
    alpha = (2 * depth) ** 0.25
    hw = HEADS * HEAD_W
    attn_t = 256
    assert d % HEAD_W == 0 and s % 1024 == 0

    o_nb = 7 * hw
    o_ga = o_nb + 2 * HEADS
    nblk = d // HEAD_W
    blk = {"ga": 0, "gb": 1, "dq": 2 * nblk, "dk": 2 * nblk + HEADS, "dv": 2 * nblk + 2 * HEADS,
           "nq": 2 * nblk + 3 * HEADS, "nk": 2 * nblk + 4 * HEADS, "nv": 2 * nblk + 5 * HEADS,
           "nz": 2 * nblk + 6 * HEADS}

    bias_tiles = _bias_tiles(rel_bias, attn_t)
    wfi, wfo = w_ffn_in, w_ffn_out
    w_in_t = jnp.swapaxes(w_in, 1, 2)
    x2 = x[0]
    for l in range(depth):
        lam_init = 0.8 - 0.6 * math.exp(-0.3 * l)
        ada = _ada(c.reshape(d, 1), w_ada[l], b_ada[l].reshape(1, -1)).reshape(N_SUB, 3, 1, d)
        shift, scale, gate = ada[:, 0], ada[:, 1], ada[:, 2]
        lng, lnb = ln_g[l].reshape(N_SUB, 1, d), ln_b[l].reshape(N_SUB, 1, d)
        w_small_t = w_in_t[l, o_nb:o_ga]

        x2 = _ffn(x2, shift[0], scale[0], gate[0], lng[0], lnb[0], wfi, wfo, layer=l, which=0, alpha=alpha)

        gates_t, h = _gates(x2, shift[1], scale[1], w_small_t, dn_a_log[l].reshape(HEADS, 1),
                            dn_dt_bias[l].reshape(HEADS, 1))
        proj, wfi2, wfo2 = _proj(h, w_in_t[l], w_ffn_in, w_ffn_out, (l, 1),
                                 gate_row0=o_ga, n_gate_cols=2 * d, n_head_cols=o_nb)
        ya = _attn(proj, bias_tiles, diff_lambda[l], diff_subln_w[l].reshape(HEAD_W, 1),
                   q_blk=blk["dq"], k_blk=blk["dk"], v_blk=blk["dv"], lam_init=lam_init, t=attn_t)
        yb = _gdn(proj, gates_t, conv_w[l], dn_norm_w[l].reshape(1, HEAD_W),
                  q_blk=blk["nq"], k_blk=blk["nk"], v_blk=blk["nv"], z_blk=blk["nz"])
        x2 = _merge(ya, yb, proj, x2, gate[1], lng[1], lnb[1], w_branch_a[l].astype(BF16),
                    w_branch_b[l].astype(BF16), w_out[l].astype(BF16),
                    ga_blk=blk["ga"], gb_blk=blk["gb"], alpha=alpha)

        x2 = _ffn(x2, shift[2], scale[2], gate[2], lng[2], lnb[2], wfi2[None, None], wfo2[None, None],
                  layer=0, which=0, alpha=alpha, tf=512)
    return x2[None]
```

```python
import functools
import math

import numpy as np
import jax
import jax.numpy as jnp
from jax import lax
from jax.experimental import pallas as pl
from jax.experimental.pallas import tpu as pltpu

N_SUB = 3
HEADS = 8
HEAD_W = 128
DIFF_QK_DIM = 64
DN_CONV = 4
DN_CHUNK = 64
REL_BUCKETS = 32
REL_MAX_DIST = 128
LN_EPS = 1e-5
RMS_EPS = 1e-6
MASK_VALUE = -1e30
LOG2E = math.log2(math.e)
V_ROWS = HEAD_W + 16

F32 = jnp.float32
BF16 = jnp.bfloat16

V7X_VMEM_BYTES = 64 * 1024 * 1024
VMEM_LIMIT = V7X_VMEM_BYTES - 4 * 1024 * 1024


def _params(*sem):
    return pltpu.CompilerParams(dimension_semantics=sem, vmem_limit_bytes=VMEM_LIMIT)


def _tile(n, preferred):
    t = min(preferred, n)
    while n % t:
        t -= 128
    return t


def _silu(x):
    return x * jax.nn.sigmoid(x)


def _dot(a, b):
    return jnp.dot(a, b, preferred_element_type=F32)


def _dot_nt(a, b):
    return lax.dot_general(a, b, (((1,), (1,)), ((), ())), preferred_element_type=F32)


def _dot_tn(a, b):
    return lax.dot_general(a, b, (((0,), (0,)), ((), ())), preferred_element_type=F32)


def _layer_norm(r, g, b):
    mu = jnp.mean(r, axis=-1, keepdims=True)
    d = r - mu
    var = jnp.mean(d * d, axis=-1, keepdims=True)
    return d * lax.rsqrt(var + LN_EPS) * g + b


def _ada_kernel(c_ref, w_ref, b_ref, o_ref):
    sc = _silu(c_ref[...])
    o_ref[...] = jnp.sum(w_ref[...] * sc, axis=0, keepdims=True) + b_ref[...]


def _ada(c_col, w, b):
    d, n = w.shape
    tn = _tile(n, 1024)
    return pl.pallas_call(
        _ada_kernel,
        grid=(n // tn,),
        in_specs=[pl.BlockSpec((d, 1), lambda j: (0, 0)),
                  pl.BlockSpec((d, tn), lambda j: (0, j)),
                  pl.BlockSpec((1, tn), lambda j: (0, j))],
        out_specs=pl.BlockSpec((1, tn), lambda j: (0, j)),
        out_shape=jax.ShapeDtypeStruct((1, n), F32),
        compiler_params=_params("arbitrary"),
        name="ada",
    )(c_col, w, b)


def _ffn_kernel(x_ref, shift_ref, scale_ref, gate_ref, lng_ref, lnb_ref, wg_ref, wu_ref, wo_ref,
                o_ref, h_sc, a_sc, *, alpha, gate_mul):
    j = pl.program_id(1)
    n_chunks = pl.num_programs(1) - 1

    def drain(slot):
        return _dot(a_sc[slot], wo_ref[...].astype(BF16))

    def activations(slot):
        h = h_sc[...]
        g = _dot(h, wg_ref[...].astype(BF16))
        u = _dot(h, wu_ref[...].astype(BF16))
        a_sc[slot] = (_silu(g) * u).astype(BF16)

    @pl.when(j == 0)
    def _():
        h_sc[...] = (x_ref[...] * (1.0 + scale_ref[...]) + shift_ref[...]).astype(BF16)
        o_ref[...] = jnp.zeros_like(o_ref)
        activations(0)

    for slot in range(2):
        @pl.when(jnp.logical_and(jnp.logical_and(j > 0, j < n_chunks), (j & 1) == slot))
        def _():
            o_ref[...] += drain(1 - slot)
            activations(slot)

        @pl.when(jnp.logical_and(j == n_chunks, ((j - 1) & 1) == slot))
        def _():
            r = alpha * x_ref[...] + (gate_mul * gate_ref[...]) * (o_ref[...] + drain(slot))
            o_ref[...] = _layer_norm(r, lng_ref[...], lnb_ref[...])


def _ffn(x, shift, scale, gate, lng, lnb, w_in, w_out, *, layer, which, alpha, tm=1024, tf=256):
    s, d = x.shape
    f = w_out.shape[2]
    tm, tf = _tile(s, tm), _tile(f, tf)
    nf = f // tf
    vec = pl.BlockSpec((1, d), lambda i, j: (0, 0))
    up = lambda j: jnp.minimum(j, nf - 1)
    down = lambda j: jnp.maximum(j - 1, 0)
    return pl.pallas_call(
        functools.partial(_ffn_kernel, alpha=alpha, gate_mul=0.5),
        grid=(s // tm, nf + 1),
        in_specs=[pl.BlockSpec((tm, d), lambda i, j: (i, 0)), vec, vec, vec, vec, vec,
                  pl.BlockSpec((None, None, d, tf), lambda i, j: (layer, which, 0, up(j))),
                  pl.BlockSpec((None, None, d, tf), lambda i, j: (layer, which, 0, up(j) + nf)),
                  pl.BlockSpec((None, None, tf, d), lambda i, j: (layer, which, down(j), 0))],
        out_specs=pl.BlockSpec((tm, d), lambda i, j: (i, 0)),
        out_shape=jax.ShapeDtypeStruct((s, d), F32),
        scratch_shapes=[pltpu.VMEM((tm, d), BF16), pltpu.VMEM((2, tm, tf), BF16)],
        compiler_params=_params("parallel", "arbitrary"),
        name="ffn",
    )(x, shift, scale, gate, lng, lnb, w_in, w_in, w_out)


def _proj_kernel(h_ref, w_ref, ca_ref, cb_ref, o_ref, ca16_ref, cb16_ref, w_sc):
    @pl.when(pl.program_id(1) == 0)
    def _():
        w_sc[...] = w_ref[...].astype(BF16)

    o_ref[...] = _dot_nt(h_ref[...], w_sc[...]).astype(BF16)
    ca16_ref[...] = ca_ref[...].astype(BF16)
    cb16_ref[...] = cb_ref[...].astype(BF16)


def _cast_slabs(arr, lead_idx, n_steps, step_of):
    rows, cols = arr.shape[-2:]
    r = 16
    while rows % r or rows // r > n_steps:
        r += 16
    slab = lambda *g: jnp.minimum(step_of(*g), rows // r - 1)
    return (pl.BlockSpec((None,) * len(lead_idx) + (r, cols), lambda *g: (*lead_idx, slab(*g), 0)),
            pl.BlockSpec((r, cols), lambda *g: (slab(*g), 0)), jax.ShapeDtypeStruct((rows, cols), BF16))


def _proj(h, w_all_t, cast_a, cast_b, cast_idx, *, gate_row0, n_gate_cols, n_head_cols, tm=1024, tn=1024):
    s, d = h.shape
    tm = _tile(s, tm)
    tn = math.gcd(_tile(n_gate_cols, tn), _tile(n_head_cols, tn))
    n_gate, n_head = n_gate_cols // tn, n_head_cols // tn
    assert gate_row0 % 8 == 0
    first_row = lambda j: pl.multiple_of(jnp.where(j < n_gate, gate_row0 + j * tn, (j - n_gate) * tn), 8)
    n_rows = s // tm
    n_steps = (n_gate + n_head) * n_rows
    step_of = lambda j, i: j * n_rows + i
    (a_in, a_out, a_shape), (b_in, b_out, b_shape) = (_cast_slabs(arr, cast_idx, n_steps, step_of)
                                                      for arr in (cast_a, cast_b))
    return pl.pallas_call(
        _proj_kernel,
        grid=(n_gate + n_head, n_rows),
        in_specs=[pl.BlockSpec((tm, d), lambda j, i: (i, 0)),
                  pl.BlockSpec((pl.Element(tn), pl.Element(d)), lambda j, i: (first_row(j), 0)), a_in, b_in],
        out_specs=[pl.BlockSpec((tm, tn), lambda j, i: (i, j)), a_out, b_out],
        out_shape=[jax.ShapeDtypeStruct((s, n_gate_cols + n_head_cols), BF16), a_shape, b_shape],
        scratch_shapes=[pltpu.VMEM((tn, d), BF16)],
        compiler_params=_params("arbitrary", "arbitrary"),
        name="proj",
    )(h, w_all_t, cast_a, cast_b)


def _bucket_starts():
    n = np.arange(0, 2 * REL_MAX_DIST)
    max_exact = REL_BUCKETS // 2
    nf = np.maximum(n, max_exact).astype(np.float32)
    large = max_exact + (np.log(nf / np.float32(max_exact)) / np.float32(math.log(REL_MAX_DIST / max_exact))
                         * np.float32(REL_BUCKETS - max_exact)).astype(np.int32)
    bucket = np.where(n < max_exact, n, np.minimum(large, REL_BUCKETS - 1))
    assert np.all(np.diff(bucket) >= 0) and bucket[-1] == REL_BUCKETS - 1
    return [int(np.min(n[bucket >= b])) for b in range(REL_BUCKETS)]


BUCKET_STARTS = _bucket_starts()


def _bias_kernel(rb_ref, o_ref, *, t):
    h = pl.program_id(0)
    jj = lax.broadcasted_iota(jnp.int32, (t, 2 * t), 0)
    ii = lax.broadcasted_iota(jnp.int32, (t, 2 * t), 1)
    ii = jnp.where(ii >= t, ii - t, ii)
    far = rb_ref[REL_BUCKETS - 1, h]
    for n in range(2):
        rel = ii - jj + n * t
        bias = jnp.full((t, 2 * t), (rb_ref[0, h] - far) * LOG2E, F32)
        for b in range(1, REL_BUCKETS):
            bias = jnp.where(rel >= BUCKET_STARTS[b], (rb_ref[b, h] - far) * LOG2E, bias)
        o_ref[0, n] = jnp.where(rel < 0, MASK_VALUE, bias)


def _bias_tiles(rel_bias, t):
    return pl.pallas_call(
        functools.partial(_bias_kernel, t=t),
        grid=(HEADS,),
        in_specs=[pl.BlockSpec(memory_space=pltpu.SMEM)],
        out_specs=pl.BlockSpec((1, 2, t, 2 * t), lambda h: (h, 0, 0, 0)),
        out_shape=jax.ShapeDtypeStruct((HEADS, 2, t, 2 * t), F32),
        compiler_params=_params("arbitrary"),
        name="bias_tiles",
    )(rel_bias)


def _attn_kernel(q_ref, k_ref, v_ref, bias_ref, lam_ref, sub_ref, o_ref,
                 vt_sc, qt_sc, s_sc, m_sc, acc_sc, *, t, hp, lam_init):
    qi = pl.program_id(1)
    n_kv = vt_sc.shape[1]
    heads = range(hp)
    cols = [slice(hh * HEAD_W, (hh + 1) * HEAD_W) for hh in heads]

    @pl.when(qi == 0)
    def _():
        ones_tile = (lax.broadcasted_iota(jnp.int32, (V_ROWS - HEAD_W, t), 0) == 0).astype(BF16)

        def body(c, carry):
            r0 = pl.multiple_of(c * t, t)
            for hh in heads:
                vt_sc[hh, c, 0:HEAD_W] = v_ref[pl.ds(r0, t), cols[hh]].astype(F32).T.astype(BF16)
                vt_sc[hh, c, HEAD_W:V_ROWS] = ones_tile
            return carry
        lax.fori_loop(0, n_kv, body, 0)

    row = lax.broadcasted_iota(jnp.int32, (HEAD_W, t), 0)
    for hh in heads:
        qt = (q_ref[:, cols[hh]].astype(F32) * (DIFF_QK_DIM ** -0.5 * LOG2E)).T
        zero = jnp.zeros_like(qt)
        qt_sc[hh] = jnp.concatenate([jnp.where(row < DIFF_QK_DIM, qt, zero),
                                     jnp.where(row >= DIFF_QK_DIM, qt, zero)], axis=1).astype(BF16)

    m_sc[...] = jnp.full_like(m_sc, -jnp.inf)
    acc_sc[...] = jnp.zeros_like(acc_sc)

    def scores(c, slot, biased, which=heads):
        r0 = pl.multiple_of(c * t, t)
        for hh in which:
            s = _dot(k_ref[pl.ds(r0, t), cols[hh]], qt_sc[hh])
            if biased:
                s = s + bias_ref[hh, qi - c]
            s_sc[slot, hh] = s

    def consume(c, slot, which=heads):
        for hh in which:
            s = s_sc[slot, hh]
            m_old = m_sc[hh]
            m_new = jnp.maximum(m_old, jnp.max(s, axis=0, keepdims=True))
            p = jnp.exp2(s - m_new).astype(BF16)
            acc_sc[hh] = jnp.exp2(m_old - m_new) * acc_sc[hh] + _dot(vt_sc[hh, c], p)
            m_sc[hh] = m_new

    pl.when(qi >= 2)(lambda: scores(0, 0, False))
    pl.when(qi < 2)(lambda: scores(0, 0, True))

    n_pairs = jnp.maximum(qi - 2, 0) >> 1

    def pair_body(j, carry):
        c = 2 * j
        scores(c + 1, 1, False)
        for hh in heads:
            consume(c, 0, [hh])
            scores(c + 2, 0, False, [hh])
        consume(c + 1, 1)
        return carry
    lax.fori_loop(0, n_pairs, pair_body, 0)

    def body(c, carry):
        far_next = c + 1 <= qi - 2
        for slot in range(2):
            mine = (c & 1) == slot

            @pl.when(jnp.logical_and(mine, far_next))
            def _():
                scores(c + 1, 1 - slot, False)
                consume(c, slot)

            @pl.when(jnp.logical_and(mine, jnp.logical_not(far_next)))
            def _():
                scores(c + 1, 1 - slot, True)
                consume(c, slot)
        return carry
    lax.fori_loop(2 * n_pairs, qi, body, 0)

    for slot in range(2):
        pl.when((qi & 1) == slot)(functools.partial(consume, qi, slot))

    lp = lam_ref[...]
    lam = (jnp.exp(jnp.sum(lp[0:1] * lp[1:2], axis=1, keepdims=True))
           - jnp.exp(jnp.sum(lp[2:3] * lp[3:4], axis=1, keepdims=True)) + lam_init)
    for hh in heads:
        acc = acc_sc[hh]
        o = acc[0:HEAD_W] / acc[HEAD_W:HEAD_W + 1]
        od = o[:, :t] - lam * o[:, t:]
        ms = jnp.mean(od * od, axis=0, keepdims=True)
        y = od * lax.rsqrt(ms + LN_EPS) * sub_ref[...] * (1.0 - lam_init)
        o_ref[:, cols[hh]] = y.T.astype(BF16)


def _attn(proj, bias_tiles, lam_params, subln_col, *, q_blk, k_blk, v_blk, lam_init, t, hp=4):
    s = proj.shape[0]
    w = hp * HEAD_W
    assert q_blk % hp == 0 and k_blk % hp == 0 and v_blk % hp == 0
    once = pl.Buffered(1)
    return pl.pallas_call(
        functools.partial(_attn_kernel, t=t, hp=hp, lam_init=lam_init),
        grid=(HEADS // hp, s // t),
        in_specs=[pl.BlockSpec((t, w), lambda g, i: (i, q_blk // hp + g)),
                  pl.BlockSpec((s, w), lambda g, i: (0, k_blk // hp + g), pipeline_mode=once),
                  pl.BlockSpec((s, w), lambda g, i: (0, v_blk // hp + g), pipeline_mode=once),
                  pl.BlockSpec((hp, 2, t, 2 * t), lambda g, i: (g, 0, 0, 0), pipeline_mode=once),
                  pl.BlockSpec(lam_params.shape, lambda g, i: (0, 0)),
                  pl.BlockSpec((HEAD_W, 1), lambda g, i: (0, 0))],
        out_specs=pl.BlockSpec((t, w), lambda g, i: (i, g)),
        out_shape=jax.ShapeDtypeStruct((s, HEADS * HEAD_W), BF16),
        scratch_shapes=[pltpu.VMEM((hp, s // t, V_ROWS, t), BF16),
                        pltpu.VMEM((hp, HEAD_W, 2 * t), BF16),
                        pltpu.VMEM((2, hp, t, 2 * t), F32),
                        pltpu.VMEM((hp, 1, 2 * t), F32),
                        pltpu.VMEM((hp, V_ROWS, 2 * t), F32)],
        compiler_params=_params("arbitrary", "arbitrary"),
        name="diff_attn",
    )(proj, proj, proj, bias_tiles, lam_params, subln_col)


def _gates_kernel(x_ref, shift_ref, scale_ref, ws_ref, alog_ref, dtb_ref, o_ref, h_ref):
    h = (x_ref[...] * (1.0 + scale_ref[...]) + shift_ref[...]).astype(BF16)
    h_ref[...] = h
    tr = _dot_nt(ws_ref[...], h.astype(F32))
    beta = jax.nn.sigmoid(tr[0:HEADS])
    x = tr[HEADS:2 * HEADS] + dtb_ref[...]
    softplus = jnp.maximum(x, 0.0) + jnp.log1p(jnp.exp(-jnp.abs(x)))
    g = -jnp.exp(alog_ref[...]) * softplus
    pos = lax.broadcasted_iota(jnp.int32, g.shape, 1) % DN_CHUNK
    shift = 1
    while shift < DN_CHUNK:
        g = g + jnp.where(pos >= shift, pltpu.roll(g, shift, axis=1), 0.0)
        shift *= 2
    o_ref[0:HEADS] = beta
    o_ref[HEADS:2 * HEADS] = g


def _gates(x, shift, scale, w_small_t, alog_col, dtb_col, *, tb=1024):
    s, d = x.shape
    col = pl.BlockSpec((HEADS, 1), lambda i: (0, 0))
    vec = pl.BlockSpec((1, d), lambda i: (0, 0))
    return pl.pallas_call(
        _gates_kernel,
        grid=(s // tb,),
        in_specs=[pl.BlockSpec((tb, d), lambda i: (i, 0)), vec, vec,
                  pl.BlockSpec(w_small_t.shape, lambda i: (0, 0)), col, col],
        out_specs=[pl.BlockSpec((2 * HEADS, tb), lambda i: (0, i)), pl.BlockSpec((tb, d), lambda i: (i, 0))],
        out_shape=[jax.ShapeDtypeStruct((2 * HEADS, s), F32), jax.ShapeDtypeStruct((s, d), BF16)],
        compiler_params=_params("parallel"),
        name="gdn_gates",
    )(x, shift, scale, w_small_t, alog_col, dtb_col)


GROUP = 2 * DN_CHUNK
INV_BASE = 16


def _gdn_kernel(q_ref, k_ref, v_ref, z_ref, gt_ref, cwq_ref, cwk_ref, cwv_ref, nw_ref, *rest, tb, hp, n_cast):
    cast_in, (o_ref, *cast_out), (pad_sc, state_sc, o_sc) = rest[:n_cast], rest[n_cast:2 * n_cast + 1], rest[-3:]
    for src, dst in zip(cast_in, cast_out):
        dst[...] = src[...].astype(BF16)
    g = pl.program_id(0)
    ib = pl.program_id(1)
    halo = 8
    heads = range(hp)
    cols = [slice(hh * HEAD_W, (hh + 1) * HEAD_W) for hh in heads]

    @pl.when(ib == 0)
    def _():
        state_sc[...] = jnp.zeros_like(state_sc)
        pad_sc[:, 0:halo, :] = jnp.zeros((3 * hp, halo, HEAD_W), F32)

    def conv_silu(a, x_ref, cw_ref, hh):
        a = a * hp + hh
        pad_sc[a, halo:halo + tb, :] = x_ref[:, cols[hh]].astype(F32)
        cw = cw_ref[:, cols[hh]]
        y = cw[0:1] * pad_sc[a, halo - 3:halo - 3 + tb, :]
        for j in range(1, DN_CONV):
            y = y + cw[j:j + 1] * pad_sc[a, halo - 3 + j:halo - 3 + j + tb, :]
        pad_sc[a, 0:halo, :] = pad_sc[a, tb:tb + halo, :]
        return _silu(y)

    def l2n(x):
        return x * lax.rsqrt(jnp.sum(x * x, axis=-1, keepdims=True) + RMS_EPS)

    ii = lax.broadcasted_iota(jnp.int32, (GROUP, GROUP), 0)
    jj = lax.broadcasted_iota(jnp.int32, (GROUP, GROUP), 1)
    same = (ii >= DN_CHUNK) == (jj >= DN_CHUNK)
    tril = jnp.logical_and(same, ii >= jj)
    eye = (ii == jj).astype(F32)
    bits = INV_BASE.bit_length() - 1
    in_base = jnp.logical_and(ii >> bits == jj >> bits, ii > jj)
    below = []
    while (1 << bits) < DN_CHUNK:
        below.append(jnp.logical_and(ii >> (bits + 1) == jj >> (bits + 1), ii >> bits == (jj >> bits) + 1))
        bits += 1

    q, k, v, beta_c, cum_c, cum_t = [], [], [], [], [], []
    for hh in heads:
        q.append(l2n(conv_silu(0, q_ref, cwq_ref, hh)) * (HEAD_W ** -0.5))
        k.append(l2n(conv_silu(1, k_ref, cwk_ref, hh)))
        v.append(conv_silu(2, v_ref, cwv_ref, hh))
        head = g * hp + hh
        beta_t = jnp.broadcast_to(gt_ref[pl.ds(head, 1), :], (HEAD_W, tb))
        ct = jnp.broadcast_to(gt_ref[pl.ds(HEADS + head, 1), :], (HEAD_W, tb))
        beta_c.append(beta_t.T)
        cum_c.append(ct.T)
        cum_t.append(ct)

    groups = range(tb // GROUP)
    rows = [slice(r * GROUP, (r + 1) * GROUP) for r in groups]
    u, w, qk, qd, kg, cg = {}, {}, {}, {}, {}, {}

    def intra_chunk(r):
        kb, x, tinv, decay, merges = {}, {}, {}, {}, {}
        for hh in heads:
            sid = (r, hh)
            kg[sid], cg[sid] = k[hh][rows[r]], cum_c[hh][rows[r]]
            gdiff = cg[sid] - cum_t[hh][:, rows[r]]
            decay[hh] = jnp.where(tril, jnp.exp(jnp.where(tril, gdiff, 0.0)), 0.0)
            kb[hh] = kg[sid] * beta_c[hh][rows[r]]
            a = _dot_nt(kb[hh].astype(BF16), kg[sid].astype(BF16)) * decay[hh]
            diag = jnp.where(in_base, a, 0.0)
            x[hh] = diag.astype(BF16)
            tinv[hh] = eye - diag
            merges[hh] = [jnp.where(m, a, 0.0).astype(BF16) for m in below]
        yield
        power = 2
        while power < INV_BASE:
            for hh in heads:
                x[hh] = _dot(x[hh], x[hh]).astype(BF16)
            yield
            for hh in heads:
                tinv[hh] = tinv[hh] + _dot(tinv[hh].astype(BF16), x[hh])
            yield
            power *= 2
        for level in range(len(below)):
            half = {}
            for hh in heads:
                half[hh] = _dot(tinv[hh].astype(BF16), merges[hh][level]).astype(BF16)
            yield
            for hh in heads:
                tinv[hh] = tinv[hh] - _dot(half[hh], tinv[hh].astype(BF16))
            yield
        for hh in heads:
            sid = (r, hh)
            eg = jnp.exp(cg[sid])
            rhs = jnp.concatenate([v[hh][rows[r]] * beta_c[hh][rows[r]], kb[hh] * eg], axis=1).astype(BF16)
            uw = _dot(tinv[hh].astype(BF16), rhs)
            u[sid], w[sid] = uw[:, :HEAD_W], uw[:, HEAD_W:].astype(BF16)
            qg = q[hh][rows[r]]
            qk[sid] = (_dot_nt(qg.astype(BF16), kg[sid].astype(BF16)) * decay[hh]).astype(BF16)
            qd[sid] = (qg * eg).astype(BF16)
        yield

    def recurrence(r):
        v_new = {hh: [] for hh in heads}
        o_inter = {hh: [] for hh in heads}
        for c in range(2):
            cr = slice(c * DN_CHUNK, (c + 1) * DN_CHUNK)
            last = c * DN_CHUNK + DN_CHUNK - 1
            st, g_last = {}, {}
            for hh in heads:
                sid = (r, hh)
                g_last[hh] = cg[sid][last:last + 1, :]
                st[hh] = state_sc[hh]
                st16 = st[hh].astype(BF16)
                v_new[hh].append(u[sid][cr] - _dot(w[sid][cr], st16))
                o_inter[hh].append(_dot(qd[sid][cr], st16))
            yield
            for hh in heads:
                sid = (r, hh)
                kd = (kg[sid][cr] * jnp.exp(g_last[hh] - cg[sid][cr])).astype(BF16)
                state_sc[hh] = st[hh] * jnp.exp(g_last[hh]) + _dot_tn(kd, v_new[hh][c].astype(BF16))
            yield
        for hh in heads:
            vn_all = jnp.concatenate(v_new[hh], axis=0).astype(BF16)
            o_sc[hh, rows[r], :] = jnp.concatenate(o_inter[hh], axis=0) + _dot(qk[(r, hh)], vn_all)
        yield

    def interleave(slow, fast, ratio):
        slow_live = fast_live = True
        while slow_live or fast_live:
            if slow_live:
                slow_live = next(slow, "done") != "done"
            for _ in range(ratio):
                if fast_live:
                    fast_live = next(fast, "done") != "done"

    for _ in intra_chunk(0):
        pass
    for r in groups:
        nxt = intra_chunk(r + 1) if r + 1 < len(groups) else iter(())
        interleave(recurrence(r), nxt, 3)

    for hh in heads:
        o = o_sc[hh]
        on = o * lax.rsqrt(jnp.mean(o * o, axis=-1, keepdims=True) + RMS_EPS) * nw_ref[...]
        o_ref[:, cols[hh]] = (on * _silu(z_ref[:, cols[hh]].astype(F32))).astype(BF16)


def _gdn(proj, gates_t, conv_w, norm_w, casts, cast_idx, *, q_blk, k_blk, v_blk, z_blk, tb=512, hp=8):
    s = proj.shape[0]
    w = hp * HEAD_W
    assert all(blk % hp == 0 for blk in (q_blk, k_blk, v_blk, z_blk))
    col = lambda blk: pl.BlockSpec((tb, w), lambda g, i: (i, blk // hp + g))
    cw = lambda blk: pl.BlockSpec((DN_CONV, w), lambda g, i: (0, blk // hp + g))
    n_t = s // tb
    slabs = [_cast_slabs(arr, cast_idx, (HEADS // hp) * n_t, lambda g, i: g * n_t + i) for arr in casts]
    return pl.pallas_call(
        functools.partial(_gdn_kernel, tb=tb, hp=hp, n_cast=len(casts)),
        grid=(HEADS // hp, n_t),
        in_specs=[col(q_blk), col(k_blk), col(v_blk), col(z_blk),
                  pl.BlockSpec((2 * HEADS, tb), lambda g, i: (0, i)),
                  cw(0), cw(HEADS), cw(2 * HEADS),
                  pl.BlockSpec((1, HEAD_W), lambda g, i: (0, 0))] + [sl[0] for sl in slabs],
        out_specs=[pl.BlockSpec((tb, w), lambda g, i: (i, g))] + [sl[1] for sl in slabs],
        out_shape=[jax.ShapeDtypeStruct((s, HEADS * HEAD_W), BF16)] + [sl[2] for sl in slabs],
        scratch_shapes=[pltpu.VMEM((3 * hp, tb + 8, HEAD_W), F32),
                        pltpu.VMEM((hp, HEAD_W, HEAD_W), F32),
                        pltpu.VMEM((hp, tb, HEAD_W), F32)],
        compiler_params=_params("arbitrary", "arbitrary"),
        name="gdn",
    )(proj, proj, proj, proj, gates_t, conv_w, conv_w, conv_w, norm_w, *casts)


def _merge_kernel(ya_ref, yb_ref, ga_ref, gb_ref, x_ref, gate_ref, lng_ref, lnb_ref,
                  wa_ref, wb_ref, wo_ref, o_ref, m_sc, *, alpha):
    i = pl.program_id(0)
    n_tiles = pl.num_programs(0) - 1

    def merge(slot):
        a = _dot(ya_ref[...], wa_ref[...])
        b = _dot(yb_ref[...], wb_ref[...])
        m_sc[slot] = (jax.nn.sigmoid(ga_ref[...].astype(F32)) * a
                      + jax.nn.sigmoid(gb_ref[...].astype(F32)) * b).astype(BF16)

    def project(slot):
        y = _dot(m_sc[slot], wo_ref[...])
        r = alpha * x_ref[...] + gate_ref[...] * y
        o_ref[...] = _layer_norm(r, lng_ref[...], lnb_ref[...])

    pl.when(i == 0)(lambda: merge(0))
    for slot in range(2):
        @pl.when(jnp.logical_and(jnp.logical_and(i > 0, i < n_tiles), (i & 1) == slot))
        def _():
            project(1 - slot)
            merge(slot)

        pl.when(jnp.logical_and(i == n_tiles, ((i - 1) & 1) == slot))(functools.partial(project, slot))


def _merge(ya, yb, proj, x, gate, lng, lnb, w_a, w_b, w_o, *, ga_blk, gb_blk, alpha, tm=256):
    s, d = x.shape
    dv = ya.shape[1]
    n = s // tm
    vec = pl.BlockSpec((1, d), lambda i: (0, 0))
    const = lambda shape: pl.BlockSpec(shape, lambda i: (0, 0), pipeline_mode=pl.Buffered(1))
    cur = lambda i: jnp.minimum(i, n - 1)
    prev = lambda i: jnp.maximum(i - 1, 0)
    return pl.pallas_call(
        functools.partial(_merge_kernel, alpha=alpha),
        grid=(n + 1,),
        in_specs=[pl.BlockSpec((tm, dv), lambda i: (cur(i), 0)),
                  pl.BlockSpec((tm, dv), lambda i: (cur(i), 0)),
                  pl.BlockSpec((tm, d), lambda i: (cur(i), ga_blk)),
                  pl.BlockSpec((tm, d), lambda i: (cur(i), gb_blk)),
                  pl.BlockSpec((tm, d), lambda i: (prev(i), 0)), vec, vec, vec,
                  const(w_a.shape), const(w_b.shape), const(w_o.shape)],
        out_specs=pl.BlockSpec((tm, d), lambda i: (prev(i), 0)),
        out_shape=jax.ShapeDtypeStruct((s, d), F32),
        scratch_shapes=[pltpu.VMEM((2, tm, d), BF16)],
        compiler_params=_params("arbitrary"),
        name="merge",
    )(ya, yb, proj, proj, x, gate, lng, lnb, w_a, w_b, w_o)


def kernel(x, c, w_ada, b_ada, ln_g, ln_b, w_ffn_in, w_ffn_out, w_in, conv_w, dn_a_log, dn_dt_bias,
           dn_norm_w, diff_lambda, diff_subln_w, rel_bias, w_branch_a, w_branch_b, w_out):
    bsz, s, d = x.shape
    assert bsz == 1, "one sequence per call"
    depth = w_ada.shape[0]
    alpha = (2 * depth) ** 0.25
    hw = HEADS * HEAD_W
    attn_t = 256
    assert d % HEAD_W == 0 and s % 1024 == 0

    o_nb = 7 * hw
    o_ga = o_nb + 2 * HEADS
    nblk = d // HEAD_W
    blk = {"ga": 0, "gb": 1, "dq": 2 * nblk, "dk": 2 * nblk + HEADS, "dv": 2 * nblk + 2 * HEADS,
           "nq": 2 * nblk + 3 * HEADS, "nk": 2 * nblk + 4 * HEADS, "nv": 2 * nblk + 5 * HEADS,
           "nz": 2 * nblk + 6 * HEADS}

    bias_tiles = _bias_tiles(rel_bias, attn_t)
    wfi, wfo = w_ffn_in, w_ffn_out
    w_in_t = jnp.swapaxes(w_in, 1, 2)
    x2 = x[0]
    for l in range(depth):
        lam_init = 0.8 - 0.6 * math.exp(-0.3 * l)
        ada = _ada(c.reshape(d, 1), w_ada[l], b_ada[l].reshape(1, -1)).reshape(N_SUB, 3, 1, d)
        shift, scale, gate = ada[:, 0], ada[:, 1], ada[:, 2]
        lng, lnb = ln_g[l].reshape(N_SUB, 1, d), ln_b[l].reshape(N_SUB, 1, d)
        w_small_t = w_in_t[l, o_nb:o_ga]

        x2 = _ffn(x2, shift[0], scale[0], gate[0], lng[0], lnb[0], wfi, wfo, layer=l, which=0, alpha=alpha)

        gates_t, h = _gates(x2, shift[1], scale[1], w_small_t, dn_a_log[l].reshape(HEADS, 1),
                            dn_dt_bias[l].reshape(HEADS, 1))
        proj, wfi2, wfo2 = _proj(h, w_in_t[l], w_ffn_in, w_ffn_out, (l, 1),
                                 gate_row0=o_ga, n_gate_cols=2 * d, n_head_cols=o_nb)
        ya = _attn(proj, bias_tiles, diff_lambda[l], diff_subln_w[l].reshape(HEAD_W, 1),
                   q_blk=blk["dq"], k_blk=blk["dk"], v_blk=blk["dv"], lam_init=lam_init, t=attn_t)
        yb, wa16, wb16, wo16 = _gdn(proj, gates_t, conv_w[l], dn_norm_w[l].reshape(1, HEAD_W),
                                    (w_branch_a, w_branch_b, w_out), (l,),
                                    q_blk=blk["nq"], k_blk=blk["nk"], v_blk=blk["nv"], z_blk=blk["nz"])
        x2 = _merge(ya, yb, proj, x2, gate[1], lng[1], lnb[1], wa16, wb16, wo16,
                    ga_blk=blk["ga"], gb_blk=blk["gb"], alpha=alpha)

        x2 = _ffn(x2, shift[2], scale[2], gate[2], lng[2], lnb[2], wfi2[None, None], wfo2[None, None],
                  layer=0, which=0, alpha=alpha, tf=512)
    return x2[None]
```

```python
import functools
import math

import numpy as np
import jax
import jax.numpy as jnp
from jax import lax
from jax.experimental import pallas as pl
from jax.experimental.pallas import tpu as pltpu

N_SUB = 3
HEADS = 8
HEAD_W = 128
DIFF_QK_DIM = 64
DN_CONV = 4
DN_CHUNK = 64
REL_BUCKETS = 32
REL_MAX_DIST = 128
LN_EPS = 1e-5
RMS_EPS = 1e-6
MASK_VALUE = -1e30
LOG2E = math.log2(math.e)
V_ROWS = HEAD_W + 16

F32 = jnp.float32
BF16 = jnp.bfloat16

V7X_VMEM_BYTES = 64 * 1024 * 1024
VMEM_LIMIT = V7X_VMEM_BYTES - 4 * 1024 * 1024


def _params(*sem):
    return pltpu.CompilerParams(dimension_semantics=sem, vmem_limit_bytes=VMEM_LIMIT)


def _tile(n, preferred):
    t = min(preferred, n)
    while n % t:
        t -= 128
    return t


def _silu(x):
    return x * jax.nn.sigmoid(x)


def _dot(a, b):
    return jnp.dot(a, b, preferred_element_type=F32)


def _dot_nt(a, b):
    return lax.dot_general(a, b, (((1,), (1,)), ((), ())), preferred_element_type=F32)


def _dot_tn(a, b):
    return lax.dot_general(a, b, (((0,), (0,)), ((), ())), preferred_element_type=F32)


def _layer_norm(r, g, b):
    mu = jnp.mean(r, axis=-1, keepdims=True)
    d = r - mu
    var = jnp.mean(d * d, axis=-1, keepdims=True)
    return d * lax.rsqrt(var + LN_EPS) * g + b


def _ada_kernel(c_ref, w_ref, b_ref, o_ref):
    sc = _silu(c_ref[...])
    o_ref[...] = jnp.sum(w_ref[...] * sc, axis=0, keepdims=True) + b_ref[...]


def _ada(c_col, w, b):
    d, n = w.shape
    tn = _tile(n, 1024)
    return pl.pallas_call(
        _ada_kernel,
        grid=(n // tn,),
        in_specs=[pl.BlockSpec((d, 1), lambda j: (0, 0)),
                  pl.BlockSpec((d, tn), lambda j: (0, j)),
                  pl.BlockSpec((1, tn), lambda j: (0, j))],
        out_specs=pl.BlockSpec((1, tn), lambda j: (0, j)),
        out_shape=jax.ShapeDtypeStruct((1, n), F32),
        compiler_params=_params("arbitrary"),
        name="ada",
    )(c_col, w, b)


def _ffn_kernel(x_ref, shift_ref, scale_ref, gate_ref, lng_ref, lnb_ref, wg_ref, wu_ref, wo_ref,
                o_ref, h_sc, a_sc, *, alpha, gate_mul):
    j = pl.program_id(1)
    n_chunks = pl.num_programs(1) - 1

    def drain(slot):
        return _dot(a_sc[slot], wo_ref[...].astype(BF16))

    def activations(slot):
        h = h_sc[...]
        g = _dot(h, wg_ref[...].astype(BF16))
        u = _dot(h, wu_ref[...].astype(BF16))
        a_sc[slot] = (_silu(g) * u).astype(BF16)

    @pl.when(j == 0)
    def _():
        h_sc[...] = (x_ref[...] * (1.0 + scale_ref[...]) + shift_ref[...]).astype(BF16)
        o_ref[...] = jnp.zeros_like(o_ref)
        activations(0)

    for slot in range(2):
        @pl.when(jnp.logical_and(jnp.logical_and(j > 0, j < n_chunks), (j & 1) == slot))
        def _():
            o_ref[...] += drain(1 - slot)
            activations(slot)

        @pl.when(jnp.logical_and(j == n_chunks, ((j - 1) & 1) == slot))
        def _():
            r = alpha * x_ref[...] + (gate_mul * gate_ref[...]) * (o_ref[...] + drain(slot))
            o_ref[...] = _layer_norm(r, lng_ref[...], lnb_ref[...])


def _ffn(x, shift, scale, gate, lng, lnb, w_in, w_out, *, layer, which, alpha, tm=1024, tf=256):
    s, d = x.shape
    f = w_out.shape[2]
    tm, tf = _tile(s, tm), _tile(f, tf)
    nf = f // tf
    vec = pl.BlockSpec((1, d), lambda i, j: (0, 0))
    up = lambda j: jnp.minimum(j, nf - 1)
    down = lambda j: jnp.maximum(j - 1, 0)
    return pl.pallas_call(
        functools.partial(_ffn_kernel, alpha=alpha, gate_mul=0.5),
        grid=(s // tm, nf + 1),
        in_specs=[pl.BlockSpec((tm, d), lambda i, j: (i, 0)), vec, vec, vec, vec, vec,
                  pl.BlockSpec((None, None, d, tf), lambda i, j: (layer, which, 0, up(j))),
                  pl.BlockSpec((None, None, d, tf), lambda i, j: (layer, which, 0, up(j) + nf)),
                  pl.BlockSpec((None, None, tf, d), lambda i, j: (layer, which, down(j), 0))],
        out_specs=pl.BlockSpec((tm, d), lambda i, j: (i, 0)),
        out_shape=jax.ShapeDtypeStruct((s, d), F32),
        scratch_shapes=[pltpu.VMEM((tm, d), BF16), pltpu.VMEM((2, tm, tf), BF16)],
        compiler_params=_params("parallel", "arbitrary"),
        name="ffn",
    )(x, shift, scale, gate, lng, lnb, w_in, w_in, w_out)


def _proj_kernel(h_ref, w_ref, ca_ref, cb_ref, o_ref, ca16_ref, cb16_ref, w_sc):
    @pl.when(pl.program_id(1) == 0)
    def _():
        w_sc[...] = w_ref[...].astype(BF16)

    o_ref[...] = _dot_nt(h_ref[...], w_sc[...]).astype(BF16)
    ca16_ref[...] = ca_ref[...].astype(BF16)
    cb16_ref[...] = cb_ref[...].astype(BF16)


def _cast_slabs(arr, lead_idx, n_steps, step_of):
    rows, cols = arr.shape[-2:]
    r = 16
    while rows % r or rows // r > n_steps:
        r += 16
    slab = lambda *g: jnp.minimum(step_of(*g), rows // r - 1)
    return (pl.BlockSpec((None,) * len(lead_idx) + (r, cols), lambda *g: (*lead_idx, slab(*g), 0)),
            pl.BlockSpec((r, cols), lambda *g: (slab(*g), 0)), jax.ShapeDtypeStruct((rows, cols), BF16))


def _proj(h, w_all_t, cast_a, cast_b, cast_idx, *, gate_row0, n_gate_cols, n_head_cols, tm=1024, tn=1024):
    s, d = h.shape
    tm = _tile(s, tm)
    tn = math.gcd(_tile(n_gate_cols, tn), _tile(n_head_cols, tn))
    n_gate, n_head = n_gate_cols // tn, n_head_cols // tn
    assert gate_row0 % 8 == 0
    first_row = lambda j: pl.multiple_of(jnp.where(j < n_gate, gate_row0 + j * tn, (j - n_gate) * tn), 8)
    n_rows = s // tm
    n_steps = (n_gate + n_head) * n_rows
    step_of = lambda j, i: j * n_rows + i
    (a_in, a_out, a_shape), (b_in, b_out, b_shape) = (_cast_slabs(arr, cast_idx, n_steps, step_of)
                                                      for arr in (cast_a, cast_b))
    return pl.pallas_call(
        _proj_kernel,
        grid=(n_gate + n_head, n_rows),
        in_specs=[pl.BlockSpec((tm, d), lambda j, i: (i, 0)),
                  pl.BlockSpec((pl.Element(tn), pl.Element(d)), lambda j, i: (first_row(j), 0)), a_in, b_in],
        out_specs=[pl.BlockSpec((tm, tn), lambda j, i: (i, j)), a_out, b_out],
        out_shape=[jax.ShapeDtypeStruct((s, n_gate_cols + n_head_cols), BF16), a_shape, b_shape],
        scratch_shapes=[pltpu.VMEM((tn, d), BF16)],
        compiler_params=_params("arbitrary", "arbitrary"),
        name="proj",
    )(h, w_all_t, cast_a, cast_b)


def _bucket_starts():
    n = np.arange(0, 2 * REL_MAX_DIST)
    max_exact = REL_BUCKETS // 2
    nf = np.maximum(n, max_exact).astype(np.float32)
    large = max_exact + (np.log(nf / np.float32(max_exact)) / np.float32(math.log(REL_MAX_DIST / max_exact))
                         * np.float32(REL_BUCKETS - max_exact)).astype(np.int32)
    bucket = np.where(n < max_exact, n, np.minimum(large, REL_BUCKETS - 1))
    assert np.all(np.diff(bucket) >= 0) and bucket[-1] == REL_BUCKETS - 1
    return [int(np.min(n[bucket >= b])) for b in range(REL_BUCKETS)]


BUCKET_STARTS = _bucket_starts()


def _bias_kernel(rb_ref, o_ref, *, t):
    h = pl.program_id(0)
    jj = lax.broadcasted_iota(jnp.int32, (t, 2 * t), 0)
    ii = lax.broadcasted_iota(jnp.int32, (t, 2 * t), 1)
    ii = jnp.where(ii >= t, ii - t, ii)
    far = rb_ref[REL_BUCKETS - 1, h]
    for n in range(2):
        rel = ii - jj + n * t
        bias = jnp.full((t, 2 * t), (rb_ref[0, h] - far) * LOG2E, F32)
        for b in range(1, REL_BUCKETS):
            bias = jnp.where(rel >= BUCKET_STARTS[b], (rb_ref[b, h] - far) * LOG2E, bias)
        o_ref[0, n] = jnp.where(rel < 0, MASK_VALUE, bias)


def _bias_tiles(rel_bias, t):
    return pl.pallas_call(
        functools.partial(_bias_kernel, t=t),
        grid=(HEADS,),
        in_specs=[pl.BlockSpec(memory_space=pltpu.SMEM)],
        out_specs=pl.BlockSpec((1, 2, t, 2 * t), lambda h: (h, 0, 0, 0)),
        out_shape=jax.ShapeDtypeStruct((HEADS, 2, t, 2 * t), F32),
        compiler_params=_params("arbitrary"),
        name="bias_tiles",
    )(rel_bias)


def _attn_kernel(q_ref, k_ref, v_ref, bias_ref, lam_ref, sub_ref, o_ref,
                 vt_sc, qt_sc, s_sc, m_sc, acc_sc, *, t, hp, lam_init):
    qi = pl.program_id(1)
    n_kv = vt_sc.shape[1]
    heads = range(hp)
    cols = [slice(hh * HEAD_W, (hh + 1) * HEAD_W) for hh in heads]

    @pl.when(qi == 0)
    def _():
        ones_tile = (lax.broadcasted_iota(jnp.int32, (V_ROWS - HEAD_W, t), 0) == 0).astype(BF16)

        def body(c, carry):
            r0 = pl.multiple_of(c * t, t)
            for hh in heads:
                vt_sc[hh, c, 0:HEAD_W] = v_ref[pl.ds(r0, t), cols[hh]].astype(F32).T.astype(BF16)
                vt_sc[hh, c, HEAD_W:V_ROWS] = ones_tile
            return carry
        lax.fori_loop(0, n_kv, body, 0)

    row = lax.broadcasted_iota(jnp.int32, (HEAD_W, t), 0)
    for hh in heads:
        qt = (q_ref[:, cols[hh]].astype(F32) * (DIFF_QK_DIM ** -0.5 * LOG2E)).T
        zero = jnp.zeros_like(qt)
        qt_sc[hh] = jnp.concatenate([jnp.where(row < DIFF_QK_DIM, qt, zero),
                                     jnp.where(row >= DIFF_QK_DIM, qt, zero)], axis=1).astype(BF16)

    m_sc[...] = jnp.full_like(m_sc, -jnp.inf)
    acc_sc[...] = jnp.zeros_like(acc_sc)

    def scores(c, slot, biased, which=heads):
        r0 = pl.multiple_of(c * t, t)
        for hh in which:
            s = _dot(k_ref[pl.ds(r0, t), cols[hh]], qt_sc[hh])
            if biased:
                s = s + bias_ref[hh, qi - c]
            s_sc[slot, hh] = s

    def consume(c, slot, which=heads):
        for hh in which:
            s = s_sc[slot, hh]
            m_old = m_sc[hh]
            m_new = jnp.maximum(m_old, jnp.max(s, axis=0, keepdims=True))
            p = jnp.exp2(s - m_new).astype(BF16)
            acc_sc[hh] = jnp.exp2(m_old - m_new) * acc_sc[hh] + _dot(vt_sc[hh, c], p)
            m_sc[hh] = m_new

    pl.when(qi >= 2)(lambda: scores(0, 0, False))
    pl.when(qi < 2)(lambda: scores(0, 0, True))

    n_pairs = jnp.maximum(qi - 2, 0) >> 1

    def pair_body(j, carry):
        c = 2 * j
        scores(c + 1, 1, False)
        for hh in heads:
            consume(c, 0, [hh])
            scores(c + 2, 0, False, [hh])
        consume(c + 1, 1)
        return carry
    lax.fori_loop(0, n_pairs, pair_body, 0)

    @pl.when(jnp.logical_and(qi >= 2, (qi & 1) == 1))
    def _():
        scores(qi - 2, 1, False)
        consume(qi - 3, 0)

    for slot in range(2):
        @pl.when(jnp.logical_and(qi >= 2, (qi & 1) == slot))
        def _():
            scores(qi - 1, 1 - slot, True)
            for hh in heads:
                consume(qi - 2, slot, [hh])
                scores(qi, slot, True, [hh])
            consume(qi - 1, 1 - slot)
            consume(qi, slot)

    @pl.when(qi == 1)
    def _():
        scores(1, 1, True)
        consume(0, 0)
        consume(1, 1)

    pl.when(qi == 0)(lambda: consume(0, 0))

    lp = lam_ref[...]
    lam = (jnp.exp(jnp.sum(lp[0:1] * lp[1:2], axis=1, keepdims=True))
           - jnp.exp(jnp.sum(lp[2:3] * lp[3:4], axis=1, keepdims=True)) + lam_init)
    for hh in heads:
        acc = acc_sc[hh]
        o = acc[0:HEAD_W] / acc[HEAD_W:HEAD_W + 1]
        od = o[:, :t] - lam * o[:, t:]
        ms = jnp.mean(od * od, axis=0, keepdims=True)
        y = od * lax.rsqrt(ms + LN_EPS) * sub_ref[...] * (1.0 - lam_init)
        o_ref[:, cols[hh]] = y.T.astype(BF16)


def _attn(proj, bias_tiles, lam_params, subln_col, *, q_blk, k_blk, v_blk, lam_init, t, hp=4):
    s = proj.shape[0]
    w = hp * HEAD_W
    assert q_blk % hp == 0 and k_blk % hp == 0 and v_blk % hp == 0
    once = pl.Buffered(1)
    return pl.pallas_call(
        functools.partial(_attn_kernel, t=t, hp=hp, lam_init=lam_init),
        grid=(HEADS // hp, s // t),
        in_specs=[pl.BlockSpec((t, w), lambda g, i: (i, q_blk // hp + g)),
                  pl.BlockSpec((s, w), lambda g, i: (0, k_blk // hp + g), pipeline_mode=once),
                  pl.BlockSpec((s, w), lambda g, i: (0, v_blk // hp + g), pipeline_mode=once),
                  pl.BlockSpec((hp, 2, t, 2 * t), lambda g, i: (g, 0, 0, 0), pipeline_mode=once),
                  pl.BlockSpec(lam_params.shape, lambda g, i: (0, 0)),
                  pl.BlockSpec((HEAD_W, 1), lambda g, i: (0, 0))],
        out_specs=pl.BlockSpec((t, w), lambda g, i: (i, g)),
        out_shape=jax.ShapeDtypeStruct((s, HEADS * HEAD_W), BF16),
        scratch_shapes=[pltpu.VMEM((hp, s // t, V_ROWS, t), BF16),
                        pltpu.VMEM((hp, HEAD_W, 2 * t), BF16),
                        pltpu.VMEM((2, hp, t, 2 * t), F32),
                        pltpu.VMEM((hp, 1, 2 * t), F32),
                        pltpu.VMEM((hp, V_ROWS, 2 * t), F32)],
        compiler_params=_params("arbitrary", "arbitrary"),
        name="diff_attn",
    )(proj, proj, proj, bias_tiles, lam_params, subln_col)


def _gates_kernel(x_ref, shift_ref, scale_ref, ws_ref, alog_ref, dtb_ref, o_ref, h_ref):
    h = (x_ref[...] * (1.0 + scale_ref[...]) + shift_ref[...]).astype(BF16)
    h_ref[...] = h
    tr = _dot_nt(ws_ref[...], h.astype(F32))
    beta = jax.nn.sigmoid(tr[0:HEADS])
    x = tr[HEADS:2 * HEADS] + dtb_ref[...]
    softplus = jnp.maximum(x, 0.0) + jnp.log1p(jnp.exp(-jnp.abs(x)))
    g = -jnp.exp(alog_ref[...]) * softplus
    pos = lax.broadcasted_iota(jnp.int32, g.shape, 1) % DN_CHUNK
    shift = 1
    while shift < DN_CHUNK:
        g = g + jnp.where(pos >= shift, pltpu.roll(g, shift, axis=1), 0.0)
        shift *= 2
    o_ref[0:HEADS] = beta
    o_ref[HEADS:2 * HEADS] = g


def _gates(x, shift, scale, w_small_t, alog_col, dtb_col, *, tb=1024):
    s, d = x.shape
    col = pl.BlockSpec((HEADS, 1), lambda i: (0, 0))
    vec = pl.BlockSpec((1, d), lambda i: (0, 0))
    return pl.pallas_call(
        _gates_kernel,
        grid=(s // tb,),
        in_specs=[pl.BlockSpec((tb, d), lambda i: (i, 0)), vec, vec,
                  pl.BlockSpec(w_small_t.shape, lambda i: (0, 0)), col, col],
        out_specs=[pl.BlockSpec((2 * HEADS, tb), lambda i: (0, i)), pl.BlockSpec((tb, d), lambda i: (i, 0))],
        out_shape=[jax.ShapeDtypeStruct((2 * HEADS, s), F32), jax.ShapeDtypeStruct((s, d), BF16)],
        compiler_params=_params("parallel"),
        name="gdn_gates",
    )(x, shift, scale, w_small_t, alog_col, dtb_col)


GROUP = 2 * DN_CHUNK
INV_BASE = 16


def _gdn_kernel(q_ref, k_ref, v_ref, z_ref, gt_ref, cwq_ref, cwk_ref, cwv_ref, nw_ref, *rest, tb, hp, n_cast):
    cast_in, (o_ref, *cast_out), (pad_sc, state_sc, o_sc) = rest[:n_cast], rest[n_cast:2 * n_cast + 1], rest[-3:]
    for src, dst in zip(cast_in, cast_out):
        dst[...] = src[...].astype(BF16)
    g = pl.program_id(0)
    ib = pl.program_id(1)
    halo = 8
    heads = range(hp)
    cols = [slice(hh * HEAD_W, (hh + 1) * HEAD_W) for hh in heads]

    @pl.when(ib == 0)
    def _():
        state_sc[...] = jnp.zeros_like(state_sc)
        pad_sc[:, 0:halo, :] = jnp.zeros((3 * hp, halo, HEAD_W), F32)

    def conv_silu(a, x_ref, cw_ref, hh):
        a = a * hp + hh
        pad_sc[a, halo:halo + tb, :] = x_ref[:, cols[hh]].astype(F32)
        cw = cw_ref[:, cols[hh]]
        y = cw[0:1] * pad_sc[a, halo - 3:halo - 3 + tb, :]
        for j in range(1, DN_CONV):
            y = y + cw[j:j + 1] * pad_sc[a, halo - 3 + j:halo - 3 + j + tb, :]
        pad_sc[a, 0:halo, :] = pad_sc[a, tb:tb + halo, :]
        return _silu(y)

    def l2n(x):
        return x * lax.rsqrt(jnp.sum(x * x, axis=-1, keepdims=True) + RMS_EPS)

    ii = lax.broadcasted_iota(jnp.int32, (GROUP, GROUP), 0)
    jj = lax.broadcasted_iota(jnp.int32, (GROUP, GROUP), 1)
    same = (ii >= DN_CHUNK) == (jj >= DN_CHUNK)
    tril = jnp.logical_and(same, ii >= jj)
    eye = (ii == jj).astype(F32)
    bits = INV_BASE.bit_length() - 1
    in_base = jnp.logical_and(ii >> bits == jj >> bits, ii > jj)
    below = []
    while (1 << bits) < DN_CHUNK:
        below.append(jnp.logical_and(ii >> (bits + 1) == jj >> (bits + 1), ii >> bits == (jj >> bits) + 1))
        bits += 1

    q, k, v, beta_c, cum_c, cum_t = [], [], [], [], [], []
    for hh in heads:
        q.append(l2n(conv_silu(0, q_ref, cwq_ref, hh)) * (HEAD_W ** -0.5))
        k.append(l2n(conv_silu(1, k_ref, cwk_ref, hh)))
        v.append(conv_silu(2, v_ref, cwv_ref, hh))
        head = g * hp + hh
        beta_t = jnp.broadcast_to(gt_ref[pl.ds(head, 1), :], (HEAD_W, tb))
        ct = jnp.broadcast_to(gt_ref[pl.ds(HEADS + head, 1), :], (HEAD_W, tb))
        beta_c.append(beta_t.T)
        cum_c.append(ct.T)
        cum_t.append(ct)

    groups = range(tb // GROUP)
    rows = [slice(r * GROUP, (r + 1) * GROUP) for r in groups]
    u, w, qk, qd, kg, cg = {}, {}, {}, {}, {}, {}

    def intra_chunk(r):
        kb, x, tinv, decay, merges = {}, {}, {}, {}, {}
        for hh in heads:
            sid = (r, hh)
            kg[sid], cg[sid] = k[hh][rows[r]], cum_c[hh][rows[r]]
            gdiff = cg[sid] - cum_t[hh][:, rows[r]]
            decay[hh] = jnp.where(tril, jnp.exp(jnp.where(tril, gdiff, 0.0)), 0.0)
            kb[hh] = kg[sid] * beta_c[hh][rows[r]]
            a = _dot_nt(kb[hh].astype(BF16), kg[sid].astype(BF16)) * decay[hh]
            diag = jnp.where(in_base, a, 0.0)
            x[hh] = diag.astype(BF16)
            tinv[hh] = eye - diag
            merges[hh] = [jnp.where(m, a, 0.0).astype(BF16) for m in below]
        yield
        power = 2
        while power < INV_BASE:
            for hh in heads:
                x[hh] = _dot(x[hh], x[hh]).astype(BF16)
            yield
            for hh in heads:
                tinv[hh] = tinv[hh] + _dot(tinv[hh].astype(BF16), x[hh])
            yield
            power *= 2
        for level in range(len(below)):
            half = {}
            for hh in heads:
                half[hh] = _dot(tinv[hh].astype(BF16), merges[hh][level]).astype(BF16)
            yield
            for hh in heads:
                tinv[hh] = tinv[hh] - _dot(half[hh], tinv[hh].astype(BF16))
            yield
        for hh in heads:
            sid = (r, hh)
            eg = jnp.exp(cg[sid])
            rhs = jnp.concatenate([v[hh][rows[r]] * beta_c[hh][rows[r]], kb[hh] * eg], axis=1).astype(BF16)
            uw = _dot(tinv[hh].astype(BF16), rhs)
            u[sid], w[sid] = uw[:, :HEAD_W], uw[:, HEAD_W:].astype(BF16)
            qg = q[hh][rows[r]]
            qk[sid] = (_dot_nt(qg.astype(BF16), kg[sid].astype(BF16)) * decay[hh]).astype(BF16)
            qd[sid] = (qg * eg).astype(BF16)
        yield

    def recurrence(r):
        v_new = {hh: [] for hh in heads}
        o_inter = {hh: [] for hh in heads}
        for c in range(2):
            cr = slice(c * DN_CHUNK, (c + 1) * DN_CHUNK)
            last = c * DN_CHUNK + DN_CHUNK - 1
            st, g_last = {}, {}
            for hh in heads:
                sid = (r, hh)
                g_last[hh] = cg[sid][last:last + 1, :]
                st[hh] = state_sc[hh]
                st16 = st[hh].astype(BF16)
                v_new[hh].append(u[sid][cr] - _dot(w[sid][cr], st16))
                o_inter[hh].append(_dot(qd[sid][cr], st16))
            yield
            for hh in heads:
                sid = (r, hh)
                kd = (kg[sid][cr] * jnp.exp(g_last[hh] - cg[sid][cr])).astype(BF16)
                state_sc[hh] = st[hh] * jnp.exp(g_last[hh]) + _dot_tn(kd, v_new[hh][c].astype(BF16))
            yield
        for hh in heads:
            vn_all = jnp.concatenate(v_new[hh], axis=0).astype(BF16)
            o_sc[hh, rows[r], :] = jnp.concatenate(o_inter[hh], axis=0) + _dot(qk[(r, hh)], vn_all)
        yield

    def interleave(slow, fast, ratio):
        slow_live = fast_live = True
        while slow_live or fast_live:
            if slow_live:
                slow_live = next(slow, "done") != "done"
            for _ in range(ratio):
                if fast_live:
                    fast_live = next(fast, "done") != "done"

    for _ in intra_chunk(0):
        pass
    for r in groups:
        nxt = intra_chunk(r + 1) if r + 1 < len(groups) else iter(())
        interleave(recurrence(r), nxt, 3)

    for hh in heads:
        o = o_sc[hh]
        on = o * lax.rsqrt(jnp.mean(o * o, axis=-1, keepdims=True) + RMS_EPS) * nw_ref[...]
        o_ref[:, cols[hh]] = (on * _silu(z_ref[:, cols[hh]].astype(F32))).astype(BF16)


def _gdn(proj, gates_t, conv_w, norm_w, casts, cast_idx, *, q_blk, k_blk, v_blk, z_blk, tb=512, hp=8):
    s = proj.shape[0]
    w = hp * HEAD_W
    assert all(blk % hp == 0 for blk in (q_blk, k_blk, v_blk, z_blk))
    col = lambda blk: pl.BlockSpec((tb, w), lambda g, i: (i, blk // hp + g))
    cw = lambda blk: pl.BlockSpec((DN_CONV, w), lambda g, i: (0, blk // hp + g))
    n_t = s // tb
    slabs = [_cast_slabs(arr, cast_idx, (HEADS // hp) * n_t, lambda g, i: g * n_t + i) for arr in casts]
    return pl.pallas_call(
        functools.partial(_gdn_kernel, tb=tb, hp=hp, n_cast=len(casts)),
        grid=(HEADS // hp, n_t),
        in_specs=[col(q_blk), col(k_blk), col(v_blk), col(z_blk),
                  pl.BlockSpec((2 * HEADS, tb), lambda g, i: (0, i)),
                  cw(0), cw(HEADS), cw(2 * HEADS),
                  pl.BlockSpec((1, HEAD_W), lambda g, i: (0, 0))] + [sl[0] for sl in slabs],
        out_specs=[pl.BlockSpec((tb, w), lambda g, i: (i, g))] + [sl[1] for sl in slabs],
        out_shape=[jax.ShapeDtypeStruct((s, HEADS * HEAD_W), BF16)] + [sl[2] for sl in slabs],
        scratch_shapes=[pltpu.VMEM((3 * hp, tb + 8, HEAD_W), F32),
                        pltpu.VMEM((hp, HEAD_W, HEAD_W), F32),
                        pltpu.VMEM((hp, tb, HEAD_W), F32)],
        compiler_params=_params("arbitrary", "arbitrary"),
        name="gdn",
    )(proj, proj, proj, proj, gates_t, conv_w, conv_w, conv_w, norm_w, *casts)


def _merge_kernel(ya_ref, yb_ref, ga_ref, gb_ref, x_ref, gate_ref, lng_ref, lnb_ref,
                  wa_ref, wb_ref, wo_ref, o_ref, m_sc, *, alpha):
    i = pl.program_id(0)
    n_tiles = pl.num_programs(0) - 1

    def merge(slot):
        a = _dot(ya_ref[...], wa_ref[...])
        b = _dot(yb_ref[...], wb_ref[...])
        m_sc[slot] = (jax.nn.sigmoid(ga_ref[...].astype(F32)) * a
                      + jax.nn.sigmoid(gb_ref[...].astype(F32)) * b).astype(BF16)

    def project(slot):
        y = _dot(m_sc[slot], wo_ref[...])
        r = alpha * x_ref[...] + gate_ref[...] * y
        o_ref[...] = _layer_norm(r, lng_ref[...], lnb_ref[...])

    pl.when(i == 0)(lambda: merge(0))
    for slot in range(2):
        @pl.when(jnp.logical_and(jnp.logical_and(i > 0, i < n_tiles), (i & 1) == slot))
        def _():
            project(1 - slot)
            merge(slot)

        pl.when(jnp.logical_and(i == n_tiles, ((i - 1) & 1) == slot))(functools.partial(project, slot))


def _merge(ya, yb, proj, x, gate, lng, lnb, w_a, w_b, w_o, *, ga_blk, gb_blk, alpha, tm=256):
    s, d = x.shape
    dv = ya.shape[1]
    n = s // tm
    vec = pl.BlockSpec((1, d), lambda i: (0, 0))
    const = lambda shape: pl.BlockSpec(shape, lambda i: (0, 0), pipeline_mode=pl.Buffered(1))
    cur = lambda i: jnp.minimum(i, n - 1)
    prev = lambda i: jnp.maximum(i - 1, 0)
    return pl.pallas_call(
        functools.partial(_merge_kernel, alpha=alpha),
        grid=(n + 1,),
        in_specs=[pl.BlockSpec((tm, dv), lambda i: (cur(i), 0)),
                  pl.BlockSpec((tm, dv), lambda i: (cur(i), 0)),
                  pl.BlockSpec((tm, d), lambda i: (cur(i), ga_blk)),
                  pl.BlockSpec((tm, d), lambda i: (cur(i), gb_blk)),
                  pl.BlockSpec((tm, d), lambda i: (prev(i), 0)), vec, vec, vec,
                  const(w_a.shape), const(w_b.shape), const(w_o.shape)],
        out_specs=pl.BlockSpec((tm, d), lambda i: (prev(i), 0)),
        out_shape=jax.ShapeDtypeStruct((s, d), F32),
        scratch_shapes=[pltpu.VMEM((2, tm, d), BF16)],
        compiler_params=_params("arbitrary"),
        name="merge",
    )(ya, yb, proj, proj, x, gate, lng, lnb, w_a, w_b, w_o)


def kernel(x, c, w_ada, b_ada, ln_g, ln_b, w_ffn_in, w_ffn_out, w_in, conv_w, dn_a_log, dn_dt_bias,
           dn_norm_w, diff_lambda, diff_subln_w, rel_bias, w_branch_a, w_branch_b, w_out):
    bsz, s, d = x.shape
    assert bsz == 1, "one sequence per call"
    depth = w_ada.shape[0]
    alpha = (2 * depth) ** 0.25
    hw = HEADS * HEAD_W
    attn_t = 256
    assert d % HEAD_W == 0 and s % 1024 == 0

    o_nb = 7 * hw
    o_ga = o_nb + 2 * HEADS
    nblk = d // HEAD_W
    blk = {"ga": 0, "gb": 1, "dq": 2 * nblk, "dk": 2 * nblk + HEADS, "dv": 2 * nblk + 2 * HEADS,
           "nq": 2 * nblk + 3 * HEADS, "nk": 2 * nblk + 4 * HEADS, "nv": 2 * nblk + 5 * HEADS,
           "nz": 2 * nblk + 6 * HEADS}

    bias_tiles = _bias_tiles(rel_bias, attn_t)
    wfi, wfo = w_ffn_in, w_ffn_out
    w_in_t = jnp.swapaxes(w_in, 1, 2)
    x2 = x[0]
    for l in range(depth):
        lam_init = 0.8 - 0.6 * math.exp(-0.3 * l)
        ada = _ada(c.reshape(d, 1), w_ada[l], b_ada[l].reshape(1, -1)).reshape(N_SUB, 3, 1, d)
        shift, scale, gate = ada[:, 0], ada[:, 1], ada[:, 2]
        lng, lnb = ln_g[l].reshape(N_SUB, 1, d), ln_b[l].reshape(N_SUB, 1, d)
        w_small_t = w_in_t[l, o_nb:o_ga]

        x2 = _ffn(x2, shift[0], scale[0], gate[0], lng[0], lnb[0], wfi, wfo, layer=l, which=0, alpha=alpha)

        gates_t, h = _gates(x2, shift[1], scale[1], w_small_t, dn_a_log[l].reshape(HEADS, 1),
                            dn_dt_bias[l].reshape(HEADS, 1))
        proj, wfi2, wfo2 = _proj(h, w_in_t[l], w_ffn_in, w_ffn_out, (l, 1),
                                 gate_row0=o_ga, n_gate_cols=2 * d, n_head_cols=o_nb)
        ya = _attn(proj, bias_tiles, diff_lambda[l], diff_subln_w[l].reshape(HEAD_W, 1),
                   q_blk=blk["dq"], k_blk=blk["dk"], v_blk=blk["dv"], lam_init=lam_init, t=attn_t)
        yb, wa16, wb16, wo16 = _gdn(proj, gates_t, conv_w[l], dn_norm_w[l].reshape(1, HEAD_W),
                                    (w_branch_a, w_branch_b, w_out), (l,),
                                    q_blk=blk["nq"], k_blk=blk["nk"], v_blk=blk["nv"], z_blk=blk["nz"])
        x2 = _merge(ya, yb, proj, x2, gate[1], lng[1], lnb[1], wa16, wb16, wo16,
                    ga_blk=blk["ga"], gb_blk=blk["gb"], alpha=alpha)

        x2 = _ffn(x2, shift[2], scale[2], gate[2], lng[2], lnb[2], wfi2[None, None], wfo2[None, None],
                  layer=0, which=0, alpha=alpha, tf=512)
    return x2[None]
```

```python
import functools
import math

import numpy as np
import jax
import jax.numpy as jnp
from jax import lax
from jax.experimental import pallas as pl
from jax.experimental.pallas import tpu as pltpu

N_SUB = 3
HEADS = 8
HEAD_W = 128
DIFF_QK_DIM = 64
DN_CONV = 4
DN_CHUNK = 64
REL_BUCKETS = 32
REL_MAX_DIST = 128
LN_EPS = 1e-5
RMS_EPS = 1e-6
MASK_VALUE = -1e30
LOG2E = math.log2(math.e)
V_ROWS = HEAD_W + 16

F32 = jnp.float32
BF16 = jnp.bfloat16

V7X_VMEM_BYTES = 64 * 1024 * 1024
VMEM_LIMIT = V7X_VMEM_BYTES - 4 * 1024 * 1024


def _params(*sem):
    return pltpu.CompilerParams(dimension_semantics=sem, vmem_limit_bytes=VMEM_LIMIT)


def _tile(n, preferred):
    t = min(preferred, n)
    while n % t:
        t -= 128
    return t


def _silu(x):
    return x * jax.nn.sigmoid(x)


def _dot(a, b):
    return jnp.dot(a, b, preferred_element_type=F32)


def _dot_nt(a, b):
    return lax.dot_general(a, b, (((1,), (1,)), ((), ())), preferred_element_type=F32)


def _dot_tn(a, b):
    return lax.dot_general(a, b, (((0,), (0,)), ((), ())), preferred_element_type=F32)


def _layer_norm(r, g, b):
    mu = jnp.mean(r, axis=-1, keepdims=True)
    d = r - mu
    var = jnp.mean(d * d, axis=-1, keepdims=True)
    return d * lax.rsqrt(var + LN_EPS) * g + b


def _ada_kernel(c_ref, w_ref, b_ref, o_ref):
    sc = _silu(c_ref[...])
    o_ref[...] = jnp.sum(w_ref[...] * sc, axis=0, keepdims=True) + b_ref[...]


def _ada(c_col, w, b):
    d, n = w.shape
    tn = _tile(n, 1024)
    return pl.pallas_call(
        _ada_kernel,
        grid=(n // tn,),
        in_specs=[pl.BlockSpec((d, 1), lambda j: (0, 0)),
                  pl.BlockSpec((d, tn), lambda j: (0, j)),
                  pl.BlockSpec((1, tn), lambda j: (0, j))],
        out_specs=pl.BlockSpec((1, tn), lambda j: (0, j)),
        out_shape=jax.ShapeDtypeStruct((1, n), F32),
        compiler_params=_params("arbitrary"),
        name="ada",
    )(c_col, w, b)


def _ffn_kernel(x_ref, shift_ref, scale_ref, gate_ref, lng_ref, lnb_ref, wg_ref, wu_ref, wo_ref,
                o_ref, h_sc, a_sc, *, alpha, gate_mul):
    j = pl.program_id(1)
    n_chunks = pl.num_programs(1) - 1

    def drain(slot):
        return _dot(a_sc[slot], wo_ref[...].astype(BF16))

    def activations(slot):
        h = h_sc[...]
        g = _dot(h, wg_ref[...].astype(BF16))
        u = _dot(h, wu_ref[...].astype(BF16))
        a_sc[slot] = (_silu(g) * u).astype(BF16)

    @pl.when(j == 0)
    def _():
        h_sc[...] = (x_ref[...] * (1.0 + scale_ref[...]) + shift_ref[...]).astype(BF16)
        o_ref[...] = jnp.zeros_like(o_ref)
        activations(0)

    for slot in range(2):
        @pl.when(jnp.logical_and(jnp.logical_and(j > 0, j < n_chunks), (j & 1) == slot))
        def _():
            o_ref[...] += drain(1 - slot)
            activations(slot)

        @pl.when(jnp.logical_and(j == n_chunks, ((j - 1) & 1) == slot))
        def _():
            r = alpha * x_ref[...] + (gate_mul * gate_ref[...]) * (o_ref[...] + drain(slot))
            o_ref[...] = _layer_norm(r, lng_ref[...], lnb_ref[...])


def _ffn(x, shift, scale, gate, lng, lnb, w_in, w_out, *, layer, which, alpha, tm=1024, tf=256):
    s, d = x.shape
    f = w_out.shape[2]
    tm, tf = _tile(s, tm), _tile(f, tf)
    nf = f // tf
    vec = pl.BlockSpec((1, d), lambda i, j: (0, 0))
    up = lambda j: jnp.minimum(j, nf - 1)
    down = lambda j: jnp.maximum(j - 1, 0)
    return pl.pallas_call(
        functools.partial(_ffn_kernel, alpha=alpha, gate_mul=0.5),
        grid=(s // tm, nf + 1),
        in_specs=[pl.BlockSpec((tm, d), lambda i, j: (i, 0)), vec, vec, vec, vec, vec,
                  pl.BlockSpec((None, None, d, tf), lambda i, j: (layer, which, 0, up(j))),
                  pl.BlockSpec((None, None, d, tf), lambda i, j: (layer, which, 0, up(j) + nf)),
                  pl.BlockSpec((None, None, tf, d), lambda i, j: (layer, which, down(j), 0))],
        out_specs=pl.BlockSpec((tm, d), lambda i, j: (i, 0)),
        out_shape=jax.ShapeDtypeStruct((s, d), F32),
        scratch_shapes=[pltpu.VMEM((tm, d), BF16), pltpu.VMEM((2, tm, tf), BF16)],
        compiler_params=_params("parallel", "arbitrary"),
        name="ffn",
    )(x, shift, scale, gate, lng, lnb, w_in, w_in, w_out)


def _proj_kernel(h_ref, w_ref, ca_ref, cb_ref, o_ref, ca16_ref, cb16_ref, w_sc):
    @pl.when(pl.program_id(1) == 0)
    def _():
        w_sc[...] = w_ref[...].astype(BF16)

    o_ref[...] = _dot_nt(h_ref[...], w_sc[...]).astype(BF16)
    ca16_ref[...] = ca_ref[...].astype(BF16)
    cb16_ref[...] = cb_ref[...].astype(BF16)


def _cast_slabs(arr, lead_idx, n_steps, step_of):
    rows, cols = arr.shape[-2:]
    r = 16
    while rows % r or rows // r > n_steps:
        r += 16
    slab = lambda *g: jnp.minimum(step_of(*g), rows // r - 1)
    return (pl.BlockSpec((None,) * len(lead_idx) + (r, cols), lambda *g: (*lead_idx, slab(*g), 0)),
            pl.BlockSpec((r, cols), lambda *g: (slab(*g), 0)), jax.ShapeDtypeStruct((rows, cols), BF16))


def _proj(h, w_all_t, cast_a, cast_b, cast_idx, *, gate_row0, n_gate_cols, n_head_cols, tm=1024, tn=1024):
    s, d = h.shape
    tm = _tile(s, tm)
    tn = math.gcd(_tile(n_gate_cols, tn), _tile(n_head_cols, tn))
    n_gate, n_head = n_gate_cols // tn, n_head_cols // tn
    assert gate_row0 % 8 == 0
    first_row = lambda j: pl.multiple_of(jnp.where(j < n_gate, gate_row0 + j * tn, (j - n_gate) * tn), 8)
    n_rows = s // tm
    n_steps = (n_gate + n_head) * n_rows
    step_of = lambda j, i: j * n_rows + i
    (a_in, a_out, a_shape), (b_in, b_out, b_shape) = (_cast_slabs(arr, cast_idx, n_steps, step_of)
                                                      for arr in (cast_a, cast_b))
    return pl.pallas_call(
        _proj_kernel,
        grid=(n_gate + n_head, n_rows),
        in_specs=[pl.BlockSpec((tm, d), lambda j, i: (i, 0)),
                  pl.BlockSpec((pl.Element(tn), pl.Element(d)), lambda j, i: (first_row(j), 0)), a_in, b_in],
        out_specs=[pl.BlockSpec((tm, tn), lambda j, i: (i, j)), a_out, b_out],
        out_shape=[jax.ShapeDtypeStruct((s, n_gate_cols + n_head_cols), BF16), a_shape, b_shape],
        scratch_shapes=[pltpu.VMEM((tn, d), BF16)],
        compiler_params=_params("arbitrary", "arbitrary"),
        name="proj",
    )(h, w_all_t, cast_a, cast_b)


def _bucket_starts():
    n = np.arange(0, 2 * REL_MAX_DIST)
    max_exact = REL_BUCKETS // 2
    nf = np.maximum(n, max_exact).astype(np.float32)
    large = max_exact + (np.log(nf / np.float32(max_exact)) / np.float32(math.log(REL_MAX_DIST / max_exact))
                         * np.float32(REL_BUCKETS - max_exact)).astype(np.int32)
    bucket = np.where(n < max_exact, n, np.minimum(large, REL_BUCKETS - 1))
    assert np.all(np.diff(bucket) >= 0) and bucket[-1] == REL_BUCKETS - 1
    return [int(np.min(n[bucket >= b])) for b in range(REL_BUCKETS)]


BUCKET_STARTS = _bucket_starts()


def _bias_kernel(rb_ref, o_ref, *, t):
    h = pl.program_id(0)
    jj = lax.broadcasted_iota(jnp.int32, (t, 2 * t), 0)
    ii = lax.broadcasted_iota(jnp.int32, (t, 2 * t), 1)
    ii = jnp.where(ii >= t, ii - t, ii)
    far = rb_ref[REL_BUCKETS - 1, h]
    for n in range(2):
        rel = ii - jj + n * t
        bias = jnp.full((t, 2 * t), (rb_ref[0, h] - far) * LOG2E, F32)
        for b in range(1, REL_BUCKETS):
            bias = jnp.where(rel >= BUCKET_STARTS[b], (rb_ref[b, h] - far) * LOG2E, bias)
        o_ref[0, n] = jnp.where(rel < 0, MASK_VALUE, bias)


def _bias_tiles(rel_bias, t):
    return pl.pallas_call(
        functools.partial(_bias_kernel, t=t),
        grid=(HEADS,),
        in_specs=[pl.BlockSpec(memory_space=pltpu.SMEM)],
        out_specs=pl.BlockSpec((1, 2, t, 2 * t), lambda h: (h, 0, 0, 0)),
        out_shape=jax.ShapeDtypeStruct((HEADS, 2, t, 2 * t), F32),
        compiler_params=_params("arbitrary"),
        name="bias_tiles",
    )(rel_bias)


def _attn_kernel(q_ref, k_ref, v_ref, bias_ref, lam_ref, sub_ref, o_ref,
                 vt_sc, qt_sc, s_sc, m_sc, acc_sc, *, t, hp, lam_init):
    qi = pl.program_id(1)
    n_kv = vt_sc.shape[1]
    heads = range(hp)
    cols = [slice(hh * HEAD_W, (hh + 1) * HEAD_W) for hh in heads]

    @pl.when(qi == 0)
    def _():
        ones_tile = (lax.broadcasted_iota(jnp.int32, (V_ROWS - HEAD_W, t), 0) == 0).astype(BF16)

        def body(c, carry):
            r0 = pl.multiple_of(c * t, t)
            for hh in heads:
                vt_sc[hh, c, 0:HEAD_W] = v_ref[pl.ds(r0, t), cols[hh]].astype(F32).T.astype(BF16)
                vt_sc[hh, c, HEAD_W:V_ROWS] = ones_tile
            return carry
        lax.fori_loop(0, n_kv, body, 0)

    row = lax.broadcasted_iota(jnp.int32, (HEAD_W, t), 0)
    for hh in heads:
        qt = (q_ref[:, cols[hh]].astype(F32) * (DIFF_QK_DIM ** -0.5 * LOG2E)).T
        zero = jnp.zeros_like(qt)
        qt_sc[hh] = jnp.concatenate([jnp.where(row < DIFF_QK_DIM, qt, zero),
                                     jnp.where(row >= DIFF_QK_DIM, qt, zero)], axis=1).astype(BF16)

    m_sc[...] = jnp.full_like(m_sc, -jnp.inf)
    acc_sc[...] = jnp.zeros_like(acc_sc)

    def scores(c, slot, biased, which=heads):
        r0 = pl.multiple_of(c * t, t)
        for hh in which:
            s = _dot(k_ref[pl.ds(r0, t), cols[hh]], qt_sc[hh])
            if biased:
                s = s + bias_ref[hh, qi - c]
            s_sc[slot, hh] = s

    def consume(c, slot, which=heads):
        for hh in which:
            s = s_sc[slot, hh]
            m_old = m_sc[hh]
            m_new = jnp.maximum(m_old, jnp.max(s, axis=0, keepdims=True))
            p = jnp.exp2(s - m_new).astype(BF16)
            acc_sc[hh] = jnp.exp2(m_old - m_new) * acc_sc[hh] + _dot(vt_sc[hh, c], p)
            m_sc[hh] = m_new

    pl.when(qi >= 2)(lambda: scores(0, 0, False))
    pl.when(qi < 2)(lambda: scores(0, 0, True))

    n_pairs = jnp.maximum(qi - 2, 0) >> 1
    n_quads = n_pairs >> 1

    def run(c, n):
        scores(c + 1, 1, False)
        for k in range(n - 1):
            for hh in heads:
                consume(c + k, k & 1, [hh])
                scores(c + k + 2, k & 1, False, [hh])
        consume(c + n - 1, 1)

    def quad_body(j, carry):
        run(4 * j, 4)
        return carry
    lax.fori_loop(0, n_quads, quad_body, 0)

    def pair_body(j, carry):
        run(2 * j, 2)
        return carry
    lax.fori_loop(2 * n_quads, n_pairs, pair_body, 0)

    @pl.when(jnp.logical_and(qi >= 2, (qi & 1) == 1))
    def _():
        scores(qi - 2, 1, False)
        consume(qi - 3, 0)

    for slot in range(2):
        @pl.when(jnp.logical_and(qi >= 2, (qi & 1) == slot))
        def _():
            scores(qi - 1, 1 - slot, True)
            for hh in heads:
                consume(qi - 2, slot, [hh])
                scores(qi, slot, True, [hh])
            consume(qi - 1, 1 - slot)
            consume(qi, slot)

    @pl.when(qi == 1)
    def _():
        scores(1, 1, True)
        consume(0, 0)
        consume(1, 1)

    pl.when(qi == 0)(lambda: consume(0, 0))

    lp = lam_ref[...]
    lam = (jnp.exp(jnp.sum(lp[0:1] * lp[1:2], axis=1, keepdims=True))
           - jnp.exp(jnp.sum(lp[2:3] * lp[3:4], axis=1, keepdims=True)) + lam_init)
    for hh in heads:
        acc = acc_sc[hh]
        o = acc[0:HEAD_W] / acc[HEAD_W:HEAD_W + 1]
        od = o[:, :t] - lam * o[:, t:]
        ms = jnp.mean(od * od, axis=0, keepdims=True)
        y = od * lax.rsqrt(ms + LN_EPS) * sub_ref[...] * (1.0 - lam_init)
        o_ref[:, cols[hh]] = y.T.astype(BF16)


def _attn(proj, bias_tiles, lam_params, subln_col, *, q_blk, k_blk, v_blk, lam_init, t, hp=4):
    s = proj.shape[0]
    w = hp * HEAD_W
    assert q_blk % hp == 0 and k_blk % hp == 0 and v_blk % hp == 0
    once = pl.Buffered(1)
    return pl.pallas_call(
        functools.partial(_attn_kernel, t=t, hp=hp, lam_init=lam_init),
        grid=(HEADS // hp, s // t),
        in_specs=[pl.BlockSpec((t, w), lambda g, i: (i, q_blk // hp + g)),
                  pl.BlockSpec((s, w), lambda g, i: (0, k_blk // hp + g), pipeline_mode=once),
                  pl.BlockSpec((s, w), lambda g, i: (0, v_blk // hp + g), pipeline_mode=once),
                  pl.BlockSpec((hp, 2, t, 2 * t), lambda g, i: (g, 0, 0, 0), pipeline_mode=once),
                  pl.BlockSpec(lam_params.shape, lambda g, i: (0, 0)),
                  pl.BlockSpec((HEAD_W, 1), lambda g, i: (0, 0))],
        out_specs=pl.BlockSpec((t, w), lambda g, i: (i, g)),
        out_shape=jax.ShapeDtypeStruct((s, HEADS * HEAD_W), BF16),
        scratch_shapes=[pltpu.VMEM((hp, s // t, V_ROWS, t), BF16),
                        pltpu.VMEM((hp, HEAD_W, 2 * t), BF16),
                        pltpu.VMEM((2, hp, t, 2 * t), F32),
                        pltpu.VMEM((hp, 1, 2 * t), F32),
                        pltpu.VMEM((hp, V_ROWS, 2 * t), F32)],
        compiler_params=_params("arbitrary", "arbitrary"),
        name="diff_attn",
    )(proj, proj, proj, bias_tiles, lam_params, subln_col)


def _gates_kernel(x_ref, shift_ref, scale_ref, ws_ref, alog_ref, dtb_ref, o_ref, h_ref):
    h = (x_ref[...] * (1.0 + scale_ref[...]) + shift_ref[...]).astype(BF16)
    h_ref[...] = h
    tr = _dot_nt(ws_ref[...], h.astype(F32))
    beta = jax.nn.sigmoid(tr[0:HEADS])
    x = tr[HEADS:2 * HEADS] + dtb_ref[...]
    softplus = jnp.maximum(x, 0.0) + jnp.log1p(jnp.exp(-jnp.abs(x)))
    g = -jnp.exp(alog_ref[...]) * softplus
    pos = lax.broadcasted_iota(jnp.int32, g.shape, 1) % DN_CHUNK
    shift = 1
    while shift < DN_CHUNK:
        g = g + jnp.where(pos >= shift, pltpu.roll(g, shift, axis=1), 0.0)
        shift *= 2
    o_ref[0:HEADS] = beta
    o_ref[HEADS:2 * HEADS] = g


def _gates(x, shift, scale, w_small_t, alog_col, dtb_col, *, tb=1024):
    s, d = x.shape
    col = pl.BlockSpec((HEADS, 1), lambda i: (0, 0))
    vec = pl.BlockSpec((1, d), lambda i: (0, 0))
    return pl.pallas_call(
        _gates_kernel,
        grid=(s // tb,),
        in_specs=[pl.BlockSpec((tb, d), lambda i: (i, 0)), vec, vec,
                  pl.BlockSpec(w_small_t.shape, lambda i: (0, 0)), col, col],
        out_specs=[pl.BlockSpec((2 * HEADS, tb), lambda i: (0, i)), pl.BlockSpec((tb, d), lambda i: (i, 0))],
        out_shape=[jax.ShapeDtypeStruct((2 * HEADS, s), F32), jax.ShapeDtypeStruct((s, d), BF16)],
        compiler_params=_params("parallel"),
        name="gdn_gates",
    )(x, shift, scale, w_small_t, alog_col, dtb_col)


GROUP = 2 * DN_CHUNK
INV_BASE = 16


def _gdn_kernel(q_ref, k_ref, v_ref, z_ref, gt_ref, cwq_ref, cwk_ref, cwv_ref, nw_ref, *rest, tb, hp, n_cast):
    cast_in, (o_ref, *cast_out), (pad_sc, state_sc, o_sc) = rest[:n_cast], rest[n_cast:2 * n_cast + 1], rest[-3:]
    for src, dst in zip(cast_in, cast_out):
        dst[...] = src[...].astype(BF16)
    g = pl.program_id(0)
    ib = pl.program_id(1)
    halo = 8
    heads = range(hp)
    cols = [slice(hh * HEAD_W, (hh + 1) * HEAD_W) for hh in heads]

    @pl.when(ib == 0)
    def _():
        state_sc[...] = jnp.zeros_like(state_sc)
        pad_sc[:, 0:halo, :] = jnp.zeros((3 * hp, halo, HEAD_W), F32)

    def conv_silu(a, x_ref, cw_ref, hh):
        a = a * hp + hh
        pad_sc[a, halo:halo + tb, :] = x_ref[:, cols[hh]].astype(F32)
        cw = cw_ref[:, cols[hh]]
        y = cw[0:1] * pad_sc[a, halo - 3:halo - 3 + tb, :]
        for j in range(1, DN_CONV):
            y = y + cw[j:j + 1] * pad_sc[a, halo - 3 + j:halo - 3 + j + tb, :]
        pad_sc[a, 0:halo, :] = pad_sc[a, tb:tb + halo, :]
        return _silu(y)

    def l2n(x):
        return x * lax.rsqrt(jnp.sum(x * x, axis=-1, keepdims=True) + RMS_EPS)

    ii = lax.broadcasted_iota(jnp.int32, (GROUP, GROUP), 0)
    jj = lax.broadcasted_iota(jnp.int32, (GROUP, GROUP), 1)
    same = (ii >= DN_CHUNK) == (jj >= DN_CHUNK)
    tril = jnp.logical_and(same, ii >= jj)
    eye = (ii == jj).astype(F32)
    bits = INV_BASE.bit_length() - 1
    in_base = jnp.logical_and(ii >> bits == jj >> bits, ii > jj)
    below = []
    while (1 << bits) < DN_CHUNK:
        below.append(jnp.logical_and(ii >> (bits + 1) == jj >> (bits + 1), ii >> bits == (jj >> bits) + 1))
        bits += 1

    q, k, v, beta_c, cum_c, cum_t = [], [], [], [], [], []
    for hh in heads:
        q.append(l2n(conv_silu(0, q_ref, cwq_ref, hh)) * (HEAD_W ** -0.5))
        k.append(l2n(conv_silu(1, k_ref, cwk_ref, hh)))
        v.append(conv_silu(2, v_ref, cwv_ref, hh))
        head = g * hp + hh
        beta_t = jnp.broadcast_to(gt_ref[pl.ds(head, 1), :], (HEAD_W, tb))
        ct = jnp.broadcast_to(gt_ref[pl.ds(HEADS + head, 1), :], (HEAD_W, tb))
        beta_c.append(beta_t.T)
        cum_c.append(ct.T)
        cum_t.append(ct)

    groups = range(tb // GROUP)
    rows = [slice(r * GROUP, (r + 1) * GROUP) for r in groups]
    u, w, qk, qd, kg, cg = {}, {}, {}, {}, {}, {}

    def intra_chunk(r):
        kb, x, tinv, decay, merges = {}, {}, {}, {}, {}
        for hh in heads:
            sid = (r, hh)
            kg[sid], cg[sid] = k[hh][rows[r]], cum_c[hh][rows[r]]
            gdiff = cg[sid] - cum_t[hh][:, rows[r]]
            decay[hh] = jnp.where(tril, jnp.exp(jnp.where(tril, gdiff, 0.0)), 0.0)
            kb[hh] = kg[sid] * beta_c[hh][rows[r]]
            a = _dot_nt(kb[hh].astype(BF16), kg[sid].astype(BF16)) * decay[hh]
            diag = jnp.where(in_base, a, 0.0)
            x[hh] = diag.astype(BF16)
            tinv[hh] = eye - diag
            merges[hh] = [jnp.where(m, a, 0.0).astype(BF16) for m in below]
        yield
        power = 2
        while power < INV_BASE:
            for hh in heads:
                x[hh] = _dot(x[hh], x[hh]).astype(BF16)
            yield
            for hh in heads:
                tinv[hh] = tinv[hh] + _dot(tinv[hh].astype(BF16), x[hh])
            yield
            power *= 2
        for level in range(len(below)):
            half = {}
            for hh in heads:
                half[hh] = _dot(tinv[hh].astype(BF16), merges[hh][level]).astype(BF16)
            yield
            for hh in heads:
                tinv[hh] = tinv[hh] - _dot(half[hh], tinv[hh].astype(BF16))
            yield
        for hh in heads:
            sid = (r, hh)
            eg = jnp.exp(cg[sid])
            rhs = jnp.concatenate([v[hh][rows[r]] * beta_c[hh][rows[r]], kb[hh] * eg], axis=1).astype(BF16)
            uw = _dot(tinv[hh].astype(BF16), rhs)
            u[sid], w[sid] = uw[:, :HEAD_W], uw[:, HEAD_W:].astype(BF16)
            qg = q[hh][rows[r]]
            qk[sid] = (_dot_nt(qg.astype(BF16), kg[sid].astype(BF16)) * decay[hh]).astype(BF16)
            qd[sid] = (qg * eg).astype(BF16)
        yield

    def recurrence(r):
        v_new = {hh: [] for hh in heads}
        o_inter = {hh: [] for hh in heads}
        for c in range(2):
            cr = slice(c * DN_CHUNK, (c + 1) * DN_CHUNK)
            last = c * DN_CHUNK + DN_CHUNK - 1
            st, g_last = {}, {}
            for hh in heads:
                sid = (r, hh)
                g_last[hh] = cg[sid][last:last + 1, :]
                st[hh] = state_sc[hh]
                st16 = st[hh].astype(BF16)
                v_new[hh].append(u[sid][cr] - _dot(w[sid][cr], st16))
                o_inter[hh].append(_dot(qd[sid][cr], st16))
            yield
            for hh in heads:
                sid = (r, hh)
                kd = (kg[sid][cr] * jnp.exp(g_last[hh] - cg[sid][cr])).astype(BF16)
                state_sc[hh] = st[hh] * jnp.exp(g_last[hh]) + _dot_tn(kd, v_new[hh][c].astype(BF16))
            yield
        for hh in heads:
            vn_all = jnp.concatenate(v_new[hh], axis=0).astype(BF16)
            o_sc[hh, rows[r], :] = jnp.concatenate(o_inter[hh], axis=0) + _dot(qk[(r, hh)], vn_all)
        yield

    def interleave(slow, fast, ratio):
        slow_live = fast_live = True
        while slow_live or fast_live:
            if slow_live:
                slow_live = next(slow, "done") != "done"
            for _ in range(ratio):
                if fast_live:
                    fast_live = next(fast, "done") != "done"

    for _ in intra_chunk(0):
        pass
    for r in groups:
        nxt = intra_chunk(r + 1) if r + 1 < len(groups) else iter(())
        interleave(recurrence(r), nxt, 3)

    for hh in heads:
        o = o_sc[hh]
        on = o * lax.rsqrt(jnp.mean(o * o, axis=-1, keepdims=True) + RMS_EPS) * nw_ref[...]
        o_ref[:, cols[hh]] = (on * _silu(z_ref[:, cols[hh]].astype(F32))).astype(BF16)


def _gdn(proj, gates_t, conv_w, norm_w, casts, cast_idx, *, q_blk, k_blk, v_blk, z_blk, tb=512, hp=8):
    s = proj.shape[0]
    w = hp * HEAD_W
    assert all(blk % hp == 0 for blk in (q_blk, k_blk, v_blk, z_blk))
    col = lambda blk: pl.BlockSpec((tb, w), lambda g, i: (i, blk // hp + g))
    cw = lambda blk: pl.BlockSpec((DN_CONV, w), lambda g, i: (0, blk // hp + g))
    n_t = s // tb
    slabs = [_cast_slabs(arr, cast_idx, (HEADS // hp) * n_t, lambda g, i: g * n_t + i) for arr in casts]
    return pl.pallas_call(
        functools.partial(_gdn_kernel, tb=tb, hp=hp, n_cast=len(casts)),
        grid=(HEADS // hp, n_t),
        in_specs=[col(q_blk), col(k_blk), col(v_blk), col(z_blk),
                  pl.BlockSpec((2 * HEADS, tb), lambda g, i: (0, i)),
                  cw(0), cw(HEADS), cw(2 * HEADS),
                  pl.BlockSpec((1, HEAD_W), lambda g, i: (0, 0))] + [sl[0] for sl in slabs],
        out_specs=[pl.BlockSpec((tb, w), lambda g, i: (i, g))] + [sl[1] for sl in slabs],
        out_shape=[jax.ShapeDtypeStruct((s, HEADS * HEAD_W), BF16)] + [sl[2] for sl in slabs],
        scratch_shapes=[pltpu.VMEM((3 * hp, tb + 8, HEAD_W), F32),
                        pltpu.VMEM((hp, HEAD_W, HEAD_W), F32),
                        pltpu.VMEM((hp, tb, HEAD_W), F32)],
        compiler_params=_params("arbitrary", "arbitrary"),
        name="gdn",
    )(proj, proj, proj, proj, gates_t, conv_w, conv_w, conv_w, norm_w, *casts)


def _merge_kernel(ya_ref, yb_ref, ga_ref, gb_ref, x_ref, gate_ref, lng_ref, lnb_ref,
                  wa_ref, wb_ref, wo_ref, o_ref, m_sc, *, alpha):
    i = pl.program_id(0)
    n_tiles = pl.num_programs(0) - 1

    def merge(slot):
        a = _dot(ya_ref[...], wa_ref[...])
        b = _dot(yb_ref[...], wb_ref[...])
        m_sc[slot] = (jax.nn.sigmoid(ga_ref[...].astype(F32)) * a
                      + jax.nn.sigmoid(gb_ref[...].astype(F32)) * b).astype(BF16)

    def project(slot):
        y = _dot(m_sc[slot], wo_ref[...])
        r = alpha * x_ref[...] + gate_ref[...] * y
        o_ref[...] = _layer_norm(r, lng_ref[...], lnb_ref[...])

    pl.when(i == 0)(lambda: merge(0))
    for slot in range(2):
        @pl.when(jnp.logical_and(jnp.logical_and(i > 0, i < n_tiles), (i & 1) == slot))
        def _():
            project(1 - slot)
            merge(slot)

        pl.when(jnp.logical_and(i == n_tiles, ((i - 1) & 1) == slot))(functools.partial(project, slot))


def _merge(ya, yb, proj, x, gate, lng, lnb, w_a, w_b, w_o, *, ga_blk, gb_blk, alpha, tm=256):
    s, d = x.shape
    dv = ya.shape[1]
    n = s // tm
    vec = pl.BlockSpec((1, d), lambda i: (0, 0))
    const = lambda shape: pl.BlockSpec(shape, lambda i: (0, 0), pipeline_mode=pl.Buffered(1))
    cur = lambda i: jnp.minimum(i, n - 1)
    prev = lambda i: jnp.maximum(i - 1, 0)
    return pl.pallas_call(
        functools.partial(_merge_kernel, alpha=alpha),
        grid=(n + 1,),
        in_specs=[pl.BlockSpec((tm, dv), lambda i: (cur(i), 0)),
                  pl.BlockSpec((tm, dv), lambda i: (cur(i), 0)),
                  pl.BlockSpec((tm, d), lambda i: (cur(i), ga_blk)),
                  pl.BlockSpec((tm, d), lambda i: (cur(i), gb_blk)),
                  pl.BlockSpec((tm, d), lambda i: (prev(i), 0)), vec, vec, vec,
                  const(w_a.shape), const(w_b.shape), const(w_o.shape)],
        out_specs=pl.BlockSpec((tm, d), lambda i: (prev(i), 0)),
        out_shape=jax.ShapeDtypeStruct((s, d), F32),
        scratch_shapes=[pltpu.VMEM((2, tm, d), BF16)],
        compiler_params=_params("arbitrary"),
        name="merge",
    )(ya, yb, proj, proj, x, gate, lng, lnb, w_a, w_b, w_o)


def kernel(x, c, w_ada, b_ada, ln_g, ln_b, w_ffn_in, w_ffn_out, w_in, conv_w, dn_a_log, dn_dt_bias,
           dn_norm_w, diff_lambda, diff_subln_w, rel_bias, w_branch_a, w_branch_b, w_out):
    bsz, s, d = x.shape
    assert bsz == 1, "one sequence per call"
    depth = w_ada.shape[0]
    alpha = (2 * depth) ** 0.25
    hw = HEADS * HEAD_W
    attn_t = 256
    assert d % HEAD_W == 0 and s % 1024 == 0

    o_nb = 7 * hw
    o_ga = o_nb + 2 * HEADS
    nblk = d // HEAD_W
    blk = {"ga": 0, "gb": 1, "dq": 2 * nblk, "dk": 2 * nblk + HEADS, "dv": 2 * nblk + 2 * HEADS,
           "nq": 2 * nblk + 3 * HEADS, "nk": 2 * nblk + 4 * HEADS, "nv": 2 * nblk + 5 * HEADS,
           "nz": 2 * nblk + 6 * HEADS}

    bias_tiles = _bias_tiles(rel_bias, attn_t)
    wfi, wfo = w_ffn_in, w_ffn_out
    w_in_t = jnp.swapaxes(w_in, 1, 2)
    x2 = x[0]
    for l in range(depth):
        lam_init = 0.8 - 0.6 * math.exp(-0.3 * l)
        ada = _ada(c.reshape(d, 1), w_ada[l], b_ada[l].reshape(1, -1)).reshape(N_SUB, 3, 1, d)
        shift, scale, gate = ada[:, 0], ada[:, 1], ada[:, 2]
        lng, lnb = ln_g[l].reshape(N_SUB, 1, d), ln_b[l].reshape(N_SUB, 1, d)
        w_small_t = w_in_t[l, o_nb:o_ga]

        x2 = _ffn(x2, shift[0], scale[0], gate[0], lng[0], lnb[0], wfi, wfo, layer=l, which=0, alpha=alpha)

        gates_t, h = _gates(x2, shift[1], scale[1], w_small_t, dn_a_log[l].reshape(HEADS, 1),
                            dn_dt_bias[l].reshape(HEADS, 1))
        proj, wfi2, wfo2 = _proj(h, w_in_t[l], w_ffn_in, w_ffn_out, (l, 1),
                                 gate_row0=o_ga, n_gate_cols=2 * d, n_head_cols=o_nb)
        ya = _attn(proj, bias_tiles, diff_lambda[l], diff_subln_w[l].reshape(HEAD_W, 1),
                   q_blk=blk["dq"], k_blk=blk["dk"], v_blk=blk["dv"], lam_init=lam_init, t=attn_t)
        yb, wa16, wb16, wo16 = _gdn(proj, gates_t, conv_w[l], dn_norm_w[l].reshape(1, HEAD_W),
                                    (w_branch_a, w_branch_b, w_out), (l,),
                                    q_blk=blk["nq"], k_blk=blk["nk"], v_blk=blk["nv"], z_blk=blk["nz"])
        x2 = _merge(ya, yb, proj, x2, gate[1], lng[1], lnb[1], wa16, wb16, wo16,
                    ga_blk=blk["ga"], gb_blk=blk["gb"], alpha=alpha)

        x2 = _ffn(x2, shift[2], scale[2], gate[2], lng[2], lnb[2], wfi2[None, None], wfo2[None, None],
                  layer=0, which=0, alpha=alpha, tf=512)
    return x2[None]
```

```python
import functools
import math

import numpy as np
import jax
import jax.numpy as jnp
from jax import lax
from jax.experimental import pallas as pl
from jax.experimental.pallas import tpu as pltpu

N_SUB = 3
HEADS = 8
HEAD_W = 128
DIFF_QK_DIM = 64
DN_CONV = 4
DN_CHUNK = 64
REL_BUCKETS = 32
REL_MAX_DIST = 128
LN_EPS = 1e-5
RMS_EPS = 1e-6
MASK_VALUE = -1e30
LOG2E = math.log2(math.e)
V_ROWS = HEAD_W + 16

F32 = jnp.float32
BF16 = jnp.bfloat16

V7X_VMEM_BYTES = 64 * 1024 * 1024
VMEM_LIMIT = V7X_VMEM_BYTES - 4 * 1024 * 1024


def _params(*sem):
    return pltpu.CompilerParams(dimension_semantics=sem, vmem_limit_bytes=VMEM_LIMIT)


def _tile(n, preferred):
    t = min(preferred, n)
    while n % t:
        t -= 128
    return t


def _silu(x):
    return x * jax.nn.sigmoid(x)


def _dot(a, b):
    return jnp.dot(a, b, preferred_element_type=F32)


def _dot_nt(a, b):
    return lax.dot_general(a, b, (((1,), (1,)), ((), ())), preferred_element_type=F32)


def _dot_tn(a, b):
    return lax.dot_general(a, b, (((0,), (0,)), ((), ())), preferred_element_type=F32)


def _layer_norm(r, g, b):
    mu = jnp.mean(r, axis=-1, keepdims=True)
    d = r - mu
    var = jnp.mean(d * d, axis=-1, keepdims=True)
    return d * lax.rsqrt(var + LN_EPS) * g + b


def _ada_kernel(c_ref, w_ref, b_ref, o_ref):
    sc = _silu(c_ref[...])
    o_ref[...] = jnp.sum(w_ref[...] * sc, axis=0, keepdims=True) + b_ref[...]


def _ada(c_col, w, b):
    d, n = w.shape
    tn = _tile(n, 1024)
    return pl.pallas_call(
        _ada_kernel,
        grid=(n // tn,),
        in_specs=[pl.BlockSpec((d, 1), lambda j: (0, 0)),
                  pl.BlockSpec((d, tn), lambda j: (0, j)),
                  pl.BlockSpec((1, tn), lambda j: (0, j))],
        out_specs=pl.BlockSpec((1, tn), lambda j: (0, j)),
        out_shape=jax.ShapeDtypeStruct((1, n), F32),
        compiler_params=_params("arbitrary"),
        name="ada",
    )(c_col, w, b)


def _ffn_kernel(x_ref, shift_ref, scale_ref, gate_ref, lng_ref, lnb_ref, wg_ref, wu_ref, wo_ref,
                o_ref, h_sc, a_sc, *, alpha, gate_mul):
    j = pl.program_id(1)
    n_chunks = pl.num_programs(1) - 1

    def drain(slot):
        return _dot(a_sc[slot], wo_ref[...].astype(BF16))

    def activations(slot):
        h = h_sc[...]
        g = _dot(h, wg_ref[...].astype(BF16))
        u = _dot(h, wu_ref[...].astype(BF16))
        a_sc[slot] = (_silu(g) * u).astype(BF16)

    @pl.when(j == 0)
    def _():
        h_sc[...] = (x_ref[...] * (1.0 + scale_ref[...]) + shift_ref[...]).astype(BF16)
        o_ref[...] = jnp.zeros_like(o_ref)
        activations(0)

    for slot in range(2):
        @pl.when(jnp.logical_and(jnp.logical_and(j > 0, j < n_chunks), (j & 1) == slot))
        def _():
            o_ref[...] += drain(1 - slot)
            activations(slot)

        @pl.when(jnp.logical_and(j == n_chunks, ((j - 1) & 1) == slot))
        def _():
            r = alpha * x_ref[...] + (gate_mul * gate_ref[...]) * (o_ref[...] + drain(slot))
            o_ref[...] = _layer_norm(r, lng_ref[...], lnb_ref[...])


def _ffn(x, shift, scale, gate, lng, lnb, w_in, w_out, *, layer, which, alpha, tm=1024, tf=256):
    s, d = x.shape
    f = w_out.shape[2]
    tm, tf = _tile(s, tm), _tile(f, tf)
    nf = f // tf
    vec = pl.BlockSpec((1, d), lambda i, j: (0, 0))
    up = lambda j: jnp.minimum(j, nf - 1)
    down = lambda j: jnp.maximum(j - 1, 0)
    return pl.pallas_call(
        functools.partial(_ffn_kernel, alpha=alpha, gate_mul=0.5),
        grid=(s // tm, nf + 1),
        in_specs=[pl.BlockSpec((tm, d), lambda i, j: (i, 0)), vec, vec, vec, vec, vec,
                  pl.BlockSpec((None, None, d, tf), lambda i, j: (layer, which, 0, up(j))),
                  pl.BlockSpec((None, None, d, tf), lambda i, j: (layer, which, 0, up(j) + nf)),
                  pl.BlockSpec((None, None, tf, d), lambda i, j: (layer, which, down(j), 0))],
        out_specs=pl.BlockSpec((tm, d), lambda i, j: (i, 0)),
        out_shape=jax.ShapeDtypeStruct((s, d), F32),
        scratch_shapes=[pltpu.VMEM((tm, d), BF16), pltpu.VMEM((2, tm, tf), BF16)],
        compiler_params=_params("parallel", "arbitrary"),
        name="ffn",
    )(x, shift, scale, gate, lng, lnb, w_in, w_in, w_out)


def _proj_kernel(h_ref, w_ref, ca_ref, cb_ref, o_ref, ca16_ref, cb16_ref, w_sc):
    @pl.when(pl.program_id(1) == 0)
    def _():
        w_sc[...] = w_ref[...].astype(BF16)

    o_ref[...] = _dot_nt(h_ref[...], w_sc[...]).astype(BF16)
    ca16_ref[...] = ca_ref[...].astype(BF16)
    cb16_ref[...] = cb_ref[...].astype(BF16)


def _cast_slabs(arr, lead_idx, n_steps, step_of):
    rows, cols = arr.shape[-2:]
    r = 16
    while rows % r or rows // r > n_steps:
        r += 16
    slab = lambda *g: jnp.minimum(step_of(*g), rows // r - 1)
    return (pl.BlockSpec((None,) * len(lead_idx) + (r, cols), lambda *g: (*lead_idx, slab(*g), 0)),
            pl.BlockSpec((r, cols), lambda *g: (slab(*g), 0)), jax.ShapeDtypeStruct((rows, cols), BF16))


def _proj(h, w_all_t, cast_a, cast_b, cast_idx, *, gate_row0, n_gate_cols, n_head_cols, tm=1024, tn=1024):
    s, d = h.shape
    tm = _tile(s, tm)
    tn = math.gcd(_tile(n_gate_cols, tn), _tile(n_head_cols, tn))
    n_gate, n_head = n_gate_cols // tn, n_head_cols // tn
    assert gate_row0 % 8 == 0
    first_row = lambda j: pl.multiple_of(jnp.where(j < n_gate, gate_row0 + j * tn, (j - n_gate) * tn), 8)
    n_rows = s // tm
    n_steps = (n_gate + n_head) * n_rows
    step_of = lambda j, i: j * n_rows + i
    (a_in, a_out, a_shape), (b_in, b_out, b_shape) = (_cast_slabs(arr, cast_idx, n_steps, step_of)
                                                      for arr in (cast_a, cast_b))
    return pl.pallas_call(
        _proj_kernel,
        grid=(n_gate + n_head, n_rows),
        in_specs=[pl.BlockSpec((tm, d), lambda j, i: (i, 0)),
                  pl.BlockSpec((pl.Element(tn), pl.Element(d)), lambda j, i: (first_row(j), 0)), a_in, b_in],
        out_specs=[pl.BlockSpec((tm, tn), lambda j, i: (i, j)), a_out, b_out],
        out_shape=[jax.ShapeDtypeStruct((s, n_gate_cols + n_head_cols), BF16), a_shape, b_shape],
        scratch_shapes=[pltpu.VMEM((tn, d), BF16)],
        compiler_params=_params("arbitrary", "arbitrary"),
        name="proj",
    )(h, w_all_t, cast_a, cast_b)


def _bucket_starts():
    n = np.arange(0, 2 * REL_MAX_DIST)
    max_exact = REL_BUCKETS // 2
    nf = np.maximum(n, max_exact).astype(np.float32)
    large = max_exact + (np.log(nf / np.float32(max_exact)) / np.float32(math.log(REL_MAX_DIST / max_exact))
                         * np.float32(REL_BUCKETS - max_exact)).astype(np.int32)
    bucket = np.where(n < max_exact, n, np.minimum(large, REL_BUCKETS - 1))
    assert np.all(np.diff(bucket) >= 0) and bucket[-1] == REL_BUCKETS - 1
    return [int(np.min(n[bucket >= b])) for b in range(REL_BUCKETS)]


BUCKET_STARTS = _bucket_starts()


def _bias_kernel(rb_ref, o_ref, *, t):
    h = pl.program_id(0)
    jj = lax.broadcasted_iota(jnp.int32, (t, 2 * t), 0)
    ii = lax.broadcasted_iota(jnp.int32, (t, 2 * t), 1)
    ii = jnp.where(ii >= t, ii - t, ii)
    far = rb_ref[REL_BUCKETS - 1, h]
    for n in range(2):
        rel = ii - jj + n * t
        bias = jnp.full((t, 2 * t), (rb_ref[0, h] - far) * LOG2E, F32)
        for b in range(1, REL_BUCKETS):
            bias = jnp.where(rel >= BUCKET_STARTS[b], (rb_ref[b, h] - far) * LOG2E, bias)
        o_ref[0, n] = jnp.where(rel < 0, MASK_VALUE, bias)


def _bias_tiles(rel_bias, t):
    return pl.pallas_call(
        functools.partial(_bias_kernel, t=t),
        grid=(HEADS,),
        in_specs=[pl.BlockSpec(memory_space=pltpu.SMEM)],
        out_specs=pl.BlockSpec((1, 2, t, 2 * t), lambda h: (h, 0, 0, 0)),
        out_shape=jax.ShapeDtypeStruct((HEADS, 2, t, 2 * t), F32),
        compiler_params=_params("arbitrary"),
        name="bias_tiles",
    )(rel_bias)


def _attn_kernel(q_ref, k_ref, v_ref, bias_ref, lam_ref, sub_ref, o_ref,
                 vt_sc, qt_sc, s_sc, m_sc, acc_sc, *, t, hp, lam_init):
    qi = pl.program_id(1)
    n_kv = vt_sc.shape[1]
    heads = range(hp)
    cols = [slice(hh * HEAD_W, (hh + 1) * HEAD_W) for hh in heads]

    @pl.when(qi == 0)
    def _():
        ones_tile = (lax.broadcasted_iota(jnp.int32, (V_ROWS - HEAD_W, t), 0) == 0).astype(BF16)

        def body(c, carry):
            r0 = pl.multiple_of(c * t, t)
            for hh in heads:
                vt_sc[hh, c, 0:HEAD_W] = v_ref[pl.ds(r0, t), cols[hh]].astype(F32).T.astype(BF16)
                vt_sc[hh, c, HEAD_W:V_ROWS] = ones_tile
            return carry
        lax.fori_loop(0, n_kv, body, 0)

    row = lax.broadcasted_iota(jnp.int32, (HEAD_W, t), 0)
    for hh in heads:
        qt = (q_ref[:, cols[hh]].astype(F32) * (DIFF_QK_DIM ** -0.5 * LOG2E)).T
        zero = jnp.zeros_like(qt)
        qt_sc[hh] = jnp.concatenate([jnp.where(row < DIFF_QK_DIM, qt, zero),
                                     jnp.where(row >= DIFF_QK_DIM, qt, zero)], axis=1).astype(BF16)

    m_sc[...] = jnp.full_like(m_sc, -jnp.inf)
    acc_sc[...] = jnp.zeros_like(acc_sc)

    def scores(c, slot, biased, which=heads):
        r0 = pl.multiple_of(c * t, t)
        for hh in which:
            s = _dot(k_ref[pl.ds(r0, t), cols[hh]], qt_sc[hh])
            if biased:
                s = s + bias_ref[hh, qi - c]
            s_sc[slot, hh] = s

    def consume(c, slot, which=heads):
        for hh in which:
            s = s_sc[slot, hh]
            m_old = m_sc[hh]
            m_new = jnp.maximum(m_old, jnp.max(s, axis=0, keepdims=True))
            p = jnp.exp2(s - m_new).astype(BF16)
            acc_sc[hh] = jnp.exp2(m_old - m_new) * acc_sc[hh] + _dot(vt_sc[hh, c], p)
            m_sc[hh] = m_new

    pl.when(qi >= 2)(lambda: scores(0, 0, False))
    pl.when(qi < 2)(lambda: scores(0, 0, True))

    n_pairs = jnp.maximum(qi - 2, 0) >> 1

    def run(c, n):
        scores(c + 1, 1, False)
        for k in range(n - 1):
            for hh in heads:
                consume(c + k, k & 1, [hh])
                scores(c + k + 2, k & 1, False, [hh])
        consume(c + n - 1, 1)

    done = 0
    for shift in (2, 1, 0):
        size = 2 << shift
        count = n_pairs >> shift

        def body(j, carry, size=size):
            run(size * j, size)
            return carry
        lax.fori_loop(done, count, body, 0)
        done = 2 * count

    @pl.when(jnp.logical_and(qi >= 2, (qi & 1) == 1))
    def _():
        scores(qi - 2, 1, False)
        consume(qi - 3, 0)

    for slot in range(2):
        @pl.when(jnp.logical_and(qi >= 2, (qi & 1) == slot))
        def _():
            scores(qi - 1, 1 - slot, True)
            for hh in heads:
                consume(qi - 2, slot, [hh])
                scores(qi, slot, True, [hh])
            consume(qi - 1, 1 - slot)
            consume(qi, slot)

    @pl.when(qi == 1)
    def _():
        scores(1, 1, True)
        consume(0, 0)
        consume(1, 1)

    pl.when(qi == 0)(lambda: consume(0, 0))

    lp = lam_ref[...]
    lam = (jnp.exp(jnp.sum(lp[0:1] * lp[1:2], axis=1, keepdims=True))
           - jnp.exp(jnp.sum(lp[2:3] * lp[3:4], axis=1, keepdims=True)) + lam_init)
    for hh in heads:
        acc = acc_sc[hh]
        o = acc[0:HEAD_W] / acc[HEAD_W:HEAD_W + 1]
        od = o[:, :t] - lam * o[:, t:]
        ms = jnp.mean(od * od, axis=0, keepdims=True)
        y = od * lax.rsqrt(ms + LN_EPS) * sub_ref[...] * (1.0 - lam_init)
        o_ref[:, cols[hh]] = y.T.astype(BF16)


def _attn(proj, bias_tiles, lam_params, subln_col, *, q_blk, k_blk, v_blk, lam_init, t, hp=4):
    s = proj.shape[0]
    w = hp * HEAD_W
    assert q_blk % hp == 0 and k_blk % hp == 0 and v_blk % hp == 0
    once = pl.Buffered(1)
    return pl.pallas_call(
        functools.partial(_attn_kernel, t=t, hp=hp, lam_init=lam_init),
        grid=(HEADS // hp, s // t),
        in_specs=[pl.BlockSpec((t, w), lambda g, i: (i, q_blk // hp + g)),
                  pl.BlockSpec((s, w), lambda g, i: (0, k_blk // hp + g), pipeline_mode=once),
                  pl.BlockSpec((s, w), lambda g, i: (0, v_blk // hp + g), pipeline_mode=once),
                  pl.BlockSpec((hp, 2, t, 2 * t), lambda g, i: (g, 0, 0, 0), pipeline_mode=once),
                  pl.BlockSpec(lam_params.shape, lambda g, i: (0, 0)),
                  pl.BlockSpec((HEAD_W, 1), lambda g, i: (0, 0))],
        out_specs=pl.BlockSpec((t, w), lambda g, i: (i, g)),
        out_shape=jax.ShapeDtypeStruct((s, HEADS * HEAD_W), BF16),
        scratch_shapes=[pltpu.VMEM((hp, s // t, V_ROWS, t), BF16),
                        pltpu.VMEM((hp, HEAD_W, 2 * t), BF16),
                        pltpu.VMEM((2, hp, t, 2 * t), F32),
                        pltpu.VMEM((hp, 1, 2 * t), F32),
                        pltpu.VMEM((hp, V_ROWS, 2 * t), F32)],
        compiler_params=_params("arbitrary", "arbitrary"),
        name="diff_attn",
    )(proj, proj, proj, bias_tiles, lam_params, subln_col)


def _gates_kernel(x_ref, shift_ref, scale_ref, ws_ref, alog_ref, dtb_ref, o_ref, h_ref):
    h = (x_ref[...] * (1.0 + scale_ref[...]) + shift_ref[...]).astype(BF16)
    h_ref[...] = h
    tr = _dot_nt(ws_ref[...], h.astype(F32))
    beta = jax.nn.sigmoid(tr[0:HEADS])
    x = tr[HEADS:2 * HEADS] + dtb_ref[...]
    softplus = jnp.maximum(x, 0.0) + jnp.log1p(jnp.exp(-jnp.abs(x)))
    g = -jnp.exp(alog_ref[...]) * softplus
    pos = lax.broadcasted_iota(jnp.int32, g.shape, 1) % DN_CHUNK
    shift = 1
    while shift < DN_CHUNK:
        g = g + jnp.where(pos >= shift, pltpu.roll(g, shift, axis=1), 0.0)
        shift *= 2
    o_ref[0:HEADS] = beta
    o_ref[HEADS:2 * HEADS] = g


def _gates(x, shift, scale, w_small_t, alog_col, dtb_col, *, tb=1024):
    s, d = x.shape
    col = pl.BlockSpec((HEADS, 1), lambda i: (0, 0))
    vec = pl.BlockSpec((1, d), lambda i: (0, 0))
    return pl.pallas_call(
        _gates_kernel,
        grid=(s // tb,),
        in_specs=[pl.BlockSpec((tb, d), lambda i: (i, 0)), vec, vec,
                  pl.BlockSpec(w_small_t.shape, lambda i: (0, 0)), col, col],
        out_specs=[pl.BlockSpec((2 * HEADS, tb), lambda i: (0, i)), pl.BlockSpec((tb, d), lambda i: (i, 0))],
        out_shape=[jax.ShapeDtypeStruct((2 * HEADS, s), F32), jax.ShapeDtypeStruct((s, d), BF16)],
        compiler_params=_params("parallel"),
        name="gdn_gates",
    )(x, shift, scale, w_small_t, alog_col, dtb_col)


GROUP = 2 * DN_CHUNK
INV_BASE = 16


def _gdn_kernel(q_ref, k_ref, v_ref, z_ref, gt_ref, cwq_ref, cwk_ref, cwv_ref, nw_ref, *rest, tb, hp, n_cast):
    cast_in, (o_ref, *cast_out), (pad_sc, state_sc, o_sc) = rest[:n_cast], rest[n_cast:2 * n_cast + 1], rest[-3:]
    for src, dst in zip(cast_in, cast_out):
        dst[...] = src[...].astype(BF16)
    g = pl.program_id(0)
    ib = pl.program_id(1)
    halo = 8
    heads = range(hp)
    cols = [slice(hh * HEAD_W, (hh + 1) * HEAD_W) for hh in heads]

    @pl.when(ib == 0)
    def _():
        state_sc[...] = jnp.zeros_like(state_sc)
        pad_sc[:, 0:halo, :] = jnp.zeros((3 * hp, halo, HEAD_W), F32)

    def conv_silu(a, x_ref, cw_ref, hh):
        a = a * hp + hh
        pad_sc[a, halo:halo + tb, :] = x_ref[:, cols[hh]].astype(F32)
        cw = cw_ref[:, cols[hh]]
        y = cw[0:1] * pad_sc[a, halo - 3:halo - 3 + tb, :]
        for j in range(1, DN_CONV):
            y = y + cw[j:j + 1] * pad_sc[a, halo - 3 + j:halo - 3 + j + tb, :]
        pad_sc[a, 0:halo, :] = pad_sc[a, tb:tb + halo, :]
        return _silu(y)

    def l2n(x):
        return x * lax.rsqrt(jnp.sum(x * x, axis=-1, keepdims=True) + RMS_EPS)

    ii = lax.broadcasted_iota(jnp.int32, (GROUP, GROUP), 0)
    jj = lax.broadcasted_iota(jnp.int32, (GROUP, GROUP), 1)
    same = (ii >= DN_CHUNK) == (jj >= DN_CHUNK)
    tril = jnp.logical_and(same, ii >= jj)
    eye = (ii == jj).astype(F32)
    bits = INV_BASE.bit_length() - 1
    in_base = jnp.logical_and(ii >> bits == jj >> bits, ii > jj)
    below = []
    while (1 << bits) < DN_CHUNK:
        below.append(jnp.logical_and(ii >> (bits + 1) == jj >> (bits + 1), ii >> bits == (jj >> bits) + 1))
        bits += 1

    q, k, v, beta_c, cum_c, cum_t = [], [], [], [], [], []
    for hh in heads:
        q.append(l2n(conv_silu(0, q_ref, cwq_ref, hh)) * (HEAD_W ** -0.5))
        k.append(l2n(conv_silu(1, k_ref, cwk_ref, hh)))
        v.append(conv_silu(2, v_ref, cwv_ref, hh))
        head = g * hp + hh
        beta_t = jnp.broadcast_to(gt_ref[pl.ds(head, 1), :], (HEAD_W, tb))
        ct = jnp.broadcast_to(gt_ref[pl.ds(HEADS + head, 1), :], (HEAD_W, tb))
        beta_c.append(beta_t.T)
        cum_c.append(ct.T)
        cum_t.append(ct)

    groups = range(tb // GROUP)
    rows = [slice(r * GROUP, (r + 1) * GROUP) for r in groups]
    u, w, qk, qd, kg, cg = {}, {}, {}, {}, {}, {}

    def intra_chunk(r):
        kb, x, tinv, decay, merges = {}, {}, {}, {}, {}
        for hh in heads:
            sid = (r, hh)
            kg[sid], cg[sid] = k[hh][rows[r]], cum_c[hh][rows[r]]
            gdiff = cg[sid] - cum_t[hh][:, rows[r]]
            decay[hh] = jnp.where(tril, jnp.exp(jnp.where(tril, gdiff, 0.0)), 0.0)
            kb[hh] = kg[sid] * beta_c[hh][rows[r]]
            a = _dot_nt(kb[hh].astype(BF16), kg[sid].astype(BF16)) * decay[hh]
            diag = jnp.where(in_base, a, 0.0)
            x[hh] = diag.astype(BF16)
            tinv[hh] = eye - diag
            merges[hh] = [jnp.where(m, a, 0.0).astype(BF16) for m in below]
        yield
        power = 2
        while power < INV_BASE:
            for hh in heads:
                x[hh] = _dot(x[hh], x[hh]).astype(BF16)
            yield
            for hh in heads:
                tinv[hh] = tinv[hh] + _dot(tinv[hh].astype(BF16), x[hh])
            yield
            power *= 2
        for level in range(len(below)):
            half = {}
            for hh in heads:
                half[hh] = _dot(tinv[hh].astype(BF16), merges[hh][level]).astype(BF16)
            yield
            for hh in heads:
                tinv[hh] = tinv[hh] - _dot(half[hh], tinv[hh].astype(BF16))
            yield
        for hh in heads:
            sid = (r, hh)
            eg = jnp.exp(cg[sid])
            rhs = jnp.concatenate([v[hh][rows[r]] * beta_c[hh][rows[r]], kb[hh] * eg], axis=1).astype(BF16)
            uw = _dot(tinv[hh].astype(BF16), rhs)
            u[sid], w[sid] = uw[:, :HEAD_W], uw[:, HEAD_W:].astype(BF16)
            qg = q[hh][rows[r]]
            qk[sid] = (_dot_nt(qg.astype(BF16), kg[sid].astype(BF16)) * decay[hh]).astype(BF16)
            qd[sid] = (qg * eg).astype(BF16)
        yield

    def recurrence(r):
        v_new = {hh: [] for hh in heads}
        o_inter = {hh: [] for hh in heads}
        for c in range(2):
            cr = slice(c * DN_CHUNK, (c + 1) * DN_CHUNK)
            last = c * DN_CHUNK + DN_CHUNK - 1
            st, g_last = {}, {}
            for hh in heads:
                sid = (r, hh)
                g_last[hh] = cg[sid][last:last + 1, :]
                st[hh] = state_sc[hh]
                st16 = st[hh].astype(BF16)
                v_new[hh].append(u[sid][cr] - _dot(w[sid][cr], st16))
                o_inter[hh].append(_dot(qd[sid][cr], st16))
            yield
            for hh in heads:
                sid = (r, hh)
                kd = (kg[sid][cr] * jnp.exp(g_last[hh] - cg[sid][cr])).astype(BF16)
                state_sc[hh] = st[hh] * jnp.exp(g_last[hh]) + _dot_tn(kd, v_new[hh][c].astype(BF16))
            yield
        for hh in heads:
            vn_all = jnp.concatenate(v_new[hh], axis=0).astype(BF16)
            o_sc[hh, rows[r], :] = jnp.concatenate(o_inter[hh], axis=0) + _dot(qk[(r, hh)], vn_all)
        yield

    def interleave(slow, fast, ratio):
        slow_live = fast_live = True
        while slow_live or fast_live:
            if slow_live:
                slow_live = next(slow, "done") != "done"
            for _ in range(ratio):
                if fast_live:
                    fast_live = next(fast, "done") != "done"

    for _ in intra_chunk(0):
        pass
    for r in groups:
        nxt = intra_chunk(r + 1) if r + 1 < len(groups) else iter(())
        interleave(recurrence(r), nxt, 3)

    for hh in heads:
        o = o_sc[hh]
        on = o * lax.rsqrt(jnp.mean(o * o, axis=-1, keepdims=True) + RMS_EPS) * nw_ref[...]
        o_ref[:, cols[hh]] = (on * _silu(z_ref[:, cols[hh]].astype(F32))).astype(BF16)


def _gdn(proj, gates_t, conv_w, norm_w, casts, cast_idx, *, q_blk, k_blk, v_blk, z_blk, tb=512, hp=8):
    s = proj.shape[0]
    w = hp * HEAD_W
    assert all(blk % hp == 0 for blk in (q_blk, k_blk, v_blk, z_blk))
    col = lambda blk: pl.BlockSpec((tb, w), lambda g, i: (i, blk // hp + g))
    cw = lambda blk: pl.BlockSpec((DN_CONV, w), lambda g, i: (0, blk // hp + g))
    n_t = s // tb
    slabs = [_cast_slabs(arr, cast_idx, (HEADS // hp) * n_t, lambda g, i: g * n_t + i) for arr in casts]
    return pl.pallas_call(
        functools.partial(_gdn_kernel, tb=tb, hp=hp, n_cast=len(casts)),
        grid=(HEADS // hp, n_t),
        in_specs=[col(q_blk), col(k_blk), col(v_blk), col(z_blk),
                  pl.BlockSpec((2 * HEADS, tb), lambda g, i: (0, i)),
                  cw(0), cw(HEADS), cw(2 * HEADS),
                  pl.BlockSpec((1, HEAD_W), lambda g, i: (0, 0))] + [sl[0] for sl in slabs],
        out_specs=[pl.BlockSpec((tb, w), lambda g, i: (i, g))] + [sl[1] for sl in slabs],
        out_shape=[jax.ShapeDtypeStruct((s, HEADS * HEAD_W), BF16)] + [sl[2] for sl in slabs],
        scratch_shapes=[pltpu.VMEM((3 * hp, tb + 8, HEAD_W), F32),
                        pltpu.VMEM((hp, HEAD_W, HEAD_W), F32),
                        pltpu.VMEM((hp, tb, HEAD_W), F32)],
        compiler_params=_params("arbitrary", "arbitrary"),
        name="gdn",
    )(proj, proj, proj, proj, gates_t, conv_w, conv_w, conv_w, norm_w, *casts)


def _merge_kernel(ya_ref, yb_ref, ga_ref, gb_ref, x_ref, gate_ref, lng_ref, lnb_ref,
                  wa_ref, wb_ref, wo_ref, o_ref, m_sc, *, alpha):
    i = pl.program_id(0)
    n_tiles = pl.num_programs(0) - 1

    def merge(slot):
        a = _dot(ya_ref[...], wa_ref[...])
        b = _dot(yb_ref[...], wb_ref[...])
        m_sc[slot] = (jax.nn.sigmoid(ga_ref[...].astype(F32)) * a
                      + jax.nn.sigmoid(gb_ref[...].astype(F32)) * b).astype(BF16)

    def project(slot):
        y = _dot(m_sc[slot], wo_ref[...])
        r = alpha * x_ref[...] + gate_ref[...] * y
        o_ref[...] = _layer_norm(r, lng_ref[...], lnb_ref[...])

    pl.when(i == 0)(lambda: merge(0))
    for slot in range(2):
        @pl.when(jnp.logical_and(jnp.logical_and(i > 0, i < n_tiles), (i & 1) == slot))
        def _():
            project(1 - slot)
            merge(slot)

        pl.when(jnp.logical_and(i == n_tiles, ((i - 1) & 1) == slot))(functools.partial(project, slot))


def _merge(ya, yb, proj, x, gate, lng, lnb, w_a, w_b, w_o, *, ga_blk, gb_blk, alpha, tm=256):
    s, d = x.shape
    dv = ya.shape[1]
    n = s // tm
    vec = pl.BlockSpec((1, d), lambda i: (0, 0))
    const = lambda shape: pl.BlockSpec(shape, lambda i: (0, 0), pipeline_mode=pl.Buffered(1))
    cur = lambda i: jnp.minimum(i, n - 1)
    prev = lambda i: jnp.maximum(i - 1, 0)
    return pl.pallas_call(
        functools.partial(_merge_kernel, alpha=alpha),
        grid=(n + 1,),
        in_specs=[pl.BlockSpec((tm, dv), lambda i: (cur(i), 0)),
                  pl.BlockSpec((tm, dv), lambda i: (cur(i), 0)),
                  pl.BlockSpec((tm, d), lambda i: (cur(i), ga_blk)),
                  pl.BlockSpec((tm, d), lambda i: (cur(i), gb_blk)),
                  pl.BlockSpec((tm, d), lambda i: (prev(i), 0)), vec, vec, vec,
                  const(w_a.shape), const(w_b.shape), const(w_o.shape)],
        out_specs=pl.BlockSpec((tm, d), lambda i: (prev(i), 0)),
        out_shape=jax.ShapeDtypeStruct((s, d), F32),
        scratch_shapes=[pltpu.VMEM((2, tm, d), BF16)],
        compiler_params=_params("arbitrary"),
        name="merge",
    )(ya, yb, proj, proj, x, gate, lng, lnb, w_a, w_b, w_o)


def kernel(x, c, w_ada, b_ada, ln_g, ln_b, w_ffn_in, w_ffn_out, w_in, conv_w, dn_a_log, dn_dt_bias,
           dn_norm_w, diff_lambda, diff_subln_w, rel_bias, w_branch_a, w_branch_b, w_out):
    bsz, s, d = x.shape
    assert bsz == 1, "one sequence per call"
    depth = w_ada.shape[0]
    alpha = (2 * depth) ** 0.25
    hw = HEADS * HEAD_W
    attn_t = 256
    assert d % HEAD_W == 0 and s % 1024 == 0

    o_nb = 7 * hw
    o_ga = o_nb + 2 * HEADS
    nblk = d // HEAD_W
    blk = {"ga": 0, "gb": 1, "dq": 2 * nblk, "dk": 2 * nblk + HEADS, "dv": 2 * nblk + 2 * HEADS,
           "nq": 2 * nblk + 3 * HEADS, "nk": 2 * nblk + 4 * HEADS, "nv": 2 * nblk + 5 * HEADS,
           "nz": 2 * nblk + 6 * HEADS}

    bias_tiles = _bias_tiles(rel_bias, attn_t)
    wfi, wfo = w_ffn_in, w_ffn_out
    w_in_t = jnp.swapaxes(w_in, 1, 2)
    x2 = x[0]
    for l in range(depth):
        lam_init = 0.8 - 0.6 * math.exp(-0.3 * l)
        ada = _ada(c.reshape(d, 1), w_ada[l], b_ada[l].reshape(1, -1)).reshape(N_SUB, 3, 1, d)
        shift, scale, gate = ada[:, 0], ada[:, 1], ada[:, 2]
        lng, lnb = ln_g[l].reshape(N_SUB, 1, d), ln_b[l].reshape(N_SUB, 1, d)
        w_small_t = w_in_t[l, o_nb:o_ga]

        x2 = _ffn(x2, shift[0], scale[0], gate[0], lng[0], lnb[0], wfi, wfo, layer=l, which=0, alpha=alpha)

        gates_t, h = _gates(x2, shift[1], scale[1], w_small_t, dn_a_log[l].reshape(HEADS, 1),
                            dn_dt_bias[l].reshape(HEADS, 1))
        proj, wfi2, wfo2 = _proj(h, w_in_t[l], w_ffn_in, w_ffn_out, (l, 1),
                                 gate_row0=o_ga, n_gate_cols=2 * d, n_head_cols=o_nb)
        ya = _attn(proj, bias_tiles, diff_lambda[l], diff_subln_w[l].reshape(HEAD_W, 1),
                   q_blk=blk["dq"], k_blk=blk["dk"], v_blk=blk["dv"], lam_init=lam_init, t=attn_t)
        yb, wa16, wb16, wo16 = _gdn(proj, gates_t, conv_w[l], dn_norm_w[l].reshape(1, HEAD_W),
                                    (w_branch_a, w_branch_b, w_out), (l,),
                                    q_blk=blk["nq"], k_blk=blk["nk"], v_blk=blk["nv"], z_blk=blk["nz"])
        x2 = _merge(ya, yb, proj, x2, gate[1], lng[1], lnb[1], wa16, wb16, wo16,
                    ga_blk=blk["ga"], gb_blk=blk["gb"], alpha=alpha)

        x2 = _ffn(x2, shift[2], scale[2], gate[2], lng[2], lnb[2], wfi2[None, None], wfo2[None, None],
                  layer=0, which=0, alpha=alpha, tf=512)
    return x2[None]
```

```python
import functools
import math

import numpy as np
import jax
import jax.numpy as jnp
from jax import lax
from jax.experimental import pallas as pl
from jax.experimental.pallas import tpu as pltpu

N_SUB = 3
HEADS = 8
HEAD_W = 128
DIFF_QK_DIM = 64
DN_CONV = 4
DN_CHUNK = 64
REL_BUCKETS = 32
REL_MAX_DIST = 128
LN_EPS = 1e-5
RMS_EPS = 1e-6
MASK_VALUE = -1e30
LOG2E = math.log2(math.e)
V_ROWS = HEAD_W + 16

F32 = jnp.float32
BF16 = jnp.bfloat16

V7X_VMEM_BYTES = 64 * 1024 * 1024
VMEM_LIMIT = V7X_VMEM_BYTES - 4 * 1024 * 1024


def _params(*sem):
    return pltpu.CompilerParams(dimension_semantics=sem, vmem_limit_bytes=VMEM_LIMIT)


def _tile(n, preferred):
    t = min(preferred, n)
    while n % t:
        t -= 128
    return t


def _silu(x):
    return x * jax.nn.sigmoid(x)


def _dot(a, b):
    return jnp.dot(a, b, preferred_element_type=F32)


def _dot_nt(a, b):
    return lax.dot_general(a, b, (((1,), (1,)), ((), ())), preferred_element_type=F32)


def _dot_tn(a, b):
    return lax.dot_general(a, b, (((0,), (0,)), ((), ())), preferred_element_type=F32)


def _layer_norm(r, g, b):
    mu = jnp.mean(r, axis=-1, keepdims=True)
    d = r - mu
    var = jnp.mean(d * d, axis=-1, keepdims=True)
    return d * lax.rsqrt(var + LN_EPS) * g + b


def _ada_kernel(c_ref, w_ref, b_ref, o_ref):
    sc = _silu(c_ref[...])
    o_ref[...] = jnp.sum(w_ref[...] * sc, axis=0, keepdims=True) + b_ref[...]


def _ada(c_col, w, b):
    d, n = w.shape
    tn = _tile(n, 1024)
    return pl.pallas_call(
        _ada_kernel,
        grid=(n // tn,),
        in_specs=[pl.BlockSpec((d, 1), lambda j: (0, 0)),
                  pl.BlockSpec((d, tn), lambda j: (0, j)),
                  pl.BlockSpec((1, tn), lambda j: (0, j))],
        out_specs=pl.BlockSpec((1, tn), lambda j: (0, j)),
        out_shape=jax.ShapeDtypeStruct((1, n), F32),
        compiler_params=_params("arbitrary"),
        name="ada",
    )(c_col, w, b)


def _ffn_kernel(x_ref, shift_ref, scale_ref, gate_ref, lng_ref, lnb_ref, wg_ref, wu_ref, wo_ref,
                o_ref, h_sc, a_sc, *, alpha, gate_mul):
    j = pl.program_id(1)
    n_chunks = pl.num_programs(1) - 1

    def drain(slot):
        return _dot(a_sc[slot], wo_ref[...].astype(BF16))

    def activations(slot):
        h = h_sc[...]
        g = _dot(h, wg_ref[...].astype(BF16))
        u = _dot(h, wu_ref[...].astype(BF16))
        a_sc[slot] = (_silu(g) * u).astype(BF16)

    @pl.when(j == 0)
    def _():
        h_sc[...] = (x_ref[...] * (1.0 + scale_ref[...]) + shift_ref[...]).astype(BF16)
        o_ref[...] = jnp.zeros_like(o_ref)
        activations(0)

    for slot in range(2):
        @pl.when(jnp.logical_and(jnp.logical_and(j > 0, j < n_chunks), (j & 1) == slot))
        def _():
            o_ref[...] += drain(1 - slot)
            activations(slot)

        @pl.when(jnp.logical_and(j == n_chunks, ((j - 1) & 1) == slot))
        def _():
            r = alpha * x_ref[...] + (gate_mul * gate_ref[...]) * (o_ref[...] + drain(slot))
            o_ref[...] = _layer_norm(r, lng_ref[...], lnb_ref[...])


def _ffn(x, shift, scale, gate, lng, lnb, w_in, w_out, *, layer, which, alpha, tm=1024, tf=256):
    s, d = x.shape
    f = w_out.shape[2]
    tm, tf = _tile(s, tm), _tile(f, tf)
    nf = f // tf
    vec = pl.BlockSpec((1, d), lambda i, j: (0, 0))
    up = lambda j: jnp.minimum(j, nf - 1)
    down = lambda j: jnp.maximum(j - 1, 0)
    return pl.pallas_call(
        functools.partial(_ffn_kernel, alpha=alpha, gate_mul=0.5),
        grid=(s // tm, nf + 1),
        in_specs=[pl.BlockSpec((tm, d), lambda i, j: (i, 0)), vec, vec, vec, vec, vec,
                  pl.BlockSpec((None, None, d, tf), lambda i, j: (layer, which, 0, up(j))),
                  pl.BlockSpec((None, None, d, tf), lambda i, j: (layer, which, 0, up(j) + nf)),
                  pl.BlockSpec((None, None, tf, d), lambda i, j: (layer, which, down(j), 0))],
        out_specs=pl.BlockSpec((tm, d), lambda i, j: (i, 0)),
        out_shape=jax.ShapeDtypeStruct((s, d), F32),
        scratch_shapes=[pltpu.VMEM((tm, d), BF16), pltpu.VMEM((2, tm, tf), BF16)],
        compiler_params=_params("parallel", "arbitrary"),
        name="ffn",
    )(x, shift, scale, gate, lng, lnb, w_in, w_in, w_out)


def _proj_kernel(h_ref, w_ref, ca_ref, cb_ref, o_ref, ca16_ref, cb16_ref, w_sc):
    @pl.when(pl.program_id(1) == 0)
    def _():
        w_sc[...] = w_ref[...].astype(BF16)

    o_ref[...] = _dot_nt(h_ref[...], w_sc[...]).astype(BF16)
    ca16_ref[...] = ca_ref[...].astype(BF16)
    cb16_ref[...] = cb_ref[...].astype(BF16)


def _cast_slabs(arr, lead_idx, n_steps, step_of):
    rows, cols = arr.shape[-2:]
    r = 16
    while rows % r or rows // r > n_steps:
        r += 16
    slab = lambda *g: jnp.minimum(step_of(*g), rows // r - 1)
    return (pl.BlockSpec((None,) * len(lead_idx) + (r, cols), lambda *g: (*lead_idx, slab(*g), 0)),
            pl.BlockSpec((r, cols), lambda *g: (slab(*g), 0)), jax.ShapeDtypeStruct((rows, cols), BF16))


def _proj(h, w_all_t, cast_a, cast_b, cast_idx, *, gate_row0, n_gate_cols, n_head_cols, tm=1024, tn=1024):
    s, d = h.shape
    tm = _tile(s, tm)
    tn = math.gcd(_tile(n_gate_cols, tn), _tile(n_head_cols, tn))
    n_gate, n_head = n_gate_cols // tn, n_head_cols // tn
    assert gate_row0 % 8 == 0
    first_row = lambda j: pl.multiple_of(jnp.where(j < n_gate, gate_row0 + j * tn, (j - n_gate) * tn), 8)
    n_rows = s // tm
    n_steps = (n_gate + n_head) * n_rows
    step_of = lambda j, i: j * n_rows + i
    (a_in, a_out, a_shape), (b_in, b_out, b_shape) = (_cast_slabs(arr, cast_idx, n_steps, step_of)
                                                      for arr in (cast_a, cast_b))
    return pl.pallas_call(
        _proj_kernel,
        grid=(n_gate + n_head, n_rows),
        in_specs=[pl.BlockSpec((tm, d), lambda j, i: (i, 0)),
                  pl.BlockSpec((pl.Element(tn), pl.Element(d)), lambda j, i: (first_row(j), 0)), a_in, b_in],
        out_specs=[pl.BlockSpec((tm, tn), lambda j, i: (i, j)), a_out, b_out],
        out_shape=[jax.ShapeDtypeStruct((s, n_gate_cols + n_head_cols), BF16), a_shape, b_shape],
        scratch_shapes=[pltpu.VMEM((tn, d), BF16)],
        compiler_params=_params("arbitrary", "arbitrary"),
        name="proj",
    )(h, w_all_t, cast_a, cast_b)


def _bucket_starts():
    n = np.arange(0, 2 * REL_MAX_DIST)
    max_exact = REL_BUCKETS // 2
    nf = np.maximum(n, max_exact).astype(np.float32)
    large = max_exact + (np.log(nf / np.float32(max_exact)) / np.float32(math.log(REL_MAX_DIST / max_exact))
                         * np.float32(REL_BUCKETS - max_exact)).astype(np.int32)
    bucket = np.where(n < max_exact, n, np.minimum(large, REL_BUCKETS - 1))
    assert np.all(np.diff(bucket) >= 0) and bucket[-1] == REL_BUCKETS - 1
    return [int(np.min(n[bucket >= b])) for b in range(REL_BUCKETS)]


BUCKET_STARTS = _bucket_starts()


def _bias_kernel(rb_ref, o_ref, *, t):
    h = pl.program_id(0)
    jj = lax.broadcasted_iota(jnp.int32, (t, 2 * t), 0)
    ii = lax.broadcasted_iota(jnp.int32, (t, 2 * t), 1)
    ii = jnp.where(ii >= t, ii - t, ii)
    far = rb_ref[REL_BUCKETS - 1, h]
    for n in range(2):
        rel = ii - jj + n * t
        bias = jnp.full((t, 2 * t), (rb_ref[0, h] - far) * LOG2E, F32)
        for b in range(1, REL_BUCKETS):
            bias = jnp.where(rel >= BUCKET_STARTS[b], (rb_ref[b, h] - far) * LOG2E, bias)
        o_ref[0, n] = jnp.where(rel < 0, MASK_VALUE, bias)


def _bias_tiles(rel_bias, t):
    return pl.pallas_call(
        functools.partial(_bias_kernel, t=t),
        grid=(HEADS,),
        in_specs=[pl.BlockSpec(memory_space=pltpu.SMEM)],
        out_specs=pl.BlockSpec((1, 2, t, 2 * t), lambda h: (h, 0, 0, 0)),
        out_shape=jax.ShapeDtypeStruct((HEADS, 2, t, 2 * t), F32),
        compiler_params=_params("arbitrary"),
        name="bias_tiles",
    )(rel_bias)


def _attn_kernel(q_ref, k_ref, v_ref, bias_ref, lam_ref, sub_ref, o_ref,
                 vt_sc, qt_sc, s_sc, m_sc, acc_sc, *, t, hp, lam_init):
    qi = pl.program_id(1)
    n_kv = vt_sc.shape[1]
    heads = range(hp)
    cols = [slice(hh * HEAD_W, (hh + 1) * HEAD_W) for hh in heads]

    @pl.when(qi == 0)
    def _():
        ones_tile = (lax.broadcasted_iota(jnp.int32, (V_ROWS - HEAD_W, t), 0) == 0).astype(BF16)

        def body(c, carry):
            r0 = pl.multiple_of(c * t, t)
            for hh in heads:
                vt_sc[hh, c, 0:HEAD_W] = v_ref[pl.ds(r0, t), cols[hh]].astype(F32).T.astype(BF16)
                vt_sc[hh, c, HEAD_W:V_ROWS] = ones_tile
            return carry
        lax.fori_loop(0, n_kv, body, 0)

    row = lax.broadcasted_iota(jnp.int32, (HEAD_W, t), 0)
    for hh in heads:
        qt = (q_ref[:, cols[hh]].astype(F32) * (DIFF_QK_DIM ** -0.5 * LOG2E)).T
        zero = jnp.zeros_like(qt)
        qt_sc[hh] = jnp.concatenate([jnp.where(row < DIFF_QK_DIM, qt, zero),
                                     jnp.where(row >= DIFF_QK_DIM, qt, zero)], axis=1).astype(BF16)

    m_sc[...] = jnp.full_like(m_sc, -jnp.inf)
    acc_sc[...] = jnp.zeros_like(acc_sc)

    def scores(c, slot, biased, which=heads):
        r0 = pl.multiple_of(c * t, t)
        for hh in which:
            s = _dot(k_ref[pl.ds(r0, t), cols[hh]], qt_sc[hh])
            if biased:
                s = s + bias_ref[hh, qi - c]
            s_sc[slot, hh] = s

    def consume(c, slot, which=heads):
        for hh in which:
            s = s_sc[slot, hh]
            m_old = m_sc[hh]
            m_new = jnp.maximum(m_old, jnp.max(s, axis=0, keepdims=True))
            p = jnp.exp2(s - m_new).astype(BF16)
            acc_sc[hh] = jnp.exp2(m_old - m_new) * acc_sc[hh] + _dot(vt_sc[hh, c], p)
            m_sc[hh] = m_new

    pl.when(qi >= 2)(lambda: scores(0, 0, False))
    pl.when(qi < 2)(lambda: scores(0, 0, True))

    n_pairs = jnp.maximum(qi - 2, 0) >> 1

    def run(c, n):
        scores(c + 1, 1, False)
        for k in range(n - 1):
            for hh in heads:
                consume(c + k, k & 1, [hh])
                scores(c + k + 2, k & 1, False, [hh])
        consume(c + n - 1, 1)

    done = 0
    for shift in (2, 1, 0):
        size = 2 << shift
        count = n_pairs >> shift

        def body(j, carry, size=size):
            run(size * j, size)
            return carry
        lax.fori_loop(done, count, body, 0)
        done = 2 * count

    @pl.when(jnp.logical_and(qi >= 2, (qi & 1) == 1))
    def _():
        scores(qi - 2, 1, False)
        consume(qi - 3, 0)

    def finish():
        lp = lam_ref[...]
        lam = (jnp.exp(jnp.sum(lp[0:1] * lp[1:2], axis=1, keepdims=True))
               - jnp.exp(jnp.sum(lp[2:3] * lp[3:4], axis=1, keepdims=True)) + lam_init)
        for hh in heads:
            acc = acc_sc[hh]
            o = acc[0:HEAD_W] / acc[HEAD_W:HEAD_W + 1]
            od = o[:, :t] - lam * o[:, t:]
            ms = jnp.mean(od * od, axis=0, keepdims=True)
            y = od * lax.rsqrt(ms + LN_EPS) * sub_ref[...] * (1.0 - lam_init)
            o_ref[:, cols[hh]] = y.T.astype(BF16)

    for slot in range(2):
        @pl.when(jnp.logical_and(qi >= 2, (qi & 1) == slot))
        def _():
            scores(qi - 1, 1 - slot, True)
            for hh in heads:
                consume(qi - 2, slot, [hh])
                scores(qi, slot, True, [hh])
            consume(qi - 1, 1 - slot)
            consume(qi, slot)
            finish()

    @pl.when(qi == 1)
    def _():
        scores(1, 1, True)
        consume(0, 0)
        consume(1, 1)
        finish()

    @pl.when(qi == 0)
    def _():
        consume(0, 0)
        finish()


def _attn(proj, bias_tiles, lam_params, subln_col, *, q_blk, k_blk, v_blk, lam_init, t, hp=4):
    s = proj.shape[0]
    w = hp * HEAD_W
    assert q_blk % hp == 0 and k_blk % hp == 0 and v_blk % hp == 0
    once = pl.Buffered(1)
    return pl.pallas_call(
        functools.partial(_attn_kernel, t=t, hp=hp, lam_init=lam_init),
        grid=(HEADS // hp, s // t),
        in_specs=[pl.BlockSpec((t, w), lambda g, i: (i, q_blk // hp + g)),
                  pl.BlockSpec((s, w), lambda g, i: (0, k_blk // hp + g), pipeline_mode=once),
                  pl.BlockSpec((s, w), lambda g, i: (0, v_blk // hp + g), pipeline_mode=once),
                  pl.BlockSpec((hp, 2, t, 2 * t), lambda g, i: (g, 0, 0, 0), pipeline_mode=once),
                  pl.BlockSpec(lam_params.shape, lambda g, i: (0, 0)),
                  pl.BlockSpec((HEAD_W, 1), lambda g, i: (0, 0))],
        out_specs=pl.BlockSpec((t, w), lambda g, i: (i, g)),
        out_shape=jax.ShapeDtypeStruct((s, HEADS * HEAD_W), BF16),
        scratch_shapes=[pltpu.VMEM((hp, s // t, V_ROWS, t), BF16),
                        pltpu.VMEM((hp, HEAD_W, 2 * t), BF16),
                        pltpu.VMEM((2, hp, t, 2 * t), F32),
                        pltpu.VMEM((hp, 1, 2 * t), F32),
                        pltpu.VMEM((hp, V_ROWS, 2 * t), F32)],
        compiler_params=_params("arbitrary", "arbitrary"),
        name="diff_attn",
    )(proj, proj, proj, bias_tiles, lam_params, subln_col)


def _gates_kernel(x_ref, shift_ref, scale_ref, ws_ref, alog_ref, dtb_ref, o_ref, h_ref):
    h = (x_ref[...] * (1.0 + scale_ref[...]) + shift_ref[...]).astype(BF16)
    h_ref[...] = h
    tr = _dot_nt(ws_ref[...], h.astype(F32))
    beta = jax.nn.sigmoid(tr[0:HEADS])
    x = tr[HEADS:2 * HEADS] + dtb_ref[...]
    softplus = jnp.maximum(x, 0.0) + jnp.log1p(jnp.exp(-jnp.abs(x)))
    g = -jnp.exp(alog_ref[...]) * softplus
    pos = lax.broadcasted_iota(jnp.int32, g.shape, 1) % DN_CHUNK
    shift = 1
    while shift < DN_CHUNK:
        g = g + jnp.where(pos >= shift, pltpu.roll(g, shift, axis=1), 0.0)
        shift *= 2
    o_ref[0:HEADS] = beta
    o_ref[HEADS:2 * HEADS] = g


def _gates(x, shift, scale, w_small_t, alog_col, dtb_col, *, tb=1024):
    s, d = x.shape
    col = pl.BlockSpec((HEADS, 1), lambda i: (0, 0))
    vec = pl.BlockSpec((1, d), lambda i: (0, 0))
    return pl.pallas_call(
        _gates_kernel,
        grid=(s // tb,),
        in_specs=[pl.BlockSpec((tb, d), lambda i: (i, 0)), vec, vec,
                  pl.BlockSpec(w_small_t.shape, lambda i: (0, 0)), col, col],
        out_specs=[pl.BlockSpec((2 * HEADS, tb), lambda i: (0, i)), pl.BlockSpec((tb, d), lambda i: (i, 0))],
        out_shape=[jax.ShapeDtypeStruct((2 * HEADS, s), F32), jax.ShapeDtypeStruct((s, d), BF16)],
        compiler_params=_params("parallel"),
        name="gdn_gates",
    )(x, shift, scale, w_small_t, alog_col, dtb_col)


GROUP = 2 * DN_CHUNK
INV_BASE = 16


def _gdn_kernel(q_ref, k_ref, v_ref, z_ref, gt_ref, cwq_ref, cwk_ref, cwv_ref, nw_ref, *rest, tb, hp, n_cast):
    cast_in, (o_ref, *cast_out), (pad_sc, state_sc, o_sc) = rest[:n_cast], rest[n_cast:2 * n_cast + 1], rest[-3:]
    for src, dst in zip(cast_in, cast_out):
        dst[...] = src[...].astype(BF16)
    g = pl.program_id(0)
    ib = pl.program_id(1)
    halo = 8
    heads = range(hp)
    cols = [slice(hh * HEAD_W, (hh + 1) * HEAD_W) for hh in heads]

    @pl.when(ib == 0)
    def _():
        state_sc[...] = jnp.zeros_like(state_sc)
        pad_sc[:, 0:halo, :] = jnp.zeros((3 * hp, halo, HEAD_W), F32)

    def conv_silu(a, x_ref, cw_ref, hh):
        a = a * hp + hh
        pad_sc[a, halo:halo + tb, :] = x_ref[:, cols[hh]].astype(F32)
        cw = cw_ref[:, cols[hh]]
        y = cw[0:1] * pad_sc[a, halo - 3:halo - 3 + tb, :]
        for j in range(1, DN_CONV):
            y = y + cw[j:j + 1] * pad_sc[a, halo - 3 + j:halo - 3 + j + tb, :]
        pad_sc[a, 0:halo, :] = pad_sc[a, tb:tb + halo, :]
        return _silu(y)

    def l2n(x):
        return x * lax.rsqrt(jnp.sum(x * x, axis=-1, keepdims=True) + RMS_EPS)

    ii = lax.broadcasted_iota(jnp.int32, (GROUP, GROUP), 0)
    jj = lax.broadcasted_iota(jnp.int32, (GROUP, GROUP), 1)
    same = (ii >= DN_CHUNK) == (jj >= DN_CHUNK)
    tril = jnp.logical_and(same, ii >= jj)
    eye = (ii == jj).astype(F32)
    bits = INV_BASE.bit_length() - 1
    in_base = jnp.logical_and(ii >> bits == jj >> bits, ii > jj)
    below = []
    while (1 << bits) < DN_CHUNK:
        below.append(jnp.logical_and(ii >> (bits + 1) == jj >> (bits + 1), ii >> bits == (jj >> bits) + 1))
        bits += 1

    q, k, v, beta_c, cum_c, cum_t = [], [], [], [], [], []
    for hh in heads:
        q.append(l2n(conv_silu(0, q_ref, cwq_ref, hh)) * (HEAD_W ** -0.5))
        k.append(l2n(conv_silu(1, k_ref, cwk_ref, hh)))
        v.append(conv_silu(2, v_ref, cwv_ref, hh))
        head = g * hp + hh
        beta_t = jnp.broadcast_to(gt_ref[pl.ds(head, 1), :], (HEAD_W, tb))
        ct = jnp.broadcast_to(gt_ref[pl.ds(HEADS + head, 1), :], (HEAD_W, tb))
        beta_c.append(beta_t.T)
        cum_c.append(ct.T)
        cum_t.append(ct)

    groups = range(tb // GROUP)
    rows = [slice(r * GROUP, (r + 1) * GROUP) for r in groups]
    u, w, qk, qd, kg, cg = {}, {}, {}, {}, {}, {}

    def intra_chunk(r):
        kb, x, tinv, decay, merges = {}, {}, {}, {}, {}
        for hh in heads:
            sid = (r, hh)
            kg[sid], cg[sid] = k[hh][rows[r]], cum_c[hh][rows[r]]
            gdiff = cg[sid] - cum_t[hh][:, rows[r]]
            decay[hh] = jnp.where(tril, jnp.exp(jnp.where(tril, gdiff, 0.0)), 0.0)
            kb[hh] = kg[sid] * beta_c[hh][rows[r]]
            a = _dot_nt(kb[hh].astype(BF16), kg[sid].astype(BF16)) * decay[hh]
            diag = jnp.where(in_base, a, 0.0)
            x[hh] = diag.astype(BF16)
            tinv[hh] = eye - diag
            merges[hh] = [jnp.where(m, a, 0.0).astype(BF16) for m in below]
        yield
        power = 2
        while power < INV_BASE:
            for hh in heads:
                x[hh] = _dot(x[hh], x[hh]).astype(BF16)
            yield
            for hh in heads:
                tinv[hh] = tinv[hh] + _dot(tinv[hh].astype(BF16), x[hh])
            yield
            power *= 2
        for level in range(len(below)):
            half = {}
            for hh in heads:
                half[hh] = _dot(tinv[hh].astype(BF16), merges[hh][level]).astype(BF16)
            yield
            for hh in heads:
                tinv[hh] = tinv[hh] - _dot(half[hh], tinv[hh].astype(BF16))
            yield
        for hh in heads:
            sid = (r, hh)
            eg = jnp.exp(cg[sid])
            rhs = jnp.concatenate([v[hh][rows[r]] * beta_c[hh][rows[r]], kb[hh] * eg], axis=1).astype(BF16)
            uw = _dot(tinv[hh].astype(BF16), rhs)
            u[sid], w[sid] = uw[:, :HEAD_W], uw[:, HEAD_W:].astype(BF16)
            qg = q[hh][rows[r]]
            qk[sid] = (_dot_nt(qg.astype(BF16), kg[sid].astype(BF16)) * decay[hh]).astype(BF16)
            qd[sid] = (qg * eg).astype(BF16)
        yield

    def recurrence(r):
        v_new = {hh: [] for hh in heads}
        o_inter = {hh: [] for hh in heads}
        for c in range(2):
            cr = slice(c * DN_CHUNK, (c + 1) * DN_CHUNK)
            last = c * DN_CHUNK + DN_CHUNK - 1
            st, g_last = {}, {}
            for hh in heads:
                sid = (r, hh)
                g_last[hh] = cg[sid][last:last + 1, :]
                st[hh] = state_sc[hh]
                st16 = st[hh].astype(BF16)
                v_new[hh].append(u[sid][cr] - _dot(w[sid][cr], st16))
                o_inter[hh].append(_dot(qd[sid][cr], st16))
            yield
            for hh in heads:
                sid = (r, hh)
                kd = (kg[sid][cr] * jnp.exp(g_last[hh] - cg[sid][cr])).astype(BF16)
                state_sc[hh] = st[hh] * jnp.exp(g_last[hh]) + _dot_tn(kd, v_new[hh][c].astype(BF16))
            yield
        for hh in heads:
            vn_all = jnp.concatenate(v_new[hh], axis=0).astype(BF16)
            o_sc[hh, rows[r], :] = jnp.concatenate(o_inter[hh], axis=0) + _dot(qk[(r, hh)], vn_all)
        yield

    def interleave(slow, fast, ratio):
        slow_live = fast_live = True
        while slow_live or fast_live:
            if slow_live:
                slow_live = next(slow, "done") != "done"
            for _ in range(ratio):
                if fast_live:
                    fast_live = next(fast, "done") != "done"

    for _ in intra_chunk(0):
        pass
    for r in groups:
        nxt = intra_chunk(r + 1) if r + 1 < len(groups) else iter(())
        interleave(recurrence(r), nxt, 3)

    for hh in heads:
        o = o_sc[hh]
        on = o * lax.rsqrt(jnp.mean(o * o, axis=-1, keepdims=True) + RMS_EPS) * nw_ref[...]
        o_ref[:, cols[hh]] = (on * _silu(z_ref[:, cols[hh]].astype(F32))).astype(BF16)


def _gdn(proj, gates_t, conv_w, norm_w, casts, cast_idx, *, q_blk, k_blk, v_blk, z_blk, tb=512, hp=8):
    s = proj.shape[0]
    w = hp * HEAD_W
    assert all(blk % hp == 0 for blk in (q_blk, k_blk, v_blk, z_blk))
    col = lambda blk: pl.BlockSpec((tb, w), lambda g, i: (i, blk // hp + g))
    cw = lambda blk: pl.BlockSpec((DN_CONV, w), lambda g, i: (0, blk // hp + g))
    n_t = s // tb
    slabs = [_cast_slabs(arr, cast_idx, (HEADS // hp) * n_t, lambda g, i: g * n_t + i) for arr in casts]
    return pl.pallas_call(
        functools.partial(_gdn_kernel, tb=tb, hp=hp, n_cast=len(casts)),
        grid=(HEADS // hp, n_t),
        in_specs=[col(q_blk), col(k_blk), col(v_blk), col(z_blk),
                  pl.BlockSpec((2 * HEADS, tb), lambda g, i: (0, i)),
                  cw(0), cw(HEADS), cw(2 * HEADS),
                  pl.BlockSpec((1, HEAD_W), lambda g, i: (0, 0))] + [sl[0] for sl in slabs],
        out_specs=[pl.BlockSpec((tb, w), lambda g, i: (i, g))] + [sl[1] for sl in slabs],
        out_shape=[jax.ShapeDtypeStruct((s, HEADS * HEAD_W), BF16)] + [sl[2] for sl in slabs],
        scratch_shapes=[pltpu.VMEM((3 * hp, tb + 8, HEAD_W), F32),
                        pltpu.VMEM((hp, HEAD_W, HEAD_W), F32),
                        pltpu.VMEM((hp, tb, HEAD_W), F32)],
        compiler_params=_params("arbitrary", "arbitrary"),
        name="gdn",
    )(proj, proj, proj, proj, gates_t, conv_w, conv_w, conv_w, norm_w, *casts)


def _merge_kernel(ya_ref, yb_ref, ga_ref, gb_ref, x_ref, gate_ref, lng_ref, lnb_ref,
                  wa_ref, wb_ref, wo_ref, o_ref, m_sc, *, alpha):
    i = pl.program_id(0)
    n_tiles = pl.num_programs(0) - 1

    def merge(slot):
        a = _dot(ya_ref[...], wa_ref[...])
        b = _dot(yb_ref[...], wb_ref[...])
        m_sc[slot] = (jax.nn.sigmoid(ga_ref[...].astype(F32)) * a
                      + jax.nn.sigmoid(gb_ref[...].astype(F32)) * b).astype(BF16)

    def project(slot):
        y = _dot(m_sc[slot], wo_ref[...])
        r = alpha * x_ref[...] + gate_ref[...] * y
        o_ref[...] = _layer_norm(r, lng_ref[...], lnb_ref[...])

    pl.when(i == 0)(lambda: merge(0))
    for slot in range(2):
        @pl.when(jnp.logical_and(jnp.logical_and(i > 0, i < n_tiles), (i & 1) == slot))
        def _():
            project(1 - slot)
            merge(slot)

        pl.when(jnp.logical_and(i == n_tiles, ((i - 1) & 1) == slot))(functools.partial(project, slot))


def _merge(ya, yb, proj, x, gate, lng, lnb, w_a, w_b, w_o, *, ga_blk, gb_blk, alpha, tm=256):
    s, d = x.shape
    dv = ya.shape[1]
    n = s // tm
    vec = pl.BlockSpec((1, d), lambda i: (0, 0))
    const = lambda shape: pl.BlockSpec(shape, lambda i: (0, 0), pipeline_mode=pl.Buffered(1))
    cur = lambda i: jnp.minimum(i, n - 1)
    prev = lambda i: jnp.maximum(i - 1, 0)
    return pl.pallas_call(
        functools.partial(_merge_kernel, alpha=alpha),
        grid=(n + 1,),
        in_specs=[pl.BlockSpec((tm, dv), lambda i: (cur(i), 0)),
                  pl.BlockSpec((tm, dv), lambda i: (cur(i), 0)),
                  pl.BlockSpec((tm, d), lambda i: (cur(i), ga_blk)),
                  pl.BlockSpec((tm, d), lambda i: (cur(i), gb_blk)),
                  pl.BlockSpec((tm, d), lambda i: (prev(i), 0)), vec, vec, vec,
                  const(w_a.shape), const(w_b.shape), const(w_o.shape)],
        out_specs=pl.BlockSpec((tm, d), lambda i: (prev(i), 0)),
        out_shape=jax.ShapeDtypeStruct((s, d), F32),
        scratch_shapes=[pltpu.VMEM((2, tm, d), BF16)],
        compiler_params=_params("arbitrary"),
        name="merge",
    )(ya, yb, proj, proj, x, gate, lng, lnb, w_a, w_b, w_o)


def kernel(x, c, w_ada, b_ada, ln_g, ln_b, w_ffn_in, w_ffn_out, w_in, conv_w, dn_a_log, dn_dt_bias,
           dn_norm_w, diff_lambda, diff_subln_w, rel_bias, w_branch_a, w_branch_b, w_out):
    bsz, s, d = x.shape
    assert bsz == 1, "one sequence per call"
    depth = w_ada.shape[0]
    alpha = (2 * depth) ** 0.25
    hw = HEADS * HEAD_W
    attn_t = 256
    assert d % HEAD_W == 0 and s % 1024 == 0

    o_nb = 7 * hw
    o_ga = o_nb + 2 * HEADS
    nblk = d // HEAD_W
    blk = {"ga": 0, "gb": 1, "dq": 2 * nblk, "dk": 2 * nblk + HEADS, "dv": 2 * nblk + 2 * HEADS,
           "nq": 2 * nblk + 3 * HEADS, "nk": 2 * nblk + 4 * HEADS, "nv": 2 * nblk + 5 * HEADS,
           "nz": 2 * nblk + 6 * HEADS}

    bias_tiles = _bias_tiles(rel_bias, attn_t)
    wfi, wfo = w_ffn_in, w_ffn_out
    w_in_t = jnp.swapaxes(w_in, 1, 2)
    x2 = x[0]
    for l in range(depth):
        lam_init = 0.8 - 0.6 * math.exp(-0.3 * l)
        ada = _ada(c.reshape(d, 1), w_ada[l], b_ada[l].reshape(1, -1)).reshape(N_SUB, 3, 1, d)
        shift, scale, gate = ada[:, 0], ada[:, 1], ada[:, 2]
        lng, lnb = ln_g[l].reshape(N_SUB, 1, d), ln_b[l].reshape(N_SUB, 1, d)
        w_small_t = w_in_t[l, o_nb:o_ga]

        x2 = _ffn(x2, shift[0], scale[0], gate[0], lng[0], lnb[0], wfi, wfo, layer=l, which=0, alpha=alpha)

        gates_t, h = _gates(x2, shift[1], scale[1], w_small_t, dn_a_log[l].reshape(HEADS, 1),
                            dn_dt_bias[l].reshape(HEADS, 1))
        proj, wfi2, wfo2 = _proj(h, w_in_t[l], w_ffn_in, w_ffn_out, (l, 1),
                                 gate_row0=o_ga, n_gate_cols=2 * d, n_head_cols=o_nb)
        ya = _attn(proj, bias_tiles, diff_lambda[l], diff_subln_w[l].reshape(HEAD_W, 1),
                   q_blk=blk["dq"], k_blk=blk["dk"], v_blk=blk["dv"], lam_init=lam_init, t=attn_t)
        yb, wa16, wb16, wo16 = _gdn(proj, gates_t, conv_w[l], dn_norm_w[l].reshape(1, HEAD_W),
                                    (w_branch_a, w_branch_b, w_out), (l,),
                                    q_blk=blk["nq"], k_blk=blk["nk"], v_blk=blk["nv"], z_blk=blk["nz"])
        x2 = _merge(ya, yb, proj, x2, gate[1], lng[1], lnb[1], wa16, wb16, wo16,
                    ga_blk=blk["ga"], gb_blk=blk["gb"], alpha=alpha)

        x2 = _ffn(x2, shift[2], scale[2], gate[2], lng[2], lnb[2], wfi2[None, None], wfo2[None, None],
                  layer=0, which=0, alpha=alpha, tf=512)
    return x2[None]
```

```python
import functools
import math

import numpy as np
import jax
import jax.numpy as jnp
from jax import lax
from jax.experimental import pallas as pl
from jax.experimental.pallas import tpu as pltpu

N_SUB = 3
HEADS = 8
HEAD_W = 128
DIFF_QK_DIM = 64
DN_CONV = 4
DN_CHUNK = 64
REL_BUCKETS = 32
REL_MAX_DIST = 128
LN_EPS = 1e-5
RMS_EPS = 1e-6
MASK_VALUE = -1e30
LOG2E = math.log2(math.e)
V_ROWS = HEAD_W + 16

F32 = jnp.float32
BF16 = jnp.bfloat16

V7X_VMEM_BYTES = 64 * 1024 * 1024
VMEM_LIMIT = V7X_VMEM_BYTES - 4 * 1024 * 1024


def _params(*sem):
    return pltpu.CompilerParams(dimension_semantics=sem, vmem_limit_bytes=VMEM_LIMIT)


def _tile(n, preferred):
    t = min(preferred, n)
    while n % t:
        t -= 128
    return t


def _silu(x):
    return x * jax.nn.sigmoid(x)


def _dot(a, b):
    return jnp.dot(a, b, preferred_element_type=F32)


def _dot_nt(a, b):
    return lax.dot_general(a, b, (((1,), (1,)), ((), ())), preferred_element_type=F32)


def _dot_tn(a, b):
    return lax.dot_general(a, b, (((0,), (0,)), ((), ())), preferred_element_type=F32)


def _layer_norm(r, g, b):
    mu = jnp.mean(r, axis=-1, keepdims=True)
    d = r - mu
    var = jnp.mean(d * d, axis=-1, keepdims=True)
    return d * lax.rsqrt(var + LN_EPS) * g + b


def _ada_kernel(c_ref, w_ref, b_ref, o_ref):
    sc = _silu(c_ref[...])
    o_ref[...] = jnp.sum(w_ref[...] * sc, axis=0, keepdims=True) + b_ref[...]


def _ada(c_col, w, b):
    d, n = w.shape
    tn = _tile(n, 1024)
    return pl.pallas_call(
        _ada_kernel,
        grid=(n // tn,),
        in_specs=[pl.BlockSpec((d, 1), lambda j: (0, 0)),
                  pl.BlockSpec((d, tn), lambda j: (0, j)),
                  pl.BlockSpec((1, tn), lambda j: (0, j))],
        out_specs=pl.BlockSpec((1, tn), lambda j: (0, j)),
        out_shape=jax.ShapeDtypeStruct((1, n), F32),
        compiler_params=_params("arbitrary"),
        name="ada",
    )(c_col, w, b)


def _ffn_kernel(x_ref, shift_ref, scale_ref, gate_ref, lng_ref, lnb_ref, wg_ref, wu_ref, wo_ref,
                o_ref, h_sc, a_sc, *, alpha, gate_mul):
    j = pl.program_id(1)
    n_chunks = pl.num_programs(1) - 1

    def drain(slot):
        return _dot(a_sc[slot], wo_ref[...].astype(BF16))

    def activations(slot):
        h = h_sc[...]
        g = _dot(h, wg_ref[...].astype(BF16))
        u = _dot(h, wu_ref[...].astype(BF16))
        a_sc[slot] = (_silu(g) * u).astype(BF16)

    @pl.when(j == 0)
    def _():
        h_sc[...] = (x_ref[...] * (1.0 + scale_ref[...]) + shift_ref[...]).astype(BF16)
        o_ref[...] = jnp.zeros_like(o_ref)
        activations(0)

    for slot in range(2):
        @pl.when(jnp.logical_and(jnp.logical_and(j > 0, j < n_chunks), (j & 1) == slot))
        def _():
            o_ref[...] += drain(1 - slot)
            activations(slot)

        @pl.when(jnp.logical_and(j == n_chunks, ((j - 1) & 1) == slot))
        def _():
            r = alpha * x_ref[...] + (gate_mul * gate_ref[...]) * (o_ref[...] + drain(slot))
            o_ref[...] = _layer_norm(r, lng_ref[...], lnb_ref[...])


def _ffn(x, shift, scale, gate, lng, lnb, w_in, w_out, *, layer, which, alpha, tm=1024, tf=256):
    s, d = x.shape
    f = w_out.shape[2]
    tm, tf = _tile(s, tm), _tile(f, tf)
    nf = f // tf
    vec = pl.BlockSpec((1, d), lambda i, j: (0, 0))
    up = lambda j: jnp.minimum(j, nf - 1)
    down = lambda j: jnp.maximum(j - 1, 0)
    return pl.pallas_call(
        functools.partial(_ffn_kernel, alpha=alpha, gate_mul=0.5),
        grid=(s // tm, nf + 1),
        in_specs=[pl.BlockSpec((tm, d), lambda i, j: (i, 0)), vec, vec, vec, vec, vec,
                  pl.BlockSpec((None, None, d, tf), lambda i, j: (layer, which, 0, up(j))),
                  pl.BlockSpec((None, None, d, tf), lambda i, j: (layer, which, 0, up(j) + nf)),
                  pl.BlockSpec((None, None, tf, d), lambda i, j: (layer, which, down(j), 0))],
        out_specs=pl.BlockSpec((tm, d), lambda i, j: (i, 0)),
        out_shape=jax.ShapeDtypeStruct((s, d), F32),
        scratch_shapes=[pltpu.VMEM((tm, d), BF16), pltpu.VMEM((2, tm, tf), BF16)],
        compiler_params=_params("parallel", "arbitrary"),
        name="ffn",
    )(x, shift, scale, gate, lng, lnb, w_in, w_in, w_out)


def _proj_kernel(h_ref, w_ref, ca_ref, cb_ref, o_ref, ca16_ref, cb16_ref, w_sc):
    @pl.when(pl.program_id(1) == 0)
    def _():
        w_sc[...] = w_ref[...].astype(BF16)

    o_ref[...] = _dot_nt(h_ref[...], w_sc[...]).astype(BF16)
    ca16_ref[...] = ca_ref[...].astype(BF16)
    cb16_ref[...] = cb_ref[...].astype(BF16)


def _cast_slabs(arr, lead_idx, n_steps, step_of):
    rows, cols = arr.shape[-2:]
    r = 16
    while rows % r or rows // r > n_steps:
        r += 16
    slab = lambda *g: jnp.minimum(step_of(*g), rows // r - 1)
    return (pl.BlockSpec((None,) * len(lead_idx) + (r, cols), lambda *g: (*lead_idx, slab(*g), 0)),
            pl.BlockSpec((r, cols), lambda *g: (slab(*g), 0)), jax.ShapeDtypeStruct((rows, cols), BF16))


def _proj(h, w_all_t, cast_a, cast_b, cast_idx, *, gate_row0, n_gate_cols, n_head_cols, tm=1024, tn=1024):
    s, d = h.shape
    tm = _tile(s, tm)
    tn = math.gcd(_tile(n_gate_cols, tn), _tile(n_head_cols, tn))
    n_gate, n_head = n_gate_cols // tn, n_head_cols // tn
    assert gate_row0 % 8 == 0
    first_row = lambda j: pl.multiple_of(jnp.where(j < n_gate, gate_row0 + j * tn, (j - n_gate) * tn), 8)
    n_rows = s // tm
    n_steps = (n_gate + n_head) * n_rows
    step_of = lambda j, i: j * n_rows + i
    (a_in, a_out, a_shape), (b_in, b_out, b_shape) = (_cast_slabs(arr, cast_idx, n_steps, step_of)
                                                      for arr in (cast_a, cast_b))
    return pl.pallas_call(
        _proj_kernel,
        grid=(n_gate + n_head, n_rows),
        in_specs=[pl.BlockSpec((tm, d), lambda j, i: (i, 0)),
                  pl.BlockSpec((pl.Element(tn), pl.Element(d)), lambda j, i: (first_row(j), 0)), a_in, b_in],
        out_specs=[pl.BlockSpec((tm, tn), lambda j, i: (i, j)), a_out, b_out],
        out_shape=[jax.ShapeDtypeStruct((s, n_gate_cols + n_head_cols), BF16), a_shape, b_shape],
        scratch_shapes=[pltpu.VMEM((tn, d), BF16)],
        compiler_params=_params("arbitrary", "arbitrary"),
        name="proj",
    )(h, w_all_t, cast_a, cast_b)


def _bucket_starts():
    n = np.arange(0, 2 * REL_MAX_DIST)
    max_exact = REL_BUCKETS // 2
    nf = np.maximum(n, max_exact).astype(np.float32)
    large = max_exact + (np.log(nf / np.float32(max_exact)) / np.float32(math.log(REL_MAX_DIST / max_exact))
                         * np.float32(REL_BUCKETS - max_exact)).astype(np.int32)
    bucket = np.where(n < max_exact, n, np.minimum(large, REL_BUCKETS - 1))
    assert np.all(np.diff(bucket) >= 0) and bucket[-1] == REL_BUCKETS - 1
    return [int(np.min(n[bucket >= b])) for b in range(REL_BUCKETS)]


BUCKET_STARTS = _bucket_starts()


def _bias_kernel(rb_ref, o_ref, *, t):
    h = pl.program_id(0)
    jj = lax.broadcasted_iota(jnp.int32, (t, 2 * t), 0)
    ii = lax.broadcasted_iota(jnp.int32, (t, 2 * t), 1)
    ii = jnp.where(ii >= t, ii - t, ii)
    far = rb_ref[REL_BUCKETS - 1, h]
    for n in range(2):
        rel = ii - jj + n * t
        bias = jnp.full((t, 2 * t), (rb_ref[0, h] - far) * LOG2E, F32)
        for b in range(1, REL_BUCKETS):
            bias = jnp.where(rel >= BUCKET_STARTS[b], (rb_ref[b, h] - far) * LOG2E, bias)
        o_ref[0, n] = jnp.where(rel < 0, MASK_VALUE, bias)


def _bias_tiles(rel_bias, t):
    return pl.pallas_call(
        functools.partial(_bias_kernel, t=t),
        grid=(HEADS,),
        in_specs=[pl.BlockSpec(memory_space=pltpu.SMEM)],
        out_specs=pl.BlockSpec((1, 2, t, 2 * t), lambda h: (h, 0, 0, 0)),
        out_shape=jax.ShapeDtypeStruct((HEADS, 2, t, 2 * t), F32),
        compiler_params=_params("arbitrary"),
        name="bias_tiles",
    )(rel_bias)


def _attn_kernel(q_ref, k_ref, v_ref, bias_ref, lam_ref, sub_ref, o_ref,
                 vt_sc, qt_sc, s_sc, m_sc, acc_sc, *, t, hp, lam_init):
    qi = pl.program_id(1)
    n_kv = vt_sc.shape[1]
    heads = range(hp)
    cols = [slice(hh * HEAD_W, (hh + 1) * HEAD_W) for hh in heads]

    @pl.when(qi == 0)
    def _():
        ones_tile = (lax.broadcasted_iota(jnp.int32, (V_ROWS - HEAD_W, t), 0) == 0).astype(BF16)

        def body(c, carry):
            r0 = pl.multiple_of(c * t, t)
            for hh in heads:
                vt_sc[hh, c, 0:HEAD_W] = v_ref[pl.ds(r0, t), cols[hh]].astype(F32).T.astype(BF16)
                vt_sc[hh, c, HEAD_W:V_ROWS] = ones_tile
            return carry
        lax.fori_loop(0, n_kv, body, 0)

    def start(biased):
        row = lax.broadcasted_iota(jnp.int32, (HEAD_W, t), 0)
        for hh in heads:
            qt = (q_ref[:, cols[hh]].astype(F32) * (DIFF_QK_DIM ** -0.5 * LOG2E)).T
            zero = jnp.zeros_like(qt)
            qt_sc[hh] = jnp.concatenate([jnp.where(row < DIFF_QK_DIM, qt, zero),
                                         jnp.where(row >= DIFF_QK_DIM, qt, zero)], axis=1).astype(BF16)
        m_sc[...] = jnp.full_like(m_sc, -jnp.inf)
        acc_sc[...] = jnp.zeros_like(acc_sc)
        scores(0, 0, biased)

    def scores(c, slot, biased, which=heads):
        r0 = pl.multiple_of(c * t, t)
        for hh in which:
            s = _dot(k_ref[pl.ds(r0, t), cols[hh]], qt_sc[hh])
            if biased:
                s = s + bias_ref[hh, qi - c]
            s_sc[slot, hh] = s

    def consume(c, slot, which=heads):
        for hh in which:
            s = s_sc[slot, hh]
            m_old = m_sc[hh]
            m_new = jnp.maximum(m_old, jnp.max(s, axis=0, keepdims=True))
            p = jnp.exp2(s - m_new).astype(BF16)
            acc_sc[hh] = jnp.exp2(m_old - m_new) * acc_sc[hh] + _dot(vt_sc[hh, c], p)
            m_sc[hh] = m_new

    pl.when(qi >= 2)(lambda: start(False))
    pl.when(qi < 2)(lambda: start(True))

    n_pairs = jnp.maximum(qi - 2, 0) >> 1

    def run(c, n):
        scores(c + 1, 1, False)
        for k in range(n - 1):
            for hh in heads:
                consume(c + k, k & 1, [hh])
                scores(c + k + 2, k & 1, False, [hh])
        consume(c + n - 1, 1)

    done = 0
    for shift in (2, 1, 0):
        size = 2 << shift
        count = n_pairs >> shift

        def body(j, carry, size=size):
            run(size * j, size)
            return carry
        lax.fori_loop(done, count, body, 0)
        done = 2 * count

    @pl.when(jnp.logical_and(qi >= 2, (qi & 1) == 1))
    def _():
        scores(qi - 2, 1, False)
        consume(qi - 3, 0)

    def finish():
        lp = lam_ref[...]
        lam = (jnp.exp(jnp.sum(lp[0:1] * lp[1:2], axis=1, keepdims=True))
               - jnp.exp(jnp.sum(lp[2:3] * lp[3:4], axis=1, keepdims=True)) + lam_init)
        for hh in heads:
            acc = acc_sc[hh]
            o = acc[0:HEAD_W] / acc[HEAD_W:HEAD_W + 1]
            od = o[:, :t] - lam * o[:, t:]
            ms = jnp.mean(od * od, axis=0, keepdims=True)
            y = od * lax.rsqrt(ms + LN_EPS) * sub_ref[...] * (1.0 - lam_init)
            o_ref[:, cols[hh]] = y.T.astype(BF16)

    for slot in range(2):
        @pl.when(jnp.logical_and(qi >= 2, (qi & 1) == slot))
        def _():
            scores(qi - 1, 1 - slot, True)
            for hh in heads:
                consume(qi - 2, slot, [hh])
                scores(qi, slot, True, [hh])
            consume(qi - 1, 1 - slot)
            consume(qi, slot)
            finish()

    @pl.when(qi == 1)
    def _():
        scores(1, 1, True)
        consume(0, 0)
        consume(1, 1)
        finish()

    @pl.when(qi == 0)
    def _():
        consume(0, 0)
        finish()


def _attn(proj, bias_tiles, lam_params, subln_col, *, q_blk, k_blk, v_blk, lam_init, t, hp=4):
    s = proj.shape[0]
    w = hp * HEAD_W
    assert q_blk % hp == 0 and k_blk % hp == 0 and v_blk % hp == 0
    once = pl.Buffered(1)
    return pl.pallas_call(
        functools.partial(_attn_kernel, t=t, hp=hp, lam_init=lam_init),
        grid=(HEADS // hp, s // t),
        in_specs=[pl.BlockSpec((t, w), lambda g, i: (i, q_blk // hp + g)),
                  pl.BlockSpec((s, w), lambda g, i: (0, k_blk // hp + g), pipeline_mode=once),
                  pl.BlockSpec((s, w), lambda g, i: (0, v_blk // hp + g), pipeline_mode=once),
                  pl.BlockSpec((hp, 2, t, 2 * t), lambda g, i: (g, 0, 0, 0), pipeline_mode=once),
                  pl.BlockSpec(lam_params.shape, lambda g, i: (0, 0)),
                  pl.BlockSpec((HEAD_W, 1), lambda g, i: (0, 0))],
        out_specs=pl.BlockSpec((t, w), lambda g, i: (i, g)),
        out_shape=jax.ShapeDtypeStruct((s, HEADS * HEAD_W), BF16),
        scratch_shapes=[pltpu.VMEM((hp, s // t, V_ROWS, t), BF16),
                        pltpu.VMEM((hp, HEAD_W, 2 * t), BF16),
                        pltpu.VMEM((2, hp, t, 2 * t), F32),
                        pltpu.VMEM((hp, 1, 2 * t), F32),
                        pltpu.VMEM((hp, V_ROWS, 2 * t), F32)],
        compiler_params=_params("arbitrary", "arbitrary"),
        name="diff_attn",
    )(proj, proj, proj, bias_tiles, lam_params, subln_col)


def _gates_kernel(x_ref, shift_ref, scale_ref, ws_ref, alog_ref, dtb_ref, o_ref, h_ref):
    h = (x_ref[...] * (1.0 + scale_ref[...]) + shift_ref[...]).astype(BF16)
    h_ref[...] = h
    tr = _dot_nt(ws_ref[...], h.astype(F32))
    beta = jax.nn.sigmoid(tr[0:HEADS])
    x = tr[HEADS:2 * HEADS] + dtb_ref[...]
    softplus = jnp.maximum(x, 0.0) + jnp.log1p(jnp.exp(-jnp.abs(x)))
    g = -jnp.exp(alog_ref[...]) * softplus
    pos = lax.broadcasted_iota(jnp.int32, g.shape, 1) % DN_CHUNK
    shift = 1
    while shift < DN_CHUNK:
        g = g + jnp.where(pos >= shift, pltpu.roll(g, shift, axis=1), 0.0)
        shift *= 2
    o_ref[0:HEADS] = beta
    o_ref[HEADS:2 * HEADS] = g


def _gates(x, shift, scale, w_small_t, alog_col, dtb_col, *, tb=1024):
    s, d = x.shape
    col = pl.BlockSpec((HEADS, 1), lambda i: (0, 0))
    vec = pl.BlockSpec((1, d), lambda i: (0, 0))
    return pl.pallas_call(
        _gates_kernel,
        grid=(s // tb,),
        in_specs=[pl.BlockSpec((tb, d), lambda i: (i, 0)), vec, vec,
                  pl.BlockSpec(w_small_t.shape, lambda i: (0, 0)), col, col],
        out_specs=[pl.BlockSpec((2 * HEADS, tb), lambda i: (0, i)), pl.BlockSpec((tb, d), lambda i: (i, 0))],
        out_shape=[jax.ShapeDtypeStruct((2 * HEADS, s), F32), jax.ShapeDtypeStruct((s, d), BF16)],
        compiler_params=_params("parallel"),
        name="gdn_gates",
    )(x, shift, scale, w_small_t, alog_col, dtb_col)


GROUP = 2 * DN_CHUNK
INV_BASE = 16


def _gdn_kernel(q_ref, k_ref, v_ref, z_ref, gt_ref, cwq_ref, cwk_ref, cwv_ref, nw_ref, *rest, tb, hp, n_cast):
    cast_in, (o_ref, *cast_out), (pad_sc, state_sc, o_sc) = rest[:n_cast], rest[n_cast:2 * n_cast + 1], rest[-3:]
    for src, dst in zip(cast_in, cast_out):
        dst[...] = src[...].astype(BF16)
    g = pl.program_id(0)
    ib = pl.program_id(1)
    halo = 8
    heads = range(hp)
    cols = [slice(hh * HEAD_W, (hh + 1) * HEAD_W) for hh in heads]

    @pl.when(ib == 0)
    def _():
        state_sc[...] = jnp.zeros_like(state_sc)
        pad_sc[:, 0:halo, :] = jnp.zeros((3 * hp, halo, HEAD_W), F32)

    def conv_silu(a, x_ref, cw_ref, hh):
        a = a * hp + hh
        pad_sc[a, halo:halo + tb, :] = x_ref[:, cols[hh]].astype(F32)
        cw = cw_ref[:, cols[hh]]
        y = cw[0:1] * pad_sc[a, halo - 3:halo - 3 + tb, :]
        for j in range(1, DN_CONV):
            y = y + cw[j:j + 1] * pad_sc[a, halo - 3 + j:halo - 3 + j + tb, :]
        pad_sc[a, 0:halo, :] = pad_sc[a, tb:tb + halo, :]
        return _silu(y)

    def l2n(x):
        return x * lax.rsqrt(jnp.sum(x * x, axis=-1, keepdims=True) + RMS_EPS)

    ii = lax.broadcasted_iota(jnp.int32, (GROUP, GROUP), 0)
    jj = lax.broadcasted_iota(jnp.int32, (GROUP, GROUP), 1)
    same = (ii >= DN_CHUNK) == (jj >= DN_CHUNK)
    tril = jnp.logical_and(same, ii >= jj)
    eye = (ii == jj).astype(F32)
    bits = INV_BASE.bit_length() - 1
    in_base = jnp.logical_and(ii >> bits == jj >> bits, ii > jj)
    below = []
    while (1 << bits) < DN_CHUNK:
        below.append(jnp.logical_and(ii >> (bits + 1) == jj >> (bits + 1), ii >> bits == (jj >> bits) + 1))
        bits += 1

    q, k, v, beta_c, cum_c, cum_t = [], [], [], [], [], []
    for hh in heads:
        q.append(l2n(conv_silu(0, q_ref, cwq_ref, hh)) * (HEAD_W ** -0.5))
        k.append(l2n(conv_silu(1, k_ref, cwk_ref, hh)))
        v.append(conv_silu(2, v_ref, cwv_ref, hh))
        head = g * hp + hh
        beta_t = jnp.broadcast_to(gt_ref[pl.ds(head, 1), :], (HEAD_W, tb))
        ct = jnp.broadcast_to(gt_ref[pl.ds(HEADS + head, 1), :], (HEAD_W, tb))
        beta_c.append(beta_t.T)
        cum_c.append(ct.T)
        cum_t.append(ct)

    groups = range(tb // GROUP)
    rows = [slice(r * GROUP, (r + 1) * GROUP) for r in groups]
    u, w, qk, qd, kg, cg = {}, {}, {}, {}, {}, {}

    def intra_chunk(r):
        kb, x, tinv, decay, merges = {}, {}, {}, {}, {}
        for hh in heads:
            sid = (r, hh)
            kg[sid], cg[sid] = k[hh][rows[r]], cum_c[hh][rows[r]]
            gdiff = cg[sid] - cum_t[hh][:, rows[r]]
            decay[hh] = jnp.where(tril, jnp.exp(jnp.where(tril, gdiff, 0.0)), 0.0)
            kb[hh] = kg[sid] * beta_c[hh][rows[r]]
            a = _dot_nt(kb[hh].astype(BF16), kg[sid].astype(BF16)) * decay[hh]
            diag = jnp.where(in_base, a, 0.0)
            x[hh] = diag.astype(BF16)
            tinv[hh] = eye - diag
            merges[hh] = [jnp.where(m, a, 0.0).astype(BF16) for m in below]
        yield
        power = 2
        while power < INV_BASE:
            for hh in heads:
                x[hh] = _dot(x[hh], x[hh]).astype(BF16)
            yield
            for hh in heads:
                tinv[hh] = tinv[hh] + _dot(tinv[hh].astype(BF16), x[hh])
            yield
            power *= 2
        for level in range(len(below)):
            half = {}
            for hh in heads:
                half[hh] = _dot(tinv[hh].astype(BF16), merges[hh][level]).astype(BF16)
            yield
            for hh in heads:
                tinv[hh] = tinv[hh] - _dot(half[hh], tinv[hh].astype(BF16))
            yield
        for hh in heads:
            sid = (r, hh)
            eg = jnp.exp(cg[sid])
            rhs = jnp.concatenate([v[hh][rows[r]] * beta_c[hh][rows[r]], kb[hh] * eg], axis=1).astype(BF16)
            uw = _dot(tinv[hh].astype(BF16), rhs)
            u[sid], w[sid] = uw[:, :HEAD_W], uw[:, HEAD_W:].astype(BF16)
            qg = q[hh][rows[r]]
            qk[sid] = (_dot_nt(qg.astype(BF16), kg[sid].astype(BF16)) * decay[hh]).astype(BF16)
            qd[sid] = (qg * eg).astype(BF16)
        yield

    def recurrence(r):
        v_new = {hh: [] for hh in heads}
        o_inter = {hh: [] for hh in heads}
        for c in range(2):
            cr = slice(c * DN_CHUNK, (c + 1) * DN_CHUNK)
            last = c * DN_CHUNK + DN_CHUNK - 1
            st, g_last = {}, {}
            for hh in heads:
                sid = (r, hh)
                g_last[hh] = cg[sid][last:last + 1, :]
                st[hh] = state_sc[hh]
                st16 = st[hh].astype(BF16)
                v_new[hh].append(u[sid][cr] - _dot(w[sid][cr], st16))
                o_inter[hh].append(_dot(qd[sid][cr], st16))
            yield
            for hh in heads:
                sid = (r, hh)
                kd = (kg[sid][cr] * jnp.exp(g_last[hh] - cg[sid][cr])).astype(BF16)
                state_sc[hh] = st[hh] * jnp.exp(g_last[hh]) + _dot_tn(kd, v_new[hh][c].astype(BF16))
            yield
        for hh in heads:
            vn_all = jnp.concatenate(v_new[hh], axis=0).astype(BF16)
            o_sc[hh, rows[r], :] = jnp.concatenate(o_inter[hh], axis=0) + _dot(qk[(r, hh)], vn_all)
        yield

    def interleave(slow, fast, ratio):
        slow_live = fast_live = True
        while slow_live or fast_live:
            if slow_live:
                slow_live = next(slow, "done") != "done"
            for _ in range(ratio):
                if fast_live:
                    fast_live = next(fast, "done") != "done"

    for _ in intra_chunk(0):
        pass
    for r in groups:
        nxt = intra_chunk(r + 1) if r + 1 < len(groups) else iter(())
        interleave(recurrence(r), nxt, 3)

    for hh in heads:
        o = o_sc[hh]
        on = o * lax.rsqrt(jnp.mean(o * o, axis=-1, keepdims=True) + RMS_EPS) * nw_ref[...]
        o_ref[:, cols[hh]] = (on * _silu(z_ref[:, cols[hh]].astype(F32))).astype(BF16)


def _gdn(proj, gates_t, conv_w, norm_w, casts, cast_idx, *, q_blk, k_blk, v_blk, z_blk, tb=512, hp=8):
    s = proj.shape[0]
    w = hp * HEAD_W
    assert all(blk % hp == 0 for blk in (q_blk, k_blk, v_blk, z_blk))
    col = lambda blk: pl.BlockSpec((tb, w), lambda g, i: (i, blk // hp + g))
    cw = lambda blk: pl.BlockSpec((DN_CONV, w), lambda g, i: (0, blk // hp + g))
    n_t = s // tb
    slabs = [_cast_slabs(arr, cast_idx, (HEADS // hp) * n_t, lambda g, i: g * n_t + i) for arr in casts]
    return pl.pallas_call(
        functools.partial(_gdn_kernel, tb=tb, hp=hp, n_cast=len(casts)),
        grid=(HEADS // hp, n_t),
        in_specs=[col(q_blk), col(k_blk), col(v_blk), col(z_blk),
                  pl.BlockSpec((2 * HEADS, tb), lambda g, i: (0, i)),
                  cw(0), cw(HEADS), cw(2 * HEADS),
                  pl.BlockSpec((1, HEAD_W), lambda g, i: (0, 0))] + [sl[0] for sl in slabs],
        out_specs=[pl.BlockSpec((tb, w), lambda g, i: (i, g))] + [sl[1] for sl in slabs],
        out_shape=[jax.ShapeDtypeStruct((s, HEADS * HEAD_W), BF16)] + [sl[2] for sl in slabs],
        scratch_shapes=[pltpu.VMEM((3 * hp, tb + 8, HEAD_W), F32),
                        pltpu.VMEM((hp, HEAD_W, HEAD_W), F32),
                        pltpu.VMEM((hp, tb, HEAD_W), F32)],
        compiler_params=_params("arbitrary", "arbitrary"),
        name="gdn",
    )(proj, proj, proj, proj, gates_t, conv_w, conv_w, conv_w, norm_w, *casts)


def _merge_kernel(ya_ref, yb_ref, ga_ref, gb_ref, x_ref, gate_ref, lng_ref, lnb_ref,
                  wa_ref, wb_ref, wo_ref, o_ref, m_sc, *, alpha):
    i = pl.program_id(0)
    n_tiles = pl.num_programs(0) - 1

    def merge(slot):
        a = _dot(ya_ref[...], wa_ref[...])
        b = _dot(yb_ref[...], wb_ref[...])
        m_sc[slot] = (jax.nn.sigmoid(ga_ref[...].astype(F32)) * a
                      + jax.nn.sigmoid(gb_ref[...].astype(F32)) * b).astype(BF16)

    def project(slot):
        y = _dot(m_sc[slot], wo_ref[...])
        r = alpha * x_ref[...] + gate_ref[...] * y
        o_ref[...] = _layer_norm(r, lng_ref[...], lnb_ref[...])

    pl.when(i == 0)(lambda: merge(0))
    for slot in range(2):
        @pl.when(jnp.logical_and(jnp.logical_and(i > 0, i < n_tiles), (i & 1) == slot))
        def _():
            project(1 - slot)
            merge(slot)

        pl.when(jnp.logical_and(i == n_tiles, ((i - 1) & 1) == slot))(functools.partial(project, slot))


def _merge(ya, yb, proj, x, gate, lng, lnb, w_a, w_b, w_o, *, ga_blk, gb_blk, alpha, tm=256):
    s, d = x.shape
    dv = ya.shape[1]
    n = s // tm
    vec = pl.BlockSpec((1, d), lambda i: (0, 0))
    const = lambda shape: pl.BlockSpec(shape, lambda i: (0, 0), pipeline_mode=pl.Buffered(1))
    cur = lambda i: jnp.minimum(i, n - 1)
    prev = lambda i: jnp.maximum(i - 1, 0)
    return pl.pallas_call(
        functools.partial(_merge_kernel, alpha=alpha),
        grid=(n + 1,),
        in_specs=[pl.BlockSpec((tm, dv), lambda i: (cur(i), 0)),
                  pl.BlockSpec((tm, dv), lambda i: (cur(i), 0)),
                  pl.BlockSpec((tm, d), lambda i: (cur(i), ga_blk)),
                  pl.BlockSpec((tm, d), lambda i: (cur(i), gb_blk)),
                  pl.BlockSpec((tm, d), lambda i: (prev(i), 0)), vec, vec, vec,
                  const(w_a.shape), const(w_b.shape), const(w_o.shape)],
        out_specs=pl.BlockSpec((tm, d), lambda i: (prev(i), 0)),
        out_shape=jax.ShapeDtypeStruct((s, d), F32),
        scratch_shapes=[pltpu.VMEM((2, tm, d), BF16)],
        compiler_params=_params("arbitrary"),
        name="merge",
    )(ya, yb, proj, proj, x, gate, lng, lnb, w_a, w_b, w_o)


def kernel(x, c, w_ada, b_ada, ln_g, ln_b, w_ffn_in, w_ffn_out, w_in, conv_w, dn_a_log, dn_dt_bias,
           dn_norm_w, diff_lambda, diff_subln_w, rel_bias, w_branch_a, w_branch_b, w_out):
    bsz, s, d = x.shape
    assert bsz == 1, "one sequence per call"
    depth = w_ada.shape[0]
    alpha = (2 * depth) ** 0.25
    hw = HEADS * HEAD_W
    attn_t = 256
    assert d % HEAD_W == 0 and s % 1024 == 0

    o_nb = 7 * hw
    o_ga = o_nb + 2 * HEADS
    nblk = d // HEAD_W
    blk = {"ga": 0, "gb": 1, "dq": 2 * nblk, "dk": 2 * nblk + HEADS, "dv": 2 * nblk + 2 * HEADS,
           "nq": 2 * nblk + 3 * HEADS, "nk": 2 * nblk + 4 * HEADS, "nv": 2 * nblk + 5 * HEADS,
           "nz": 2 * nblk + 6 * HEADS}

    bias_tiles = _bias_tiles(rel_bias, attn_t)
    wfi, wfo = w_ffn_in, w_ffn_out
    w_in_t = jnp.swapaxes(w_in, 1, 2)
    x2 = x[0]
    for l in range(depth):
        lam_init = 0.8 - 0.6 * math.exp(-0.3 * l)
        ada = _ada(c.reshape(d, 1), w_ada[l], b_ada[l].reshape(1, -1)).reshape(N_SUB, 3, 1, d)
        shift, scale, gate = ada[:, 0], ada[:, 1], ada[:, 2]
        lng, lnb = ln_g[l].reshape(N_SUB, 1, d), ln_b[l].reshape(N_SUB, 1, d)
        w_small_t = w_in_t[l, o_nb:o_ga]

        x2 = _ffn(x2, shift[0], scale[0], gate[0], lng[0], lnb[0], wfi, wfo, layer=l, which=0, alpha=alpha)

        gates_t, h = _gates(x2, shift[1], scale[1], w_small_t, dn_a_log[l].reshape(HEADS, 1),
                            dn_dt_bias[l].reshape(HEADS, 1))
        proj, wfi2, wfo2 = _proj(h, w_in_t[l], w_ffn_in, w_ffn_out, (l, 1),
                                 gate_row0=o_ga, n_gate_cols=2 * d, n_head_cols=o_nb)
        ya = _attn(proj, bias_tiles, diff_lambda[l], diff_subln_w[l].reshape(HEAD_W, 1),
                   q_blk=blk["dq"], k_blk=blk["dk"], v_blk=blk["dv"], lam_init=lam_init, t=attn_t)
        yb, wa16, wb16, wo16 = _gdn(proj, gates_t, conv_w[l], dn_norm_w[l].reshape(1, HEAD_W),
                                    (w_branch_a, w_branch_b, w_out), (l,),
                                    q_blk=blk["nq"], k_blk=blk["nk"], v_blk=blk["nv"], z_blk=blk["nz"])
        x2 = _merge(ya, yb, proj, x2, gate[1], lng[1], lnb[1], wa16, wb16, wo16,
                    ga_blk=blk["ga"], gb_blk=blk["gb"], alpha=alpha)

        x2 = _ffn(x2, shift[2], scale[2], gate[2], lng[2], lnb[2], wfi2[None, None], wfo2[None, None],
                  layer=0, which=0, alpha=alpha, tf=512)
    return x2[None]
```
